```python
import math
import jax
import jax.numpy as jnp
from jax import lax
import numpy as np

D_MODEL = 1024
BATCH = 8
SEQ = 2048
DEPTH = 2
DEC_BATCH = 128
DEC_SEQ = 8
PAST_LEN = 16384
PAGE_SIZE = 128

N_BRANCH = 4
BRANCH_W = D_MODEL // N_BRANCH
RET_HEADS = 4
RET_DK = BRANCH_W // RET_HEADS
RET_DV = BRANCH_W // RET_HEADS
RET_CHUNK = 64
ROPE_BASE = 10000.0
GLA_HEADS = 4
GLA_DK = BRANCH_W // 2 // GLA_HEADS
GLA_DV = BRANCH_W // GLA_HEADS
GLA_GATE_RANK = 16
GLA_GATE_NORM = 16.0
GLA_CHUNK = 16
GLA_QK = GLA_HEADS * GLA_DK
S5_GROUP = 16
S5_GROUPS = BRANCH_W // S5_GROUP
S5_STATE = 64
RWKV_HEADS = 4
RWKV_HD = BRANCH_W // RWKV_HEADS
RWKV_W_RANK = 32
RWKV_A_RANK = 32
RWKV_G_RANK = 64
RWKV_GN_EPS = 64e-5
D_FF = 2816
N_EXPERTS = 8
TOP_K = 2
D_FF_EXPERT = 3584
MOE_BLOCK = 256
N_DENSE = (DEPTH + 1) // 2
N_MOE = DEPTH // 2
P_DIM = 256
ALPHA = (2 * DEPTH) ** 0.25
BETA = (8 * DEPTH) ** -0.25
LN_EPS = 1e-5
RET_COLS = 4 * BRANCH_W
GLA_COLS = 2 * GLA_QK + 2 * BRANCH_W + GLA_GATE_RANK
S5_COLS = BRANCH_W
RWKV_COLS = 4 * BRANCH_W
GATE_COLS = N_BRANCH * D_MODEL
D_IN = RET_COLS + GLA_COLS + S5_COLS + RWKV_COLS + GATE_COLS
IN_SPLITS = (RET_COLS, RET_COLS + GLA_COLS, RET_COLS + GLA_COLS + S5_COLS,
             RET_COLS + GLA_COLS + S5_COLS + RWKV_COLS)

kernel_name = 'hybrid_ret_gla_s5_rwkv7_deepnorm_step'


def head_layer_norm(x, g, b, eps):
    xf = x.astype(jnp.float32)
    mu = jnp.mean(xf, -1, keepdims=True)
    var = jnp.mean(jnp.square(xf - mu), -1, keepdims=True)
    return (xf - mu) * lax.rsqrt(var + eps) * g + b


def layer_norm(x, g, b):
    return head_layer_norm(x, g, b, LN_EPS).astype(x.dtype)


def head_rms_norm(x, g):
    xf = x.astype(jnp.float32)
    return xf * lax.rsqrt(jnp.mean(jnp.square(xf), -1, keepdims=True) + LN_EPS) * g


def rotary(x, pos):
    half = x.shape[-1] // 2
    freq = ROPE_BASE ** (-jnp.arange(half, dtype=jnp.float32) / half)
    ang = pos.astype(jnp.float32)[:, None] * freq[None, :]
    cos, sin = jnp.cos(ang)[:, None, :], jnp.sin(ang)[:, None, :]
    xf = x.astype(jnp.float32)
    x1, x2 = xf[..., :half], xf[..., half:]
    return jnp.concatenate([x1 * cos - x2 * sin, x1 * sin + x2 * cos], axis=-1)


def retention(q, k, v, s0):
    B, T, H, _ = q.shape
    dv = v.shape[-1]
    C = math.gcd(T, RET_CHUNK)
    N = T // C
    q, k, v = (t.astype(jnp.float32).reshape(B, N, C, H, -1) for t in (q, k, v))
    log_g = jnp.log1p(-jnp.exp2(-5.0 - jnp.arange(H, dtype=jnp.float32)))
    idx = jnp.arange(C, dtype=jnp.float32)
    diff = idx[:, None] - idx[None, :]
    dmask = jnp.where(diff >= 0, jnp.exp(log_g[:, None, None] * jnp.maximum(diff, 0.0)), 0.0)
    scores = jnp.einsum('bnihd,bnjhd->bnhij', q, k) * dmask
    intra = jnp.einsum('bnhij,bnjhe->bnihe', scores, v)
    k_dec = jnp.exp(log_g[None, :] * (C - 1.0 - idx)[:, None])
    upd = jnp.einsum('bnjhd,bnjhe->nbhde', k * k_dec[:, :, None], v)
    c_dec = jnp.exp(log_g * C)[None, :, None, None]

    def step(s, u):
        return s * c_dec + u, s

    s_last, s_prev = lax.scan(step, s0.astype(jnp.float32), upd)
    q_dec = jnp.exp(log_g[None, :] * (idx + 1.0)[:, None])
    inter = jnp.einsum('bnihd,nbhde->bnihe', q * q_dec[:, :, None], s_prev)
    return (intra + inter).reshape(B, T, H, dv), s_last


def gla(q, k, v, glog, s0):
    B, T, H, _ = q.shape
    dv = v.shape[-1]
    C = math.gcd(T, GLA_CHUNK)
    N = T // C
    q, k, v, glog = (t.astype(jnp.float32).reshape(B, N, C, H, -1) for t in (q, k, v, glog))
    bcum = jnp.cumsum(glog, axis=2)
    causal = (jnp.arange(C)[:, None] >= jnp.arange(C)[None, :])[None, None, :, :, None, None]
    rel = jnp.where(causal, bcum[:, :, :, None] - bcum[:, :, None, :], -jnp.inf)
    attn = jnp.sum(q[:, :, :, None] * k[:, :, None, :] * jnp.exp(rel), axis=-1)
    intra = jnp.einsum('bnijh,bnjhe->bnihe', attn, v)
    b_last = bcum[:, :, -1:]
    upd = jnp.einsum('bnjhd,bnjhe->nbhde', k * jnp.exp(b_last - bcum), v)
    c_dec = jnp.moveaxis(jnp.exp(b_last[:, :, 0]), 1, 0)[..., None]

    def step(s, inp):
        u, dcy = inp
        return s * dcy + u, s

    s_last, s_prev = lax.scan(step, s0.astype(jnp.float32), (upd, c_dec))
    inter = jnp.einsum('bnihd,nbhde->bnihe', q * jnp.exp(bcum), s_prev)
    return (intra + inter).reshape(B, T, H, dv), s_last


def complex_affine_combine(e1, e2):
    a1r, a1i, b1r, b1i = e1
    a2r, a2i, b2r, b2i = e2
    return (a2r * a1r - a2i * a1i, a2r * a1i + a2i * a1r,
            a2r * b1r - a2i * b1i + b2r, a2r * b1i + a2i * b1r + b2i)


def s5(u, h0_re, h0_im, log_dt, a_re, a_im, b_re, b_im, c_re, c_im, d, w_glu):
    B, T, _ = u.shape
    uf = u.astype(jnp.float32)
    ug = uf.reshape(B, T, S5_GROUPS, S5_GROUP)
    dt = jnp.exp(log_dt.astype(jnp.float32))[:, None]
    are, aim = a_re.astype(jnp.float32), a_im.astype(jnp.float32)
    mag = jnp.exp(dt * are)
    ang = dt * aim
    abar_re, abar_im = mag * jnp.cos(ang), mag * jnp.sin(ang)
    den = are * are + aim * aim
    n_re = abar_re - 1.0
    f_re = (n_re * are + abar_im * aim) / den
    f_im = (abar_im * are - n_re * aim) / den
    bb_re = f_re[..., None] * b_re - f_im[..., None] * b_im
    bb_im = f_re[..., None] * b_im + f_im[..., None] * b_re
    x_re = jnp.einsum('gpc,btgc->btgp', bb_re, ug)
    x_im = jnp.einsum('gpc,btgc->btgp', bb_im, ug)
    h0r, h0i = h0_re.astype(jnp.float32), h0_im.astype(jnp.float32)
    x_re = x_re.at[:, 0].add(abar_re * h0r - abar_im * h0i)
    x_im = x_im.at[:, 0].add(abar_re * h0i + abar_im * h0r)
    ar = jnp.broadcast_to(abar_re, x_re.shape)
    ai = jnp.broadcast_to(abar_im, x_im.shape)
    _, _, h_re, h_im = lax.associative_scan(complex_affine_combine, (ar, ai, x_re, x_im), axis=1)
    y = jnp.einsum('gcp,btgp->btgc', c_re, h_re) - jnp.einsum('gcp,btgp->btgc', c_im, h_im)
    y = jax.nn.gelu(y.reshape(B, T, BRANCH_W) + d * uf)
    out = y * jax.nn.sigmoid(y @ w_glu)
    return out, h_re[:, -1], h_im[:, -1]


def rwkv7(r, k, v, logw, a, kk, s0):
    decay = jnp.exp(logw)
    seq = tuple(jnp.moveaxis(t.astype(jnp.float32), 1, 0) for t in (r, k, v, decay, a, kk))

    def step(s, inp):
        r_t, k_t, v_t, w_t, a_t, kk_t = inp
        sk = jnp.einsum('bhvk,bhk->bhv', s, kk_t)
        s = (s * w_t[:, :, None, :] - sk[..., None] * (kk_t * a_t)[:, :, None, :]
             + v_t[..., None] * k_t[:, :, None, :])
        return s, jnp.einsum('bhvk,bhk->bhv', s, r_t)

    s_last, y = lax.scan(step, s0.astype(jnp.float32), seq)
    return jnp.moveaxis(y, 0, 1), s_last


def token_mixers(x, pos, st_ret, st_gla, st_s5re, st_s5im, st_rwkv, st_shift, L):
    B, T, _ = x.shape
    cols = x @ L['w_in']
    ret_c, gla_c, s5_c, rwkv_c, gate_c = jnp.split(cols, IN_SPLITS, axis=-1)

    rq, rk, rv, rg = jnp.split(ret_c, 4, axis=-1)
    rq = rotary(rq.reshape(B, T, RET_HEADS, RET_DK), pos)
    rk = rotary(rk.reshape(B, T, RET_HEADS, RET_DK), pos) * (RET_DK ** -0.5)
    ro, ret_new = retention(rq, rk, rv.reshape(B, T, RET_HEADS, RET_DV), st_ret)
    ro = head_layer_norm(ro, L['ret_gn_g'], L['ret_gn_b'], LN_EPS).reshape(B, T, BRANCH_W) * jax.nn.silu(rg)

    gq, gk, gv, gg, gr = jnp.split(gla_c, (GLA_QK, 2 * GLA_QK, 2 * GLA_QK + BRANCH_W,
                                           2 * GLA_QK + 2 * BRANCH_W), axis=-1)
    glog = jax.nn.log_sigmoid((gr @ L['gla_wg2'] + L['gla_bg']).astype(jnp.float32)) / GLA_GATE_NORM
    go, gla_new = gla(gq.reshape(B, T, GLA_HEADS, GLA_DK) * (GLA_DK ** -0.5),
                      gk.reshape(B, T, GLA_HEADS, GLA_DK),
                      gv.reshape(B, T, GLA_HEADS, GLA_DV),
                      glog.reshape(B, T, GLA_HEADS, GLA_DK), st_gla)
    go = head_rms_norm(go, L['gla_gn']).reshape(B, T, BRANCH_W) * jax.nn.silu(gg)

    so, s5re_new, s5im_new = s5(s5_c, st_s5re, st_s5im, L['s5_log_dt'], L['s5_a_re'], L['s5_a_im'],
                                L['s5_b_re'], L['s5_b_im'], L['s5_c_re'], L['s5_c_im'],
                                L['s5_d'], L['s5_w_glu'])

    prev = jnp.concatenate([st_shift[:, None].astype(rwkv_c.dtype), rwkv_c[:, :-1]], axis=1)
    delta = prev - rwkv_c
    cr, ck, cv, cz = jnp.split(rwkv_c, 4, axis=-1)
    dr, dk, dv, dz = jnp.split(delta, 4, axis=-1)
    mu = L['rwkv_mu']
    r = cr + dr * mu[0]
    k = ck + dk * mu[1]
    v = cv + dv * mu[2]
    zw = cz + dz * mu[3]
    za = cz + dz * mu[4]
    zg = cz + dz * mu[5]
    w_raw = (L['rwkv_w0'] + jnp.tanh(zw @ L['rwkv_w1']) @ L['rwkv_w2']).astype(jnp.float32)
    logw = -jnp.exp(-jax.nn.softplus(-w_raw) - 0.5)
    a = jax.nn.sigmoid((L['rwkv_a0'] + (za @ L['rwkv_a1']) @ L['rwkv_a2']).astype(jnp.float32))
    g = jax.nn.sigmoid(zg @ L['rwkv_g1']) @ L['rwkv_g2']
    hs = (B, T, RWKV_HEADS, RWKV_HD)
    kk = (k * L['rwkv_kk']).astype(jnp.float32).reshape(hs)
    kk = kk * lax.rsqrt(jnp.maximum(jnp.sum(kk * kk, -1, keepdims=True), 1e-24))
    k = (k * (1.0 + (a - 1.0) * L['rwkv_ka'])).astype(jnp.float32).reshape(hs)
    r = r.astype(jnp.float32).reshape(hs)
    v = v.astype(jnp.float32).reshape(hs)
    wy, rwkv_new = rwkv7(r, k, v, logw.reshape(hs), a.reshape(hs), kk, st_rwkv)
    wy = head_layer_norm(wy, L['rwkv_gn_g'], L['rwkv_gn_b'], RWKV_GN_EPS)
    wy = wy + jnp.sum(r * k * L['rwkv_rk'], -1, keepdims=True) * v
    wo = wy.reshape(B, T, BRANCH_W) * g
    shift_new = rwkv_c[:, -1]

    branches = jnp.stack([ro, go, so, wo], axis=2).astype(x.dtype)
    proj = jnp.einsum('btnc,ncd->btnd', branches, L['w_branch'])
    gates = jax.nn.sigmoid(gate_c.reshape(B, T, N_BRANCH, D_MODEL))
    h = jnp.sum(proj * gates, axis=2) @ L['w_o']
    new = [ret_new.astype(st_ret.dtype), gla_new.astype(st_gla.dtype), s5re_new.astype(st_s5re.dtype),
           s5im_new.astype(st_s5im.dtype), rwkv_new.astype(st_rwkv.dtype), shift_new.astype(st_shift.dtype)]
    return h, new


def swiglu(x, w1, w3, w2):
    return (jax.nn.silu(x @ w1) * (x @ w3)) @ w2


def moe_swiglu(x, router, w1, w3, w2):
    B, T, D = x.shape
    n = B * T
    nk = n * TOP_K
    xt = x.reshape(n, D)
    logits = (xt @ router).astype(jnp.float32)
    top_logit, top_idx = lax.top_k(logits, TOP_K)
    top_w = jax.nn.softmax(top_logit, axis=-1).astype(x.dtype)
    flat_e = top_idx.reshape(nk)
    flat_tok = jnp.arange(nk, dtype=jnp.int32) // TOP_K
    order = jnp.argsort(flat_e)
    se = flat_e[order]
    counts = jnp.bincount(flat_e, length=N_EXPERTS)
    padded = (counts + MOE_BLOCK - 1) // MOE_BLOCK * MOE_BLOCK
    pad_end = jnp.cumsum(padded)
    start = jnp.cumsum(counts) - counts
    slot = (pad_end - padded)[se] + jnp.arange(nk, dtype=jnp.int32) - start[se]
    n_blocks = -(-(nk + N_EXPERTS * (MOE_BLOCK - 1)) // MOE_BLOCK)
    cap = n_blocks * MOE_BLOCK
    slot_tok = jnp.full((cap,), n, jnp.int32).at[slot].set(flat_tok[order])
    slot_w = jnp.zeros((cap,), x.dtype).at[slot].set(top_w.reshape(nk)[order])
    block_e = jnp.minimum(jnp.searchsorted(pad_end, jnp.arange(n_blocks, dtype=jnp.int32) * MOE_BLOCK,
                                           side='right'), N_EXPERTS - 1)
    xb = jnp.concatenate([xt, jnp.zeros((1, D), xt.dtype)], 0)[slot_tok].reshape(n_blocks, MOE_BLOCK, D)

    def expert_block(args):
        xblk, e = args
        return (jax.nn.silu(xblk @ w1[e]) * (xblk @ w3[e])) @ w2[e]

    yb = lax.map(expert_block, (xb, block_e)).reshape(cap, D)
    y = jax.ops.segment_sum(yb * slot_w[:, None], slot_tok, num_segments=n + 1)[:n]
    return y.reshape(B, T, D)


def run_trunk(x, p, pos, st_ret, st_gla, st_s5re, st_s5im, st_rwkv, st_shift, lp, ch):
    ffn_w1, ffn_w3, ffn_w2, moe_router, moe_w1, moe_w3, moe_w2 = ch
    outs = [[], [], [], [], [], []]
    for i in range(DEPTH):
        L = {name: arr[i] for name, arr in lp.items()}
        h, new = token_mixers(x, pos, st_ret[i], st_gla[i], st_s5re[i], st_s5im[i], st_rwkv[i], st_shift[i], L)
        x = layer_norm(ALPHA * x + h, L['ln1_g'], L['ln1_b'])
        if i % 2 == 0:
            f = swiglu(x, ffn_w1[i // 2], ffn_w3[i // 2], ffn_w2[i // 2])
        else:
            f = moe_swiglu(x, moe_router[i // 2], moe_w1[i // 2], moe_w3[i // 2], moe_w2[i // 2])
        e = (p[i].astype(x.dtype) @ L['w_pe']) * jax.nn.sigmoid(x @ L['w_pg'])
        x = layer_norm(ALPHA * x + f + e, L['ln2_g'], L['ln2_b'])
        for lst, s in zip(outs, new):
            lst.append(s)
    return x, [jnp.stack(lst, 0) for lst in outs]


def setup_inputs(seed: int = 0) -> dict:
    key = jax.random.key(seed)
    keys = jax.random.split(key, 64)
    counter = [0]

    def nxt():
        counter[0] += 1
        return keys[counter[0] - 1]

    def nrm(shape, scale):
        return jax.random.normal(nxt(), shape, jnp.float32) * scale

    def uni(shape, lo, hi):
        return jax.random.uniform(nxt(), shape, jnp.float32, lo, hi)

    def gain(shape):
        return 1.0 + nrm(shape, 0.02)

    inp = {}
    inp['x_prompt'] = nrm((BATCH, SEQ, D_MODEL), 1.0)
    inp['x_sample'] = nrm((DEC_BATCH, DEC_SEQ, D_MODEL), 1.0)
    inp['state_ret'] = nrm((DEPTH, DEC_BATCH, RET_HEADS, RET_DK, RET_DV), 1.0)
    inp['state_gla'] = nrm((DEPTH, DEC_BATCH, GLA_HEADS, GLA_DK, GLA_DV), 0.5)
    inp['state_s5_re'] = nrm((DEPTH, DEC_BATCH, S5_GROUPS, S5_STATE), 0.1)
    inp['state_s5_im'] = nrm((DEPTH, DEC_BATCH, S5_GROUPS, S5_STATE), 0.1)
    inp['state_rwkv'] = nrm((DEPTH, DEC_BATCH, RWKV_HEADS, RWKV_HD, RWKV_HD), 0.5)
    inp['state_shift'] = nrm((DEPTH, DEC_BATCH, RWKV_COLS), 1.0)
    inp['p_prompt'] = nrm((DEPTH, BATCH, SEQ, P_DIM), 1.0)
    inp['p_sample'] = nrm((DEPTH, DEC_BATCH, DEC_SEQ, P_DIM), 1.0)
    inp['w_in'] = nrm((DEPTH, D_MODEL, D_IN), D_MODEL ** -0.5)
    inp['ret_gn_g'] = gain((DEPTH, RET_HEADS, RET_DV))
    inp['ret_gn_b'] = nrm((DEPTH, RET_HEADS, RET_DV), 0.02)
    inp['gla_wg2'] = nrm((DEPTH, GLA_GATE_RANK, GLA_QK), GLA_GATE_RANK ** -0.5)
    inp['gla_bg'] = nrm((DEPTH, GLA_QK), 0.1)
    inp['gla_gn'] = gain((DEPTH, GLA_HEADS, GLA_DV))
    inp['s5_log_dt'] = uni((DEPTH, S5_GROUPS), math.log(1e-3), math.log(1e-1))
    inp['s5_a_re'] = -0.5 + nrm((DEPTH, S5_GROUPS, S5_STATE), 0.01)
    inp['s5_a_im'] = math.pi * jnp.arange(S5_STATE, dtype=jnp.float32) + nrm((DEPTH, S5_GROUPS, S5_STATE), 0.01)
    inp['s5_b_re'] = nrm((DEPTH, S5_GROUPS, S5_STATE, S5_GROUP), (2 * S5_GROUP) ** -0.5)
    inp['s5_b_im'] = nrm((DEPTH, S5_GROUPS, S5_STATE, S5_GROUP), (2 * S5_GROUP) ** -0.5)
    inp['s5_c_re'] = nrm((DEPTH, S5_GROUPS, S5_GROUP, S5_STATE), S5_STATE ** -0.5)
    inp['s5_c_im'] = nrm((DEPTH, S5_GROUPS, S5_GROUP, S5_STATE), S5_STATE ** -0.5)
    inp['s5_d'] = nrm((DEPTH, BRANCH_W), 1.0)
    inp['s5_w_glu'] = nrm((DEPTH, BRANCH_W, BRANCH_W), BRANCH_W ** -0.5)
    inp['rwkv_mu'] = uni((DEPTH, 6, BRANCH_W), 0.0, 1.0)
    inp['rwkv_w0'] = -1.0 + nrm((DEPTH, BRANCH_W), 0.3)
    inp['rwkv_w1'] = nrm((DEPTH, BRANCH_W, RWKV_W_RANK), BRANCH_W ** -0.5)
    inp['rwkv_w2'] = nrm((DEPTH, RWKV_W_RANK, BRANCH_W), 0.1 * RWKV_W_RANK ** -0.5)
    inp['rwkv_a0'] = nrm((DEPTH, BRANCH_W), 0.1)
    inp['rwkv_a1'] = nrm((DEPTH, BRANCH_W, RWKV_A_RANK), BRANCH_W ** -0.5)
    inp['rwkv_a2'] = nrm((DEPTH, RWKV_A_RANK, BRANCH_W), RWKV_A_RANK ** -0.5)
    inp['rwkv_g1'] = nrm((DEPTH, BRANCH_W, RWKV_G_RANK), BRANCH_W ** -0.5)
    inp['rwkv_g2'] = nrm((DEPTH, RWKV_G_RANK, BRANCH_W), RWKV_G_RANK ** -0.5)
    inp['rwkv_kk'] = 0.85 + nrm((DEPTH, BRANCH_W), 0.02)
    inp['rwkv_ka'] = 1.0 + nrm((DEPTH, BRANCH_W), 0.02)
    inp['rwkv_rk'] = nrm((DEPTH, RWKV_HEADS, RWKV_HD), 0.1)
    inp['rwkv_gn_g'] = gain((DEPTH, RWKV_HEADS, RWKV_HD))
    inp['rwkv_gn_b'] = nrm((DEPTH, RWKV_HEADS, RWKV_HD), 0.02)
    inp['w_branch'] = nrm((DEPTH, N_BRANCH, BRANCH_W, D_MODEL), BETA * BRANCH_W ** -0.5)
    inp['w_o'] = nrm((DEPTH, D_MODEL, D_MODEL), BETA * D_MODEL ** -0.5)
    inp['ln1_g'] = gain((DEPTH, D_MODEL))
    inp['ln1_b'] = nrm((DEPTH, D_MODEL), 0.02)
    inp['ln2_g'] = gain((DEPTH, D_MODEL))
    inp['ln2_b'] = nrm((DEPTH, D_MODEL), 0.02)
    inp['w_pe'] = nrm((DEPTH, P_DIM, D_MODEL), BETA * P_DIM ** -0.5)
    inp['w_pg'] = nrm((DEPTH, D_MODEL, D_MODEL), D_MODEL ** -0.5)
    inp['ffn_w1'] = nrm((N_DENSE, D_MODEL, D_FF), D_MODEL ** -0.5)
    inp['ffn_w3'] = nrm((N_DENSE, D_MODEL, D_FF), D_MODEL ** -0.5)
    inp['ffn_w2'] = nrm((N_DENSE, D_FF, D_MODEL), BETA * D_FF ** -0.5)
    inp['moe_router'] = nrm((N_MOE, D_MODEL, N_EXPERTS), D_MODEL ** -0.5)
    inp['moe_w1'] = nrm((N_MOE, N_EXPERTS, D_MODEL, D_FF_EXPERT), D_MODEL ** -0.5)
    inp['moe_w3'] = nrm((N_MOE, N_EXPERTS, D_MODEL, D_FF_EXPERT), D_MODEL ** -0.5)
    inp['moe_w2'] = nrm((N_MOE, N_EXPERTS, D_FF_EXPERT, D_MODEL), BETA * D_FF_EXPERT ** -0.5)
    return inp


def reference(x_prompt, x_sample, state_ret, state_gla, state_s5_re, state_s5_im, state_rwkv, state_shift,
              p_prompt, p_sample, w_in, ret_gn_g, ret_gn_b, gla_wg2, gla_bg, gla_gn,
              s5_log_dt, s5_a_re, s5_a_im, s5_b_re, s5_b_im, s5_c_re, s5_c_im, s5_d, s5_w_glu,
              rwkv_mu, rwkv_w0, rwkv_w1, rwkv_w2, rwkv_a0, rwkv_a1, rwkv_a2, rwkv_g1, rwkv_g2,
              rwkv_kk, rwkv_ka, rwkv_rk, rwkv_gn_g, rwkv_gn_b, w_branch, w_o,
              ln1_g, ln1_b, ln2_g, ln2_b, w_pe, w_pg, ffn_w1, ffn_w3, ffn_w2,
              moe_router, moe_w1, moe_w3, moe_w2):
    lp = dict(w_in=w_in, ret_gn_g=ret_gn_g, ret_gn_b=ret_gn_b, gla_wg2=gla_wg2, gla_bg=gla_bg, gla_gn=gla_gn,
              s5_log_dt=s5_log_dt, s5_a_re=s5_a_re, s5_a_im=s5_a_im, s5_b_re=s5_b_re, s5_b_im=s5_b_im,
              s5_c_re=s5_c_re, s5_c_im=s5_c_im, s5_d=s5_d, s5_w_glu=s5_w_glu,
              rwkv_mu=rwkv_mu, rwkv_w0=rwkv_w0, rwkv_w1=rwkv_w1, rwkv_w2=rwkv_w2, rwkv_a0=rwkv_a0,
              rwkv_a1=rwkv_a1, rwkv_a2=rwkv_a2, rwkv_g1=rwkv_g1, rwkv_g2=rwkv_g2, rwkv_kk=rwkv_kk,
              rwkv_ka=rwkv_ka, rwkv_rk=rwkv_rk, rwkv_gn_g=rwkv_gn_g, rwkv_gn_b=rwkv_gn_b,
              w_branch=w_branch, w_o=w_o, ln1_g=ln1_g, ln1_b=ln1_b, ln2_g=ln2_g, ln2_b=ln2_b,
              w_pe=w_pe, w_pg=w_pg)
    ch = (ffn_w1, ffn_w3, ffn_w2, moe_router, moe_w1, moe_w3, moe_w2)
    bp = x_prompt.shape[0]

    def empty(st):
        return jnp.zeros((DEPTH, bp) + st.shape[2:], st.dtype)

    pos_prompt = jnp.arange(x_prompt.shape[1], dtype=jnp.int32)
    pos_sample = PAST_LEN + jnp.arange(x_sample.shape[1], dtype=jnp.int32)
    y_prompt, sp = run_trunk(x_prompt, p_prompt, pos_prompt, empty(state_ret), empty(state_gla),
                             empty(state_s5_re), empty(state_s5_im), empty(state_rwkv), empty(state_shift), lp, ch)
    y_sample, ss = run_trunk(x_sample, p_sample, pos_sample, state_ret, state_gla, state_s5_re, state_s5_im,
                             state_rwkv, state_shift, lp, ch)
    return (y_prompt, y_sample, sp[0], ss[0], sp[1], ss[1], sp[2], ss[2], sp[3], ss[3], sp[4], ss[4], sp[5], ss[5])
```

```python
import functools
import math

import jax
import jax.numpy as jnp
from jax import lax
from jax.experimental import pallas as pl
from jax.experimental.pallas import tpu as pltpu

F32 = jnp.float32
BF16 = jnp.bfloat16

LANES = 128
SUBLANES = 8
VMEM_LIMIT = 56 * 1024 * 1024

N_BRANCH = 4
BRANCH_W = 256
HEADS = 4
HEAD_D = 64
GLA_DK = 32
GLA_QK = HEADS * GLA_DK
GLA_GATE_RANK = 16
GLA_GATE_NORM = 16.0
S5_GROUP = 16
S5_GROUPS = 16
S5_STATE = 64
S5_CH = S5_GROUPS * S5_STATE
ROPE_BASE = 10000.0
RWKV_GN_EPS = 64e-5
LN_EPS = 1e-5
N_EXPERTS = 8
TOP_K = 2
MOE_BLOCK = 256
ROW_TILE = 256

COL_RET = 0
COL_GLA_Q = 1024
COL_GLA_K = 1152
COL_GLA_V = 1280
COL_GLA_G = 1536
COL_S5 = 1792
COL_RWKV = 2048
COL_GATE = 3072
COL_GLA_R = 7168
D_IN_PAD = 7296


def _cparams(*sem):
    return pltpu.CompilerParams(dimension_semantics=sem, vmem_limit_bytes=VMEM_LIMIT)


def _split_bf16(x):
    hi = x.astype(BF16)
    lo = (x - hi.astype(F32)).astype(BF16)
    return hi, lo


def _seg_dot(x, m_ref):
    hi, lo = _split_bf16(x)
    m = m_ref[...]
    return (jnp.dot(hi, m, preferred_element_type=F32)
            + jnp.dot(lo, m, preferred_element_type=F32))


def _bdot(x, w):
    return jnp.dot(x.astype(BF16), w, preferred_element_type=F32)


def _sigmoid(x):
    return 1.0 / (1.0 + jnp.exp(-x))


def _silu(x):
    return x * _sigmoid(x)


def _log1p_exp_neg_abs(x):
    return jnp.log1p(jnp.exp(-jnp.abs(x)))


def _layer_norm(x, g, b):
    mu = jnp.mean(x, axis=-1, keepdims=True)
    xc = x - mu
    var = jnp.mean(xc * xc, axis=-1, keepdims=True)
    return xc * lax.rsqrt(var + LN_EPS) * g + b


def _row_spec(tm, width, col_block):
    return pl.BlockSpec((tm, width), lambda i, cb=col_block: (i, cb))


def _const_spec(shape):
    nd = len(shape)
    return pl.BlockSpec(shape, lambda i, nd=nd: (0,) * nd)


def _rowwise(body, n_rows, row_in, const_in, out_widths, out_dtypes=None, tm=ROW_TILE, name=None):
    assert n_rows % tm == 0
    out_dtypes = out_dtypes or [F32] * len(out_widths)
    in_specs = [_row_spec(tm, w, cb) for _, w, cb in row_in] + [_const_spec(a.shape) for a in const_in]
    out_specs = [_row_spec(tm, w, 0) for w in out_widths]
    out_shape = [jax.ShapeDtypeStruct((n_rows, w), dt) for w, dt in zip(out_widths, out_dtypes)]
    return pl.pallas_call(
        body,
        grid=(n_rows // tm,),
        in_specs=in_specs,
        out_specs=out_specs,
        out_shape=out_shape,
        compiler_params=_cparams("parallel"),
        name=name,
    )(*[a for a, _, _ in row_in], *const_in)


def _matmul_kernel(x_ref, w_ref, o_ref):
    o_ref[...] = _bdot(x_ref[...], w_ref[...]).astype(o_ref.dtype)


def _matmul(x, w, tm=512, tn=None, out_dtype=F32, name=None):
    m, k = x.shape
    n = w.shape[1]
    tn = tn or n
    tm = tm if m % tm == 0 else ROW_TILE
    assert m % tm == 0 and n % tn == 0
    return pl.pallas_call(
        _matmul_kernel,
        grid=(n // tn, m // tm),
        in_specs=[pl.BlockSpec((tm, k), lambda j, i: (i, 0)),
                  pl.BlockSpec((k, tn), lambda j, i: (0, j))],
        out_specs=pl.BlockSpec((tm, tn), lambda j, i: (i, j)),
        out_shape=jax.ShapeDtypeStruct((m, n), out_dtype),
        compiler_params=_cparams("parallel", "parallel"),
        name=name,
    )(x, w)


def _prep_kernel(rq_ref, rk_ref, cos_ref, sin_ref,
                 gq_ref, gr_ref, wg2_ref, bg_ref,
                 c_ref, prev_ref,
                 mu_ref, w0_ref, w1_ref, w2_ref, a0_ref, a1_ref, a2_ref, g1_ref, g2_ref,
                 kkp_ref, kap_ref, rkp_ref, ones_ref,
                 oq_ref, ok_ref, ogq_ref, oal_ref,
                 orr_ref, okm_ref, ov_ref, odec_ref, okk_ref, obeta_ref, obonus_ref, og_ref):
    cos = cos_ref[...]
    sin = sin_ref[...]
    lane = lax.broadcasted_iota(jnp.int32, cos.shape, 1)
    first_half = (lane % HEAD_D) < (HEAD_D // 2)

    def rot(x):
        partner = jnp.where(first_half,
                            pltpu.roll(x, BRANCH_W - HEAD_D // 2, 1),
                            pltpu.roll(x, HEAD_D // 2, 1))
        return x * cos + partner * sin

    oq_ref[...] = rot(rq_ref[...])
    ok_ref[...] = rot(rk_ref[...]) * (HEAD_D ** -0.5)

    z = _bdot(gr_ref[...], wg2_ref[...]) + bg_ref[...]
    glog = (jnp.minimum(z, 0.0) - _log1p_exp_neg_abs(z)) / GLA_GATE_NORM
    oal_ref[...] = jnp.exp(glog)
    ogq_ref[...] = gq_ref[...] * (GLA_DK ** -0.5)

    c = c_ref[...]
    d = prev_ref[...] - c
    mu = mu_ref[...]
    cr, ck, cv, cz = (c[:, i * BRANCH_W:(i + 1) * BRANCH_W] for i in range(4))
    dr, dk, dv, dz = (d[:, i * BRANCH_W:(i + 1) * BRANCH_W] for i in range(4))
    r = cr + dr * mu[0:1]
    k = ck + dk * mu[1:2]
    v = cv + dv * mu[2:3]
    zw = cz + dz * mu[3:4]
    za = cz + dz * mu[4:5]
    zg = cz + dz * mu[5:6]
    w_raw = w0_ref[...] + _bdot(jnp.tanh(_bdot(zw, w1_ref[...])), w2_ref[...])
    sp = jnp.maximum(-w_raw, 0.0) + _log1p_exp_neg_abs(w_raw)
    logw = -jnp.exp(-sp - 0.5)
    odec_ref[...] = jnp.exp(logw)
    a = _sigmoid(a0_ref[...] + _bdot(_bdot(za, a1_ref[...]), a2_ref[...]))
    og_ref[...] = _bdot(_sigmoid(_bdot(zg, g1_ref[...])), g2_ref[...])
    kk = k * kkp_ref[...]
    ss = _seg_dot(kk * kk, ones_ref)
    kk = kk * lax.rsqrt(jnp.maximum(ss, 1e-24))
    km = k * (1.0 + (a - 1.0) * kap_ref[...])
    orr_ref[...] = r
    okm_ref[...] = km
    ov_ref[...] = v
    okk_ref[...] = kk
    obeta_ref[...] = kk * a
    obonus_ref[...] = _seg_dot(r * km * rkp_ref[...], ones_ref) * v


def _scan_kernel(*refs, mode, vh_n, tc):
    if mode == "rwkv":
        q_ref, k_ref, v_ref, dec_ref, kk_ref, beta_ref, s0_ref, o_ref, so_ref, s_scr = refs
    else:
        q_ref, k_ref, v_ref, dec_ref, s0_ref, o_ref, so_ref, s_scr = refs
    ti = pl.program_id(1)

    @pl.when(ti == 0)
    def _():
        s_scr[...] = s0_ref[...]

    def step(t, carry):
        qt = q_ref[t]
        kt = k_ref[t]
        dec = dec_ref[...] if mode == "ret" else dec_ref[t]
        if mode == "rwkv":
            kkt = kk_ref[t]
            bt = beta_ref[t]
        for vh in range(vh_n):
            s = s_scr[vh]
            vrow = v_ref[t, pl.ds(vh, 1), :]
            if mode == "rwkv":
                sk = jnp.sum(s * kkt, axis=0, keepdims=True)
                s = s * dec - sk * bt + vrow * kt
            else:
                s = s * dec + vrow * kt
            s_scr[vh] = s
            o_ref[t, pl.ds(vh, 1), :] = jnp.sum(s * qt, axis=0, keepdims=True)
        return carry

    lax.fori_loop(0, tc, step, 0)

    @pl.when(ti == pl.num_programs(1) - 1)
    def _():
        so_ref[...] = s_scr[...]


def _lin_scan(mode, q, k, v, dec, s0, extra=(), tc=32):
    t_n, k_n, l_n = q.shape
    vh_n = v.shape[1]
    tc = min(tc, t_n)
    assert t_n % tc == 0 and l_n % LANES == 0
    tk_spec = pl.BlockSpec((tc, k_n, LANES), lambda l, t: (t, 0, l))
    tv_spec = pl.BlockSpec((tc, vh_n, LANES), lambda l, t: (t, 0, l))
    s_spec = pl.BlockSpec((vh_n, k_n, LANES), lambda l, t: (0, 0, l))
    dec_spec = pl.BlockSpec((1, LANES), lambda l, t: (0, l)) if mode == "ret" else tk_spec
    in_specs = [tk_spec, tk_spec, tv_spec, dec_spec] + [tk_spec] * len(extra) + [s_spec]
    o, s_out = pl.pallas_call(
        functools.partial(_scan_kernel, mode=mode, vh_n=vh_n, tc=tc),
        grid=(l_n // LANES, t_n // tc),
        in_specs=in_specs,
        out_specs=[tv_spec, s_spec],
        out_shape=[jax.ShapeDtypeStruct((t_n, vh_n, l_n), F32),
                   jax.ShapeDtypeStruct((vh_n, k_n, l_n), F32)],
        scratch_shapes=[pltpu.VMEM((vh_n, k_n, LANES), F32)],
        compiler_params=_cparams("parallel", "arbitrary"),
        name="scan_" + mode,
    )(q, k, v, dec, *extra, s0)
    return o, s_out


def _s5_kernel(u_ref, bb_ref, cc_ref, ar_ref, ai_ref, h0r_ref, h0i_ref,
               y_ref, hr_out, hi_out, hr_scr, hi_scr, xs_scr, hs_scr, *, b_n, tc):
    ti = pl.program_id(0)

    @pl.when(ti == 0)
    def _():
        hr_scr[...] = h0r_ref[...]
        hi_scr[...] = h0i_ref[...]

    xs_scr[...] = _bdot(u_ref[...], bb_ref[...])
    ar = ar_ref[...]
    ai = ai_ref[...]

    def step(t, carry):
        hr, hi = carry
        row = pl.multiple_of(t * b_n, SUBLANES)
        x = xs_scr[pl.ds(row, b_n), :]
        nr = ar * hr - ai * hi + x[:, :S5_CH]
        ni = ar * hi + ai * hr + x[:, S5_CH:]
        hs_scr[pl.ds(row, b_n), :S5_CH] = nr
        hs_scr[pl.ds(row, b_n), S5_CH:] = ni
        return nr, ni

    hr, hi = lax.fori_loop(0, tc, step, (hr_scr[...], hi_scr[...]))
    hr_scr[...] = hr
    hi_scr[...] = hi
    y_ref[...] = _bdot(hs_scr[...], cc_ref[...])

    @pl.when(ti == pl.num_programs(0) - 1)
    def _():
        hr_out[...] = hr
        hi_out[...] = hi


def _s5_scan(u_tb, bb, cc, ar, ai, h0r, h0i, b_n, t_n):
    tc = min(t_n, max(1, 1024 // b_n))
    assert t_n % tc == 0
    rows = tc * b_n
    return pl.pallas_call(
        functools.partial(_s5_kernel, b_n=b_n, tc=tc),
        grid=(t_n // tc,),
        in_specs=[pl.BlockSpec((rows, BRANCH_W), lambda t: (t, 0)),
                  _const_spec(bb.shape), _const_spec(cc.shape),
                  _const_spec(ar.shape), _const_spec(ai.shape),
                  _const_spec(h0r.shape), _const_spec(h0i.shape)],
        out_specs=[pl.BlockSpec((rows, BRANCH_W), lambda t: (t, 0)),
                   _const_spec(h0r.shape), _const_spec(h0i.shape)],
        out_shape=[jax.ShapeDtypeStruct((t_n * b_n, BRANCH_W), F32),
                   jax.ShapeDtypeStruct(h0r.shape, F32),
                   jax.ShapeDtypeStruct(h0i.shape, F32)],
        scratch_shapes=[pltpu.VMEM((b_n, S5_CH), F32), pltpu.VMEM((b_n, S5_CH), F32),
                        pltpu.VMEM((rows, 2 * S5_CH), F32), pltpu.VMEM((rows, 2 * S5_CH), F32)],
        compiler_params=_cparams("arbitrary"),
        name="scan_s5",
    )(u_tb, bb, cc, ar, ai, h0r, h0i)


def _post_kernel(x_ref, ro_ref, rg_ref, go_ref, gg_ref, sy_ref, su_ref, wy_ref, wbon_ref, wg_ref,
                 gate0_ref, gate1_ref, gate2_ref, gate3_ref,
                 avg_ref, rgn_g, rgn_b, ggn_g, s5d_ref, wglu_ref, wgn_g, wgn_b,
                 wbr_ref, wo_ref, ln_g, ln_b, o_ref, *, alpha):
    def seg_mean(v):
        return _seg_dot(v, avg_ref)

    ro = ro_ref[...]
    mu = seg_mean(ro)
    rc = ro - mu
    var = seg_mean(rc * rc)
    b0 = (rc * lax.rsqrt(var + LN_EPS) * rgn_g[...] + rgn_b[...]) * _silu(rg_ref[...])
    go = go_ref[...]
    ms = seg_mean(go * go)
    b1 = go * lax.rsqrt(ms + LN_EPS) * ggn_g[...] * _silu(gg_ref[...])
    y = jax.nn.gelu(sy_ref[...] + s5d_ref[...] * su_ref[...])
    b2 = y * _sigmoid(_bdot(y, wglu_ref[...]))
    wy = wy_ref[...]
    mu = seg_mean(wy)
    wc = wy - mu
    var = seg_mean(wc * wc)
    b3 = (wc * lax.rsqrt(var + RWKV_GN_EPS) * wgn_g[...] + wgn_b[...] + wbon_ref[...]) * wg_ref[...]

    m = None
    gates = (gate0_ref, gate1_ref, gate2_ref, gate3_ref)
    for i, br in enumerate((b0, b1, b2, b3)):
        term = _bdot(br, wbr_ref[i]) * _sigmoid(gates[i][...])
        m = term if m is None else m + term
    h = _bdot(m, wo_ref[...])
    o_ref[...] = _layer_norm(alpha * x_ref[...] + h, ln_g[...], ln_b[...])


def _embed_ln2(x, f, p_ref, wpe_ref, wpg_ref, ln_g, ln_b, alpha):
    e = _bdot(p_ref[...], wpe_ref[...]) * _sigmoid(_bdot(x, wpg_ref[...]))
    return _layer_norm(alpha * x + f + e, ln_g[...], ln_b[...])


def _ffn_kernel(x_ref, p_ref, w1_ref, w3_ref, w2_ref, wpe_ref, wpg_ref, ln_g, ln_b, o_ref, *, alpha):
    x = x_ref[...]
    xb = x.astype(BF16)
    h = _silu(jnp.dot(xb, w1_ref[...], preferred_element_type=F32)) * jnp.dot(
        xb, w3_ref[...], preferred_element_type=F32)
    f = _bdot(h, w2_ref[...])
    o_ref[...] = _embed_ln2(x, f, p_ref, wpe_ref, wpg_ref, ln_g, ln_b, alpha)


def _router_kernel(x_ref, wh_ref, wl_ref, idx_ref, wgt_ref):
    xh, xl = _split_bf16(x_ref[...])
    wh = wh_ref[...]
    logits = (jnp.dot(xh, wh, preferred_element_type=F32)
              + jnp.dot(xl, wh, preferred_element_type=F32)
              + jnp.dot(xh, wl_ref[...], preferred_element_type=F32))
    col = lax.broadcasted_iota(jnp.int32, logits.shape, 1)
    neg = jnp.float32(-jnp.inf)
    lg = jnp.where(col < N_EXPERTS, logits, neg)
    m1 = jnp.max(lg, axis=1, keepdims=True)
    i1 = jnp.min(jnp.where(lg == m1, col, LANES), axis=1, keepdims=True)
    lg2 = jnp.where(col == i1, neg, lg)
    m2 = jnp.max(lg2, axis=1, keepdims=True)
    i2 = jnp.min(jnp.where(lg2 == m2, col, LANES), axis=1, keepdims=True)
    e2 = jnp.exp(m2 - m1)
    den = 1.0 + e2
    idx_ref[...] = jnp.where(col == 0, i1, jnp.where(col == 1, i2, 0))
    wgt_ref[...] = jnp.where(col == 0, 1.0 / den, jnp.where(col == 1, e2 / den, 0.0))


def _moe_block_kernel(be_ref, nb_ref, tok_ref, x_hbm, w1_ref, w3_ref, w2_ref, o_ref, xbuf, sem):
    j = pl.program_id(0)

    @pl.when(j < nb_ref[0])
    def _():
        def row_copy(r):
            return pltpu.make_async_copy(x_hbm.at[pl.ds(tok_ref[0, 0, r], 1)], xbuf.at[pl.ds(r, 1)], sem)

        def start(r, c):
            row_copy(r).start()
            return c

        def wait(r, c):
            row_copy(r).wait()
            return c

        lax.fori_loop(0, MOE_BLOCK, start, 0)
        lax.fori_loop(0, MOE_BLOCK, wait, 0)
        xb = xbuf[...].astype(BF16)
        h = _silu(jnp.dot(xb, w1_ref[0], preferred_element_type=F32)) * jnp.dot(
            xb, w3_ref[0], preferred_element_type=F32)
        o_ref[...] = _bdot(h, w2_ref[0])

    @pl.when(j >= nb_ref[0])
    def _():
        o_ref[...] = jnp.zeros_like(o_ref)


def _moe_blocks(x, block_e, nb_used, slot_tok, w1, w3, w2):
    n_blocks = block_e.shape[0]
    d = x.shape[1]
    dff = w1.shape[2]
    grid_spec = pltpu.PrefetchScalarGridSpec(
        num_scalar_prefetch=2,
        grid=(n_blocks,),
        in_specs=[
            pl.BlockSpec((1, 1, MOE_BLOCK), lambda j, be, nb: (j, 0, 0), memory_space=pltpu.SMEM),
            pl.BlockSpec(memory_space=pl.ANY),
            pl.BlockSpec((1, d, dff), lambda j, be, nb: (be[j], 0, 0), pipeline_mode=pl.Buffered(1)),
            pl.BlockSpec((1, d, dff), lambda j, be, nb: (be[j], 0, 0), pipeline_mode=pl.Buffered(1)),
            pl.BlockSpec((1, dff, d), lambda j, be, nb: (be[j], 0, 0), pipeline_mode=pl.Buffered(1)),
        ],
        out_specs=pl.BlockSpec((MOE_BLOCK, d), lambda j, be, nb: (j, 0)),
        scratch_shapes=[pltpu.VMEM((MOE_BLOCK, d), F32), pltpu.SemaphoreType.DMA(())],
    )
    return pl.pallas_call(
        _moe_block_kernel,
        grid_spec=grid_spec,
        out_shape=jax.ShapeDtypeStruct((n_blocks * MOE_BLOCK, d), F32),
        compiler_params=_cparams("arbitrary"),
        name="moe_blocks",
    )(block_e, nb_used, slot_tok.reshape(n_blocks, 1, MOE_BLOCK), x, w1, w3, w2)


def _moe_combine_kernel(slot_ref, yb_hbm, x_ref, p_ref, wgt_ref, wpe_ref, wpg_ref, ln_g, ln_b,
                        o_ref, gbuf, sem, *, alpha, tm):
    def row_copy(r):
        return pltpu.make_async_copy(yb_hbm.at[pl.ds(slot_ref[0, 0, r], 1)], gbuf.at[pl.ds(r, 1)], sem)

    def start(r, c):
        row_copy(r).start()
        return c

    def wait(r, c):
        row_copy(r).wait()
        return c

    lax.fori_loop(0, TOP_K * tm, start, 0)
    lax.fori_loop(0, TOP_K * tm, wait, 0)
    wgt = wgt_ref[...]
    f = gbuf[0:tm, :] * wgt[:, 0:1] + gbuf[tm:2 * tm, :] * wgt[:, 1:2]
    o_ref[...] = _embed_ln2(x_ref[...], f, p_ref, wpe_ref, wpg_ref, ln_g, ln_b, alpha)


def _moe_combine(slots, yb, x, p, wgt, wpe, wpg, ln_g, ln_b, alpha, tm=ROW_TILE):
    n, d = x.shape
    consts = [wpe, wpg, ln_g, ln_b]
    return pl.pallas_call(
        functools.partial(_moe_combine_kernel, alpha=alpha, tm=tm),
        grid=(n // tm,),
        in_specs=[pl.BlockSpec((1, 1, TOP_K * tm), lambda i: (i, 0, 0), memory_space=pltpu.SMEM),
                  pl.BlockSpec(memory_space=pl.ANY),
                  _row_spec(tm, d, 0), _row_spec(tm, p.shape[1], 0), _row_spec(tm, LANES, 0)]
                 + [_const_spec(a.shape) for a in consts],
        out_specs=_row_spec(tm, d, 0),
        out_shape=jax.ShapeDtypeStruct((n, d), F32),
        scratch_shapes=[pltpu.VMEM((TOP_K * tm, d), F32), pltpu.SemaphoreType.DMA(())],
        compiler_params=_cparams("arbitrary"),
        name="moe_combine",
    )(slots, yb, x, p, wgt, *consts)


def _lane_layout(b_n):
    bh = b_n * HEADS
    if bh >= LANES:
        assert bh % LANES == 0
        return 1
    assert LANES % bh == 0
    return LANES // bh


def _to_k_lanes(x, b_n, t_n, dk, v_lo):
    x = x.reshape(b_n, t_n, HEADS, dk).transpose(1, 3, 0, 2).reshape(t_n, dk, b_n * HEADS)
    return jnp.tile(x, (1, 1, v_lo)) if v_lo > 1 else x


def _to_v_lanes(x, b_n, t_n, v_lo):
    x = x.reshape(b_n, t_n, HEADS, HEAD_D // v_lo, v_lo).transpose(1, 3, 4, 0, 2)
    return x.reshape(t_n, HEAD_D // v_lo, v_lo * b_n * HEADS)


def _from_v_lanes(o, b_n, t_n, v_lo):
    o = o.reshape(t_n, HEAD_D // v_lo, v_lo, b_n, HEADS).transpose(3, 0, 4, 1, 2)
    return o.reshape(b_n * t_n, HEADS * HEAD_D)


def _state_to_lanes(s, v_lo, value_last):
    b_n = s.shape[0]
    if value_last:
        s = s.transpose(3, 2, 0, 1)
    else:
        s = s.transpose(2, 3, 0, 1)
    v_n, k_n = s.shape[0], s.shape[1]
    s = s.reshape(v_n // v_lo, v_lo, k_n, b_n * HEADS).transpose(0, 2, 1, 3)
    return s.reshape(v_n // v_lo, k_n, v_lo * b_n * HEADS)


def _state_from_lanes(s, b_n, v_lo, value_last):
    vh_n, k_n, _ = s.shape
    s = s.reshape(vh_n, k_n, v_lo, b_n, HEADS).transpose(3, 4, 1, 0, 2).reshape(b_n, HEADS, k_n, vh_n * v_lo)
    return s if value_last else s.transpose(0, 1, 3, 2)


def _rotary_tables(pos):
    half = HEAD_D // 2
    freq = ROPE_BASE ** (-jnp.arange(half, dtype=F32) / half)
    ang = pos.astype(F32)[:, None] * freq[None, :]
    cos, sin = jnp.cos(ang), jnp.sin(ang)
    cos_h = jnp.concatenate([cos, cos], axis=-1)
    sin_h = jnp.concatenate([-sin, sin], axis=-1)
    return jnp.tile(cos_h, (1, HEADS)), jnp.tile(sin_h, (1, HEADS))


def _block_diag_const(block, n_blocks):
    return jnp.kron(jnp.eye(n_blocks, dtype=F32), jnp.full((block, block), 1.0, F32))


def _s5_params(log_dt, a_re, a_im, b_re, b_im, c_re, c_im):
    dt = jnp.exp(log_dt)[:, None]
    mag = jnp.exp(dt * a_re)
    ang = dt * a_im
    abar_re, abar_im = mag * jnp.cos(ang), mag * jnp.sin(ang)
    den = a_re * a_re + a_im * a_im
    n_re = abar_re - 1.0
    f_re = (n_re * a_re + abar_im * a_im) / den
    f_im = (abar_im * a_re - n_re * a_im) / den
    bb_re = f_re[..., None] * b_re - f_im[..., None] * b_im
    bb_im = f_re[..., None] * b_im + f_im[..., None] * b_re
    eye = jnp.eye(S5_GROUPS, dtype=F32)

    def in_map(bb):
        return jnp.einsum("gpc,gh->gchp", bb, eye).reshape(BRANCH_W, S5_CH)

    def out_map(cm):
        return jnp.einsum("gcp,gh->gphc", cm, eye).reshape(S5_CH, BRANCH_W)

    bb = jnp.concatenate([in_map(bb_re), in_map(bb_im)], axis=1).astype(BF16)
    cc = jnp.concatenate([out_map(c_re), -out_map(c_im)], axis=0).astype(BF16)
    return bb, cc, abar_re.reshape(1, S5_CH), abar_im.reshape(1, S5_CH)


def _reorder_w_in(w_in):
    d_model = w_in.shape[0]
    main = jnp.concatenate([w_in[:, :1792], w_in[:, 1808:]], axis=1)
    tail = jnp.concatenate([w_in[:, 1792:1808],
                            jnp.zeros((d_model, D_IN_PAD - COL_GLA_R - GLA_GATE_RANK), w_in.dtype)], axis=1)
    return jnp.concatenate([main, tail], axis=1).astype(BF16)


def _pad_cols(w, n):
    return jnp.pad(w, ((0, 0), (0, n - w.shape[1])))


def _pad_rows(w, n):
    return jnp.pad(w, ((0, n - w.shape[0]), (0, 0)))


def _mixer_scans(cols, prep, groups, states, s5p):
    oq, ok, ogq, oal, orr, okm, ov, odec, okk, obeta = prep
    bb, cc, ar, ai = s5p
    outs = {"ret": [], "gla": [], "s5": [], "rwkv": []}
    new_states = []
    for (off, b_n, t_n), st in zip(groups, states):
        st_ret, st_gla, st_s5re, st_s5im, st_rwkv = st
        rows = slice(off, off + b_n * t_n)
        v_lo = _lane_layout(b_n)
        kl = functools.partial(_to_k_lanes, b_n=b_n, t_n=t_n, v_lo=v_lo)
        vl = functools.partial(_to_v_lanes, b_n=b_n, t_n=t_n, v_lo=v_lo)
        lanes = v_lo * b_n * HEADS
        gamma = 1.0 - jnp.exp2(-5.0 - jnp.arange(HEADS, dtype=F32))
        dec = jnp.tile(gamma, (lanes // HEADS,)).reshape(1, lanes)
        o, s = _lin_scan("ret", kl(oq[rows], dk=HEAD_D), kl(ok[rows], dk=HEAD_D),
                         vl(cols[rows, COL_RET + 512:COL_RET + 768]), dec,
                         _state_to_lanes(st_ret, v_lo, True))
        outs["ret"].append(_from_v_lanes(o, b_n, t_n, v_lo))
        ret_new = _state_from_lanes(s, b_n, v_lo, True)
        o, s = _lin_scan("gla", kl(ogq[rows], dk=GLA_DK), kl(cols[rows, COL_GLA_K:COL_GLA_K + GLA_QK], dk=GLA_DK),
                         vl(cols[rows, COL_GLA_V:COL_GLA_V + BRANCH_W]), kl(oal[rows], dk=GLA_DK),
                         _state_to_lanes(st_gla, v_lo, True))
        outs["gla"].append(_from_v_lanes(o, b_n, t_n, v_lo))
        gla_new = _state_from_lanes(s, b_n, v_lo, True)
        o, s = _lin_scan("rwkv", kl(orr[rows], dk=HEAD_D), kl(okm[rows], dk=HEAD_D), vl(ov[rows]),
                         kl(odec[rows], dk=HEAD_D), _state_to_lanes(st_rwkv, v_lo, False),
                         extra=(kl(okk[rows], dk=HEAD_D), kl(obeta[rows], dk=HEAD_D)))
        outs["rwkv"].append(_from_v_lanes(o, b_n, t_n, v_lo))
        rwkv_new = _state_from_lanes(s, b_n, v_lo, False)
        u = cols[rows, COL_S5:COL_S5 + BRANCH_W]
        u_tb = u.reshape(b_n, t_n, BRANCH_W).transpose(1, 0, 2).reshape(t_n * b_n, BRANCH_W)
        y, hr, hi = _s5_scan(u_tb, bb, cc, ar, ai, st_s5re.reshape(b_n, S5_CH), st_s5im.reshape(b_n, S5_CH),
                             b_n, t_n)
        outs["s5"].append(y.reshape(t_n, b_n, BRANCH_W).transpose(1, 0, 2).reshape(b_n * t_n, BRANCH_W))
        new_states.append((ret_new, gla_new, hr.reshape(b_n, S5_GROUPS, S5_STATE),
                           hi.reshape(b_n, S5_GROUPS, S5_STATE), rwkv_new))
    cat = {name: jnp.concatenate(parts, axis=0) for name, parts in outs.items()}
    return cat, new_states


def _moe_route(idx, n):
    nk = n * TOP_K
    flat_e = idx.reshape(nk)
    onehot = (flat_e[:, None] == jnp.arange(N_EXPERTS, dtype=jnp.int32)[None, :]).astype(jnp.int32)
    incl = jnp.cumsum(onehot, axis=0)
    counts = incl[-1]
    rank = jnp.sum((incl - onehot) * onehot, axis=1)
    padded = (counts + MOE_BLOCK - 1) // MOE_BLOCK * MOE_BLOCK
    pad_end = jnp.cumsum(padded)
    slot = (pad_end - padded)[flat_e] + rank
    n_blocks = -(-(nk + N_EXPERTS * (MOE_BLOCK - 1)) // MOE_BLOCK)
    cap = n_blocks * MOE_BLOCK
    flat_tok = jnp.arange(nk, dtype=jnp.int32) // TOP_K
    slot_tok = jnp.zeros((cap,), jnp.int32).at[slot].set(flat_tok)
    block_e = jnp.minimum(jnp.searchsorted(pad_end, jnp.arange(n_blocks, dtype=jnp.int32) * MOE_BLOCK,
                                           side="right"), N_EXPERTS - 1).astype(jnp.int32)
    nb_used = (pad_end[-1] // MOE_BLOCK).astype(jnp.int32).reshape(1)
    return slot.astype(jnp.int32).reshape(n, TOP_K), slot_tok, block_e, nb_used


def kernel(x_prompt, x_sample, state_ret, state_gla, state_s5_re, state_s5_im, state_rwkv, state_shift,
           p_prompt, p_sample, w_in, ret_gn_g, ret_gn_b, gla_wg2, gla_bg, gla_gn,
           s5_log_dt, s5_a_re, s5_a_im, s5_b_re, s5_b_im, s5_c_re, s5_c_im, s5_d, s5_w_glu,
           rwkv_mu, rwkv_w0, rwkv_w1, rwkv_w2, rwkv_a0, rwkv_a1, rwkv_a2, rwkv_g1, rwkv_g2,
           rwkv_kk, rwkv_ka, rwkv_rk, rwkv_gn_g, rwkv_gn_b, w_branch, w_o,
           ln1_g, ln1_b, ln2_g, ln2_b, w_pe, w_pg, ffn_w1, ffn_w3, ffn_w2,
           moe_router, moe_w1, moe_w3, moe_w2):
    depth = w_in.shape[0]
    bp, tp, d_model = x_prompt.shape
    bs, ts, _ = x_sample.shape
    n_p, n_s = bp * tp, bs * ts
    n = n_p + n_s
    alpha = (2 * depth) ** 0.25
    groups = [(0, bp, tp), (n_p, bs, ts)]

    x = jnp.concatenate([x_prompt.reshape(n_p, d_model), x_sample.reshape(n_s, d_model)], axis=0)
    pos_p = jnp.arange(tp, dtype=jnp.int32)
    pos_s = 16384 + jnp.arange(ts, dtype=jnp.int32)
    cos_p, sin_p = _rotary_tables(pos_p)
    cos_s, sin_s = _rotary_tables(pos_s)
    cos_t = jnp.concatenate([jnp.tile(cos_p, (bp, 1)), jnp.tile(cos_s, (bs, 1))], axis=0)
    sin_t = jnp.concatenate([jnp.tile(sin_p, (bp, 1)), jnp.tile(sin_s, (bs, 1))], axis=0)
    ones_bd = _block_diag_const(HEAD_D, HEADS).astype(BF16)
    avg_bd = (_block_diag_const(HEAD_D, HEADS) / HEAD_D).astype(BF16)
    row = lambda v: v.reshape(1, -1)

    new = [[] for _ in range(6)]
    for i in range(depth):
        w_in_r = _reorder_w_in(w_in[i])
        cols = _matmul(x, w_in_r, tm=512, tn=D_IN_PAD // 3, name="in_proj")

        rwkv_c = cols[:, COL_RWKV:COL_RWKV + 4 * BRANCH_W]
        prev_p = jnp.concatenate([jnp.zeros((bp, 1, 4 * BRANCH_W), F32),
                                  rwkv_c[:n_p].reshape(bp, tp, -1)[:, :-1]], axis=1).reshape(n_p, -1)
        prev_s = jnp.concatenate([state_shift[i][:, None],
                                  rwkv_c[n_p:].reshape(bs, ts, -1)[:, :-1]], axis=1).reshape(n_s, -1)
        prev = jnp.concatenate([prev_p, prev_s], axis=0)

        prep_consts = [
            _pad_rows(_pad_cols(gla_wg2[i], LANES), LANES).astype(BF16), row(gla_bg[i]),
        ]
        rw_consts = [
            rwkv_mu[i], row(rwkv_w0[i]),
            _pad_cols(rwkv_w1[i], LANES).astype(BF16), _pad_rows(rwkv_w2[i], LANES).astype(BF16),
            row(rwkv_a0[i]),
            _pad_cols(rwkv_a1[i], LANES).astype(BF16), _pad_rows(rwkv_a2[i], LANES).astype(BF16),
            _pad_cols(rwkv_g1[i], LANES).astype(BF16), _pad_rows(rwkv_g2[i], LANES).astype(BF16),
            row(rwkv_kk[i]), row(rwkv_ka[i]), row(rwkv_rk[i]), ones_bd,
        ]
        prep = pl.pallas_call(
            _prep_kernel,
            grid=(n // ROW_TILE,),
            in_specs=[_row_spec(ROW_TILE, BRANCH_W, 0), _row_spec(ROW_TILE, BRANCH_W, 1),
                      _row_spec(ROW_TILE, BRANCH_W, 0), _row_spec(ROW_TILE, BRANCH_W, 0),
                      _row_spec(ROW_TILE, GLA_QK, COL_GLA_Q // GLA_QK),
                      _row_spec(ROW_TILE, LANES, COL_GLA_R // LANES)]
                     + [_const_spec(a.shape) for a in prep_consts]
                     + [_row_spec(ROW_TILE, 4 * BRANCH_W, COL_RWKV // (4 * BRANCH_W)),
                        _row_spec(ROW_TILE, 4 * BRANCH_W, 0)]
                     + [_const_spec(a.shape) for a in rw_consts],
            out_specs=[_row_spec(ROW_TILE, w, 0) for w in
                       [BRANCH_W, BRANCH_W, GLA_QK, GLA_QK] + [BRANCH_W] * 8],
            out_shape=[jax.ShapeDtypeStruct((n, w), F32) for w in
                       [BRANCH_W, BRANCH_W, GLA_QK, GLA_QK] + [BRANCH_W] * 8],
            compiler_params=_cparams("parallel"),
            name="mixer_prep",
        )(cols, cols, cos_t, sin_t, cols, cols, *prep_consts, cols, prev, *rw_consts)
        oq, ok, ogq, oal, orr, okm, ov, odec, okk, obeta, obonus, og = prep

        s5p = _s5_params(s5_log_dt[i], s5_a_re[i], s5_a_im[i], s5_b_re[i], s5_b_im[i], s5_c_re[i], s5_c_im[i])
        zeros_like_state = lambda s: jnp.zeros((bp,) + s.shape[2:], s.dtype)
        states = [
            tuple(zeros_like_state(s) for s in (state_ret, state_gla, state_s5_re, state_s5_im, state_rwkv)),
            (state_ret[i], state_gla[i], state_s5_re[i], state_s5_im[i], state_rwkv[i]),
        ]
        mix, new_states = _mixer_scans(cols, (oq, ok, ogq, oal, orr, okm, ov, odec, okk, obeta),
                                       groups, states, s5p)
        shift_p = rwkv_c[:n_p].reshape(bp, tp, -1)[:, -1]
        shift_s = rwkv_c[n_p:].reshape(bs, ts, -1)[:, -1]
        for lst, pair in zip(new, list(zip(*new_states)) + [(shift_p, shift_s)]):
            lst.append(pair)

        post_consts = [avg_bd, row(ret_gn_g[i]), row(ret_gn_b[i]), row(gla_gn[i]), row(s5_d[i]),
                       s5_w_glu[i].astype(BF16), row(rwkv_gn_g[i]), row(rwkv_gn_b[i]),
                       w_branch[i].astype(BF16), w_o[i].astype(BF16), row(ln1_g[i]), row(ln1_b[i])]
        (x1,) = _rowwise(
            functools.partial(_post_kernel, alpha=alpha), n,
            [(x, d_model, 0), (mix["ret"], BRANCH_W, 0), (cols, BRANCH_W, 3),
             (mix["gla"], BRANCH_W, 0), (cols, BRANCH_W, COL_GLA_G // BRANCH_W),
             (mix["s5"], BRANCH_W, 0), (cols, BRANCH_W, COL_S5 // BRANCH_W),
             (mix["rwkv"], BRANCH_W, 0), (obonus, BRANCH_W, 0), (og, BRANCH_W, 0),
             ] + [(cols, d_model, COL_GATE // d_model + gi) for gi in range(N_BRANCH)],
            post_consts, [d_model], name="mixer_post")

        p = jnp.concatenate([p_prompt[i].reshape(n_p, -1), p_sample[i].reshape(n_s, -1)], axis=0)
        tail_consts = [w_pe[i].astype(BF16), w_pg[i].astype(BF16), row(ln2_g[i]), row(ln2_b[i])]
        if i % 2 == 0:
            j = i // 2
            (x,) = _rowwise(
                functools.partial(_ffn_kernel, alpha=alpha), n,
                [(x1, d_model, 0), (p, p.shape[1], 0)],
                [ffn_w1[j].astype(BF16), ffn_w3[j].astype(BF16), ffn_w2[j].astype(BF16)] + tail_consts,
                [d_model], name="ffn")
        else:
            j = i // 2
            rh, rl = _split_bf16(_pad_cols(moe_router[j], LANES))
            idx, wgt = _rowwise(_router_kernel, n, [(x1, d_model, 0)], [rh, rl], [LANES, LANES],
                                out_dtypes=[jnp.int32, F32], name="moe_router")
            slots, slot_tok, block_e, nb_used = _moe_route(idx[:, :TOP_K], n)
            yb = _moe_blocks(x1, block_e, nb_used, slot_tok,
                             moe_w1[j].astype(BF16), moe_w3[j].astype(BF16), moe_w2[j].astype(BF16))
            slots_t = slots.reshape(n // ROW_TILE, ROW_TILE, TOP_K).transpose(0, 2, 1).reshape(
                n // ROW_TILE, 1, TOP_K * ROW_TILE)
            x = _moe_combine(slots_t, yb, x1, p, wgt, *tail_consts, alpha)

    y_prompt = x[:n_p].reshape(bp, tp, d_model)
    y_sample = x[n_p:].reshape(bs, ts, d_model)
    outs = [y_prompt, y_sample]
    for lst in new:
        outs.append(jnp.stack([pair[0] for pair in lst], 0))
        outs.append(jnp.stack([pair[1] for pair in lst], 0))
    return tuple(outs)
```

```python
import functools
import math

import jax
import jax.numpy as jnp
from jax import lax
from jax.experimental import pallas as pl
from jax.experimental.pallas import tpu as pltpu

F32 = jnp.float32
BF16 = jnp.bfloat16

LANES = 128
SUBLANES = 8
VMEM_LIMIT = 56 * 1024 * 1024

N_BRANCH = 4
BRANCH_W = 256
HEADS = 4
HEAD_D = 64
GLA_DK = 32
GLA_QK = HEADS * GLA_DK
GLA_GATE_RANK = 16
GLA_GATE_NORM = 16.0
S5_GROUP = 16
S5_GROUPS = 16
S5_STATE = 64
S5_CH = S5_GROUPS * S5_STATE
ROPE_BASE = 10000.0
RWKV_GN_EPS = 64e-5
LN_EPS = 1e-5
N_EXPERTS = 8
TOP_K = 2
MOE_BLOCK = 256
ROW_TILE = 256
PAST_LEN = 16384

COL_RET = 0
COL_GLA_Q = 1024
COL_GLA_K = 1152
COL_GLA_V = 1280
COL_GLA_G = 1536
COL_S5 = 1792
COL_RWKV = 2048
COL_GATE = 3072
COL_GLA_R = 7168
D_IN_PAD = 7296


def _cparams(*sem):
    return pltpu.CompilerParams(dimension_semantics=sem, vmem_limit_bytes=VMEM_LIMIT)


def _split_bf16(x):
    hi = x.astype(BF16)
    lo = (x - hi.astype(F32)).astype(BF16)
    return hi, lo


def _seg_dot(x, m_ref):
    hi, lo = _split_bf16(x)
    m = m_ref[...]
    return (jnp.dot(hi, m, preferred_element_type=F32)
            + jnp.dot(lo, m, preferred_element_type=F32))


def _bdot(x, w):
    return jnp.dot(x.astype(BF16), w, preferred_element_type=F32)


def _sigmoid(x):
    return 1.0 / (1.0 + jnp.exp(-x))


def _silu(x):
    return x * _sigmoid(x)


def _log1p_exp_neg_abs(x):
    return jnp.log1p(jnp.exp(-jnp.abs(x)))


def _layer_norm(x, g, b):
    mu = jnp.mean(x, axis=-1, keepdims=True)
    xc = x - mu
    var = jnp.mean(xc * xc, axis=-1, keepdims=True)
    return xc * lax.rsqrt(var + LN_EPS) * g + b


def _row_spec(tm, width, col_block):
    return pl.BlockSpec((tm, width), lambda i, cb=col_block: (i, cb))


def _const_spec(shape):
    nd = len(shape)
    return pl.BlockSpec(shape, lambda i, nd=nd: (0,) * nd)


def _rowwise(body, n_rows, row_in, const_in, out_widths, out_dtypes=None, tm=ROW_TILE, name=None):
    assert n_rows % tm == 0
    out_dtypes = out_dtypes or [F32] * len(out_widths)
    in_specs = [_row_spec(tm, w, cb) for _, w, cb in row_in] + [_const_spec(a.shape) for a in const_in]
    out_specs = [_row_spec(tm, w, 0) for w in out_widths]
    out_shape = [jax.ShapeDtypeStruct((n_rows, w), dt) for w, dt in zip(out_widths, out_dtypes)]
    return pl.pallas_call(
        body,
        grid=(n_rows // tm,),
        in_specs=in_specs,
        out_specs=out_specs,
        out_shape=out_shape,
        compiler_params=_cparams("parallel"),
        name=name,
    )(*[a for a, _, _ in row_in], *const_in)


def _matmul_kernel(x_ref, w_ref, o_ref):
    o_ref[...] = _bdot(x_ref[...], w_ref[...]).astype(o_ref.dtype)


def _matmul(x, w, tm=512, tn=None, out_dtype=F32, name=None):
    m, k = x.shape
    n = w.shape[1]
    tn = tn or n
    tm = tm if m % tm == 0 else ROW_TILE
    assert m % tm == 0 and n % tn == 0
    return pl.pallas_call(
        _matmul_kernel,
        grid=(n // tn, m // tm),
        in_specs=[pl.BlockSpec((tm, k), lambda j, i: (i, 0)),
                  pl.BlockSpec((k, tn), lambda j, i: (0, j))],
        out_specs=pl.BlockSpec((tm, tn), lambda j, i: (i, j)),
        out_shape=jax.ShapeDtypeStruct((m, n), out_dtype),
        compiler_params=_cparams("parallel", "parallel"),
        name=name,
    )(x, w)


def _prep_kernel(rq_ref, rk_ref, rv_ref, cos_ref, sin_ref,
                 gq_ref, gk_ref, gv_ref, gr_ref,
                 c_ref, tail_p_ref, tail_s_ref,
                 shift_p_ref, shift_s_ref, wg2_ref, bg_ref,
                 mu_ref, w0_ref, w1_ref, w2_ref, a0_ref, a1_ref, a2_ref, g1_ref, g2_ref,
                 kkp_ref, kap_ref, rkp_ref, ones_ref,
                 kr_ref, kg_ref, kw_ref, va_ref, obonus_ref, og_ref, *, n_p_tiles):
    i = pl.program_id(0)
    w = BRANCH_W
    cos = cos_ref[...]
    sin = sin_ref[...]
    lane = lax.broadcasted_iota(jnp.int32, cos.shape, 1)
    first_half = (lane % HEAD_D) < (HEAD_D // 2)

    def rot(x):
        partner = jnp.where(first_half,
                            pltpu.roll(x, BRANCH_W - HEAD_D // 2, 1),
                            pltpu.roll(x, HEAD_D // 2, 1))
        return x * cos + partner * sin

    kr_ref[:, 0:w] = rot(rq_ref[...])
    kr_ref[:, w:2 * w] = rot(rk_ref[...]) * (HEAD_D ** -0.5)
    va_ref[:, 0:w] = rv_ref[...]

    z = _bdot(gr_ref[...], wg2_ref[...]) + bg_ref[...]
    glog = (jnp.minimum(z, 0.0) - _log1p_exp_neg_abs(z)) / GLA_GATE_NORM
    kg_ref[:, 0:GLA_QK] = gq_ref[...] * (GLA_DK ** -0.5)
    kg_ref[:, GLA_QK:2 * GLA_QK] = gk_ref[...]
    kg_ref[:, 2 * GLA_QK:3 * GLA_QK] = jnp.exp(glog)
    va_ref[:, w:2 * w] = gv_ref[...]

    c = c_ref[...]
    tm = c.shape[0]
    bp = tail_p_ref.shape[0]
    bs = tail_s_ref.shape[0]
    tail_p = jnp.where(i == 0, shift_p_ref[...], tail_p_ref[...])
    tail_s = jnp.where(i == n_p_tiles, shift_s_ref[...], tail_s_ref[...])
    prev_p = jnp.concatenate([tail_p, c[:tm - bp]], axis=0)
    prev_s = jnp.concatenate([tail_s, c[:tm - bs]], axis=0)
    d = jnp.where(i < n_p_tiles, prev_p, prev_s) - c
    mu = mu_ref[...]
    cr, ck, cv, cz = (c[:, j * w:(j + 1) * w] for j in range(4))
    dr, dk, dv, dz = (d[:, j * w:(j + 1) * w] for j in range(4))
    r = cr + dr * mu[0:1]
    k = ck + dk * mu[1:2]
    v = cv + dv * mu[2:3]
    zw = cz + dz * mu[3:4]
    za = cz + dz * mu[4:5]
    zg = cz + dz * mu[5:6]
    w_raw = w0_ref[...] + _bdot(jnp.tanh(_bdot(zw, w1_ref[...])), w2_ref[...])
    sp = jnp.maximum(-w_raw, 0.0) + _log1p_exp_neg_abs(w_raw)
    logw = -jnp.exp(-sp - 0.5)
    a = _sigmoid(a0_ref[...] + _bdot(_bdot(za, a1_ref[...]), a2_ref[...]))
    og_ref[...] = _bdot(_sigmoid(_bdot(zg, g1_ref[...])), g2_ref[...])
    kk = k * kkp_ref[...]
    ss = _seg_dot(kk * kk, ones_ref)
    kk = kk * lax.rsqrt(jnp.maximum(ss, 1e-24))
    km = k * (1.0 + (a - 1.0) * kap_ref[...])
    kw_ref[:, 0:w] = r
    kw_ref[:, w:2 * w] = km
    kw_ref[:, 2 * w:3 * w] = jnp.exp(logw)
    kw_ref[:, 3 * w:4 * w] = kk
    kw_ref[:, 4 * w:5 * w] = kk * a
    va_ref[:, 2 * w:3 * w] = v
    obonus_ref[...] = _seg_dot(r * km * rkp_ref[...], ones_ref) * v


def _expand_k(x, kl_n):
    if kl_n == 1:
        return x
    grp = lax.broadcasted_iota(jnp.int32, x.shape, 1) // (LANES // kl_n)
    parts = [(x, kl_n)]
    span = kl_n
    while span > 1:
        half = span // 2
        shift = half * (LANES // kl_n)
        nxt = []
        for z, _ in parts:
            rz = pltpu.roll(z, shift, 1)
            low = (grp % span) < half
            nxt.append((jnp.where(low, z, rz), half))
            nxt.append((jnp.where(low, rz, z), half))
        parts = nxt
        span = half
    return jnp.concatenate([z for z, _ in parts], axis=0)


def _scan3_kernel(kr_ref, kg_ref, kw_ref, va_ref, dec_ref, s0r_ref, s0g_ref, s0w_ref,
                  o_ref, sor_ref, sog_ref, sow_ref, sr, sg, sw, ex, *, vh_n, kl_n, tc):
    ti = pl.program_id(1)

    @pl.when(ti == 0)
    def _():
        sr[...] = s0r_ref[...]
        sg[...] = s0g_ref[...]
        sw[...] = s0w_ref[...]

    dec_r = dec_ref[...]

    def out(t, j, vh, s, q):
        o_ref[t, j, pl.ds(vh, 1), :] = jnp.sum(s * q, axis=0, keepdims=True)

    def ret_step(t, op):
        for vh in range(vh_n):
            s = sr[vh] * dec_r + va_ref[t, 0, pl.ds(vh, 1), :] * op(1)
            sr[vh] = s
            out(t, 0, vh, s, op(0))

    def gla_step(t, op):
        for vh in range(vh_n):
            s = sg[vh] * op(2) + va_ref[t, 1, pl.ds(vh, 1), :] * op(1)
            sg[vh] = s
            out(t, 1, vh, s, op(0))

    def rwkv_step(t, op):
        for vh in range(vh_n):
            s = sw[vh]
            sk = jnp.sum(s * op(3), axis=0, keepdims=True)
            s = s * op(2) - sk * op(4) + va_ref[t, 2, pl.ds(vh, 1), :] * op(1)
            sw[vh] = s
            out(t, 2, vh, s, op(0))

    def run(ref, n_ops, step):
        if kl_n == 1:
            def direct(t, c):
                vals = [ref[t, j] for j in range(n_ops)]
                step(t, lambda j: vals[j])
                return c
            lax.fori_loop(0, tc, direct, 0)
            return
        k_n = ref.shape[2] * kl_n

        def stage(t, par):
            for j in range(n_ops):
                ex[par, j, 0:k_n, :] = _expand_k(ref[t, j], kl_n)

        def staged(par):
            return lambda j: ex[par, j, 0:k_n, :]

        stage(0, 0)

        def pair(i, c):
            t = 2 * i
            stage(t + 1, 1)
            step(t, staged(0))
            stage(jnp.minimum(t + 2, tc - 1), 0)
            step(t + 1, staged(1))
            return c

        lax.fori_loop(0, tc // 2, pair, 0)

    run(kr_ref, 2, ret_step)
    run(kg_ref, 3, gla_step)
    run(kw_ref, 5, rwkv_step)

    @pl.when(ti == pl.num_programs(1) - 1)
    def _():
        sor_ref[...] = sr[...]
        sog_ref[...] = sg[...]
        sow_ref[...] = sw[...]


def _scan3(kr, kg, kw, va, dec, s0r, s0g, s0w, kl_n, tc=32):
    t_n, _, _, l_n = kr.shape
    vh_n = va.shape[2]
    tc = min(tc, t_n)
    assert t_n % tc == 0 and l_n % LANES == 0

    def t_spec(a):
        return pl.BlockSpec((tc,) + a.shape[1:3] + (LANES,), lambda l, t: (t, 0, 0, l))

    def s_spec(a):
        return pl.BlockSpec(a.shape[:2] + (LANES,), lambda l, t: (0, 0, l))

    o_shape = jax.ShapeDtypeStruct((t_n, 3, vh_n, l_n), F32)
    return pl.pallas_call(
        functools.partial(_scan3_kernel, vh_n=vh_n, kl_n=kl_n, tc=tc),
        grid=(l_n // LANES, t_n // tc),
        in_specs=[t_spec(kr), t_spec(kg), t_spec(kw), t_spec(va),
                  pl.BlockSpec((1, LANES), lambda l, t: (0, l)),
                  s_spec(s0r), s_spec(s0g), s_spec(s0w)],
        out_specs=[t_spec(o_shape), s_spec(s0r), s_spec(s0g), s_spec(s0w)],
        out_shape=[o_shape] + [jax.ShapeDtypeStruct(s.shape, F32) for s in (s0r, s0g, s0w)],
        scratch_shapes=[pltpu.VMEM(s.shape[:2] + (LANES,), F32) for s in (s0r, s0g, s0w)]
                       + [pltpu.VMEM((2, kw.shape[1], HEAD_D, LANES), F32)],
        compiler_params=_cparams("parallel", "arbitrary"),
        name="scan_ret_gla_rwkv",
    )(kr, kg, kw, va, dec, s0r, s0g, s0w)


def _s5_kernel(u_ref, bb_ref, cc_ref, ar_ref, ai_ref, h0r_ref, h0i_ref,
               y_ref, hr_out, hi_out, hr_scr, hi_scr, xs_scr, hs_scr, *, b_n, tc):
    ti = pl.program_id(0)

    @pl.when(ti == 0)
    def _():
        hr_scr[...] = h0r_ref[...]
        hi_scr[...] = h0i_ref[...]

    xs_scr[...] = _bdot(u_ref[...], bb_ref[...])
    ar = ar_ref[...]
    ai = ai_ref[...]

    def step(t, carry):
        hr, hi = carry
        row = pl.multiple_of(t * b_n, SUBLANES)
        x = xs_scr[pl.ds(row, b_n), :]
        nr = ar * hr - ai * hi + x[:, :S5_CH]
        ni = ar * hi + ai * hr + x[:, S5_CH:]
        hs_scr[pl.ds(row, b_n), :S5_CH] = nr
        hs_scr[pl.ds(row, b_n), S5_CH:] = ni
        return nr, ni

    hr, hi = lax.fori_loop(0, tc, step, (hr_scr[...], hi_scr[...]))
    hr_scr[...] = hr
    hi_scr[...] = hi
    y_ref[...] = _bdot(hs_scr[...], cc_ref[...])

    @pl.when(ti == pl.num_programs(0) - 1)
    def _():
        hr_out[...] = hr
        hi_out[...] = hi


def _s5_scan(cols, row_off, bb, cc, ar, ai, h0r, h0i, b_n, t_n):
    tc = min(t_n, max(1, 1024 // b_n))
    rows = tc * b_n
    assert t_n % tc == 0 and row_off % rows == 0
    blk0 = row_off // rows
    return pl.pallas_call(
        functools.partial(_s5_kernel, b_n=b_n, tc=tc),
        grid=(t_n // tc,),
        in_specs=[pl.BlockSpec((rows, BRANCH_W), lambda t: (blk0 + t, COL_S5 // BRANCH_W)),
                  _const_spec(bb.shape), _const_spec(cc.shape),
                  _const_spec(ar.shape), _const_spec(ai.shape),
                  _const_spec(h0r.shape), _const_spec(h0i.shape)],
        out_specs=[pl.BlockSpec((rows, BRANCH_W), lambda t: (t, 0)),
                   _const_spec(h0r.shape), _const_spec(h0i.shape)],
        out_shape=[jax.ShapeDtypeStruct((t_n * b_n, BRANCH_W), F32),
                   jax.ShapeDtypeStruct(h0r.shape, F32),
                   jax.ShapeDtypeStruct(h0i.shape, F32)],
        scratch_shapes=[pltpu.VMEM((b_n, S5_CH), F32), pltpu.VMEM((b_n, S5_CH), F32),
                        pltpu.VMEM((rows, 2 * S5_CH), F32), pltpu.VMEM((rows, 2 * S5_CH), F32)],
        compiler_params=_cparams("arbitrary"),
        name="scan_s5",
    )(cols, bb, cc, ar, ai, h0r, h0i)


def _post_kernel(x_ref, ro_ref, rg_ref, go_ref, gg_ref, sy_ref, su_ref, wy_ref, wbon_ref, wg_ref,
                 gate0_ref, gate1_ref, gate2_ref, gate3_ref,
                 avg_ref, rgn_g, rgn_b, ggn_g, s5d_ref, wglu_ref, wgn_g, wgn_b,
                 wbr_ref, wo_ref, ln_g, ln_b, o_ref, *, alpha):
    def seg_mean(v):
        return _seg_dot(v, avg_ref)

    ro = ro_ref[...]
    mu = seg_mean(ro)
    rc = ro - mu
    var = seg_mean(rc * rc)
    b0 = (rc * lax.rsqrt(var + LN_EPS) * rgn_g[...] + rgn_b[...]) * _silu(rg_ref[...])
    go = go_ref[...]
    ms = seg_mean(go * go)
    b1 = go * lax.rsqrt(ms + LN_EPS) * ggn_g[...] * _silu(gg_ref[...])
    y = jax.nn.gelu(sy_ref[...] + s5d_ref[...] * su_ref[...])
    b2 = y * _sigmoid(_bdot(y, wglu_ref[...]))
    wy = wy_ref[...]
    mu = seg_mean(wy)
    wc = wy - mu
    var = seg_mean(wc * wc)
    b3 = (wc * lax.rsqrt(var + RWKV_GN_EPS) * wgn_g[...] + wgn_b[...] + wbon_ref[...]) * wg_ref[...]

    m = None
    gates = (gate0_ref, gate1_ref, gate2_ref, gate3_ref)
    for i, br in enumerate((b0, b1, b2, b3)):
        term = _bdot(br, wbr_ref[i]) * _sigmoid(gates[i][...])
        m = term if m is None else m + term
    h = _bdot(m, wo_ref[...])
    o_ref[...] = _layer_norm(alpha * x_ref[...] + h, ln_g[...], ln_b[...])


def _embed_ln2(x, f, p_ref, wpe_ref, wpg_ref, ln_g, ln_b, alpha):
    e = _bdot(p_ref[...], wpe_ref[...]) * _sigmoid(_bdot(x, wpg_ref[...]))
    return _layer_norm(alpha * x + f + e, ln_g[...], ln_b[...])


def _ffn_kernel(x_ref, p_ref, w1_ref, w3_ref, w2_ref, wpe_ref, wpg_ref, ln_g, ln_b, o_ref, *, alpha):
    x = x_ref[...]
    xb = x.astype(BF16)
    h = _silu(jnp.dot(xb, w1_ref[...], preferred_element_type=F32)) * jnp.dot(
        xb, w3_ref[...], preferred_element_type=F32)
    f = _bdot(h, w2_ref[...])
    o_ref[...] = _embed_ln2(x, f, p_ref, wpe_ref, wpg_ref, ln_g, ln_b, alpha)


def _router_kernel(x_ref, wh_ref, wl_ref, idx_ref, wgt_ref):
    xh, xl = _split_bf16(x_ref[...])
    wh = wh_ref[...]
    logits = (jnp.dot(xh, wh, preferred_element_type=F32)
              + jnp.dot(xl, wh, preferred_element_type=F32)
              + jnp.dot(xh, wl_ref[...], preferred_element_type=F32))
    col = lax.broadcasted_iota(jnp.int32, logits.shape, 1)
    neg = jnp.float32(-jnp.inf)
    lg = jnp.where(col < N_EXPERTS, logits, neg)
    m1 = jnp.max(lg, axis=1, keepdims=True)
    i1 = jnp.min(jnp.where(lg == m1, col, LANES), axis=1, keepdims=True)
    lg2 = jnp.where(col == i1, neg, lg)
    m2 = jnp.max(lg2, axis=1, keepdims=True)
    i2 = jnp.min(jnp.where(lg2 == m2, col, LANES), axis=1, keepdims=True)
    e2 = jnp.exp(m2 - m1)
    den = 1.0 + e2
    idx_ref[...] = jnp.where(col == 0, i1, jnp.where(col == 1, i2, 0))
    wgt_ref[...] = jnp.where(col == 0, 1.0 / den, jnp.where(col == 1, e2 / den, 0.0))


def _gather_rows(tok_ref, src_hbm, dst, sem, n_rows):
    def row_copy(r):
        return pltpu.make_async_copy(src_hbm.at[pl.ds(tok_ref[0, 0, r], 1)], dst.at[pl.ds(r, 1)], sem)

    def start():
        lax.fori_loop(0, n_rows, lambda r, c: (row_copy(r).start(), c)[1], 0)

    def wait():
        lax.fori_loop(0, n_rows, lambda r, c: (row_copy(r).wait(), c)[1], 0)

    return start, wait


def _moe_block_kernel(be_ref, nb_ref, tok_ref, tok_next_ref, x_hbm, w1_ref, w3_ref, w2_ref, o_ref, xbuf, sem):
    j = pl.program_id(0)
    nb = nb_ref[0]
    slot = j % 2
    start_cur, wait_cur = _gather_rows(tok_ref, x_hbm, xbuf.at[slot], sem.at[slot], MOE_BLOCK)
    start_next, _ = _gather_rows(tok_next_ref, x_hbm, xbuf.at[1 - slot], sem.at[1 - slot], MOE_BLOCK)

    @pl.when(jnp.logical_and(j == 0, nb > 0))
    def _():
        start_cur()

    @pl.when(j + 1 < nb)
    def _():
        start_next()

    @pl.when(j < nb)
    def _():
        wait_cur()
        xb = xbuf[slot].astype(BF16)
        h = _silu(jnp.dot(xb, w1_ref[0], preferred_element_type=F32)) * jnp.dot(
            xb, w3_ref[0], preferred_element_type=F32)
        o_ref[...] = _bdot(h, w2_ref[0])

    @pl.when(j >= nb)
    def _():
        o_ref[...] = jnp.zeros_like(o_ref)


def _moe_blocks(x, block_e, nb_used, slot_tok, w1, w3, w2):
    n_blocks = block_e.shape[0]
    d = x.shape[1]
    dff = w1.shape[2]
    tok = slot_tok.reshape(n_blocks, 1, MOE_BLOCK)
    grid_spec = pltpu.PrefetchScalarGridSpec(
        num_scalar_prefetch=2,
        grid=(n_blocks,),
        in_specs=[
            pl.BlockSpec((1, 1, MOE_BLOCK), lambda j, be, nb: (j, 0, 0), memory_space=pltpu.SMEM),
            pl.BlockSpec((1, 1, MOE_BLOCK), lambda j, be, nb: (jnp.minimum(j + 1, n_blocks - 1), 0, 0),
                         memory_space=pltpu.SMEM),
            pl.BlockSpec(memory_space=pl.ANY),
            pl.BlockSpec((1, d, dff), lambda j, be, nb: (be[j], 0, 0), pipeline_mode=pl.Buffered(1)),
            pl.BlockSpec((1, d, dff), lambda j, be, nb: (be[j], 0, 0), pipeline_mode=pl.Buffered(1)),
            pl.BlockSpec((1, dff, d), lambda j, be, nb: (be[j], 0, 0), pipeline_mode=pl.Buffered(1)),
        ],
        out_specs=pl.BlockSpec((MOE_BLOCK, d), lambda j, be, nb: (j, 0)),
        scratch_shapes=[pltpu.VMEM((2, MOE_BLOCK, d), F32), pltpu.SemaphoreType.DMA((2,))],
    )
    return pl.pallas_call(
        _moe_block_kernel,
        grid_spec=grid_spec,
        out_shape=jax.ShapeDtypeStruct((n_blocks * MOE_BLOCK, d), F32),
        compiler_params=_cparams("arbitrary"),
        name="moe_blocks",
    )(block_e, nb_used, tok, tok, x, w1, w3, w2)


def _moe_combine_kernel(slot_ref, slot_next_ref, yb_hbm, x_ref, p_ref, wgt_ref, wpe_ref, wpg_ref, ln_g, ln_b,
                        o_ref, gbuf, sem, *, alpha, tm):
    i = pl.program_id(0)
    slot = i % 2
    start_cur, wait_cur = _gather_rows(slot_ref, yb_hbm, gbuf.at[slot], sem.at[slot], TOP_K * tm)
    start_next, _ = _gather_rows(slot_next_ref, yb_hbm, gbuf.at[1 - slot], sem.at[1 - slot], TOP_K * tm)

    @pl.when(i == 0)
    def _():
        start_cur()

    @pl.when(i + 1 < pl.num_programs(0))
    def _():
        start_next()

    wait_cur()
    wgt = wgt_ref[...]
    f = gbuf[slot, 0:tm, :] * wgt[:, 0:1] + gbuf[slot, tm:2 * tm, :] * wgt[:, 1:2]
    o_ref[...] = _embed_ln2(x_ref[...], f, p_ref, wpe_ref, wpg_ref, ln_g, ln_b, alpha)


def _moe_combine(slots, yb, x, p, wgt, wpe, wpg, ln_g, ln_b, alpha, tm=ROW_TILE):
    n, d = x.shape
    n_tiles = n // tm
    consts = [wpe, wpg, ln_g, ln_b]
    return pl.pallas_call(
        functools.partial(_moe_combine_kernel, alpha=alpha, tm=tm),
        grid=(n_tiles,),
        in_specs=[pl.BlockSpec((1, 1, TOP_K * tm), lambda i: (i, 0, 0), memory_space=pltpu.SMEM),
                  pl.BlockSpec((1, 1, TOP_K * tm), lambda i: (jnp.minimum(i + 1, n_tiles - 1), 0, 0),
                               memory_space=pltpu.SMEM),
                  pl.BlockSpec(memory_space=pl.ANY),
                  _row_spec(tm, d, 0), _row_spec(tm, p.shape[1], 0), _row_spec(tm, LANES, 0)]
                 + [_const_spec(a.shape) for a in consts],
        out_specs=_row_spec(tm, d, 0),
        out_shape=jax.ShapeDtypeStruct((n, d), F32),
        scratch_shapes=[pltpu.VMEM((2, TOP_K * tm, d), F32), pltpu.SemaphoreType.DMA((2,))],
        compiler_params=_cparams("arbitrary"),
        name="moe_combine",
    )(slots, slots, yb, x, p, wgt, *consts)


def _lane_groups(b_n):
    bh = b_n * HEADS
    if bh >= LANES:
        assert bh % LANES == 0
        return 1
    assert LANES % bh == 0
    return LANES // bh


def _k_to_lanes(x, t_n, b_n, n_ops, dk, kl_n):
    x = x.reshape(t_n, b_n, n_ops, HEADS, kl_n, dk // kl_n).transpose(0, 2, 5, 4, 1, 3)
    return x.reshape(t_n, n_ops, dk // kl_n, kl_n * b_n * HEADS)


def _v_to_lanes(x, t_n, b_n, n_ops, vl_n):
    x = x.reshape(t_n, b_n, n_ops, HEADS, HEAD_D // vl_n, vl_n).transpose(0, 2, 4, 5, 1, 3)
    return x.reshape(t_n, n_ops, HEAD_D // vl_n, vl_n * b_n * HEADS)


def _v_from_lanes(o, t_n, b_n, n_ops, vl_n):
    o = o.reshape(t_n, n_ops, HEAD_D // vl_n, vl_n, b_n, HEADS).transpose(0, 4, 1, 5, 2, 3)
    return o.reshape(t_n * b_n, n_ops * HEADS * HEAD_D)


def _state_to_lanes(s, v_lo, value_last):
    b_n = s.shape[0]
    if value_last:
        s = s.transpose(3, 2, 0, 1)
    else:
        s = s.transpose(2, 3, 0, 1)
    v_n, k_n = s.shape[0], s.shape[1]
    s = s.reshape(v_n // v_lo, v_lo, k_n, b_n * HEADS).transpose(0, 2, 1, 3)
    return s.reshape(v_n // v_lo, k_n, v_lo * b_n * HEADS)


def _state_from_lanes(s, b_n, v_lo, value_last):
    vh_n, k_n, _ = s.shape
    s = s.reshape(vh_n, k_n, v_lo, b_n, HEADS).transpose(3, 4, 1, 0, 2).reshape(b_n, HEADS, k_n, vh_n * v_lo)
    return s if value_last else s.transpose(0, 1, 3, 2)


def _rotary_tables(pos, b_n):
    half = HEAD_D // 2
    freq = ROPE_BASE ** (-jnp.arange(half, dtype=F32) / half)
    ang = pos.astype(F32)[:, None] * freq[None, :]
    cos, sin = jnp.cos(ang), jnp.sin(ang)
    cos_h = jnp.tile(jnp.concatenate([cos, cos], axis=-1), (1, HEADS))
    sin_h = jnp.tile(jnp.concatenate([-sin, sin], axis=-1), (1, HEADS))
    return jnp.repeat(cos_h, b_n, axis=0), jnp.repeat(sin_h, b_n, axis=0)


def _block_diag_const(block, n_blocks):
    return jnp.kron(jnp.eye(n_blocks, dtype=F32), jnp.full((block, block), 1.0, F32))


def _s5_params(log_dt, a_re, a_im, b_re, b_im, c_re, c_im):
    dt = jnp.exp(log_dt)[:, None]
    mag = jnp.exp(dt * a_re)
    ang = dt * a_im
    abar_re, abar_im = mag * jnp.cos(ang), mag * jnp.sin(ang)
    den = a_re * a_re + a_im * a_im
    n_re = abar_re - 1.0
    f_re = (n_re * a_re + abar_im * a_im) / den
    f_im = (abar_im * a_re - n_re * a_im) / den
    bb_re = f_re[..., None] * b_re - f_im[..., None] * b_im
    bb_im = f_re[..., None] * b_im + f_im[..., None] * b_re
    eye = jnp.eye(S5_GROUPS, dtype=F32)

    def in_map(bb):
        return jnp.einsum("gpc,gh->gchp", bb, eye).reshape(BRANCH_W, S5_CH)

    def out_map(cm):
        return jnp.einsum("gcp,gh->gphc", cm, eye).reshape(S5_CH, BRANCH_W)

    bb = jnp.concatenate([in_map(bb_re), in_map(bb_im)], axis=1).astype(BF16)
    cc = jnp.concatenate([out_map(c_re), -out_map(c_im)], axis=0).astype(BF16)
    return bb, cc, abar_re.reshape(1, S5_CH), abar_im.reshape(1, S5_CH)


def _reorder_w_in(w_in):
    d_model = w_in.shape[0]
    main = jnp.concatenate([w_in[:, :1792], w_in[:, 1808:]], axis=1)
    tail = jnp.concatenate([w_in[:, 1792:1808],
                            jnp.zeros((d_model, D_IN_PAD - COL_GLA_R - GLA_GATE_RANK), w_in.dtype)], axis=1)
    return jnp.concatenate([main, tail], axis=1).astype(BF16)


def _pad_cols(w, n):
    return jnp.pad(w, ((0, 0), (0, n - w.shape[1])))


def _pad_rows(w, n):
    return jnp.pad(w, ((0, n - w.shape[0]), (0, 0)))


def _moe_route(idx, n):
    nk = n * TOP_K
    flat_e = idx.reshape(nk)
    onehot = (flat_e[:, None] == jnp.arange(N_EXPERTS, dtype=jnp.int32)[None, :]).astype(jnp.int32)
    incl = jnp.cumsum(onehot, axis=0)
    counts = incl[-1]
    rank = jnp.sum((incl - onehot) * onehot, axis=1)
    padded = (counts + MOE_BLOCK - 1) // MOE_BLOCK * MOE_BLOCK
    pad_end = jnp.cumsum(padded)
    slot = (pad_end - padded)[flat_e] + rank
    n_blocks = -(-(nk + N_EXPERTS * (MOE_BLOCK - 1)) // MOE_BLOCK)
    cap = n_blocks * MOE_BLOCK
    flat_tok = jnp.arange(nk, dtype=jnp.int32) // TOP_K
    slot_tok = jnp.zeros((cap,), jnp.int32).at[slot].set(flat_tok)
    block_start = jnp.arange(n_blocks, dtype=jnp.int32) * MOE_BLOCK
    block_e = jnp.minimum(jnp.sum((pad_end[None, :] <= block_start[:, None]).astype(jnp.int32), axis=1),
                          N_EXPERTS - 1).astype(jnp.int32)
    nb_used = (pad_end[-1] // MOE_BLOCK).astype(jnp.int32).reshape(1)
    return slot.astype(jnp.int32).reshape(n, TOP_K), slot_tok, block_e, nb_used


def kernel(x_prompt, x_sample, state_ret, state_gla, state_s5_re, state_s5_im, state_rwkv, state_shift,
           p_prompt, p_sample, w_in, ret_gn_g, ret_gn_b, gla_wg2, gla_bg, gla_gn,
           s5_log_dt, s5_a_re, s5_a_im, s5_b_re, s5_b_im, s5_c_re, s5_c_im, s5_d, s5_w_glu,
           rwkv_mu, rwkv_w0, rwkv_w1, rwkv_w2, rwkv_a0, rwkv_a1, rwkv_a2, rwkv_g1, rwkv_g2,
           rwkv_kk, rwkv_ka, rwkv_rk, rwkv_gn_g, rwkv_gn_b, w_branch, w_o,
           ln1_g, ln1_b, ln2_g, ln2_b, w_pe, w_pg, ffn_w1, ffn_w3, ffn_w2,
           moe_router, moe_w1, moe_w3, moe_w2):
    depth = w_in.shape[0]
    bp, tp, d_model = x_prompt.shape
    bs, ts, _ = x_sample.shape
    n_p, n_s = bp * tp, bs * ts
    n = n_p + n_s
    tm = ROW_TILE
    assert n_p % tm == 0 and n_s % tm == 0 and tm % bp == 0 and tm % bs == 0 and n_p % bs == 0
    alpha = (2 * depth) ** 0.25
    groups = [(0, bp, tp), (n_p, bs, ts)]
    w4 = 4 * BRANCH_W

    def time_major(a_p, a_s):
        return jnp.concatenate([a_p.transpose(1, 0, 2).reshape(n_p, -1),
                                a_s.transpose(1, 0, 2).reshape(n_s, -1)], axis=0)

    x = time_major(x_prompt, x_sample)
    cos_p, sin_p = _rotary_tables(jnp.arange(tp, dtype=jnp.int32), bp)
    cos_s, sin_s = _rotary_tables(PAST_LEN + jnp.arange(ts, dtype=jnp.int32), bs)
    cos_t = jnp.concatenate([cos_p, cos_s], axis=0)
    sin_t = jnp.concatenate([sin_p, sin_s], axis=0)
    ones_bd = _block_diag_const(HEAD_D, HEADS).astype(BF16)
    avg_bd = (_block_diag_const(HEAD_D, HEADS) / HEAD_D).astype(BF16)
    gamma = 1.0 - jnp.exp2(-5.0 - jnp.arange(HEADS, dtype=F32))
    row = lambda v: v.reshape(1, -1)

    new = [[] for _ in range(6)]
    for i in range(depth):
        cols = _matmul(x, _reorder_w_in(w_in[i]), tm=512, tn=D_IN_PAD // 3, name="in_proj")

        consts = [
            jnp.zeros((bp, w4), F32), state_shift[i],
            _pad_rows(_pad_cols(gla_wg2[i], LANES), LANES).astype(BF16), row(gla_bg[i]),
            rwkv_mu[i], row(rwkv_w0[i]),
            _pad_cols(rwkv_w1[i], LANES).astype(BF16), _pad_rows(rwkv_w2[i], LANES).astype(BF16),
            row(rwkv_a0[i]),
            _pad_cols(rwkv_a1[i], LANES).astype(BF16), _pad_rows(rwkv_a2[i], LANES).astype(BF16),
            _pad_cols(rwkv_g1[i], LANES).astype(BF16), _pad_rows(rwkv_g2[i], LANES).astype(BF16),
            row(rwkv_kk[i]), row(rwkv_ka[i]), row(rwkv_rk[i]), ones_bd,
        ]
        rwkv_blk = COL_RWKV // w4
        out_w = [2 * BRANCH_W, 3 * GLA_QK, 5 * BRANCH_W, 3 * BRANCH_W, BRANCH_W, BRANCH_W]
        ka_ret, ka_gla, ka_rwkv, va, obonus, og = pl.pallas_call(
            functools.partial(_prep_kernel, n_p_tiles=n_p // tm),
            grid=(n // tm,),
            in_specs=[_row_spec(tm, BRANCH_W, 0), _row_spec(tm, BRANCH_W, 1), _row_spec(tm, BRANCH_W, 2),
                      _row_spec(tm, BRANCH_W, 0), _row_spec(tm, BRANCH_W, 0),
                      _row_spec(tm, GLA_QK, COL_GLA_Q // GLA_QK), _row_spec(tm, GLA_QK, COL_GLA_K // GLA_QK),
                      _row_spec(tm, BRANCH_W, COL_GLA_V // BRANCH_W), _row_spec(tm, LANES, COL_GLA_R // LANES),
                      _row_spec(tm, w4, rwkv_blk),
                      pl.BlockSpec((bp, w4), lambda r: (jnp.maximum(r * (tm // bp) - 1, 0), rwkv_blk)),
                      pl.BlockSpec((bs, w4), lambda r: (jnp.maximum(r * (tm // bs) - 1, 0), rwkv_blk))]
                     + [_const_spec(a.shape) for a in consts],
            out_specs=[_row_spec(tm, w, 0) for w in out_w],
            out_shape=[jax.ShapeDtypeStruct((n, w), F32) for w in out_w],
            compiler_params=_cparams("parallel"),
            name="mixer_prep",
        )(cols, cols, cols, cos_t, sin_t, cols, cols, cols, cols, cols, cols, cols, *consts)

        s5p = _s5_params(s5_log_dt[i], s5_a_re[i], s5_a_im[i], s5_b_re[i], s5_b_im[i], s5_c_re[i], s5_c_im[i])
        mix_parts, s5_parts, layer_new = [], [], []
        for gi, (off, b_n, t_n) in enumerate(groups):
            if gi == 0:
                st = [jnp.zeros((b_n,) + s.shape[2:], s.dtype)
                      for s in (state_ret, state_gla, state_s5_re, state_s5_im, state_rwkv)]
            else:
                st = [state_ret[i], state_gla[i], state_s5_re[i], state_s5_im[i], state_rwkv[i]]
            rows = slice(off, off + b_n * t_n)
            kl_n = _lane_groups(b_n)
            lanes = kl_n * b_n * HEADS
            dec = jnp.tile(gamma, (lanes // HEADS,)).reshape(1, lanes)
            o, s_ret, s_gla, s_rwkv = _scan3(
                _k_to_lanes(ka_ret[rows], t_n, b_n, 2, HEAD_D, kl_n),
                _k_to_lanes(ka_gla[rows], t_n, b_n, 3, GLA_DK, kl_n),
                _k_to_lanes(ka_rwkv[rows], t_n, b_n, 5, HEAD_D, kl_n),
                _v_to_lanes(va[rows], t_n, b_n, 3, kl_n), dec,
                _state_to_lanes(st[0], kl_n, True), _state_to_lanes(st[1], kl_n, True),
                _state_to_lanes(st[4], kl_n, False), kl_n)
            mix_parts.append(_v_from_lanes(o, t_n, b_n, 3, kl_n))
            y, hr, hi = _s5_scan(cols, off, *s5p, st[2].reshape(b_n, S5_CH), st[3].reshape(b_n, S5_CH), b_n, t_n)
            s5_parts.append(y)
            shift_new = cols[off + (t_n - 1) * b_n:off + t_n * b_n, COL_RWKV:COL_RWKV + w4]
            layer_new.append((_state_from_lanes(s_ret, b_n, kl_n, True), _state_from_lanes(s_gla, b_n, kl_n, True),
                              hr.reshape(b_n, S5_GROUPS, S5_STATE), hi.reshape(b_n, S5_GROUPS, S5_STATE),
                              _state_from_lanes(s_rwkv, b_n, kl_n, False), shift_new))
        for lst, pair in zip(new, zip(*layer_new)):
            lst.append(pair)
        mix = jnp.concatenate(mix_parts, axis=0)
        s5y = jnp.concatenate(s5_parts, axis=0)

        post_consts = [avg_bd, row(ret_gn_g[i]), row(ret_gn_b[i]), row(gla_gn[i]), row(s5_d[i]),
                       s5_w_glu[i].astype(BF16), row(rwkv_gn_g[i]), row(rwkv_gn_b[i]),
                       w_branch[i].astype(BF16), w_o[i].astype(BF16), row(ln1_g[i]), row(ln1_b[i])]
        (x1,) = _rowwise(
            functools.partial(_post_kernel, alpha=alpha), n,
            [(x, d_model, 0), (mix, BRANCH_W, 0), (cols, BRANCH_W, 3),
             (mix, BRANCH_W, 1), (cols, BRANCH_W, COL_GLA_G // BRANCH_W),
             (s5y, BRANCH_W, 0), (cols, BRANCH_W, COL_S5 // BRANCH_W),
             (mix, BRANCH_W, 2), (obonus, BRANCH_W, 0), (og, BRANCH_W, 0),
             ] + [(cols, d_model, COL_GATE // d_model + gi) for gi in range(N_BRANCH)],
            post_consts, [d_model], name="mixer_post")

        p = time_major(p_prompt[i], p_sample[i])
        tail_consts = [w_pe[i].astype(BF16), w_pg[i].astype(BF16), row(ln2_g[i]), row(ln2_b[i])]
        j = i // 2
        if i % 2 == 0:
            (x,) = _rowwise(
                functools.partial(_ffn_kernel, alpha=alpha), n,
                [(x1, d_model, 0), (p, p.shape[1], 0)],
                [ffn_w1[j].astype(BF16), ffn_w3[j].astype(BF16), ffn_w2[j].astype(BF16)] + tail_consts,
                [d_model], name="ffn")
        else:
            rh, rl = _split_bf16(_pad_cols(moe_router[j], LANES))
            idx, wgt = _rowwise(_router_kernel, n, [(x1, d_model, 0)], [rh, rl], [LANES, LANES],
                                out_dtypes=[jnp.int32, F32], name="moe_router")
            slots, slot_tok, block_e, nb_used = _moe_route(idx[:, :TOP_K], n)
            yb = _moe_blocks(x1, block_e, nb_used, slot_tok,
                             moe_w1[j].astype(BF16), moe_w3[j].astype(BF16), moe_w2[j].astype(BF16))
            slots_t = slots.reshape(n // tm, tm, TOP_K).transpose(0, 2, 1).reshape(n // tm, 1, TOP_K * tm)
            x = _moe_combine(slots_t, yb, x1, p, wgt, *tail_consts, alpha)

    y_prompt = x[:n_p].reshape(tp, bp, d_model).transpose(1, 0, 2)
    y_sample = x[n_p:].reshape(ts, bs, d_model).transpose(1, 0, 2)
    outs = [y_prompt, y_sample]
    for lst in new:
        outs.append(jnp.stack([pair[0] for pair in lst], 0))
        outs.append(jnp.stack([pair[1] for pair in lst], 0))
    return tuple(outs)
```

```python
import functools

import jax
import jax.numpy as jnp
from jax import lax
from jax.experimental import pallas as pl
from jax.experimental.pallas import tpu as pltpu

F32 = jnp.float32
BF16 = jnp.bfloat16

LANES = 128
SUBLANES = 8
VMEM_LIMIT = 56 * 1024 * 1024

N_BRANCH = 4
BRANCH_W = 256
HEADS = 4
HEAD_D = 64
GLA_DK = 32
GLA_QK = HEADS * GLA_DK
GLA_GATE_RANK = 16
GLA_GATE_NORM = 16.0
S5_GROUP = 16
S5_GROUPS = 16
S5_STATE = 64
S5_CH = S5_GROUPS * S5_STATE
ROPE_BASE = 10000.0
RWKV_GN_EPS = 64e-5
LN_EPS = 1e-5
N_EXPERTS = 8
TOP_K = 2
MOE_BLOCK = 256
ROW_TILE = 256
PAST_LEN = 16384

COL_RET = 0
COL_GLA_Q = 1024
COL_GLA_K = 1152
COL_GLA_V = 1280
COL_GLA_G = 1536
COL_S5 = 1792
COL_RWKV = 2048
COL_GATE = 3072
COL_GLA_R = 7168
D_IN_PAD = 7296


def _cparams(*sem):
    return pltpu.CompilerParams(dimension_semantics=sem, vmem_limit_bytes=VMEM_LIMIT)


def _split_bf16(x):
    hi = x.astype(BF16)
    lo = (x - hi.astype(F32)).astype(BF16)
    return hi, lo


def _seg_dot(x, m_ref):
    hi, lo = _split_bf16(x)
    m = m_ref[...]
    return (jnp.dot(hi, m, preferred_element_type=F32)
            + jnp.dot(lo, m, preferred_element_type=F32))


def _bdot(x, w):
    return jnp.dot(x.astype(BF16), w, preferred_element_type=F32)


def _sigmoid(x):
    return 1.0 / (1.0 + jnp.exp(-x))


def _silu(x):
    return x * _sigmoid(x)


def _log1p_exp_neg_abs(x):
    return jnp.log1p(jnp.exp(-jnp.abs(x)))


def _layer_norm(x, g, b):
    mu = jnp.mean(x, axis=-1, keepdims=True)
    xc = x - mu
    var = jnp.mean(xc * xc, axis=-1, keepdims=True)
    return xc * lax.rsqrt(var + LN_EPS) * g + b


def _row_spec(tm, width, col_block):
    return pl.BlockSpec((tm, width), lambda i, cb=col_block: (i, cb))


def _const_spec(shape):
    nd = len(shape)
    return pl.BlockSpec(shape, lambda i, nd=nd: (0,) * nd)


def _rowwise(body, n_rows, row_in, const_in, out_widths, out_dtypes=None, tm=ROW_TILE, name=None):
    assert n_rows % tm == 0
    out_dtypes = out_dtypes or [F32] * len(out_widths)
    in_specs = [_row_spec(tm, w, cb) for _, w, cb in row_in] + [_const_spec(a.shape) for a in const_in]
    out_specs = [_row_spec(tm, w, 0) for w in out_widths]
    out_shape = [jax.ShapeDtypeStruct((n_rows, w), dt) for w, dt in zip(out_widths, out_dtypes)]
    return pl.pallas_call(
        body,
        grid=(n_rows // tm,),
        in_specs=in_specs,
        out_specs=out_specs,
        out_shape=out_shape,
        compiler_params=_cparams("parallel"),
        name=name,
    )(*[a for a, _, _ in row_in], *const_in)


def _matmul_kernel(x_ref, w_ref, o_ref):
    o_ref[...] = _bdot(x_ref[...], w_ref[...]).astype(o_ref.dtype)


def _matmul(x, w, tm=512, tn=None, out_dtype=F32, name=None):
    m, k = x.shape
    n = w.shape[1]
    tn = tn or n
    tm = tm if m % tm == 0 else ROW_TILE
    assert m % tm == 0 and n % tn == 0
    return pl.pallas_call(
        _matmul_kernel,
        grid=(n // tn, m // tm),
        in_specs=[pl.BlockSpec((tm, k), lambda j, i: (i, 0)),
                  pl.BlockSpec((k, tn), lambda j, i: (0, j))],
        out_specs=pl.BlockSpec((tm, tn), lambda j, i: (i, j)),
        out_shape=jax.ShapeDtypeStruct((m, n), out_dtype),
        compiler_params=_cparams("parallel", "parallel"),
        name=name,
    )(x, w)


def _replicate_groups(w, n_grp):
    if n_grp == 1:
        return [w]
    grp = lax.broadcasted_iota(jnp.int32, w.shape, 1) // (LANES // n_grp)
    parts = [w]
    span = n_grp
    while span > 1:
        half = span // 2
        shift = half * (LANES // n_grp)
        low = (grp % span) < half
        nxt = []
        for z in parts:
            rz = pltpu.roll(z, shift, 1)
            nxt.append(jnp.where(low, z, rz))
            nxt.append(jnp.where(low, rz, z))
        parts = nxt
        span = half
    return parts


def _heads_to_lanes(ops, b_n, pad_to=LANES):
    tm = ops[0].shape[0]
    c = ops[0].shape[1] // HEADS
    packed = []
    for h in range(HEADS):
        pieces = [o[:, h * c:(h + 1) * c] for o in ops]
        if len(pieces) * c < pad_to:
            pieces.append(jnp.zeros((tm, pad_to - len(pieces) * c), F32))
        packed.append(jnp.concatenate(pieces, axis=1).reshape(tm // b_n, b_n, pad_to))
    y = jnp.stack(packed, axis=1).reshape(tm * HEADS, pad_to)
    return y.T


def _prep_kernel(rq_ref, rk_ref, rv_ref, cos_ref, sin_ref,
                 gq_ref, gk_ref, gv_ref, gr_ref,
                 c_ref, tail_ref, shift_ref, wg2_ref, bg_ref,
                 mu_ref, w0_ref, w1_ref, w2_ref, a0_ref, a1_ref, a2_ref, g1_ref, g2_ref,
                 kkp_ref, kap_ref, rkp_ref, ones_ref,
                 kr_ref, kg_ref, kw_ref, va_ref, obonus_ref, og_ref, *, grouped):
    i = pl.program_id(0)
    w = BRANCH_W
    cos = cos_ref[...]
    sin = sin_ref[...]
    lane = lax.broadcasted_iota(jnp.int32, cos.shape, 1)
    first_half = (lane % HEAD_D) < (HEAD_D // 2)

    def rot(x):
        partner = jnp.where(first_half,
                            pltpu.roll(x, BRANCH_W - HEAD_D // 2, 1),
                            pltpu.roll(x, HEAD_D // 2, 1))
        return x * cos + partner * sin

    ret_q = rot(rq_ref[...])
    ret_k = rot(rk_ref[...]) * (HEAD_D ** -0.5)

    z = _bdot(gr_ref[...], wg2_ref[...]) + bg_ref[...]
    glog = (jnp.minimum(z, 0.0) - _log1p_exp_neg_abs(z)) / GLA_GATE_NORM
    gla_q = gq_ref[...] * (GLA_DK ** -0.5)
    gla_al = jnp.exp(glog)

    c = c_ref[...]
    tm = c.shape[0]
    b_n = tail_ref.shape[0]
    tail = jnp.where(i == 0, shift_ref[...], tail_ref[...])
    d = jnp.concatenate([tail, c[:tm - b_n]], axis=0) - c
    mu = mu_ref[...]
    cr, ck, cv, cz = (c[:, j * w:(j + 1) * w] for j in range(4))
    dr, dk, dv, dz = (d[:, j * w:(j + 1) * w] for j in range(4))
    r = cr + dr * mu[0:1]
    k = ck + dk * mu[1:2]
    v = cv + dv * mu[2:3]
    zw = cz + dz * mu[3:4]
    za = cz + dz * mu[4:5]
    zg = cz + dz * mu[5:6]
    w_raw = w0_ref[...] + _bdot(jnp.tanh(_bdot(zw, w1_ref[...])), w2_ref[...])
    sp = jnp.maximum(-w_raw, 0.0) + _log1p_exp_neg_abs(w_raw)
    dec = jnp.exp(-jnp.exp(-sp - 0.5))
    a = _sigmoid(a0_ref[...] + _bdot(_bdot(za, a1_ref[...]), a2_ref[...]))
    og_ref[...] = _bdot(_sigmoid(_bdot(zg, g1_ref[...])), g2_ref[...])
    kk = k * kkp_ref[...]
    ss = _seg_dot(kk * kk, ones_ref)
    kk = kk * lax.rsqrt(jnp.maximum(ss, 1e-24))
    km = k * (1.0 + (a - 1.0) * kap_ref[...])
    obonus_ref[...] = _seg_dot(r * km * rkp_ref[...], ones_ref) * v

    if not grouped:
        for ref, ops in ((kr_ref, (ret_q, ret_k)), (kg_ref, (gla_q, gk_ref[...], gla_al)),
                         (kw_ref, (r, km, dec, kk, kk * a)), (va_ref, (rv_ref[...], gv_ref[...], v))):
            cw = ops[0].shape[1]
            for j, o in enumerate(ops):
                ref[j * cw:(j + 1) * cw, :] = o.T
        return

    n_grp = LANES // (HEADS * b_n)

    def emit_tiles(ref, row0, ops, n_rows):
        wt = _heads_to_lanes(ops, b_n)
        for lt in range(wt.shape[1] // LANES):
            for g, zt in enumerate(_replicate_groups(wt[:n_rows, lt * LANES:(lt + 1) * LANES], n_grp)):
                q = lt * n_grp + g
                ref[row0:row0 + n_rows, q * LANES:(q + 1) * LANES] = zt

    emit_tiles(kr_ref, 0, (ret_q, ret_k), 2 * HEAD_D)
    emit_tiles(kg_ref, 0, (gla_q, gk_ref[...], gla_al), 3 * GLA_DK)
    emit_tiles(kw_ref, 0, (r, km), 2 * HEAD_D)
    emit_tiles(kw_ref, 2 * HEAD_D, (dec, kk), 2 * HEAD_D)
    emit_tiles(kw_ref, 4 * HEAD_D, (kk * a,), HEAD_D)
    va_ref[0:2 * HEAD_D, :] = _heads_to_lanes((rv_ref[...], gv_ref[...]), b_n)
    va_ref[2 * HEAD_D:3 * HEAD_D, :] = _heads_to_lanes((v,), b_n)[:HEAD_D]


def _scan_kernel(kr_ref, kg_ref, kw_ref, va_ref, dec_ref, s0r_ref, s0g_ref, s0w_ref,
                 o_ref, sor_ref, sog_ref, sow_ref,
                 sr, sg, sw, vs, os_, *, vh_n, grouped, tc, b_n):
    ti = pl.program_id(1)

    @pl.when(ti == 0)
    def _():
        sr[...] = s0r_ref[...]
        sg[...] = s0g_ref[...]
        sw[...] = s0w_ref[...]

    n_grp = LANES // (HEADS * b_n) if grouped else 1
    grp_w = LANES // n_grp
    n_v = 3

    def k_tile(ref, n_ops, j, t):
        lanes = pl.ds(pl.multiple_of(t * LANES, LANES), LANES)
        if grouped:
            k_n = ref.shape[0] // n_ops
            return ref[j * k_n:(j + 1) * k_n, lanes]
        return ref[j, 0, :, lanes]

    if grouped:
        grp = lax.broadcasted_iota(jnp.int32, (vh_n, LANES), 1) // grp_w
        for lt in range(tc // n_grp):
            lanes = slice(lt * LANES, (lt + 1) * LANES)
            for j in range(n_v):
                ws = [va_ref[j * HEAD_D + vl * vh_n:j * HEAD_D + (vl + 1) * vh_n, lanes] for vl in range(n_grp)]
                for g in range(n_grp):
                    tile = None
                    for vl in range(n_grp):
                        r = pltpu.roll(ws[vl], ((vl - g) * grp_w) % LANES, 1)
                        tile = r if tile is None else jnp.where(grp == vl, r, tile)
                    vs[lt * n_grp + g, j] = tile
    else:
        def copy_in(t, c):
            lanes = pl.ds(pl.multiple_of(t * LANES, LANES), LANES)
            for j in range(n_v):
                vs[t, j] = va_ref[j, 0, :, lanes]
            return c

        lax.fori_loop(0, tc, copy_in, 0)

    dec_r = dec_ref[...]

    def out(t, j, vh, s, q):
        os_[t, j, pl.ds(vh, 1), :] = jnp.sum(s * q, axis=0, keepdims=True)

    def ret_step(t, c):
        q, k = k_tile(kr_ref, 2, 0, t), k_tile(kr_ref, 2, 1, t)
        for vh in range(vh_n):
            s = sr[vh] * dec_r + vs[t, 0, pl.ds(vh, 1), :] * k
            sr[vh] = s
            out(t, 0, vh, s, q)
        return c

    def gla_step(t, c):
        q, k, al = (k_tile(kg_ref, 3, j, t) for j in range(3))
        for vh in range(vh_n):
            s = sg[vh] * al + vs[t, 1, pl.ds(vh, 1), :] * k
            sg[vh] = s
            out(t, 1, vh, s, q)
        return c

    def rwkv_step(t, c):
        q, k, dec, kk, beta = (k_tile(kw_ref, 5, j, t) for j in range(5))
        for vh in range(vh_n):
            s = sw[vh]
            sk = jnp.sum(s * kk, axis=0, keepdims=True)
            s = s * dec - sk * beta + vs[t, 2, pl.ds(vh, 1), :] * k
            sw[vh] = s
            out(t, 2, vh, s, q)
        return c

    lax.fori_loop(0, tc, ret_step, 0)
    lax.fori_loop(0, tc, gla_step, 0)
    lax.fori_loop(0, tc, rwkv_step, 0)

    if grouped:
        grp = lax.broadcasted_iota(jnp.int32, (vh_n, LANES), 1) // grp_w
        for lt in range(tc // n_grp):
            lanes = slice(lt * LANES, (lt + 1) * LANES)
            for j in range(n_v):
                og = [os_[lt * n_grp + g, j] for g in range(n_grp)]
                for vl in range(n_grp):
                    wv = None
                    for g in range(n_grp):
                        r = pltpu.roll(og[g], ((g - vl) * grp_w) % LANES, 1)
                        wv = r if wv is None else jnp.where(grp == g, r, wv)
                    o_ref[j * HEAD_D + vl * vh_n:j * HEAD_D + (vl + 1) * vh_n, lanes] = wv
    else:
        def copy_out(t, c):
            lanes = pl.ds(pl.multiple_of(t * LANES, LANES), LANES)
            for j in range(n_v):
                o_ref[j, 0, :, lanes] = os_[t, j]
            return c

        lax.fori_loop(0, tc, copy_out, 0)

    @pl.when(ti == pl.num_programs(1) - 1)
    def _():
        sor_ref[...] = sr[...]
        sog_ref[...] = sg[...]
        sow_ref[...] = sw[...]


def _scan(kr, kg, kw, va, b_n, t_n, dec, s0r, s0g, s0w):
    grouped = b_n * HEADS < LANES
    vh_n = s0r.shape[0]
    if grouped:
        tc = min(t_n, 32)
        n_grp = LANES // (HEADS * b_n)
        assert tc % n_grp == 0 and t_n % tc == 0
        n_l, n_t = 1, t_n // tc
        k_specs = [pl.BlockSpec((a.shape[0], tc * LANES), lambda l, t: (0, t)) for a in (kr, kg, kw)]
        v_spec = pl.BlockSpec((va.shape[0], tc * HEADS * b_n), lambda l, t: (0, t))
        o_spec = v_spec
    else:
        assert b_n == LANES
        tc, n_l, n_t = t_n, HEADS, 1
        k_specs = [pl.BlockSpec((a.shape[0], 1, a.shape[2], tc * LANES), lambda l, t: (0, l, 0, 0))
                   for a in (kr, kg, kw)]
        v_spec = pl.BlockSpec((3, 1, HEAD_D, tc * LANES), lambda l, t: (0, l, 0, 0))
        o_spec = v_spec

    def s_spec(a):
        return pl.BlockSpec(a.shape[:2] + (LANES,), lambda l, t: (0, 0, l))

    return pl.pallas_call(
        functools.partial(_scan_kernel, vh_n=vh_n, grouped=grouped, tc=tc, b_n=b_n),
        grid=(n_l, n_t),
        in_specs=k_specs + [v_spec, pl.BlockSpec((1, LANES), lambda l, t: (0, l)),
                            s_spec(s0r), s_spec(s0g), s_spec(s0w)],
        out_specs=[o_spec, s_spec(s0r), s_spec(s0g), s_spec(s0w)],
        out_shape=[jax.ShapeDtypeStruct(va.shape, F32)] + [jax.ShapeDtypeStruct(s.shape, F32)
                                                           for s in (s0r, s0g, s0w)],
        scratch_shapes=[pltpu.VMEM(s.shape[:2] + (LANES,), F32) for s in (s0r, s0g, s0w)]
                       + [pltpu.VMEM((tc, 3, vh_n, LANES), F32), pltpu.VMEM((tc, 3, vh_n, LANES), F32)],
        compiler_params=_cparams("parallel", "arbitrary"),
        name="scan_ret_gla_rwkv",
    )(kr, kg, kw, va, dec, s0r, s0g, s0w)


def _s5_kernel(u_ref, bb_ref, cc_ref, ar_ref, ai_ref, h0r_ref, h0i_ref,
               y_ref, hr_out, hi_out, hr_scr, hi_scr, xs_scr, hs_scr, *, b_n, tc):
    ti = pl.program_id(0)

    @pl.when(ti == 0)
    def _():
        hr_scr[...] = h0r_ref[...]
        hi_scr[...] = h0i_ref[...]

    xs_scr[...] = _bdot(u_ref[...], bb_ref[...])
    ar = ar_ref[...]
    ai = ai_ref[...]

    def step(t, carry):
        hr, hi = carry
        row = pl.multiple_of(t * b_n, SUBLANES)
        x = xs_scr[pl.ds(row, b_n), :]
        nr = ar * hr - ai * hi + x[:, :S5_CH]
        ni = ar * hi + ai * hr + x[:, S5_CH:]
        hs_scr[pl.ds(row, b_n), :S5_CH] = nr
        hs_scr[pl.ds(row, b_n), S5_CH:] = ni
        return nr, ni

    hr, hi = lax.fori_loop(0, tc, step, (hr_scr[...], hi_scr[...]))
    hr_scr[...] = hr
    hi_scr[...] = hi
    y_ref[...] = _bdot(hs_scr[...], cc_ref[...])

    @pl.when(ti == pl.num_programs(0) - 1)
    def _():
        hr_out[...] = hr
        hi_out[...] = hi


def _s5_scan(cols, row_off, bb, cc, ar, ai, h0r, h0i, b_n, t_n):
    tc = min(t_n, max(1, 1024 // b_n))
    rows = tc * b_n
    assert t_n % tc == 0 and row_off % rows == 0
    blk0 = row_off // rows
    return pl.pallas_call(
        functools.partial(_s5_kernel, b_n=b_n, tc=tc),
        grid=(t_n // tc,),
        in_specs=[pl.BlockSpec((rows, BRANCH_W), lambda t: (blk0 + t, COL_S5 // BRANCH_W)),
                  _const_spec(bb.shape), _const_spec(cc.shape),
                  _const_spec(ar.shape), _const_spec(ai.shape),
                  _const_spec(h0r.shape), _const_spec(h0i.shape)],
        out_specs=[pl.BlockSpec((rows, BRANCH_W), lambda t: (t, 0)),
                   _const_spec(h0r.shape), _const_spec(h0i.shape)],
        out_shape=[jax.ShapeDtypeStruct((t_n * b_n, BRANCH_W), F32),
                   jax.ShapeDtypeStruct(h0r.shape, F32),
                   jax.ShapeDtypeStruct(h0i.shape, F32)],
        scratch_shapes=[pltpu.VMEM((b_n, S5_CH), F32), pltpu.VMEM((b_n, S5_CH), F32),
                        pltpu.VMEM((rows, 2 * S5_CH), F32), pltpu.VMEM((rows, 2 * S5_CH), F32)],
        compiler_params=_cparams("arbitrary"),
        name="scan_s5",
    )(cols, bb, cc, ar, ai, h0r, h0i)


def _post_kernel(x_ref, mixt_ref, rg_ref, gg_ref, sy_ref, su_ref, wbon_ref, wg_ref,
                 gate0_ref, gate1_ref, gate2_ref, gate3_ref,
                 avg_ref, rgn_g, rgn_b, ggn_g, s5d_ref, wglu_ref, wgn_g, wgn_b,
                 wbr_ref, wo_ref, ln_g, ln_b, o_ref, *, alpha, b_n):
    def seg_mean(v):
        return _seg_dot(v, avg_ref)

    tm = x_ref.shape[0]
    if b_n * HEADS < LANES:
        yt = mixt_ref[...].T.reshape(tm // b_n, HEADS, b_n, 3 * HEAD_D)
        per_head = [yt[:, h].reshape(tm, 3 * HEAD_D) for h in range(HEADS)]
        mix = jnp.concatenate([ph[:, j * HEAD_D:(j + 1) * HEAD_D] for j in range(3) for ph in per_head], axis=1)
    else:
        mix = mixt_ref[...].T
    ro = mix[:, 0:BRANCH_W]
    mu = seg_mean(ro)
    rc = ro - mu
    var = seg_mean(rc * rc)
    b0 = (rc * lax.rsqrt(var + LN_EPS) * rgn_g[...] + rgn_b[...]) * _silu(rg_ref[...])
    go = mix[:, BRANCH_W:2 * BRANCH_W]
    ms = seg_mean(go * go)
    b1 = go * lax.rsqrt(ms + LN_EPS) * ggn_g[...] * _silu(gg_ref[...])
    y = jax.nn.gelu(sy_ref[...] + s5d_ref[...] * su_ref[...])
    b2 = y * _sigmoid(_bdot(y, wglu_ref[...]))
    wy = mix[:, 2 * BRANCH_W:3 * BRANCH_W]
    mu = seg_mean(wy)
    wc = wy - mu
    var = seg_mean(wc * wc)
    b3 = (wc * lax.rsqrt(var + RWKV_GN_EPS) * wgn_g[...] + wgn_b[...] + wbon_ref[...]) * wg_ref[...]

    m = None
    gates = (gate0_ref, gate1_ref, gate2_ref, gate3_ref)
    for i, br in enumerate((b0, b1, b2, b3)):
        term = _bdot(br, wbr_ref[i]) * _sigmoid(gates[i][...])
        m = term if m is None else m + term
    h = _bdot(m, wo_ref[...])
    o_ref[...] = _layer_norm(alpha * x_ref[...] + h, ln_g[...], ln_b[...])


def _embed_ln2(x, f, p_ref, wpe_ref, wpg_ref, ln_g, ln_b, alpha):
    e = _bdot(p_ref[...], wpe_ref[...]) * _sigmoid(_bdot(x, wpg_ref[...]))
    return _layer_norm(alpha * x + f + e, ln_g[...], ln_b[...])


def _ffn_kernel(x_ref, p_ref, w1_ref, w3_ref, w2_ref, wpe_ref, wpg_ref, ln_g, ln_b, o_ref, *, alpha):
    x = x_ref[...]
    xb = x.astype(BF16)
    h = _silu(jnp.dot(xb, w1_ref[...], preferred_element_type=F32)) * jnp.dot(
        xb, w3_ref[...], preferred_element_type=F32)
    f = _bdot(h, w2_ref[...])
    o_ref[...] = _embed_ln2(x, f, p_ref, wpe_ref, wpg_ref, ln_g, ln_b, alpha)


def _router_kernel(x_ref, wh_ref, wl_ref, idx_ref, wgt_ref):
    xh, xl = _split_bf16(x_ref[...])
    wh = wh_ref[...]
    logits = (jnp.dot(xh, wh, preferred_element_type=F32)
              + jnp.dot(xl, wh, preferred_element_type=F32)
              + jnp.dot(xh, wl_ref[...], preferred_element_type=F32))
    col = lax.broadcasted_iota(jnp.int32, logits.shape, 1)
    neg = jnp.float32(-jnp.inf)
    lg = jnp.where(col < N_EXPERTS, logits, neg)
    m1 = jnp.max(lg, axis=1, keepdims=True)
    i1 = jnp.min(jnp.where(lg == m1, col, LANES), axis=1, keepdims=True)
    lg2 = jnp.where(col == i1, neg, lg)
    m2 = jnp.max(lg2, axis=1, keepdims=True)
    i2 = jnp.min(jnp.where(lg2 == m2, col, LANES), axis=1, keepdims=True)
    e2 = jnp.exp(m2 - m1)
    den = 1.0 + e2
    idx_ref[...] = jnp.where(col == 0, i1, jnp.where(col == 1, i2, 0))
    wgt_ref[...] = jnp.where(col == 0, 1.0 / den, jnp.where(col == 1, e2 / den, 0.0))


def _gather_rows(tok_ref, src_hbm, dst, sem, n_rows):
    def row_copy(r):
        return pltpu.make_async_copy(src_hbm.at[pl.ds(tok_ref[0, 0, r], 1)], dst.at[pl.ds(r, 1)], sem)

    def start():
        lax.fori_loop(0, n_rows, lambda r, c: (row_copy(r).start(), c)[1], 0)

    def wait():
        lax.fori_loop(0, n_rows, lambda r, c: (row_copy(r).wait(), c)[1], 0)

    return start, wait


def _moe_block_kernel(be_ref, nb_ref, tok_ref, tok_next_ref, x_hbm, w1_ref, w3_ref, w2_ref, o_ref, xbuf, sem):
    j = pl.program_id(0)
    nb = nb_ref[0]
    slot = j % 2
    start_cur, wait_cur = _gather_rows(tok_ref, x_hbm, xbuf.at[slot], sem.at[slot], MOE_BLOCK)
    start_next, _ = _gather_rows(tok_next_ref, x_hbm, xbuf.at[1 - slot], sem.at[1 - slot], MOE_BLOCK)

    @pl.when(jnp.logical_and(j == 0, nb > 0))
    def _():
        start_cur()

    @pl.when(j + 1 < nb)
    def _():
        start_next()

    @pl.when(j < nb)
    def _():
        wait_cur()
        xb = xbuf[slot].astype(BF16)
        h = _silu(jnp.dot(xb, w1_ref[0], preferred_element_type=F32)) * jnp.dot(
            xb, w3_ref[0], preferred_element_type=F32)
        o_ref[...] = _bdot(h, w2_ref[0])

    @pl.when(j >= nb)
    def _():
        o_ref[...] = jnp.zeros_like(o_ref)


def _moe_blocks(x, block_e, nb_used, slot_tok, w1, w3, w2):
    n_blocks = block_e.shape[0]
    d = x.shape[1]
    dff = w1.shape[2]
    tok = slot_tok.reshape(n_blocks, 1, MOE_BLOCK)
    grid_spec = pltpu.PrefetchScalarGridSpec(
        num_scalar_prefetch=2,
        grid=(n_blocks,),
        in_specs=[
            pl.BlockSpec((1, 1, MOE_BLOCK), lambda j, be, nb: (j, 0, 0), memory_space=pltpu.SMEM),
            pl.BlockSpec((1, 1, MOE_BLOCK), lambda j, be, nb: (jnp.minimum(j + 1, n_blocks - 1), 0, 0),
                         memory_space=pltpu.SMEM),
            pl.BlockSpec(memory_space=pl.ANY),
            pl.BlockSpec((1, d, dff), lambda j, be, nb: (be[j], 0, 0), pipeline_mode=pl.Buffered(1)),
            pl.BlockSpec((1, d, dff), lambda j, be, nb: (be[j], 0, 0), pipeline_mode=pl.Buffered(1)),
            pl.BlockSpec((1, dff, d), lambda j, be, nb: (be[j], 0, 0), pipeline_mode=pl.Buffered(1)),
        ],
        out_specs=pl.BlockSpec((MOE_BLOCK, d), lambda j, be, nb: (j, 0)),
        scratch_shapes=[pltpu.VMEM((2, MOE_BLOCK, d), F32), pltpu.SemaphoreType.DMA((2,))],
    )
    return pl.pallas_call(
        _moe_block_kernel,
        grid_spec=grid_spec,
        out_shape=jax.ShapeDtypeStruct((n_blocks * MOE_BLOCK, d), F32),
        compiler_params=_cparams("arbitrary"),
        name="moe_blocks",
    )(block_e, nb_used, tok, tok, x, w1, w3, w2)


def _moe_combine_kernel(slot_ref, slot_next_ref, yb_hbm, x_ref, p_ref, wgt_ref, wpe_ref, wpg_ref, ln_g, ln_b,
                        o_ref, gbuf, sem, *, alpha, tm):
    i = pl.program_id(0)
    slot = i % 2
    start_cur, wait_cur = _gather_rows(slot_ref, yb_hbm, gbuf.at[slot], sem.at[slot], TOP_K * tm)
    start_next, _ = _gather_rows(slot_next_ref, yb_hbm, gbuf.at[1 - slot], sem.at[1 - slot], TOP_K * tm)

    @pl.when(i == 0)
    def _():
        start_cur()

    @pl.when(i + 1 < pl.num_programs(0))
    def _():
        start_next()

    wait_cur()
    wgt = wgt_ref[...]
    f = gbuf[slot, 0:tm, :] * wgt[:, 0:1] + gbuf[slot, tm:2 * tm, :] * wgt[:, 1:2]
    o_ref[...] = _embed_ln2(x_ref[...], f, p_ref, wpe_ref, wpg_ref, ln_g, ln_b, alpha)


def _moe_combine(slots, yb, x, p, wgt, wpe, wpg, ln_g, ln_b, alpha, tm=ROW_TILE):
    n, d = x.shape
    n_tiles = n // tm
    consts = [wpe, wpg, ln_g, ln_b]
    return pl.pallas_call(
        functools.partial(_moe_combine_kernel, alpha=alpha, tm=tm),
        grid=(n_tiles,),
        in_specs=[pl.BlockSpec((1, 1, TOP_K * tm), lambda i: (i, 0, 0), memory_space=pltpu.SMEM),
                  pl.BlockSpec((1, 1, TOP_K * tm), lambda i: (jnp.minimum(i + 1, n_tiles - 1), 0, 0),
                               memory_space=pltpu.SMEM),
                  pl.BlockSpec(memory_space=pl.ANY),
                  _row_spec(tm, d, 0), _row_spec(tm, p.shape[1], 0), _row_spec(tm, LANES, 0)]
                 + [_const_spec(a.shape) for a in consts],
        out_specs=_row_spec(tm, d, 0),
        out_shape=jax.ShapeDtypeStruct((n, d), F32),
        scratch_shapes=[pltpu.VMEM((2, TOP_K * tm, d), F32), pltpu.SemaphoreType.DMA((2,))],
        compiler_params=_cparams("arbitrary"),
        name="moe_combine",
    )(slots, slots, yb, x, p, wgt, *consts)


def _lane_groups(b_n):
    bh = b_n * HEADS
    return LANES // bh if bh < LANES else 1


def _state_to_lanes(s, vl_n, value_last):
    s = s.transpose(3, 2, 1, 0) if value_last else s.transpose(2, 3, 1, 0)
    v_n, k_n, h_n, b_n = s.shape
    s = s.reshape(vl_n, v_n // vl_n, k_n, h_n, b_n).transpose(1, 2, 0, 3, 4)
    return s.reshape(v_n // vl_n, k_n, vl_n * h_n * b_n)


def _state_from_lanes(s, b_n, vl_n, value_last):
    vh_n, k_n, _ = s.shape
    s = s.reshape(vh_n, k_n, vl_n, HEADS, b_n).transpose(4, 3, 1, 2, 0).reshape(b_n, HEADS, k_n, vl_n * vh_n)
    return s if value_last else s.transpose(0, 1, 3, 2)


def _rotary_tables(pos, b_n):
    half = HEAD_D // 2
    freq = ROPE_BASE ** (-jnp.arange(half, dtype=F32) / half)
    ang = pos.astype(F32)[:, None] * freq[None, :]
    cos, sin = jnp.cos(ang), jnp.sin(ang)
    cos_h = jnp.tile(jnp.concatenate([cos, cos], axis=-1), (1, HEADS))
    sin_h = jnp.tile(jnp.concatenate([-sin, sin], axis=-1), (1, HEADS))
    return jnp.repeat(cos_h, b_n, axis=0), jnp.repeat(sin_h, b_n, axis=0)


def _block_diag_const(block, n_blocks):
    return jnp.kron(jnp.eye(n_blocks, dtype=F32), jnp.full((block, block), 1.0, F32))


def _s5_params(log_dt, a_re, a_im, b_re, b_im, c_re, c_im):
    dt = jnp.exp(log_dt)[:, None]
    mag = jnp.exp(dt * a_re)
    ang = dt * a_im
    abar_re, abar_im = mag * jnp.cos(ang), mag * jnp.sin(ang)
    den = a_re * a_re + a_im * a_im
    n_re = abar_re - 1.0
    f_re = (n_re * a_re + abar_im * a_im) / den
    f_im = (abar_im * a_re - n_re * a_im) / den
    bb_re = f_re[..., None] * b_re - f_im[..., None] * b_im
    bb_im = f_re[..., None] * b_im + f_im[..., None] * b_re
    eye = jnp.eye(S5_GROUPS, dtype=F32)

    def in_map(bb):
        return jnp.einsum("gpc,gh->gchp", bb, eye).reshape(BRANCH_W, S5_CH)

    def out_map(cm):
        return jnp.einsum("gcp,gh->gphc", cm, eye).reshape(S5_CH, BRANCH_W)

    bb = jnp.concatenate([in_map(bb_re), in_map(bb_im)], axis=1).astype(BF16)
    cc = jnp.concatenate([out_map(c_re), -out_map(c_im)], axis=0).astype(BF16)
    return bb, cc, abar_re.reshape(1, S5_CH), abar_im.reshape(1, S5_CH)


def _reorder_w_in(w_in):
    d_model = w_in.shape[0]
    main = jnp.concatenate([w_in[:, :1792], w_in[:, 1808:]], axis=1)
    tail = jnp.concatenate([w_in[:, 1792:1808],
                            jnp.zeros((d_model, D_IN_PAD - COL_GLA_R - GLA_GATE_RANK), w_in.dtype)], axis=1)
    return jnp.concatenate([main, tail], axis=1).astype(BF16)


def _pad_cols(w, n):
    return jnp.pad(w, ((0, 0), (0, n - w.shape[1])))


def _pad_rows(w, n):
    return jnp.pad(w, ((0, n - w.shape[0]), (0, 0)))


def _moe_route(idx, n):
    nk = n * TOP_K
    flat_e = idx.reshape(nk)
    onehot = (flat_e[:, None] == jnp.arange(N_EXPERTS, dtype=jnp.int32)[None, :]).astype(jnp.int32)
    incl = jnp.cumsum(onehot, axis=0)
    counts = incl[-1]
    rank = jnp.sum((incl - onehot) * onehot, axis=1)
    padded = (counts + MOE_BLOCK - 1) // MOE_BLOCK * MOE_BLOCK
    pad_end = jnp.cumsum(padded)
    slot = (pad_end - padded)[flat_e] + rank
    n_blocks = -(-(nk + N_EXPERTS * (MOE_BLOCK - 1)) // MOE_BLOCK)
    cap = n_blocks * MOE_BLOCK
    flat_tok = jnp.arange(nk, dtype=jnp.int32) // TOP_K
    slot_tok = jnp.zeros((cap,), jnp.int32).at[slot].set(flat_tok)
    block_start = jnp.arange(n_blocks, dtype=jnp.int32) * MOE_BLOCK
    block_e = jnp.minimum(jnp.sum((pad_end[None, :] <= block_start[:, None]).astype(jnp.int32), axis=1),
                          N_EXPERTS - 1).astype(jnp.int32)
    nb_used = (pad_end[-1] // MOE_BLOCK).astype(jnp.int32).reshape(1)
    return slot.astype(jnp.int32).reshape(n, TOP_K), slot_tok, block_e, nb_used


def kernel(x_prompt, x_sample, state_ret, state_gla, state_s5_re, state_s5_im, state_rwkv, state_shift,
           p_prompt, p_sample, w_in, ret_gn_g, ret_gn_b, gla_wg2, gla_bg, gla_gn,
           s5_log_dt, s5_a_re, s5_a_im, s5_b_re, s5_b_im, s5_c_re, s5_c_im, s5_d, s5_w_glu,
           rwkv_mu, rwkv_w0, rwkv_w1, rwkv_w2, rwkv_a0, rwkv_a1, rwkv_a2, rwkv_g1, rwkv_g2,
           rwkv_kk, rwkv_ka, rwkv_rk, rwkv_gn_g, rwkv_gn_b, w_branch, w_o,
           ln1_g, ln1_b, ln2_g, ln2_b, w_pe, w_pg, ffn_w1, ffn_w3, ffn_w2,
           moe_router, moe_w1, moe_w3, moe_w2):
    depth = w_in.shape[0]
    bp, tp, d_model = x_prompt.shape
    bs, ts, _ = x_sample.shape
    n_p, n_s = bp * tp, bs * ts
    n = n_p + n_s
    tm = ROW_TILE
    assert n_p % tm == 0 and n_s % tm == 0 and tm % bp == 0 and tm % bs == 0 and n_p % bs == 0
    alpha = (2 * depth) ** 0.25
    groups = [(0, bp, tp), (n_p, bs, ts)]
    w4 = 4 * BRANCH_W

    def time_major(a_p, a_s):
        return jnp.concatenate([a_p.transpose(1, 0, 2).reshape(n_p, -1),
                                a_s.transpose(1, 0, 2).reshape(n_s, -1)], axis=0)

    x = time_major(x_prompt, x_sample)
    cos_p, sin_p = _rotary_tables(jnp.arange(tp, dtype=jnp.int32), bp)
    cos_s, sin_s = _rotary_tables(PAST_LEN + jnp.arange(ts, dtype=jnp.int32), bs)
    ones_bd = _block_diag_const(HEAD_D, HEADS).astype(BF16)
    avg_bd = (_block_diag_const(HEAD_D, HEADS) / HEAD_D).astype(BF16)
    gamma = 1.0 - jnp.exp2(-5.0 - jnp.arange(HEADS, dtype=F32))
    row = lambda v: v.reshape(1, -1)

    new = [[] for _ in range(6)]
    for i in range(depth):
        cols = _matmul(x, _reorder_w_in(w_in[i]), tm=512, tn=D_IN_PAD // 3, name="in_proj")

        prep_consts = [
            _pad_rows(_pad_cols(gla_wg2[i], LANES), LANES).astype(BF16), row(gla_bg[i]),
            rwkv_mu[i], row(rwkv_w0[i]),
            _pad_cols(rwkv_w1[i], LANES).astype(BF16), _pad_rows(rwkv_w2[i], LANES).astype(BF16),
            row(rwkv_a0[i]),
            _pad_cols(rwkv_a1[i], LANES).astype(BF16), _pad_rows(rwkv_a2[i], LANES).astype(BF16),
            _pad_cols(rwkv_g1[i], LANES).astype(BF16), _pad_rows(rwkv_g2[i], LANES).astype(BF16),
            row(rwkv_kk[i]), row(rwkv_ka[i]), row(rwkv_rk[i]), ones_bd,
        ]
        post_consts = [avg_bd, row(ret_gn_g[i]), row(ret_gn_b[i]), row(gla_gn[i]), row(s5_d[i]),
                       s5_w_glu[i].astype(BF16), row(rwkv_gn_g[i]), row(rwkv_gn_b[i]),
                       w_branch[i].astype(BF16), w_o[i].astype(BF16), row(ln1_g[i]), row(ln1_b[i])]
        s5p = _s5_params(s5_log_dt[i], s5_a_re[i], s5_a_im[i], s5_b_re[i], s5_b_im[i], s5_c_re[i], s5_c_im[i])
        rwkv_blk = COL_RWKV // w4
        x1_parts, layer_new = [], []
        for gi, (off, b_n, t_n) in enumerate(groups):
            n_g = b_n * t_n
            t0 = off // tm
            if gi == 0:
                st = [jnp.zeros((b_n,) + s.shape[2:], s.dtype)
                      for s in (state_ret, state_gla, state_s5_re, state_s5_im, state_rwkv)]
                shift0, cos_g, sin_g = jnp.zeros((b_n, w4), F32), cos_p, sin_p
            else:
                st = [state_ret[i], state_gla[i], state_s5_re[i], state_s5_im[i], state_rwkv[i]]
                shift0, cos_g, sin_g = state_shift[i], cos_s, sin_s
            vl_n = _lane_groups(b_n)
            grouped = vl_n > 1

            def cspec(width, cb, t0=t0):
                return pl.BlockSpec((tm, width), lambda r: (r + t0, cb))

            if grouped:
                t_shapes = [(2 * HEAD_D, n_g * vl_n * HEADS), (3 * GLA_DK, n_g * vl_n * HEADS),
                            (5 * HEAD_D, n_g * vl_n * HEADS), (3 * HEAD_D, n_g * HEADS)]
            else:
                t_shapes = [(2 * BRANCH_W, n_g), (3 * GLA_QK, n_g), (5 * BRANCH_W, n_g), (3 * BRANCH_W, n_g)]
            consts = [shift0] + prep_consts
            ka_ret, ka_gla, ka_rwkv, va, obonus, og = pl.pallas_call(
                functools.partial(_prep_kernel, grouped=grouped),
                grid=(n_g // tm,),
                in_specs=[cspec(BRANCH_W, 0), cspec(BRANCH_W, 1), cspec(BRANCH_W, 2),
                          _row_spec(tm, BRANCH_W, 0), _row_spec(tm, BRANCH_W, 0),
                          cspec(GLA_QK, COL_GLA_Q // GLA_QK), cspec(GLA_QK, COL_GLA_K // GLA_QK),
                          cspec(BRANCH_W, COL_GLA_V // BRANCH_W), cspec(LANES, COL_GLA_R // LANES),
                          cspec(w4, rwkv_blk),
                          pl.BlockSpec((b_n, w4), lambda r, t0=t0, b_n=b_n:
                                       (jnp.maximum((r + t0) * (tm // b_n) - 1, 0), rwkv_blk))]
                         + [_const_spec(a.shape) for a in consts],
                out_specs=[pl.BlockSpec((sh[0], sh[1] // (n_g // tm)), lambda r: (0, r)) for sh in t_shapes]
                          + [_row_spec(tm, BRANCH_W, 0), _row_spec(tm, BRANCH_W, 0)],
                out_shape=[jax.ShapeDtypeStruct(sh, F32) for sh in t_shapes]
                          + [jax.ShapeDtypeStruct((n_g, BRANCH_W), F32)] * 2,
                compiler_params=_cparams("parallel"),
                name="mixer_prep",
            )(cols, cols, cols, cos_g, sin_g, cols, cols, cols, cols, cols, cols, *consts)

            dec = jnp.tile(jnp.repeat(gamma, b_n), vl_n).reshape(1, vl_n * HEADS * b_n)
            if not grouped:
                ka_ret = ka_ret.reshape(2, HEADS, HEAD_D, n_g)
                ka_gla = ka_gla.reshape(3, HEADS, GLA_DK, n_g)
                ka_rwkv = ka_rwkv.reshape(5, HEADS, HEAD_D, n_g)
                va = va.reshape(3, HEADS, HEAD_D, n_g)
            o, s_ret, s_gla, s_rwkv = _scan(
                ka_ret, ka_gla, ka_rwkv, va, b_n, t_n, dec,
                _state_to_lanes(st[0], vl_n, True), _state_to_lanes(st[1], vl_n, True),
                _state_to_lanes(st[4], vl_n, False))
            mixt = o if grouped else o.reshape(3 * BRANCH_W, n_g)
            y, hr, hi = _s5_scan(cols, off, *s5p, st[2].reshape(b_n, S5_CH), st[3].reshape(b_n, S5_CH), b_n, t_n)
            shift_new = cols[off + (t_n - 1) * b_n:off + t_n * b_n, COL_RWKV:COL_RWKV + w4]
            layer_new.append((_state_from_lanes(s_ret, b_n, vl_n, True), _state_from_lanes(s_gla, b_n, vl_n, True),
                              hr.reshape(b_n, S5_GROUPS, S5_STATE), hi.reshape(b_n, S5_GROUPS, S5_STATE),
                              _state_from_lanes(s_rwkv, b_n, vl_n, False), shift_new))

            x1_parts.append(pl.pallas_call(
                functools.partial(_post_kernel, alpha=alpha, b_n=b_n),
                grid=(n_g // tm,),
                in_specs=[cspec(d_model, 0),
                          pl.BlockSpec((mixt.shape[0], mixt.shape[1] // (n_g // tm)), lambda r: (0, r)),
                          cspec(BRANCH_W, 3), cspec(BRANCH_W, COL_GLA_G // BRANCH_W),
                          _row_spec(tm, BRANCH_W, 0), cspec(BRANCH_W, COL_S5 // BRANCH_W),
                          _row_spec(tm, BRANCH_W, 0), _row_spec(tm, BRANCH_W, 0)]
                         + [cspec(d_model, COL_GATE // d_model + q) for q in range(N_BRANCH)]
                         + [_const_spec(a.shape) for a in post_consts],
                out_specs=_row_spec(tm, d_model, 0),
                out_shape=jax.ShapeDtypeStruct((n_g, d_model), F32),
                compiler_params=_cparams("parallel"),
                name="mixer_post",
            )(x, mixt, cols, cols, y, cols, obonus, og, cols, cols, cols, cols, *post_consts))
        for lst, pair in zip(new, zip(*layer_new)):
            lst.append(pair)
        x1 = jnp.concatenate(x1_parts, axis=0)

        p = time_major(p_prompt[i], p_sample[i])
        tail_consts = [w_pe[i].astype(BF16), w_pg[i].astype(BF16), row(ln2_g[i]), row(ln2_b[i])]
        j = i // 2
        if i % 2 == 0:
            (x,) = _rowwise(
                functools.partial(_ffn_kernel, alpha=alpha), n,
                [(x1, d_model, 0), (p, p.shape[1], 0)],
                [ffn_w1[j].astype(BF16), ffn_w3[j].astype(BF16), ffn_w2[j].astype(BF16)] + tail_consts,
                [d_model], name="ffn")
        else:
            rh, rl = _split_bf16(_pad_cols(moe_router[j], LANES))
            idx, wgt = _rowwise(_router_kernel, n, [(x1, d_model, 0)], [rh, rl], [LANES, LANES],
                                out_dtypes=[jnp.int32, F32], name="moe_router")
            slots, slot_tok, block_e, nb_used = _moe_route(idx[:, :TOP_K], n)
            yb = _moe_blocks(x1, block_e, nb_used, slot_tok,
                             moe_w1[j].astype(BF16), moe_w3[j].astype(BF16), moe_w2[j].astype(BF16))
            slots_t = slots.reshape(n // tm, tm, TOP_K).transpose(0, 2, 1).reshape(n // tm, 1, TOP_K * tm)
            x = _moe_combine(slots_t, yb, x1, p, wgt, *tail_consts, alpha)

    y_prompt = x[:n_p].reshape(tp, bp, d_model).transpose(1, 0, 2)
    y_sample = x[n_p:].reshape(ts, bs, d_model).transpose(1, 0, 2)
    outs = [y_prompt, y_sample]
    for lst in new:
        outs.append(jnp.stack([pair[0] for pair in lst], 0))
        outs.append(jnp.stack([pair[1] for pair in lst], 0))
    return tuple(outs)
```

```python
import functools

import jax
import jax.numpy as jnp
from jax import lax
from jax.experimental import pallas as pl
from jax.experimental.pallas import tpu as pltpu

F32 = jnp.float32
BF16 = jnp.bfloat16

LANES = 128
SUBLANES = 8
VMEM_LIMIT = 56 * 1024 * 1024

N_BRANCH = 4
BRANCH_W = 256
HEADS = 4
HEAD_D = 64
GLA_DK = 32
GLA_QK = HEADS * GLA_DK
GLA_GATE_RANK = 16
GLA_GATE_NORM = 16.0
S5_GROUP = 16
S5_GROUPS = 16
S5_STATE = 64
S5_CH = S5_GROUPS * S5_STATE
ROPE_BASE = 10000.0
RWKV_GN_EPS = 64e-5
LN_EPS = 1e-5
N_EXPERTS = 8
TOP_K = 2
MOE_BLOCK = 256
ROW_TILE = 256
PAST_LEN = 16384

COL_RET = 0
COL_GLA_Q = 1024
COL_GLA_K = 1152
COL_GLA_V = 1280
COL_GLA_G = 1536
COL_S5 = 1792
COL_RWKV = 2048
COL_GATE = 3072
COL_GLA_R = 7168
D_IN_PAD = 7296


def _cparams(*sem):
    return pltpu.CompilerParams(dimension_semantics=sem, vmem_limit_bytes=VMEM_LIMIT)


def _split_bf16(x):
    hi = x.astype(BF16)
    lo = (x - hi.astype(F32)).astype(BF16)
    return hi, lo


def _seg_dot(x, m_ref):
    hi, lo = _split_bf16(x)
    m = m_ref[...]
    return (jnp.dot(hi, m, preferred_element_type=F32)
            + jnp.dot(lo, m, preferred_element_type=F32))


def _bdot(x, w):
    return jnp.dot(x.astype(BF16), w, preferred_element_type=F32)


def _sigmoid(x):
    return 1.0 / (1.0 + jnp.exp(-x))


def _silu(x):
    return x * _sigmoid(x)


def _log1p_exp_neg_abs(x):
    return jnp.log1p(jnp.exp(-jnp.abs(x)))


def _layer_norm(x, g, b):
    mu = jnp.mean(x, axis=-1, keepdims=True)
    xc = x - mu
    var = jnp.mean(xc * xc, axis=-1, keepdims=True)
    return xc * lax.rsqrt(var + LN_EPS) * g + b


def _row_spec(tm, width, col_block, row_block0=0):
    return pl.BlockSpec((tm, width), lambda i, cb=col_block, r0=row_block0: (i + r0, cb))


def _const_spec(shape):
    nd = len(shape)
    return pl.BlockSpec(shape, lambda i, nd=nd: (0,) * nd)


def _rowwise(body, n_rows, row_in, const_in, out_widths, out_dtypes=None, tm=ROW_TILE, name=None):
    assert n_rows % tm == 0
    out_dtypes = out_dtypes or [F32] * len(out_widths)
    in_specs = [_row_spec(tm, *spec[1:]) for spec in row_in] + [_const_spec(a.shape) for a in const_in]
    out_specs = [_row_spec(tm, w, 0) for w in out_widths]
    out_shape = [jax.ShapeDtypeStruct((n_rows, w), dt) for w, dt in zip(out_widths, out_dtypes)]
    return pl.pallas_call(
        body,
        grid=(n_rows // tm,),
        in_specs=in_specs,
        out_specs=out_specs,
        out_shape=out_shape,
        compiler_params=_cparams("parallel"),
        name=name,
    )(*[spec[0] for spec in row_in], *const_in)


def _matmul_kernel(x_ref, w_ref, o_ref):
    o_ref[...] = _bdot(x_ref[...], w_ref[...]).astype(o_ref.dtype)


def _matmul(x, w, tm=512, tn=None, out_dtype=F32, name=None):
    m, k = x.shape
    n = w.shape[1]
    tn = tn or n
    tm = tm if m % tm == 0 else ROW_TILE
    assert m % tm == 0 and n % tn == 0
    return pl.pallas_call(
        _matmul_kernel,
        grid=(n // tn, m // tm),
        in_specs=[pl.BlockSpec((tm, k), lambda j, i: (i, 0)),
                  pl.BlockSpec((k, tn), lambda j, i: (0, j))],
        out_specs=pl.BlockSpec((tm, tn), lambda j, i: (i, j)),
        out_shape=jax.ShapeDtypeStruct((m, n), out_dtype),
        compiler_params=_cparams("parallel", "parallel"),
        name=name,
    )(x, w)


def _replicate_groups(w, n_grp):
    if n_grp == 1:
        return [w]
    grp = lax.broadcasted_iota(jnp.int32, w.shape, 1) // (LANES // n_grp)
    parts = [w]
    span = n_grp
    while span > 1:
        half = span // 2
        shift = half * (LANES // n_grp)
        low = (grp % span) < half
        nxt = []
        for z in parts:
            rz = pltpu.roll(z, shift, 1)
            nxt.append(jnp.where(low, z, rz))
            nxt.append(jnp.where(low, rz, z))
        parts = nxt
        span = half
    return parts


def _heads_to_lanes(ops, b_n, pad_to=LANES):
    tm = ops[0].shape[0]
    c = ops[0].shape[1] // HEADS
    packed = []
    for h in range(HEADS):
        pieces = [o[:, h * c:(h + 1) * c] for o in ops]
        if len(pieces) * c < pad_to:
            pieces.append(jnp.zeros((tm, pad_to - len(pieces) * c), F32))
        packed.append(jnp.concatenate(pieces, axis=1).reshape(tm // b_n, b_n, pad_to))
    y = jnp.stack(packed, axis=1).reshape(tm * HEADS, pad_to)
    return y.T


def _prep_kernel(rq_ref, rk_ref, rv_ref, cos_ref, sin_ref,
                 gq_ref, gk_ref, gv_ref, gr_ref,
                 c_ref, tail_ref, shift_ref, wg2_ref, bg_ref,
                 mu_ref, w0_ref, w1_ref, w2_ref, a0_ref, a1_ref, a2_ref, g1_ref, g2_ref,
                 kkp_ref, kap_ref, rkp_ref, ones_ref,
                 kr_ref, kg_ref, kw_ref, va_ref, obonus_ref, og_ref, *, grouped):
    i = pl.program_id(0)
    w = BRANCH_W
    cos = cos_ref[...]
    sin = sin_ref[...]
    lane = lax.broadcasted_iota(jnp.int32, cos.shape, 1)
    first_half = (lane % HEAD_D) < (HEAD_D // 2)

    def rot(x):
        partner = jnp.where(first_half,
                            pltpu.roll(x, BRANCH_W - HEAD_D // 2, 1),
                            pltpu.roll(x, HEAD_D // 2, 1))
        return x * cos + partner * sin

    ret_q = rot(rq_ref[...])
    ret_k = rot(rk_ref[...]) * (HEAD_D ** -0.5)

    z = _bdot(gr_ref[...], wg2_ref[...]) + bg_ref[...]
    glog = (jnp.minimum(z, 0.0) - _log1p_exp_neg_abs(z)) / GLA_GATE_NORM
    gla_q = gq_ref[...] * (GLA_DK ** -0.5)
    gla_al = jnp.exp(glog)

    c = c_ref[...]
    tm = c.shape[0]
    b_n = tail_ref.shape[0]
    tail = jnp.where(i == 0, shift_ref[...], tail_ref[...])
    d = jnp.concatenate([tail, c[:tm - b_n]], axis=0) - c
    mu = mu_ref[...]
    cr, ck, cv, cz = (c[:, j * w:(j + 1) * w] for j in range(4))
    dr, dk, dv, dz = (d[:, j * w:(j + 1) * w] for j in range(4))
    r = cr + dr * mu[0:1]
    k = ck + dk * mu[1:2]
    v = cv + dv * mu[2:3]
    zw = cz + dz * mu[3:4]
    za = cz + dz * mu[4:5]
    zg = cz + dz * mu[5:6]
    w_raw = w0_ref[...] + _bdot(jnp.tanh(_bdot(zw, w1_ref[...])), w2_ref[...])
    sp = jnp.maximum(-w_raw, 0.0) + _log1p_exp_neg_abs(w_raw)
    dec = jnp.exp(-jnp.exp(-sp - 0.5))
    a = _sigmoid(a0_ref[...] + _bdot(_bdot(za, a1_ref[...]), a2_ref[...]))
    og_ref[...] = _bdot(_sigmoid(_bdot(zg, g1_ref[...])), g2_ref[...])
    kk = k * kkp_ref[...]
    ss = _seg_dot(kk * kk, ones_ref)
    kk = kk * lax.rsqrt(jnp.maximum(ss, 1e-24))
    km = k * (1.0 + (a - 1.0) * kap_ref[...])
    obonus_ref[...] = _seg_dot(r * km * rkp_ref[...], ones_ref) * v

    if not grouped:
        for ref, ops in ((kr_ref, (ret_q, ret_k)), (kg_ref, (gla_q, gk_ref[...], gla_al)),
                         (kw_ref, (r, km, dec, kk, kk * a)), (va_ref, (rv_ref[...], gv_ref[...], v))):
            cw = ops[0].shape[1]
            for j, o in enumerate(ops):
                ref[j * cw:(j + 1) * cw, :] = o.T
        return

    n_grp = LANES // (HEADS * b_n)

    def emit_tiles(ref, row0, ops, n_rows):
        wt = _heads_to_lanes(ops, b_n)
        for lt in range(wt.shape[1] // LANES):
            for g, zt in enumerate(_replicate_groups(wt[:n_rows, lt * LANES:(lt + 1) * LANES], n_grp)):
                q = lt * n_grp + g
                ref[row0:row0 + n_rows, q * LANES:(q + 1) * LANES] = zt

    emit_tiles(kr_ref, 0, (ret_q, ret_k), 2 * HEAD_D)
    emit_tiles(kg_ref, 0, (gla_q, gk_ref[...], gla_al), 3 * GLA_DK)
    emit_tiles(kw_ref, 0, (r, km), 2 * HEAD_D)
    emit_tiles(kw_ref, 2 * HEAD_D, (dec, kk), 2 * HEAD_D)
    emit_tiles(kw_ref, 4 * HEAD_D, (kk * a,), HEAD_D)
    va_ref[0:2 * HEAD_D, :] = _heads_to_lanes((rv_ref[...], gv_ref[...]), b_n)
    va_ref[2 * HEAD_D:3 * HEAD_D, :] = _heads_to_lanes((v,), b_n)[:HEAD_D]


def _scan_kernel(kr_ref, kg_ref, kw_ref, va_ref, dec_ref, s0r_ref, s0g_ref, s0w_ref,
                 o_ref, sor_ref, sog_ref, sow_ref,
                 sr, sg, sw, vs, os_, *, vh_n, grouped, tc, b_n):
    ti = pl.program_id(1)

    @pl.when(ti == 0)
    def _():
        sr[...] = s0r_ref[...]
        sg[...] = s0g_ref[...]
        sw[...] = s0w_ref[...]

    n_grp = LANES // (HEADS * b_n) if grouped else 1
    grp_w = LANES // n_grp
    n_v = 3

    def k_tile(ref, n_ops, j, t):
        lanes = pl.ds(pl.multiple_of(t * LANES, LANES), LANES)
        if grouped:
            k_n = ref.shape[0] // n_ops
            return ref[j * k_n:(j + 1) * k_n, lanes]
        return ref[j, 0, :, lanes]

    if grouped:
        grp = lax.broadcasted_iota(jnp.int32, (vh_n, LANES), 1) // grp_w
        for lt in range(tc // n_grp):
            lanes = slice(lt * LANES, (lt + 1) * LANES)
            for j in range(n_v):
                ws = [va_ref[j * HEAD_D + vl * vh_n:j * HEAD_D + (vl + 1) * vh_n, lanes] for vl in range(n_grp)]
                for g in range(n_grp):
                    tile = None
                    for vl in range(n_grp):
                        r = pltpu.roll(ws[vl], ((vl - g) * grp_w) % LANES, 1)
                        tile = r if tile is None else jnp.where(grp == vl, r, tile)
                    vs[lt * n_grp + g, j] = tile
    else:
        def copy_in(t, c):
            lanes = pl.ds(pl.multiple_of(t * LANES, LANES), LANES)
            for j in range(n_v):
                vs[t, j] = va_ref[j, 0, :, lanes]
            return c

        lax.fori_loop(0, tc, copy_in, 0)

    dec_r = dec_ref[...]

    def out(t, j, vh, s, q):
        os_[t, j, pl.ds(vh, 1), :] = jnp.sum(s * q, axis=0, keepdims=True)

    def ret_step(t, c):
        q, k = k_tile(kr_ref, 2, 0, t), k_tile(kr_ref, 2, 1, t)
        for vh in range(vh_n):
            s = sr[vh] * dec_r + vs[t, 0, pl.ds(vh, 1), :] * k
            sr[vh] = s
            out(t, 0, vh, s, q)
        return c

    def gla_step(t, c):
        q, k, al = (k_tile(kg_ref, 3, j, t) for j in range(3))
        for vh in range(vh_n):
            s = sg[vh] * al + vs[t, 1, pl.ds(vh, 1), :] * k
            sg[vh] = s
            out(t, 1, vh, s, q)
        return c

    def rwkv_step(t, c):
        q, k, dec, kk, beta = (k_tile(kw_ref, 5, j, t) for j in range(5))
        for vh in range(vh_n):
            s = sw[vh]
            sk = jnp.sum(s * kk, axis=0, keepdims=True)
            s = s * dec - sk * beta + vs[t, 2, pl.ds(vh, 1), :] * k
            sw[vh] = s
            out(t, 2, vh, s, q)
        return c

    lax.fori_loop(0, tc, ret_step, 0)
    lax.fori_loop(0, tc, gla_step, 0)
    lax.fori_loop(0, tc, rwkv_step, 0)

    if grouped:
        grp = lax.broadcasted_iota(jnp.int32, (vh_n, LANES), 1) // grp_w
        for lt in range(tc // n_grp):
            lanes = slice(lt * LANES, (lt + 1) * LANES)
            for j in range(n_v):
                og = [os_[lt * n_grp + g, j] for g in range(n_grp)]
                for vl in range(n_grp):
                    wv = None
                    for g in range(n_grp):
                        r = pltpu.roll(og[g], ((g - vl) * grp_w) % LANES, 1)
                        wv = r if wv is None else jnp.where(grp == g, r, wv)
                    o_ref[j * HEAD_D + vl * vh_n:j * HEAD_D + (vl + 1) * vh_n, lanes] = wv
    else:
        def copy_out(t, c):
            lanes = pl.ds(pl.multiple_of(t * LANES, LANES), LANES)
            for j in range(n_v):
                o_ref[j, 0, :, lanes] = os_[t, j]
            return c

        lax.fori_loop(0, tc, copy_out, 0)

    @pl.when(ti == pl.num_programs(1) - 1)
    def _():
        sor_ref[...] = sr[...]
        sog_ref[...] = sg[...]
        sow_ref[...] = sw[...]


def _scan(kr, kg, kw, va, b_n, t_n, dec, s0r, s0g, s0w):
    grouped = b_n * HEADS < LANES
    vh_n = s0r.shape[0]
    if grouped:
        tc = min(t_n, 32)
        n_grp = LANES // (HEADS * b_n)
        assert tc % n_grp == 0 and t_n % tc == 0
        n_l, n_t = 1, t_n // tc
        k_specs = [pl.BlockSpec((a.shape[0], tc * LANES), lambda l, t: (0, t)) for a in (kr, kg, kw)]
        v_spec = pl.BlockSpec((va.shape[0], tc * HEADS * b_n), lambda l, t: (0, t))
        o_spec = v_spec
    else:
        assert b_n == LANES
        tc, n_l, n_t = t_n, HEADS, 1
        k_specs = [pl.BlockSpec((a.shape[0], 1, a.shape[2], tc * LANES), lambda l, t: (0, l, 0, 0))
                   for a in (kr, kg, kw)]
        v_spec = pl.BlockSpec((3, 1, HEAD_D, tc * LANES), lambda l, t: (0, l, 0, 0))
        o_spec = v_spec

    def s_spec(a):
        return pl.BlockSpec(a.shape[:2] + (LANES,), lambda l, t: (0, 0, l))

    return pl.pallas_call(
        functools.partial(_scan_kernel, vh_n=vh_n, grouped=grouped, tc=tc, b_n=b_n),
        grid=(n_l, n_t),
        in_specs=k_specs + [v_spec, pl.BlockSpec((1, LANES), lambda l, t: (0, l)),
                            s_spec(s0r), s_spec(s0g), s_spec(s0w)],
        out_specs=[o_spec, s_spec(s0r), s_spec(s0g), s_spec(s0w)],
        out_shape=[jax.ShapeDtypeStruct(va.shape, F32)] + [jax.ShapeDtypeStruct(s.shape, F32)
                                                           for s in (s0r, s0g, s0w)],
        scratch_shapes=[pltpu.VMEM(s.shape[:2] + (LANES,), F32) for s in (s0r, s0g, s0w)]
                       + [pltpu.VMEM((tc, 3, vh_n, LANES), F32), pltpu.VMEM((tc, 3, vh_n, LANES), F32)],
        compiler_params=_cparams("parallel", "arbitrary"),
        name="scan_ret_gla_rwkv",
    )(kr, kg, kw, va, dec, s0r, s0g, s0w)


def _s5_kernel(u_ref, bb_ref, cc_ref, ar_ref, ai_ref, h0r_ref, h0i_ref,
               y_ref, hr_out, hi_out, hr_scr, hi_scr, xs_scr, hs_scr, *, b_n, tc):
    ti = pl.program_id(0)

    @pl.when(ti == 0)
    def _():
        hr_scr[...] = h0r_ref[...]
        hi_scr[...] = h0i_ref[...]

    xs_scr[...] = _bdot(u_ref[...], bb_ref[...])
    ar = ar_ref[...]
    ai = ai_ref[...]

    def step(t, carry):
        hr, hi = carry
        row = pl.multiple_of(t * b_n, SUBLANES)
        x = xs_scr[pl.ds(row, b_n), :]
        nr = ar * hr - ai * hi + x[:, :S5_CH]
        ni = ar * hi + ai * hr + x[:, S5_CH:]
        hs_scr[pl.ds(row, b_n), :S5_CH] = nr
        hs_scr[pl.ds(row, b_n), S5_CH:] = ni
        return nr, ni

    hr, hi = lax.fori_loop(0, tc, step, (hr_scr[...], hi_scr[...]))
    hr_scr[...] = hr
    hi_scr[...] = hi
    y_ref[...] = _bdot(hs_scr[...], cc_ref[...])

    @pl.when(ti == pl.num_programs(0) - 1)
    def _():
        hr_out[...] = hr
        hi_out[...] = hi


def _s5_scan(cols, row_off, bb, cc, ar, ai, h0r, h0i, b_n, t_n):
    tc = min(t_n, max(1, 1024 // b_n))
    rows = tc * b_n
    assert t_n % tc == 0 and row_off % rows == 0
    blk0 = row_off // rows
    return pl.pallas_call(
        functools.partial(_s5_kernel, b_n=b_n, tc=tc),
        grid=(t_n // tc,),
        in_specs=[pl.BlockSpec((rows, BRANCH_W), lambda t: (blk0 + t, COL_S5 // BRANCH_W)),
                  _const_spec(bb.shape), _const_spec(cc.shape),
                  _const_spec(ar.shape), _const_spec(ai.shape),
                  _const_spec(h0r.shape), _const_spec(h0i.shape)],
        out_specs=[pl.BlockSpec((rows, BRANCH_W), lambda t: (t, 0)),
                   _const_spec(h0r.shape), _const_spec(h0i.shape)],
        out_shape=[jax.ShapeDtypeStruct((t_n * b_n, BRANCH_W), F32),
                   jax.ShapeDtypeStruct(h0r.shape, F32),
                   jax.ShapeDtypeStruct(h0i.shape, F32)],
        scratch_shapes=[pltpu.VMEM((b_n, S5_CH), F32), pltpu.VMEM((b_n, S5_CH), F32),
                        pltpu.VMEM((rows, 2 * S5_CH), F32), pltpu.VMEM((rows, 2 * S5_CH), F32)],
        compiler_params=_cparams("arbitrary"),
        name="scan_s5",
    )(cols, bb, cc, ar, ai, h0r, h0i)


def _post_kernel(x_ref, mixt_ref, rg_ref, gg_ref, sy_ref, su_ref, wbon_ref, wg_ref,
                 gate0_ref, gate1_ref, gate2_ref, gate3_ref,
                 avg_ref, rgn_g, rgn_b, ggn_g, s5d_ref, wglu_ref, wgn_g, wgn_b,
                 wbr_ref, wo_ref, ln_g, ln_b, o_ref, *, alpha, b_n):
    def seg_mean(v):
        return _seg_dot(v, avg_ref)

    tm = x_ref.shape[0]
    if b_n * HEADS < LANES:
        yt = mixt_ref[...].T.reshape(tm // b_n, HEADS, b_n, 3 * HEAD_D)
        per_head = [yt[:, h].reshape(tm, 3 * HEAD_D) for h in range(HEADS)]
        mix = jnp.concatenate([ph[:, j * HEAD_D:(j + 1) * HEAD_D] for j in range(3) for ph in per_head], axis=1)
    else:
        mix = mixt_ref[...].T
    ro = mix[:, 0:BRANCH_W]
    mu = seg_mean(ro)
    rc = ro - mu
    var = seg_mean(rc * rc)
    b0 = (rc * lax.rsqrt(var + LN_EPS) * rgn_g[...] + rgn_b[...]) * _silu(rg_ref[...])
    go = mix[:, BRANCH_W:2 * BRANCH_W]
    ms = seg_mean(go * go)
    b1 = go * lax.rsqrt(ms + LN_EPS) * ggn_g[...] * _silu(gg_ref[...])
    y = jax.nn.gelu(sy_ref[...] + s5d_ref[...] * su_ref[...])
    b2 = y * _sigmoid(_bdot(y, wglu_ref[...]))
    wy = mix[:, 2 * BRANCH_W:3 * BRANCH_W]
    mu = seg_mean(wy)
    wc = wy - mu
    var = seg_mean(wc * wc)
    b3 = (wc * lax.rsqrt(var + RWKV_GN_EPS) * wgn_g[...] + wgn_b[...] + wbon_ref[...]) * wg_ref[...]

    m = None
    gates = (gate0_ref, gate1_ref, gate2_ref, gate3_ref)
    for i, br in enumerate((b0, b1, b2, b3)):
        term = _bdot(br, wbr_ref[i]) * _sigmoid(gates[i][...])
        m = term if m is None else m + term
    h = _bdot(m, wo_ref[...])
    o_ref[...] = _layer_norm(alpha * x_ref[...] + h, ln_g[...], ln_b[...])


def _embed_ln2(x, f, p_ref, wpe_ref, wpg_ref, ln_g, ln_b, alpha):
    e = _bdot(p_ref[...], wpe_ref[...]) * _sigmoid(_bdot(x, wpg_ref[...]))
    return _layer_norm(alpha * x + f + e, ln_g[...], ln_b[...])


def _ffn_kernel(x_ref, p_ref, w1_ref, w3_ref, w2_ref, wpe_ref, wpg_ref, ln_g, ln_b, o_ref, *, alpha):
    x = x_ref[...]
    xb = x.astype(BF16)
    h = _silu(jnp.dot(xb, w1_ref[...], preferred_element_type=F32)) * jnp.dot(
        xb, w3_ref[...], preferred_element_type=F32)
    f = _bdot(h, w2_ref[...])
    o_ref[...] = _embed_ln2(x, f, p_ref, wpe_ref, wpg_ref, ln_g, ln_b, alpha)


def _router_kernel(x_ref, wh_ref, wl_ref, idx_ref, wgt_ref):
    xh, xl = _split_bf16(x_ref[...])
    wh = wh_ref[...]
    logits = (jnp.dot(xh, wh, preferred_element_type=F32)
              + jnp.dot(xl, wh, preferred_element_type=F32)
              + jnp.dot(xh, wl_ref[...], preferred_element_type=F32))
    col = lax.broadcasted_iota(jnp.int32, logits.shape, 1)
    neg = jnp.float32(-jnp.inf)
    lg = jnp.where(col < N_EXPERTS, logits, neg)
    m1 = jnp.max(lg, axis=1, keepdims=True)
    i1 = jnp.min(jnp.where(lg == m1, col, LANES), axis=1, keepdims=True)
    lg2 = jnp.where(col == i1, neg, lg)
    m2 = jnp.max(lg2, axis=1, keepdims=True)
    i2 = jnp.min(jnp.where(lg2 == m2, col, LANES), axis=1, keepdims=True)
    e2 = jnp.exp(m2 - m1)
    den = 1.0 + e2
    idx_ref[...] = jnp.where(col == 0, i1, jnp.where(col == 1, i2, 0))
    wgt_ref[...] = jnp.where(col == 0, 1.0 / den, jnp.where(col == 1, e2 / den, 0.0))


def _row_copies(idx_ref, n_rows, make_copy, skip_negative=False):
    def each(action):
        def body(r, c):
            row = idx_ref[0, 0, r]
            if skip_negative:
                pl.when(row >= 0)(lambda: action(make_copy(r, row)))
            else:
                action(make_copy(r, row))
            return c
        lax.fori_loop(0, n_rows, body, 0, unroll=8)

    return (lambda: each(lambda cp: cp.start())), (lambda: each(lambda cp: cp.wait()))


def _moe_block_kernel(be_ref, nb_ref, tok_ref, tok_next_ref, dst_ref, dst_m1_ref, dst_m2_ref,
                      x_hbm, w1_ref, w3_ref, w2_ref, out_hbm, xbuf, ybuf, sem_in, sem_out):
    j = pl.program_id(0)
    last = pl.num_programs(0) - 1
    nb = nb_ref[0]
    slot = j % 2

    def gather(idx_ref, half):
        return _row_copies(idx_ref, MOE_BLOCK, lambda r, row: pltpu.make_async_copy(
            x_hbm.at[pl.ds(row, 1)], xbuf.at[half, pl.ds(r, 1)], sem_in.at[half]))

    def scatter(idx_ref, half):
        return _row_copies(idx_ref, MOE_BLOCK, lambda r, row: pltpu.make_async_copy(
            ybuf.at[half, pl.ds(r, 1)], out_hbm.at[pl.ds(row, 1)], sem_out.at[half]), skip_negative=True)

    @pl.when(jnp.logical_and(j == 0, nb > 0))
    def _():
        gather(tok_ref, slot)[0]()

    @pl.when(j + 1 < nb)
    def _():
        gather(tok_next_ref, 1 - slot)[0]()

    @pl.when(jnp.logical_and(j >= 2, j - 2 < nb))
    def _():
        scatter(dst_m2_ref, slot)[1]()

    @pl.when(j < nb)
    def _():
        gather(tok_ref, slot)[1]()
        xb = xbuf[slot].astype(BF16)
        h = _silu(jnp.dot(xb, w1_ref[0], preferred_element_type=F32)) * jnp.dot(
            xb, w3_ref[0], preferred_element_type=F32)
        ybuf[slot] = _bdot(h, w2_ref[0])
        scatter(dst_ref, slot)[0]()

    @pl.when(jnp.logical_and(j == last, jnp.logical_and(j >= 1, j - 1 < nb)))
    def _():
        scatter(dst_m1_ref, 1 - slot)[1]()

    @pl.when(jnp.logical_and(j == last, j < nb))
    def _():
        scatter(dst_ref, slot)[1]()


def _moe_blocks(x, block_e, nb_used, slot_tok, slot_dst, n_out_rows, w1, w3, w2):
    n_blocks = block_e.shape[0]
    d = x.shape[1]
    dff = w1.shape[2]
    tok = slot_tok.reshape(n_blocks, 1, MOE_BLOCK)
    dst = slot_dst.reshape(n_blocks, 1, MOE_BLOCK)

    def idx_spec(shift):
        return pl.BlockSpec((1, 1, MOE_BLOCK),
                            lambda j, be, nb: (jnp.clip(j + shift, 0, n_blocks - 1), 0, 0), memory_space=pltpu.SMEM)

    grid_spec = pltpu.PrefetchScalarGridSpec(
        num_scalar_prefetch=2,
        grid=(n_blocks,),
        in_specs=[
            idx_spec(0), idx_spec(1), idx_spec(0), idx_spec(-1), idx_spec(-2),
            pl.BlockSpec(memory_space=pl.ANY),
            pl.BlockSpec((1, d, dff), lambda j, be, nb: (be[j], 0, 0), pipeline_mode=pl.Buffered(1)),
            pl.BlockSpec((1, d, dff), lambda j, be, nb: (be[j], 0, 0), pipeline_mode=pl.Buffered(1)),
            pl.BlockSpec((1, dff, d), lambda j, be, nb: (be[j], 0, 0), pipeline_mode=pl.Buffered(1)),
        ],
        out_specs=pl.BlockSpec(memory_space=pl.ANY),
        scratch_shapes=[pltpu.VMEM((2, MOE_BLOCK, d), F32), pltpu.VMEM((2, MOE_BLOCK, d), F32),
                        pltpu.SemaphoreType.DMA((2,)), pltpu.SemaphoreType.DMA((2,))],
    )
    return pl.pallas_call(
        _moe_block_kernel,
        grid_spec=grid_spec,
        out_shape=jax.ShapeDtypeStruct((n_out_rows, d), F32),
        compiler_params=_cparams("arbitrary"),
        name="moe_blocks",
    )(block_e, nb_used, tok, tok, dst, dst, dst, x, w1, w3, w2)


def _moe_combine_kernel(y0_ref, y1_ref, x_ref, p_ref, wgt_ref, wpe_ref, wpg_ref, ln_g, ln_b, o_ref, *, alpha):
    wgt = wgt_ref[...]
    f = y0_ref[...] * wgt[:, 0:1] + y1_ref[...] * wgt[:, 1:2]
    o_ref[...] = _embed_ln2(x_ref[...], f, p_ref, wpe_ref, wpg_ref, ln_g, ln_b, alpha)


def _lane_groups(b_n):
    bh = b_n * HEADS
    return LANES // bh if bh < LANES else 1


def _state_to_lanes(s, vl_n, value_last):
    s = s.transpose(3, 2, 1, 0) if value_last else s.transpose(2, 3, 1, 0)
    v_n, k_n, h_n, b_n = s.shape
    s = s.reshape(vl_n, v_n // vl_n, k_n, h_n, b_n).transpose(1, 2, 0, 3, 4)
    return s.reshape(v_n // vl_n, k_n, vl_n * h_n * b_n)


def _state_from_lanes(s, b_n, vl_n, value_last):
    vh_n, k_n, _ = s.shape
    s = s.reshape(vh_n, k_n, vl_n, HEADS, b_n).transpose(4, 3, 1, 2, 0).reshape(b_n, HEADS, k_n, vl_n * vh_n)
    return s if value_last else s.transpose(0, 1, 3, 2)


def _rotary_tables(pos, b_n):
    half = HEAD_D // 2
    freq = ROPE_BASE ** (-jnp.arange(half, dtype=F32) / half)
    ang = pos.astype(F32)[:, None] * freq[None, :]
    cos, sin = jnp.cos(ang), jnp.sin(ang)
    cos_h = jnp.tile(jnp.concatenate([cos, cos], axis=-1), (1, HEADS))
    sin_h = jnp.tile(jnp.concatenate([-sin, sin], axis=-1), (1, HEADS))
    return jnp.repeat(cos_h, b_n, axis=0), jnp.repeat(sin_h, b_n, axis=0)


def _block_diag_const(block, n_blocks):
    return jnp.kron(jnp.eye(n_blocks, dtype=F32), jnp.full((block, block), 1.0, F32))


def _s5_params(log_dt, a_re, a_im, b_re, b_im, c_re, c_im):
    dt = jnp.exp(log_dt)[:, None]
    mag = jnp.exp(dt * a_re)
    ang = dt * a_im
    abar_re, abar_im = mag * jnp.cos(ang), mag * jnp.sin(ang)
    den = a_re * a_re + a_im * a_im
    n_re = abar_re - 1.0
    f_re = (n_re * a_re + abar_im * a_im) / den
    f_im = (abar_im * a_re - n_re * a_im) / den
    bb_re = f_re[..., None] * b_re - f_im[..., None] * b_im
    bb_im = f_re[..., None] * b_im + f_im[..., None] * b_re
    eye = jnp.eye(S5_GROUPS, dtype=F32)

    def in_map(bb):
        return jnp.einsum("gpc,gh->gchp", bb, eye).reshape(BRANCH_W, S5_CH)

    def out_map(cm):
        return jnp.einsum("gcp,gh->gphc", cm, eye).reshape(S5_CH, BRANCH_W)

    bb = jnp.concatenate([in_map(bb_re), in_map(bb_im)], axis=1).astype(BF16)
    cc = jnp.concatenate([out_map(c_re), -out_map(c_im)], axis=0).astype(BF16)
    return bb, cc, abar_re.reshape(1, S5_CH), abar_im.reshape(1, S5_CH)


def _reorder_w_in(w_in):
    d_model = w_in.shape[0]
    main = jnp.concatenate([w_in[:, :1792], w_in[:, 1808:]], axis=1)
    tail = jnp.concatenate([w_in[:, 1792:1808],
                            jnp.zeros((d_model, D_IN_PAD - COL_GLA_R - GLA_GATE_RANK), w_in.dtype)], axis=1)
    return jnp.concatenate([main, tail], axis=1).astype(BF16)


def _pad_cols(w, n):
    return jnp.pad(w, ((0, 0), (0, n - w.shape[1])))


def _pad_rows(w, n):
    return jnp.pad(w, ((0, n - w.shape[0]), (0, 0)))


def _moe_route(idx, n):
    nk = n * TOP_K
    flat_e = idx.reshape(nk)
    onehot = (flat_e[:, None] == jnp.arange(N_EXPERTS, dtype=jnp.int32)[None, :]).astype(jnp.int32)
    incl = jnp.cumsum(onehot, axis=0)
    counts = incl[-1]
    rank = jnp.sum((incl - onehot) * onehot, axis=1)
    padded = (counts + MOE_BLOCK - 1) // MOE_BLOCK * MOE_BLOCK
    pad_end = jnp.cumsum(padded)
    slot = (pad_end - padded)[flat_e] + rank
    n_blocks = -(-(nk + N_EXPERTS * (MOE_BLOCK - 1)) // MOE_BLOCK)
    cap = n_blocks * MOE_BLOCK
    pair = jnp.full((cap,), -1, jnp.int32).at[slot].set(jnp.arange(nk, dtype=jnp.int32))
    real = pair >= 0
    slot_tok = jnp.where(real, pair // TOP_K, 0)
    slot_dst = jnp.where(real, (pair % TOP_K) * n + pair // TOP_K, -1)
    block_start = jnp.arange(n_blocks, dtype=jnp.int32) * MOE_BLOCK
    block_e = jnp.minimum(jnp.sum((pad_end[None, :] <= block_start[:, None]).astype(jnp.int32), axis=1),
                          N_EXPERTS - 1).astype(jnp.int32)
    nb_used = (pad_end[-1] // MOE_BLOCK).astype(jnp.int32).reshape(1)
    return slot_tok, slot_dst, block_e, nb_used, nk


def kernel(x_prompt, x_sample, state_ret, state_gla, state_s5_re, state_s5_im, state_rwkv, state_shift,
           p_prompt, p_sample, w_in, ret_gn_g, ret_gn_b, gla_wg2, gla_bg, gla_gn,
           s5_log_dt, s5_a_re, s5_a_im, s5_b_re, s5_b_im, s5_c_re, s5_c_im, s5_d, s5_w_glu,
           rwkv_mu, rwkv_w0, rwkv_w1, rwkv_w2, rwkv_a0, rwkv_a1, rwkv_a2, rwkv_g1, rwkv_g2,
           rwkv_kk, rwkv_ka, rwkv_rk, rwkv_gn_g, rwkv_gn_b, w_branch, w_o,
           ln1_g, ln1_b, ln2_g, ln2_b, w_pe, w_pg, ffn_w1, ffn_w3, ffn_w2,
           moe_router, moe_w1, moe_w3, moe_w2):
    depth = w_in.shape[0]
    bp, tp, d_model = x_prompt.shape
    bs, ts, _ = x_sample.shape
    n_p, n_s = bp * tp, bs * ts
    n = n_p + n_s
    tm = ROW_TILE
    assert n_p % tm == 0 and n_s % tm == 0 and tm % bp == 0 and tm % bs == 0 and n_p % bs == 0
    alpha = (2 * depth) ** 0.25
    groups = [(0, bp, tp), (n_p, bs, ts)]
    w4 = 4 * BRANCH_W

    def time_major(a_p, a_s):
        return jnp.concatenate([a_p.transpose(1, 0, 2).reshape(n_p, -1),
                                a_s.transpose(1, 0, 2).reshape(n_s, -1)], axis=0)

    x = time_major(x_prompt, x_sample)
    cos_p, sin_p = _rotary_tables(jnp.arange(tp, dtype=jnp.int32), bp)
    cos_s, sin_s = _rotary_tables(PAST_LEN + jnp.arange(ts, dtype=jnp.int32), bs)
    ones_bd = _block_diag_const(HEAD_D, HEADS).astype(BF16)
    avg_bd = (_block_diag_const(HEAD_D, HEADS) / HEAD_D).astype(BF16)
    gamma = 1.0 - jnp.exp2(-5.0 - jnp.arange(HEADS, dtype=F32))
    row = lambda v: v.reshape(1, -1)

    new = [[] for _ in range(6)]
    for i in range(depth):
        cols = _matmul(x, _reorder_w_in(w_in[i]), tm=512, tn=D_IN_PAD // 3, name="in_proj")

        prep_consts = [
            _pad_rows(_pad_cols(gla_wg2[i], LANES), LANES).astype(BF16), row(gla_bg[i]),
            rwkv_mu[i], row(rwkv_w0[i]),
            _pad_cols(rwkv_w1[i], LANES).astype(BF16), _pad_rows(rwkv_w2[i], LANES).astype(BF16),
            row(rwkv_a0[i]),
            _pad_cols(rwkv_a1[i], LANES).astype(BF16), _pad_rows(rwkv_a2[i], LANES).astype(BF16),
            _pad_cols(rwkv_g1[i], LANES).astype(BF16), _pad_rows(rwkv_g2[i], LANES).astype(BF16),
            row(rwkv_kk[i]), row(rwkv_ka[i]), row(rwkv_rk[i]), ones_bd,
        ]
        post_consts = [avg_bd, row(ret_gn_g[i]), row(ret_gn_b[i]), row(gla_gn[i]), row(s5_d[i]),
                       s5_w_glu[i].astype(BF16), row(rwkv_gn_g[i]), row(rwkv_gn_b[i]),
                       w_branch[i].astype(BF16), w_o[i].astype(BF16), row(ln1_g[i]), row(ln1_b[i])]
        s5p = _s5_params(s5_log_dt[i], s5_a_re[i], s5_a_im[i], s5_b_re[i], s5_b_im[i], s5_c_re[i], s5_c_im[i])
        rwkv_blk = COL_RWKV // w4
        x1_parts, layer_new = [], []
        for gi, (off, b_n, t_n) in enumerate(groups):
            n_g = b_n * t_n
            t0 = off // tm
            if gi == 0:
                st = [jnp.zeros((b_n,) + s.shape[2:], s.dtype)
                      for s in (state_ret, state_gla, state_s5_re, state_s5_im, state_rwkv)]
                shift0, cos_g, sin_g = jnp.zeros((b_n, w4), F32), cos_p, sin_p
            else:
                st = [state_ret[i], state_gla[i], state_s5_re[i], state_s5_im[i], state_rwkv[i]]
                shift0, cos_g, sin_g = state_shift[i], cos_s, sin_s
            vl_n = _lane_groups(b_n)
            grouped = vl_n > 1

            def cspec(width, cb, t0=t0):
                return pl.BlockSpec((tm, width), lambda r: (r + t0, cb))

            if grouped:
                t_shapes = [(2 * HEAD_D, n_g * vl_n * HEADS), (3 * GLA_DK, n_g * vl_n * HEADS),
                            (5 * HEAD_D, n_g * vl_n * HEADS), (3 * HEAD_D, n_g * HEADS)]
            else:
                t_shapes = [(2 * BRANCH_W, n_g), (3 * GLA_QK, n_g), (5 * BRANCH_W, n_g), (3 * BRANCH_W, n_g)]
            consts = [shift0] + prep_consts
            ka_ret, ka_gla, ka_rwkv, va, obonus, og = pl.pallas_call(
                functools.partial(_prep_kernel, grouped=grouped),
                grid=(n_g // tm,),
                in_specs=[cspec(BRANCH_W, 0), cspec(BRANCH_W, 1), cspec(BRANCH_W, 2),
                          _row_spec(tm, BRANCH_W, 0), _row_spec(tm, BRANCH_W, 0),
                          cspec(GLA_QK, COL_GLA_Q // GLA_QK), cspec(GLA_QK, COL_GLA_K // GLA_QK),
                          cspec(BRANCH_W, COL_GLA_V // BRANCH_W), cspec(LANES, COL_GLA_R // LANES),
                          cspec(w4, rwkv_blk),
                          pl.BlockSpec((b_n, w4), lambda r, t0=t0, b_n=b_n:
                                       (jnp.maximum((r + t0) * (tm // b_n) - 1, 0), rwkv_blk))]
                         + [_const_spec(a.shape) for a in consts],
                out_specs=[pl.BlockSpec((sh[0], sh[1] // (n_g // tm)), lambda r: (0, r)) for sh in t_shapes]
                          + [_row_spec(tm, BRANCH_W, 0), _row_spec(tm, BRANCH_W, 0)],
                out_shape=[jax.ShapeDtypeStruct(sh, F32) for sh in t_shapes]
                          + [jax.ShapeDtypeStruct((n_g, BRANCH_W), F32)] * 2,
                compiler_params=_cparams("parallel"),
                name="mixer_prep",
            )(cols, cols, cols, cos_g, sin_g, cols, cols, cols, cols, cols, cols, *consts)

            dec = jnp.tile(jnp.repeat(gamma, b_n), vl_n).reshape(1, vl_n * HEADS * b_n)
            if not grouped:
                ka_ret = ka_ret.reshape(2, HEADS, HEAD_D, n_g)
                ka_gla = ka_gla.reshape(3, HEADS, GLA_DK, n_g)
                ka_rwkv = ka_rwkv.reshape(5, HEADS, HEAD_D, n_g)
                va = va.reshape(3, HEADS, HEAD_D, n_g)
            o, s_ret, s_gla, s_rwkv = _scan(
                ka_ret, ka_gla, ka_rwkv, va, b_n, t_n, dec,
                _state_to_lanes(st[0], vl_n, True), _state_to_lanes(st[1], vl_n, True),
                _state_to_lanes(st[4], vl_n, False))
            mixt = o if grouped else o.reshape(3 * BRANCH_W, n_g)
            y, hr, hi = _s5_scan(cols, off, *s5p, st[2].reshape(b_n, S5_CH), st[3].reshape(b_n, S5_CH), b_n, t_n)
            shift_new = cols[off + (t_n - 1) * b_n:off + t_n * b_n, COL_RWKV:COL_RWKV + w4]
            layer_new.append((_state_from_lanes(s_ret, b_n, vl_n, True), _state_from_lanes(s_gla, b_n, vl_n, True),
                              hr.reshape(b_n, S5_GROUPS, S5_STATE), hi.reshape(b_n, S5_GROUPS, S5_STATE),
                              _state_from_lanes(s_rwkv, b_n, vl_n, False), shift_new))

            x1_parts.append(pl.pallas_call(
                functools.partial(_post_kernel, alpha=alpha, b_n=b_n),
                grid=(n_g // tm,),
                in_specs=[cspec(d_model, 0),
                          pl.BlockSpec((mixt.shape[0], mixt.shape[1] // (n_g // tm)), lambda r: (0, r)),
                          cspec(BRANCH_W, 3), cspec(BRANCH_W, COL_GLA_G // BRANCH_W),
                          _row_spec(tm, BRANCH_W, 0), cspec(BRANCH_W, COL_S5 // BRANCH_W),
                          _row_spec(tm, BRANCH_W, 0), _row_spec(tm, BRANCH_W, 0)]
                         + [cspec(d_model, COL_GATE // d_model + q) for q in range(N_BRANCH)]
                         + [_const_spec(a.shape) for a in post_consts],
                out_specs=_row_spec(tm, d_model, 0),
                out_shape=jax.ShapeDtypeStruct((n_g, d_model), F32),
                compiler_params=_cparams("parallel"),
                name="mixer_post",
            )(x, mixt, cols, cols, y, cols, obonus, og, cols, cols, cols, cols, *post_consts))
        for lst, pair in zip(new, zip(*layer_new)):
            lst.append(pair)
        x1 = jnp.concatenate(x1_parts, axis=0)

        p = time_major(p_prompt[i], p_sample[i])
        tail_consts = [w_pe[i].astype(BF16), w_pg[i].astype(BF16), row(ln2_g[i]), row(ln2_b[i])]
        j = i // 2
        if i % 2 == 0:
            (x,) = _rowwise(
                functools.partial(_ffn_kernel, alpha=alpha), n,
                [(x1, d_model, 0), (p, p.shape[1], 0)],
                [ffn_w1[j].astype(BF16), ffn_w3[j].astype(BF16), ffn_w2[j].astype(BF16)] + tail_consts,
                [d_model], name="ffn")
        else:
            rh, rl = _split_bf16(_pad_cols(moe_router[j], LANES))
            idx, wgt = _rowwise(_router_kernel, n, [(x1, d_model, 0)], [rh, rl], [LANES, LANES],
                                out_dtypes=[jnp.int32, F32], name="moe_router")
            slot_tok, slot_dst, block_e, nb_used, n_out_rows = _moe_route(idx[:, :TOP_K], n)
            yk = _moe_blocks(x1, block_e, nb_used, slot_tok, slot_dst, n_out_rows,
                             moe_w1[j].astype(BF16), moe_w3[j].astype(BF16), moe_w2[j].astype(BF16))
            (x,) = _rowwise(
                functools.partial(_moe_combine_kernel, alpha=alpha), n,
                [(yk, d_model, 0), (yk, d_model, 0, n // tm), (x1, d_model, 0), (p, p.shape[1], 0), (wgt, LANES, 0)],
                tail_consts, [d_model], name="moe_combine")

    y_prompt = x[:n_p].reshape(tp, bp, d_model).transpose(1, 0, 2)
    y_sample = x[n_p:].reshape(ts, bs, d_model).transpose(1, 0, 2)
    outs = [y_prompt, y_sample]
    for lst in new:
        outs.append(jnp.stack([pair[0] for pair in lst], 0))
        outs.append(jnp.stack([pair[1] for pair in lst], 0))
    return tuple(outs)
```

```python
import functools

import jax
import jax.numpy as jnp
from jax import lax
from jax.experimental import pallas as pl
from jax.experimental.pallas import tpu as pltpu

F32 = jnp.float32
BF16 = jnp.bfloat16

LANES = 128
SUBLANES = 8
VMEM_LIMIT = 56 * 1024 * 1024

N_BRANCH = 4
BRANCH_W = 256
HEADS = 4
HEAD_D = 64
GLA_DK = 32
GLA_QK = HEADS * GLA_DK
GLA_GATE_RANK = 16
GLA_GATE_NORM = 16.0
S5_GROUP = 16
S5_GROUPS = 16
S5_STATE = 64
S5_CH = S5_GROUPS * S5_STATE
ROPE_BASE = 10000.0
RWKV_GN_EPS = 64e-5
LN_EPS = 1e-5
N_EXPERTS = 8
TOP_K = 2
MOE_BLOCK = 256
ROW_TILE = 256
PAST_LEN = 16384

COL_RET = 0
COL_GLA_Q = 1024
COL_GLA_K = 1152
COL_GLA_V = 1280
COL_GLA_G = 1536
COL_S5 = 1792
COL_RWKV = 2048
COL_GATE = 3072
COL_GLA_R = 7168
D_IN_PAD = 7296


def _cparams(*sem):
    return pltpu.CompilerParams(dimension_semantics=sem, vmem_limit_bytes=VMEM_LIMIT)


def _split_bf16(x):
    hi = x.astype(BF16)
    lo = (x - hi.astype(F32)).astype(BF16)
    return hi, lo


def _seg_dot(x, m_ref):
    hi, lo = _split_bf16(x)
    m = m_ref[...]
    return (jnp.dot(hi, m, preferred_element_type=F32)
            + jnp.dot(lo, m, preferred_element_type=F32))


def _bdot(x, w):
    return jnp.dot(x.astype(BF16), w, preferred_element_type=F32)


def _sigmoid(x):
    return 1.0 / (1.0 + jnp.exp(-x))


def _silu(x):
    return x * _sigmoid(x)


def _log1p_exp_neg_abs(x):
    return jnp.log1p(jnp.exp(-jnp.abs(x)))


def _layer_norm(x, g, b):
    mu = jnp.mean(x, axis=-1, keepdims=True)
    xc = x - mu
    var = jnp.mean(xc * xc, axis=-1, keepdims=True)
    return xc * lax.rsqrt(var + LN_EPS) * g + b


def _row_spec(tm, width, col_block, row_block0=0):
    return pl.BlockSpec((tm, width), lambda i, cb=col_block, r0=row_block0: (i + r0, cb))


def _const_spec(shape):
    nd = len(shape)
    return pl.BlockSpec(shape, lambda i, nd=nd: (0,) * nd)


def _rowwise(body, n_rows, row_in, const_in, out_widths, out_dtypes=None, tm=ROW_TILE, name=None):
    assert n_rows % tm == 0
    out_dtypes = out_dtypes or [F32] * len(out_widths)
    in_specs = [_row_spec(tm, *spec[1:]) for spec in row_in] + [_const_spec(a.shape) for a in const_in]
    out_specs = [_row_spec(tm, w, 0) for w in out_widths]
    out_shape = [jax.ShapeDtypeStruct((n_rows, w), dt) for w, dt in zip(out_widths, out_dtypes)]
    return pl.pallas_call(
        body,
        grid=(n_rows // tm,),
        in_specs=in_specs,
        out_specs=out_specs,
        out_shape=out_shape,
        compiler_params=_cparams("parallel"),
        name=name,
    )(*[spec[0] for spec in row_in], *const_in)


def _matmul_kernel(x_ref, w_ref, o_ref):
    o_ref[...] = _bdot(x_ref[...], w_ref[...]).astype(o_ref.dtype)


def _matmul(x, w, tm=512, tn=None, out_dtype=F32, name=None):
    m, k = x.shape
    n = w.shape[1]
    tn = tn or n
    tm = tm if m % tm == 0 else ROW_TILE
    assert m % tm == 0 and n % tn == 0
    return pl.pallas_call(
        _matmul_kernel,
        grid=(n // tn, m // tm),
        in_specs=[pl.BlockSpec((tm, k), lambda j, i: (i, 0)),
                  pl.BlockSpec((k, tn), lambda j, i: (0, j))],
        out_specs=pl.BlockSpec((tm, tn), lambda j, i: (i, j)),
        out_shape=jax.ShapeDtypeStruct((m, n), out_dtype),
        compiler_params=_cparams("parallel", "parallel"),
        name=name,
    )(x, w)


def _replicate_groups(w, n_grp):
    if n_grp == 1:
        return [w]
    grp = lax.broadcasted_iota(jnp.int32, w.shape, 1) // (LANES // n_grp)
    parts = [w]
    span = n_grp
    while span > 1:
        half = span // 2
        shift = half * (LANES // n_grp)
        low = (grp % span) < half
        nxt = []
        for z in parts:
            rz = pltpu.roll(z, shift, 1)
            nxt.append(jnp.where(low, z, rz))
            nxt.append(jnp.where(low, rz, z))
        parts = nxt
        span = half
    return parts


def _heads_to_lanes(ops, b_n, pad_to=LANES):
    tm = ops[0].shape[0]
    c = ops[0].shape[1] // HEADS
    packed = []
    for h in range(HEADS):
        pieces = [o[:, h * c:(h + 1) * c] for o in ops]
        if len(pieces) * c < pad_to:
            pieces.append(jnp.zeros((tm, pad_to - len(pieces) * c), F32))
        packed.append(jnp.concatenate(pieces, axis=1).reshape(tm // b_n, b_n, pad_to))
    y = jnp.stack(packed, axis=1).reshape(tm * HEADS, pad_to)
    return y.T


def _prep_kernel(rq_ref, rk_ref, rv_ref, cos_ref, sin_ref,
                 gq_ref, gk_ref, gv_ref, gr_ref,
                 c_ref, tail_ref, shift_ref, wg2_ref, bg_ref,
                 mu_ref, w0_ref, w1_ref, w2_ref, a0_ref, a1_ref, a2_ref, g1_ref, g2_ref,
                 kkp_ref, kap_ref, rkp_ref, ones_ref,
                 kr_ref, kg_ref, kw_ref, va_ref, obonus_ref, og_ref, *, grouped):
    i = pl.program_id(0)
    w = BRANCH_W
    cos = cos_ref[...]
    sin = sin_ref[...]
    lane = lax.broadcasted_iota(jnp.int32, cos.shape, 1)
    first_half = (lane % HEAD_D) < (HEAD_D // 2)

    def rot(x):
        partner = jnp.where(first_half,
                            pltpu.roll(x, BRANCH_W - HEAD_D // 2, 1),
                            pltpu.roll(x, HEAD_D // 2, 1))
        return x * cos + partner * sin

    ret_q = rot(rq_ref[...])
    ret_k = rot(rk_ref[...]) * (HEAD_D ** -0.5)

    z = _bdot(gr_ref[...], wg2_ref[...]) + bg_ref[...]
    glog = (jnp.minimum(z, 0.0) - _log1p_exp_neg_abs(z)) / GLA_GATE_NORM
    gla_q = gq_ref[...] * (GLA_DK ** -0.5)
    gla_al = jnp.exp(glog)

    c = c_ref[...]
    tm = c.shape[0]
    b_n = tail_ref.shape[0]
    tail = jnp.where(i == 0, shift_ref[...], tail_ref[...])
    d = jnp.concatenate([tail, c[:tm - b_n]], axis=0) - c
    mu = mu_ref[...]
    cr, ck, cv, cz = (c[:, j * w:(j + 1) * w] for j in range(4))
    dr, dk, dv, dz = (d[:, j * w:(j + 1) * w] for j in range(4))
    r = cr + dr * mu[0:1]
    k = ck + dk * mu[1:2]
    v = cv + dv * mu[2:3]
    zw = cz + dz * mu[3:4]
    za = cz + dz * mu[4:5]
    zg = cz + dz * mu[5:6]
    w_raw = w0_ref[...] + _bdot(jnp.tanh(_bdot(zw, w1_ref[...])), w2_ref[...])
    sp = jnp.maximum(-w_raw, 0.0) + _log1p_exp_neg_abs(w_raw)
    dec = jnp.exp(-jnp.exp(-sp - 0.5))
    a = _sigmoid(a0_ref[...] + _bdot(_bdot(za, a1_ref[...]), a2_ref[...]))
    og_ref[...] = _bdot(_sigmoid(_bdot(zg, g1_ref[...])), g2_ref[...])
    kk = k * kkp_ref[...]
    ss = _seg_dot(kk * kk, ones_ref)
    kk = kk * lax.rsqrt(jnp.maximum(ss, 1e-24))
    km = k * (1.0 + (a - 1.0) * kap_ref[...])
    obonus_ref[...] = _seg_dot(r * km * rkp_ref[...], ones_ref) * v

    if not grouped:
        for ref, ops in ((kr_ref, (ret_q, ret_k)), (kg_ref, (gla_q, gk_ref[...], gla_al)),
                         (kw_ref, (r, km, dec, kk, kk * a)), (va_ref, (rv_ref[...], gv_ref[...], v))):
            cw = ops[0].shape[1]
            for j, o in enumerate(ops):
                ref[j * cw:(j + 1) * cw, :] = o.T
        return

    n_grp = LANES // (HEADS * b_n)

    def emit_tiles(ref, row0, ops, n_rows):
        wt = _heads_to_lanes(ops, b_n)
        for lt in range(wt.shape[1] // LANES):
            for g, zt in enumerate(_replicate_groups(wt[:n_rows, lt * LANES:(lt + 1) * LANES], n_grp)):
                q = lt * n_grp + g
                ref[row0:row0 + n_rows, q * LANES:(q + 1) * LANES] = zt

    emit_tiles(kr_ref, 0, (ret_q, ret_k), 2 * HEAD_D)
    emit_tiles(kg_ref, 0, (gla_q, gk_ref[...], gla_al), 3 * GLA_DK)
    emit_tiles(kw_ref, 0, (r, km), 2 * HEAD_D)
    emit_tiles(kw_ref, 2 * HEAD_D, (dec, kk), 2 * HEAD_D)
    emit_tiles(kw_ref, 4 * HEAD_D, (kk * a,), HEAD_D)
    va_ref[0:2 * HEAD_D, :] = _heads_to_lanes((rv_ref[...], gv_ref[...]), b_n)
    va_ref[2 * HEAD_D:3 * HEAD_D, :] = _heads_to_lanes((v,), b_n)[:HEAD_D]


def _scan_kernel(kr_ref, kg_ref, kw_ref, va_ref, dec_ref, s0r_ref, s0g_ref, s0w_ref,
                 o_ref, sor_ref, sog_ref, sow_ref,
                 sr, sg, sw, vs, os_, *, vh_n, grouped, tc, b_n):
    ti = pl.program_id(1)

    @pl.when(ti == 0)
    def _():
        sr[...] = s0r_ref[...]
        sg[...] = s0g_ref[...]
        sw[...] = s0w_ref[...]

    n_grp = LANES // (HEADS * b_n) if grouped else 1
    grp_w = LANES // n_grp
    n_v = 3

    def k_tile(ref, n_ops, j, t):
        lanes = pl.ds(pl.multiple_of(t * LANES, LANES), LANES)
        if grouped:
            k_n = ref.shape[0] // n_ops
            return ref[j * k_n:(j + 1) * k_n, lanes]
        return ref[j, 0, :, lanes]

    if grouped:
        grp = lax.broadcasted_iota(jnp.int32, (vh_n, LANES), 1) // grp_w
        for lt in range(tc // n_grp):
            lanes = slice(lt * LANES, (lt + 1) * LANES)
            for j in range(n_v):
                ws = [va_ref[j * HEAD_D + vl * vh_n:j * HEAD_D + (vl + 1) * vh_n, lanes] for vl in range(n_grp)]
                for g in range(n_grp):
                    tile = None
                    for vl in range(n_grp):
                        r = pltpu.roll(ws[vl], ((vl - g) * grp_w) % LANES, 1)
                        tile = r if tile is None else jnp.where(grp == vl, r, tile)
                    vs[lt * n_grp + g, j] = tile
    else:
        def copy_in(t, c):
            lanes = pl.ds(pl.multiple_of(t * LANES, LANES), LANES)
            for j in range(n_v):
                vs[t, j] = va_ref[j, 0, :, lanes]
            return c

        lax.fori_loop(0, tc, copy_in, 0)

    dec_r = dec_ref[...]

    def out(t, j, vh, s, q):
        os_[t, j, pl.ds(vh, 1), :] = jnp.sum(s * q, axis=0, keepdims=True)

    def ret_step(t, c):
        q, k = k_tile(kr_ref, 2, 0, t), k_tile(kr_ref, 2, 1, t)
        for vh in range(vh_n):
            s = sr[vh] * dec_r + vs[t, 0, pl.ds(vh, 1), :] * k
            sr[vh] = s
            out(t, 0, vh, s, q)
        return c

    def gla_step(t, c):
        q, k, al = (k_tile(kg_ref, 3, j, t) for j in range(3))
        for vh in range(vh_n):
            s = sg[vh] * al + vs[t, 1, pl.ds(vh, 1), :] * k
            sg[vh] = s
            out(t, 1, vh, s, q)
        return c

    def rwkv_step(t, c):
        q, k, dec, kk, beta = (k_tile(kw_ref, 5, j, t) for j in range(5))
        for vh in range(vh_n):
            s = sw[vh]
            sk = jnp.sum(s * kk, axis=0, keepdims=True)
            s = s * dec - sk * beta + vs[t, 2, pl.ds(vh, 1), :] * k
            sw[vh] = s
            out(t, 2, vh, s, q)
        return c

    unroll = 2 if grouped else 1
    lax.fori_loop(0, tc, ret_step, 0, unroll=unroll)
    lax.fori_loop(0, tc, gla_step, 0, unroll=unroll)
    lax.fori_loop(0, tc, rwkv_step, 0, unroll=unroll)

    if grouped:
        grp = lax.broadcasted_iota(jnp.int32, (vh_n, LANES), 1) // grp_w
        for lt in range(tc // n_grp):
            lanes = slice(lt * LANES, (lt + 1) * LANES)
            for j in range(n_v):
                og = [os_[lt * n_grp + g, j] for g in range(n_grp)]
                for vl in range(n_grp):
                    wv = None
                    for g in range(n_grp):
                        r = pltpu.roll(og[g], ((g - vl) * grp_w) % LANES, 1)
                        wv = r if wv is None else jnp.where(grp == g, r, wv)
                    o_ref[j * HEAD_D + vl * vh_n:j * HEAD_D + (vl + 1) * vh_n, lanes] = wv
    else:
        def copy_out(t, c):
            lanes = pl.ds(pl.multiple_of(t * LANES, LANES), LANES)
            for j in range(n_v):
                o_ref[j, 0, :, lanes] = os_[t, j]
            return c

        lax.fori_loop(0, tc, copy_out, 0)

    @pl.when(ti == pl.num_programs(1) - 1)
    def _():
        sor_ref[...] = sr[...]
        sog_ref[...] = sg[...]
        sow_ref[...] = sw[...]


def _scan(kr, kg, kw, va, b_n, t_n, dec, s0r, s0g, s0w):
    grouped = b_n * HEADS < LANES
    vh_n = s0r.shape[0]
    if grouped:
        tc = min(t_n, 32)
        n_grp = LANES // (HEADS * b_n)
        assert tc % n_grp == 0 and t_n % tc == 0
        n_l, n_t = 1, t_n // tc
        k_specs = [pl.BlockSpec((a.shape[0], tc * LANES), lambda l, t: (0, t)) for a in (kr, kg, kw)]
        v_spec = pl.BlockSpec((va.shape[0], tc * HEADS * b_n), lambda l, t: (0, t))
        o_spec = v_spec
    else:
        assert b_n == LANES
        tc, n_l, n_t = t_n, HEADS, 1
        k_specs = [pl.BlockSpec((a.shape[0], 1, a.shape[2], tc * LANES), lambda l, t: (0, l, 0, 0))
                   for a in (kr, kg, kw)]
        v_spec = pl.BlockSpec((3, 1, HEAD_D, tc * LANES), lambda l, t: (0, l, 0, 0))
        o_spec = v_spec

    def s_spec(a):
        return pl.BlockSpec(a.shape[:2] + (LANES,), lambda l, t: (0, 0, l))

    return pl.pallas_call(
        functools.partial(_scan_kernel, vh_n=vh_n, grouped=grouped, tc=tc, b_n=b_n),
        grid=(n_l, n_t),
        in_specs=k_specs + [v_spec, pl.BlockSpec((1, LANES), lambda l, t: (0, l)),
                            s_spec(s0r), s_spec(s0g), s_spec(s0w)],
        out_specs=[o_spec, s_spec(s0r), s_spec(s0g), s_spec(s0w)],
        out_shape=[jax.ShapeDtypeStruct(va.shape, F32)] + [jax.ShapeDtypeStruct(s.shape, F32)
                                                           for s in (s0r, s0g, s0w)],
        scratch_shapes=[pltpu.VMEM(s.shape[:2] + (LANES,), F32) for s in (s0r, s0g, s0w)]
                       + [pltpu.VMEM((tc, 3, vh_n, LANES), F32), pltpu.VMEM((tc, 3, vh_n, LANES), F32)],
        compiler_params=_cparams("parallel", "arbitrary"),
        name="scan_ret_gla_rwkv",
    )(kr, kg, kw, va, dec, s0r, s0g, s0w)


def _s5_kernel(u_ref, bb_ref, cc_ref, ar_ref, ai_ref, h0r_ref, h0i_ref,
               y_ref, hr_out, hi_out, hr_scr, hi_scr, xs_scr, hs_scr, *, b_n, tc):
    ti = pl.program_id(0)

    @pl.when(ti == 0)
    def _():
        hr_scr[...] = h0r_ref[...]
        hi_scr[...] = h0i_ref[...]

    xs_scr[...] = _bdot(u_ref[...], bb_ref[...])
    ar = ar_ref[...]
    ai = ai_ref[...]

    def step(t, carry):
        hr, hi = carry
        row = pl.multiple_of(t * b_n, SUBLANES)
        x = xs_scr[pl.ds(row, b_n), :]
        nr = ar * hr - ai * hi + x[:, :S5_CH]
        ni = ar * hi + ai * hr + x[:, S5_CH:]
        hs_scr[pl.ds(row, b_n), :S5_CH] = nr
        hs_scr[pl.ds(row, b_n), S5_CH:] = ni
        return nr, ni

    hr, hi = lax.fori_loop(0, tc, step, (hr_scr[...], hi_scr[...]))
    hr_scr[...] = hr
    hi_scr[...] = hi
    y_ref[...] = _bdot(hs_scr[...], cc_ref[...])

    @pl.when(ti == pl.num_programs(0) - 1)
    def _():
        hr_out[...] = hr
        hi_out[...] = hi


def _s5_scan(cols, row_off, bb, cc, ar, ai, h0r, h0i, b_n, t_n):
    tc = min(t_n, max(1, 1024 // b_n))
    rows = tc * b_n
    assert t_n % tc == 0 and row_off % rows == 0
    blk0 = row_off // rows
    return pl.pallas_call(
        functools.partial(_s5_kernel, b_n=b_n, tc=tc),
        grid=(t_n // tc,),
        in_specs=[pl.BlockSpec((rows, BRANCH_W), lambda t: (blk0 + t, COL_S5 // BRANCH_W)),
                  _const_spec(bb.shape), _const_spec(cc.shape),
                  _const_spec(ar.shape), _const_spec(ai.shape),
                  _const_spec(h0r.shape), _const_spec(h0i.shape)],
        out_specs=[pl.BlockSpec((rows, BRANCH_W), lambda t: (t, 0)),
                   _const_spec(h0r.shape), _const_spec(h0i.shape)],
        out_shape=[jax.ShapeDtypeStruct((t_n * b_n, BRANCH_W), F32),
                   jax.ShapeDtypeStruct(h0r.shape, F32),
                   jax.ShapeDtypeStruct(h0i.shape, F32)],
        scratch_shapes=[pltpu.VMEM((b_n, S5_CH), F32), pltpu.VMEM((b_n, S5_CH), F32),
                        pltpu.VMEM((rows, 2 * S5_CH), F32), pltpu.VMEM((rows, 2 * S5_CH), F32)],
        compiler_params=_cparams("arbitrary"),
        name="scan_s5",
    )(cols, bb, cc, ar, ai, h0r, h0i)


def _post_kernel(x_ref, mixt_ref, rg_ref, gg_ref, sy_ref, su_ref, wbon_ref, wg_ref,
                 gate0_ref, gate1_ref, gate2_ref, gate3_ref,
                 avg_ref, rgn_g, rgn_b, ggn_g, s5d_ref, wglu_ref, wgn_g, wgn_b,
                 wbr_ref, wo_ref, ln_g, ln_b, o_ref, *, alpha, b_n):
    def seg_mean(v):
        return _seg_dot(v, avg_ref)

    tm = x_ref.shape[0]
    if b_n * HEADS < LANES:
        yt = mixt_ref[...].T.reshape(tm // b_n, HEADS, b_n, 3 * HEAD_D)
        per_head = [yt[:, h].reshape(tm, 3 * HEAD_D) for h in range(HEADS)]
        mix = jnp.concatenate([ph[:, j * HEAD_D:(j + 1) * HEAD_D] for j in range(3) for ph in per_head], axis=1)
    else:
        mix = mixt_ref[...].T
    ro = mix[:, 0:BRANCH_W]
    mu = seg_mean(ro)
    rc = ro - mu
    var = seg_mean(rc * rc)
    b0 = (rc * lax.rsqrt(var + LN_EPS) * rgn_g[...] + rgn_b[...]) * _silu(rg_ref[...])
    go = mix[:, BRANCH_W:2 * BRANCH_W]
    ms = seg_mean(go * go)
    b1 = go * lax.rsqrt(ms + LN_EPS) * ggn_g[...] * _silu(gg_ref[...])
    y = jax.nn.gelu(sy_ref[...] + s5d_ref[...] * su_ref[...])
    b2 = y * _sigmoid(_bdot(y, wglu_ref[...]))
    wy = mix[:, 2 * BRANCH_W:3 * BRANCH_W]
    mu = seg_mean(wy)
    wc = wy - mu
    var = seg_mean(wc * wc)
    b3 = (wc * lax.rsqrt(var + RWKV_GN_EPS) * wgn_g[...] + wgn_b[...] + wbon_ref[...]) * wg_ref[...]

    m = None
    gates = (gate0_ref, gate1_ref, gate2_ref, gate3_ref)
    for i, br in enumerate((b0, b1, b2, b3)):
        term = _bdot(br, wbr_ref[i]) * _sigmoid(gates[i][...])
        m = term if m is None else m + term
    h = _bdot(m, wo_ref[...])
    o_ref[...] = _layer_norm(alpha * x_ref[...] + h, ln_g[...], ln_b[...])


def _embed_ln2(x, f, p_ref, wpe_ref, wpg_ref, ln_g, ln_b, alpha):
    e = _bdot(p_ref[...], wpe_ref[...]) * _sigmoid(_bdot(x, wpg_ref[...]))
    return _layer_norm(alpha * x + f + e, ln_g[...], ln_b[...])


def _ffn_kernel(x_ref, p_ref, w1_ref, w3_ref, w2_ref, wpe_ref, wpg_ref, ln_g, ln_b, o_ref, *, alpha):
    x = x_ref[...]
    xb = x.astype(BF16)
    h = _silu(jnp.dot(xb, w1_ref[...], preferred_element_type=F32)) * jnp.dot(
        xb, w3_ref[...], preferred_element_type=F32)
    f = _bdot(h, w2_ref[...])
    o_ref[...] = _embed_ln2(x, f, p_ref, wpe_ref, wpg_ref, ln_g, ln_b, alpha)


def _router_kernel(x_ref, wh_ref, wl_ref, idx_ref, wgt_ref):
    xh, xl = _split_bf16(x_ref[...])
    wh = wh_ref[...]
    logits = (jnp.dot(xh, wh, preferred_element_type=F32)
              + jnp.dot(xl, wh, preferred_element_type=F32)
              + jnp.dot(xh, wl_ref[...], preferred_element_type=F32))
    col = lax.broadcasted_iota(jnp.int32, logits.shape, 1)
    neg = jnp.float32(-jnp.inf)
    lg = jnp.where(col < N_EXPERTS, logits, neg)
    m1 = jnp.max(lg, axis=1, keepdims=True)
    i1 = jnp.min(jnp.where(lg == m1, col, LANES), axis=1, keepdims=True)
    lg2 = jnp.where(col == i1, neg, lg)
    m2 = jnp.max(lg2, axis=1, keepdims=True)
    i2 = jnp.min(jnp.where(lg2 == m2, col, LANES), axis=1, keepdims=True)
    e2 = jnp.exp(m2 - m1)
    den = 1.0 + e2
    idx_ref[...] = jnp.where(col == 0, i1, jnp.where(col == 1, i2, 0))
    wgt_ref[...] = jnp.where(col == 0, 1.0 / den, jnp.where(col == 1, e2 / den, 0.0))


def _row_copies(idx_ref, n_rows, make_copy, count=None):
    def each(action):
        def body(r, c):
            action(make_copy(r, idx_ref[0, 0, r]))
            return c

        if count is None:
            lax.fori_loop(0, n_rows, body, 0, unroll=8)
            return
        @pl.when(count == n_rows)
        def _():
            lax.fori_loop(0, n_rows, body, 0, unroll=8)

        @pl.when(count < n_rows)
        def _():
            lax.fori_loop(0, count, body, 0)

    return (lambda: each(lambda cp: cp.start())), (lambda: each(lambda cp: cp.wait()))


def _moe_block_kernel(be_ref, nb_ref, bv_ref, tok_ref, tok_next_ref, dst_ref, dst_m1_ref, dst_m2_ref,
                      x_hbm, w1_ref, w3_ref, w2_ref, out_hbm, xbuf, ybuf, sem_in, sem_out):
    j = pl.program_id(0)
    last = pl.num_programs(0) - 1
    nb = nb_ref[0]
    slot = j % 2

    def gather(idx_ref, half):
        return _row_copies(idx_ref, MOE_BLOCK, lambda r, row: pltpu.make_async_copy(
            x_hbm.at[pl.ds(row, 1)], xbuf.at[half, pl.ds(r, 1)], sem_in.at[half]))

    def scatter(idx_ref, half, block):
        return _row_copies(idx_ref, MOE_BLOCK, lambda r, row: pltpu.make_async_copy(
            ybuf.at[half, pl.ds(r, 1)], out_hbm.at[pl.ds(row, 1)], sem_out.at[half]),
            count=bv_ref[jnp.maximum(block, 0)])

    @pl.when(jnp.logical_and(j == 0, nb > 0))
    def _():
        gather(tok_ref, slot)[0]()

    @pl.when(j + 1 < nb)
    def _():
        gather(tok_next_ref, 1 - slot)[0]()

    @pl.when(jnp.logical_and(j >= 2, j - 2 < nb))
    def _():
        scatter(dst_m2_ref, slot, j - 2)[1]()

    @pl.when(j < nb)
    def _():
        gather(tok_ref, slot)[1]()
        xb = xbuf[slot].astype(BF16)
        h = _silu(jnp.dot(xb, w1_ref[0], preferred_element_type=F32)) * jnp.dot(
            xb, w3_ref[0], preferred_element_type=F32)
        ybuf[slot] = _bdot(h, w2_ref[0])
        scatter(dst_ref, slot, j)[0]()

    @pl.when(jnp.logical_and(j == last, jnp.logical_and(j >= 1, j - 1 < nb)))
    def _():
        scatter(dst_m1_ref, 1 - slot, j - 1)[1]()

    @pl.when(jnp.logical_and(j == last, j < nb))
    def _():
        scatter(dst_ref, slot, j)[1]()


def _moe_blocks(x, block_e, nb_used, block_valid, slot_tok, slot_dst, n_out_rows, w1, w3, w2):
    n_blocks = block_e.shape[0]
    d = x.shape[1]
    dff = w1.shape[2]
    tok = slot_tok.reshape(n_blocks, 1, MOE_BLOCK)
    dst = slot_dst.reshape(n_blocks, 1, MOE_BLOCK)

    def idx_spec(shift):
        return pl.BlockSpec((1, 1, MOE_BLOCK),
                            lambda j, be, nb, bv: (jnp.clip(j + shift, 0, n_blocks - 1), 0, 0),
                            memory_space=pltpu.SMEM)

    grid_spec = pltpu.PrefetchScalarGridSpec(
        num_scalar_prefetch=3,
        grid=(n_blocks,),
        in_specs=[
            idx_spec(0), idx_spec(1), idx_spec(0), idx_spec(-1), idx_spec(-2),
            pl.BlockSpec(memory_space=pl.ANY),
            pl.BlockSpec((1, d, dff), lambda j, be, nb, bv: (be[j], 0, 0), pipeline_mode=pl.Buffered(1)),
            pl.BlockSpec((1, d, dff), lambda j, be, nb, bv: (be[j], 0, 0), pipeline_mode=pl.Buffered(1)),
            pl.BlockSpec((1, dff, d), lambda j, be, nb, bv: (be[j], 0, 0), pipeline_mode=pl.Buffered(1)),
        ],
        out_specs=pl.BlockSpec(memory_space=pl.ANY),
        scratch_shapes=[pltpu.VMEM((2, MOE_BLOCK, d), F32), pltpu.VMEM((2, MOE_BLOCK, d), F32),
                        pltpu.SemaphoreType.DMA((2,)), pltpu.SemaphoreType.DMA((2,))],
    )
    return pl.pallas_call(
        _moe_block_kernel,
        grid_spec=grid_spec,
        out_shape=jax.ShapeDtypeStruct((n_out_rows, d), F32),
        compiler_params=_cparams("arbitrary"),
        name="moe_blocks",
    )(block_e, nb_used, block_valid, tok, tok, dst, dst, dst, x, w1, w3, w2)


def _moe_combine_kernel(y0_ref, y1_ref, x_ref, p_ref, wgt_ref, wpe_ref, wpg_ref, ln_g, ln_b, o_ref, *, alpha):
    wgt = wgt_ref[...]
    f = y0_ref[...] * wgt[:, 0:1] + y1_ref[...] * wgt[:, 1:2]
    o_ref[...] = _embed_ln2(x_ref[...], f, p_ref, wpe_ref, wpg_ref, ln_g, ln_b, alpha)


def _lane_groups(b_n):
    bh = b_n * HEADS
    return LANES // bh if bh < LANES else 1


def _state_to_lanes(s, vl_n, value_last):
    s = s.transpose(3, 2, 1, 0) if value_last else s.transpose(2, 3, 1, 0)
    v_n, k_n, h_n, b_n = s.shape
    s = s.reshape(vl_n, v_n // vl_n, k_n, h_n, b_n).transpose(1, 2, 0, 3, 4)
    return s.reshape(v_n // vl_n, k_n, vl_n * h_n * b_n)


def _state_from_lanes(s, b_n, vl_n, value_last):
    vh_n, k_n, _ = s.shape
    s = s.reshape(vh_n, k_n, vl_n, HEADS, b_n).transpose(4, 3, 1, 2, 0).reshape(b_n, HEADS, k_n, vl_n * vh_n)
    return s if value_last else s.transpose(0, 1, 3, 2)


def _rotary_tables(pos, b_n):
    half = HEAD_D // 2
    freq = ROPE_BASE ** (-jnp.arange(half, dtype=F32) / half)
    ang = pos.astype(F32)[:, None] * freq[None, :]
    cos, sin = jnp.cos(ang), jnp.sin(ang)
    cos_h = jnp.tile(jnp.concatenate([cos, cos], axis=-1), (1, HEADS))
    sin_h = jnp.tile(jnp.concatenate([-sin, sin], axis=-1), (1, HEADS))
    return jnp.repeat(cos_h, b_n, axis=0), jnp.repeat(sin_h, b_n, axis=0)


def _block_diag_const(block, n_blocks):
    return jnp.kron(jnp.eye(n_blocks, dtype=F32), jnp.full((block, block), 1.0, F32))


def _s5_params(log_dt, a_re, a_im, b_re, b_im, c_re, c_im):
    dt = jnp.exp(log_dt)[:, None]
    mag = jnp.exp(dt * a_re)
    ang = dt * a_im
    abar_re, abar_im = mag * jnp.cos(ang), mag * jnp.sin(ang)
    den = a_re * a_re + a_im * a_im
    n_re = abar_re - 1.0
    f_re = (n_re * a_re + abar_im * a_im) / den
    f_im = (abar_im * a_re - n_re * a_im) / den
    bb_re = f_re[..., None] * b_re - f_im[..., None] * b_im
    bb_im = f_re[..., None] * b_im + f_im[..., None] * b_re
    eye = jnp.eye(S5_GROUPS, dtype=F32)

    def in_map(bb):
        return jnp.einsum("gpc,gh->gchp", bb, eye).reshape(BRANCH_W, S5_CH)

    def out_map(cm):
        return jnp.einsum("gcp,gh->gphc", cm, eye).reshape(S5_CH, BRANCH_W)

    bb = jnp.concatenate([in_map(bb_re), in_map(bb_im)], axis=1).astype(BF16)
    cc = jnp.concatenate([out_map(c_re), -out_map(c_im)], axis=0).astype(BF16)
    return bb, cc, abar_re.reshape(1, S5_CH), abar_im.reshape(1, S5_CH)


def _reorder_w_in(w_in):
    d_model = w_in.shape[0]
    main = jnp.concatenate([w_in[:, :1792], w_in[:, 1808:]], axis=1)
    tail = jnp.concatenate([w_in[:, 1792:1808],
                            jnp.zeros((d_model, D_IN_PAD - COL_GLA_R - GLA_GATE_RANK), w_in.dtype)], axis=1)
    return jnp.concatenate([main, tail], axis=1).astype(BF16)


def _pad_cols(w, n):
    return jnp.pad(w, ((0, 0), (0, n - w.shape[1])))


def _pad_rows(w, n):
    return jnp.pad(w, ((0, n - w.shape[0]), (0, 0)))


def _moe_route(idx, n):
    nk = n * TOP_K
    flat_e = idx.reshape(nk)
    onehot = (flat_e[:, None] == jnp.arange(N_EXPERTS, dtype=jnp.int32)[None, :]).astype(jnp.int32)
    incl = jnp.cumsum(onehot, axis=0)
    counts = incl[-1]
    rank = jnp.sum((incl - onehot) * onehot, axis=1)
    padded = (counts + MOE_BLOCK - 1) // MOE_BLOCK * MOE_BLOCK
    pad_end = jnp.cumsum(padded)
    slot = (pad_end - padded)[flat_e] + rank
    n_blocks = -(-(nk + N_EXPERTS * (MOE_BLOCK - 1)) // MOE_BLOCK)
    cap = n_blocks * MOE_BLOCK
    pair = jnp.full((cap,), -1, jnp.int32).at[slot].set(jnp.arange(nk, dtype=jnp.int32))
    real = pair >= 0
    slot_tok = jnp.where(real, pair // TOP_K, 0)
    slot_dst = jnp.where(real, (pair % TOP_K) * n + pair // TOP_K, 0)
    block_start = jnp.arange(n_blocks, dtype=jnp.int32) * MOE_BLOCK
    block_e = jnp.minimum(jnp.sum((pad_end[None, :] <= block_start[:, None]).astype(jnp.int32), axis=1),
                          N_EXPERTS - 1).astype(jnp.int32)
    nb_used = (pad_end[-1] // MOE_BLOCK).astype(jnp.int32).reshape(1)
    block_valid = jnp.sum(real.reshape(n_blocks, MOE_BLOCK).astype(jnp.int32), axis=1)
    return slot_tok, slot_dst, block_e, nb_used, block_valid, nk


def kernel(x_prompt, x_sample, state_ret, state_gla, state_s5_re, state_s5_im, state_rwkv, state_shift,
           p_prompt, p_sample, w_in, ret_gn_g, ret_gn_b, gla_wg2, gla_bg, gla_gn,
           s5_log_dt, s5_a_re, s5_a_im, s5_b_re, s5_b_im, s5_c_re, s5_c_im, s5_d, s5_w_glu,
           rwkv_mu, rwkv_w0, rwkv_w1, rwkv_w2, rwkv_a0, rwkv_a1, rwkv_a2, rwkv_g1, rwkv_g2,
           rwkv_kk, rwkv_ka, rwkv_rk, rwkv_gn_g, rwkv_gn_b, w_branch, w_o,
           ln1_g, ln1_b, ln2_g, ln2_b, w_pe, w_pg, ffn_w1, ffn_w3, ffn_w2,
           moe_router, moe_w1, moe_w3, moe_w2):
    depth = w_in.shape[0]
    bp, tp, d_model = x_prompt.shape
    bs, ts, _ = x_sample.shape
    n_p, n_s = bp * tp, bs * ts
    n = n_p + n_s
    tm = ROW_TILE
    assert n_p % tm == 0 and n_s % tm == 0 and tm % bp == 0 and tm % bs == 0 and n_p % bs == 0
    alpha = (2 * depth) ** 0.25
    groups = [(0, bp, tp), (n_p, bs, ts)]
    w4 = 4 * BRANCH_W

    def time_major(a_p, a_s):
        return jnp.concatenate([a_p.transpose(1, 0, 2).reshape(n_p, -1),
                                a_s.transpose(1, 0, 2).reshape(n_s, -1)], axis=0)

    x = time_major(x_prompt, x_sample)
    cos_p, sin_p = _rotary_tables(jnp.arange(tp, dtype=jnp.int32), bp)
    cos_s, sin_s = _rotary_tables(PAST_LEN + jnp.arange(ts, dtype=jnp.int32), bs)
    ones_bd = _block_diag_const(HEAD_D, HEADS).astype(BF16)
    avg_bd = (_block_diag_const(HEAD_D, HEADS) / HEAD_D).astype(BF16)
    gamma = 1.0 - jnp.exp2(-5.0 - jnp.arange(HEADS, dtype=F32))
    row = lambda v: v.reshape(1, -1)

    new = [[] for _ in range(6)]
    for i in range(depth):
        cols = _matmul(x, _reorder_w_in(w_in[i]), tm=512, tn=D_IN_PAD // 3, name="in_proj")

        prep_consts = [
            _pad_rows(_pad_cols(gla_wg2[i], LANES), LANES).astype(BF16), row(gla_bg[i]),
            rwkv_mu[i], row(rwkv_w0[i]),
            _pad_cols(rwkv_w1[i], LANES).astype(BF16), _pad_rows(rwkv_w2[i], LANES).astype(BF16),
            row(rwkv_a0[i]),
            _pad_cols(rwkv_a1[i], LANES).astype(BF16), _pad_rows(rwkv_a2[i], LANES).astype(BF16),
            _pad_cols(rwkv_g1[i], LANES).astype(BF16), _pad_rows(rwkv_g2[i], LANES).astype(BF16),
            row(rwkv_kk[i]), row(rwkv_ka[i]), row(rwkv_rk[i]), ones_bd,
        ]
        post_consts = [avg_bd, row(ret_gn_g[i]), row(ret_gn_b[i]), row(gla_gn[i]), row(s5_d[i]),
                       s5_w_glu[i].astype(BF16), row(rwkv_gn_g[i]), row(rwkv_gn_b[i]),
                       w_branch[i].astype(BF16), w_o[i].astype(BF16), row(ln1_g[i]), row(ln1_b[i])]
        s5p = _s5_params(s5_log_dt[i], s5_a_re[i], s5_a_im[i], s5_b_re[i], s5_b_im[i], s5_c_re[i], s5_c_im[i])
        rwkv_blk = COL_RWKV // w4
        x1_parts, layer_new = [], []
        for gi, (off, b_n, t_n) in enumerate(groups):
            n_g = b_n * t_n
            t0 = off // tm
            if gi == 0:
                st = [jnp.zeros((b_n,) + s.shape[2:], s.dtype)
                      for s in (state_ret, state_gla, state_s5_re, state_s5_im, state_rwkv)]
                shift0, cos_g, sin_g = jnp.zeros((b_n, w4), F32), cos_p, sin_p
            else:
                st = [state_ret[i], state_gla[i], state_s5_re[i], state_s5_im[i], state_rwkv[i]]
                shift0, cos_g, sin_g = state_shift[i], cos_s, sin_s
            vl_n = _lane_groups(b_n)
            grouped = vl_n > 1

            def cspec(width, cb, t0=t0):
                return pl.BlockSpec((tm, width), lambda r: (r + t0, cb))

            if grouped:
                t_shapes = [(2 * HEAD_D, n_g * vl_n * HEADS), (3 * GLA_DK, n_g * vl_n * HEADS),
                            (5 * HEAD_D, n_g * vl_n * HEADS), (3 * HEAD_D, n_g * HEADS)]
            else:
                t_shapes = [(2 * BRANCH_W, n_g), (3 * GLA_QK, n_g), (5 * BRANCH_W, n_g), (3 * BRANCH_W, n_g)]
            consts = [shift0] + prep_consts
            ka_ret, ka_gla, ka_rwkv, va, obonus, og = pl.pallas_call(
                functools.partial(_prep_kernel, grouped=grouped),
                grid=(n_g // tm,),
                in_specs=[cspec(BRANCH_W, 0), cspec(BRANCH_W, 1), cspec(BRANCH_W, 2),
                          _row_spec(tm, BRANCH_W, 0), _row_spec(tm, BRANCH_W, 0),
                          cspec(GLA_QK, COL_GLA_Q // GLA_QK), cspec(GLA_QK, COL_GLA_K // GLA_QK),
                          cspec(BRANCH_W, COL_GLA_V // BRANCH_W), cspec(LANES, COL_GLA_R // LANES),
                          cspec(w4, rwkv_blk),
                          pl.BlockSpec((b_n, w4), lambda r, t0=t0, b_n=b_n:
                                       (jnp.maximum((r + t0) * (tm // b_n) - 1, 0), rwkv_blk))]
                         + [_const_spec(a.shape) for a in consts],
                out_specs=[pl.BlockSpec((sh[0], sh[1] // (n_g // tm)), lambda r: (0, r)) for sh in t_shapes]
                          + [_row_spec(tm, BRANCH_W, 0), _row_spec(tm, BRANCH_W, 0)],
                out_shape=[jax.ShapeDtypeStruct(sh, F32) for sh in t_shapes]
                          + [jax.ShapeDtypeStruct((n_g, BRANCH_W), F32)] * 2,
                compiler_params=_cparams("parallel"),
                name="mixer_prep",
            )(cols, cols, cols, cos_g, sin_g, cols, cols, cols, cols, cols, cols, *consts)

            dec = jnp.tile(jnp.repeat(gamma, b_n), vl_n).reshape(1, vl_n * HEADS * b_n)
            if not grouped:
                ka_ret = ka_ret.reshape(2, HEADS, HEAD_D, n_g)
                ka_gla = ka_gla.reshape(3, HEADS, GLA_DK, n_g)
                ka_rwkv = ka_rwkv.reshape(5, HEADS, HEAD_D, n_g)
                va = va.reshape(3, HEADS, HEAD_D, n_g)
            o, s_ret, s_gla, s_rwkv = _scan(
                ka_ret, ka_gla, ka_rwkv, va, b_n, t_n, dec,
                _state_to_lanes(st[0], vl_n, True), _state_to_lanes(st[1], vl_n, True),
                _state_to_lanes(st[4], vl_n, False))
            mixt = o if grouped else o.reshape(3 * BRANCH_W, n_g)
            y, hr, hi = _s5_scan(cols, off, *s5p, st[2].reshape(b_n, S5_CH), st[3].reshape(b_n, S5_CH), b_n, t_n)
            shift_new = cols[off + (t_n - 1) * b_n:off + t_n * b_n, COL_RWKV:COL_RWKV + w4]
            layer_new.append((_state_from_lanes(s_ret, b_n, vl_n, True), _state_from_lanes(s_gla, b_n, vl_n, True),
                              hr.reshape(b_n, S5_GROUPS, S5_STATE), hi.reshape(b_n, S5_GROUPS, S5_STATE),
                              _state_from_lanes(s_rwkv, b_n, vl_n, False), shift_new))

            x1_parts.append(pl.pallas_call(
                functools.partial(_post_kernel, alpha=alpha, b_n=b_n),
                grid=(n_g // tm,),
                in_specs=[cspec(d_model, 0),
                          pl.BlockSpec((mixt.shape[0], mixt.shape[1] // (n_g // tm)), lambda r: (0, r)),
                          cspec(BRANCH_W, 3), cspec(BRANCH_W, COL_GLA_G // BRANCH_W),
                          _row_spec(tm, BRANCH_W, 0), cspec(BRANCH_W, COL_S5 // BRANCH_W),
                          _row_spec(tm, BRANCH_W, 0), _row_spec(tm, BRANCH_W, 0)]
                         + [cspec(d_model, COL_GATE // d_model + q) for q in range(N_BRANCH)]
                         + [_const_spec(a.shape) for a in post_consts],
                out_specs=_row_spec(tm, d_model, 0),
                out_shape=jax.ShapeDtypeStruct((n_g, d_model), F32),
                compiler_params=_cparams("parallel"),
                name="mixer_post",
            )(x, mixt, cols, cols, y, cols, obonus, og, cols, cols, cols, cols, *post_consts))
        for lst, pair in zip(new, zip(*layer_new)):
            lst.append(pair)
        x1 = jnp.concatenate(x1_parts, axis=0)

        p = time_major(p_prompt[i], p_sample[i])
        tail_consts = [w_pe[i].astype(BF16), w_pg[i].astype(BF16), row(ln2_g[i]), row(ln2_b[i])]
        j = i // 2
        if i % 2 == 0:
            (x,) = _rowwise(
                functools.partial(_ffn_kernel, alpha=alpha), n,
                [(x1, d_model, 0), (p, p.shape[1], 0)],
                [ffn_w1[j].astype(BF16), ffn_w3[j].astype(BF16), ffn_w2[j].astype(BF16)] + tail_consts,
                [d_model], name="ffn")
        else:
            rh, rl = _split_bf16(_pad_cols(moe_router[j], LANES))
            idx, wgt = _rowwise(_router_kernel, n, [(x1, d_model, 0)], [rh, rl], [LANES, LANES],
                                out_dtypes=[jnp.int32, F32], name="moe_router")
            slot_tok, slot_dst, block_e, nb_used, block_valid, n_out_rows = _moe_route(idx[:, :TOP_K], n)
            yk = _moe_blocks(x1, block_e, nb_used, block_valid, slot_tok, slot_dst, n_out_rows,
                             moe_w1[j].astype(BF16), moe_w3[j].astype(BF16), moe_w2[j].astype(BF16))
            (x,) = _rowwise(
                functools.partial(_moe_combine_kernel, alpha=alpha), n,
                [(yk, d_model, 0), (yk, d_model, 0, n // tm), (x1, d_model, 0), (p, p.shape[1], 0), (wgt, LANES, 0)],
                tail_consts, [d_model], name="moe_combine")

    y_prompt = x[:n_p].reshape(tp, bp, d_model).transpose(1, 0, 2)
    y_sample = x[n_p:].reshape(ts, bs, d_model).transpose(1, 0, 2)
    outs = [y_prompt, y_sample]
    for lst in new:
        outs.append(jnp.stack([pair[0] for pair in lst], 0))
        outs.append(jnp.stack([pair[1] for pair in lst], 0))
    return tuple(outs)
```

```python
import functools

import jax
import jax.numpy as jnp
from jax import lax
from jax.experimental import pallas as pl
from jax.experimental.pallas import tpu as pltpu

F32 = jnp.float32
BF16 = jnp.bfloat16

LANES = 128
SUBLANES = 8
VMEM_LIMIT = 56 * 1024 * 1024

N_BRANCH = 4
BRANCH_W = 256
HEADS = 4
HEAD_D = 64
GLA_DK = 32
GLA_QK = HEADS * GLA_DK
GLA_GATE_RANK = 16
GLA_GATE_NORM = 16.0
S5_GROUP = 16
S5_GROUPS = 16
S5_STATE = 64
S5_CH = S5_GROUPS * S5_STATE
ROPE_BASE = 10000.0
RWKV_GN_EPS = 64e-5
LN_EPS = 1e-5
N_EXPERTS = 8
TOP_K = 2
MOE_BLOCK = 256
ROW_TILE = 256
PAST_LEN = 16384

COL_RET = 0
COL_GLA_Q = 1024
COL_GLA_K = 1152
COL_GLA_V = 1280
COL_GLA_G = 1536
COL_S5 = 1792
COL_RWKV = 2048
COL_GATE = 3072
COL_GLA_R = 7168
D_IN_PAD = 7296


def _cparams(*sem):
    return pltpu.CompilerParams(dimension_semantics=sem, vmem_limit_bytes=VMEM_LIMIT)


def _split_bf16(x):
    hi = x.astype(BF16)
    lo = (x - hi.astype(F32)).astype(BF16)
    return hi, lo


def _seg_dot(x, m_ref):
    hi, lo = _split_bf16(x)
    m = m_ref[...]
    return (jnp.dot(hi, m, preferred_element_type=F32)
            + jnp.dot(lo, m, preferred_element_type=F32))


def _bdot(x, w):
    return jnp.dot(x.astype(BF16), w, preferred_element_type=F32)


def _sigmoid(x):
    return 1.0 / (1.0 + jnp.exp(-x))


def _silu(x):
    return x * _sigmoid(x)


def _log1p_exp_neg_abs(x):
    return jnp.log1p(jnp.exp(-jnp.abs(x)))


def _layer_norm(x, g, b):
    mu = jnp.mean(x, axis=-1, keepdims=True)
    xc = x - mu
    var = jnp.mean(xc * xc, axis=-1, keepdims=True)
    return xc * lax.rsqrt(var + LN_EPS) * g + b


def _row_spec(tm, width, col_block, row_block0=0):
    return pl.BlockSpec((tm, width), lambda i, cb=col_block, r0=row_block0: (i + r0, cb))


def _const_spec(shape):
    nd = len(shape)
    return pl.BlockSpec(shape, lambda i, nd=nd: (0,) * nd)


def _rowwise(body, n_rows, row_in, const_in, out_widths, out_dtypes=None, tm=ROW_TILE, name=None):
    assert n_rows % tm == 0
    out_dtypes = out_dtypes or [F32] * len(out_widths)
    in_specs = [_row_spec(tm, *spec[1:]) for spec in row_in] + [_const_spec(a.shape) for a in const_in]
    out_specs = [_row_spec(tm, w, 0) for w in out_widths]
    out_shape = [jax.ShapeDtypeStruct((n_rows, w), dt) for w, dt in zip(out_widths, out_dtypes)]
    return pl.pallas_call(
        body,
        grid=(n_rows // tm,),
        in_specs=in_specs,
        out_specs=out_specs,
        out_shape=out_shape,
        compiler_params=_cparams("parallel"),
        name=name,
    )(*[spec[0] for spec in row_in], *const_in)


def _matmul_kernel(x_ref, w_ref, o_ref):
    o_ref[...] = _bdot(x_ref[...], w_ref[...]).astype(o_ref.dtype)


def _matmul(x, w, tm=512, tn=None, out_dtype=F32, name=None):
    m, k = x.shape
    n = w.shape[1]
    tn = tn or n
    tm = tm if m % tm == 0 else ROW_TILE
    assert m % tm == 0 and n % tn == 0
    return pl.pallas_call(
        _matmul_kernel,
        grid=(n // tn, m // tm),
        in_specs=[pl.BlockSpec((tm, k), lambda j, i: (i, 0)),
                  pl.BlockSpec((k, tn), lambda j, i: (0, j))],
        out_specs=pl.BlockSpec((tm, tn), lambda j, i: (i, j)),
        out_shape=jax.ShapeDtypeStruct((m, n), out_dtype),
        compiler_params=_cparams("parallel", "parallel"),
        name=name,
    )(x, w)


def _replicate_groups(w, n_grp):
    if n_grp == 1:
        return [w]
    grp = lax.broadcasted_iota(jnp.int32, w.shape, 1) // (LANES // n_grp)
    parts = [w]
    span = n_grp
    while span > 1:
        half = span // 2
        shift = half * (LANES // n_grp)
        low = (grp % span) < half
        nxt = []
        for z in parts:
            rz = pltpu.roll(z, shift, 1)
            nxt.append(jnp.where(low, z, rz))
            nxt.append(jnp.where(low, rz, z))
        parts = nxt
        span = half
    return parts


def _heads_to_lanes(ops, b_n, pad_to=LANES):
    tm = ops[0].shape[0]
    c = ops[0].shape[1] // HEADS
    packed = []
    for h in range(HEADS):
        pieces = [o[:, h * c:(h + 1) * c] for o in ops]
        if len(pieces) * c < pad_to:
            pieces.append(jnp.zeros((tm, pad_to - len(pieces) * c), F32))
        packed.append(jnp.concatenate(pieces, axis=1).reshape(tm // b_n, b_n, pad_to))
    y = jnp.stack(packed, axis=1).reshape(tm * HEADS, pad_to)
    return y.T


def _prep_kernel(rq_ref, rk_ref, rv_ref, cos_ref, sin_ref,
                 gq_ref, gk_ref, gv_ref, gr_ref,
                 c_ref, tail_ref, shift_ref, wg2_ref, bg_ref,
                 mu_ref, w0_ref, w1_ref, w2_ref, a0_ref, a1_ref, a2_ref, g1_ref, g2_ref,
                 kkp_ref, kap_ref, rkp_ref, ones_ref,
                 kr_ref, kg_ref, kw_ref, va_ref, obonus_ref, og_ref, *, grouped):
    i = pl.program_id(0)
    w = BRANCH_W
    cos = cos_ref[...]
    sin = sin_ref[...]
    lane = lax.broadcasted_iota(jnp.int32, cos.shape, 1)
    first_half = (lane % HEAD_D) < (HEAD_D // 2)

    def rot(x):
        partner = jnp.where(first_half,
                            pltpu.roll(x, BRANCH_W - HEAD_D // 2, 1),
                            pltpu.roll(x, HEAD_D // 2, 1))
        return x * cos + partner * sin

    ret_q = rot(rq_ref[...])
    ret_k = rot(rk_ref[...]) * (HEAD_D ** -0.5)

    z = _bdot(gr_ref[...], wg2_ref[...]) + bg_ref[...]
    glog = (jnp.minimum(z, 0.0) - _log1p_exp_neg_abs(z)) / GLA_GATE_NORM
    gla_q = gq_ref[...] * (GLA_DK ** -0.5)
    gla_al = jnp.exp(glog)

    c = c_ref[...]
    tm = c.shape[0]
    b_n = tail_ref.shape[0]
    tail = jnp.where(i == 0, shift_ref[...], tail_ref[...])
    d = jnp.concatenate([tail, c[:tm - b_n]], axis=0) - c
    mu = mu_ref[...]
    cr, ck, cv, cz = (c[:, j * w:(j + 1) * w] for j in range(4))
    dr, dk, dv, dz = (d[:, j * w:(j + 1) * w] for j in range(4))
    r = cr + dr * mu[0:1]
    k = ck + dk * mu[1:2]
    v = cv + dv * mu[2:3]
    zw = cz + dz * mu[3:4]
    za = cz + dz * mu[4:5]
    zg = cz + dz * mu[5:6]
    w_raw = w0_ref[...] + _bdot(jnp.tanh(_bdot(zw, w1_ref[...])), w2_ref[...])
    sp = jnp.maximum(-w_raw, 0.0) + _log1p_exp_neg_abs(w_raw)
    dec = jnp.exp(-jnp.exp(-sp - 0.5))
    a = _sigmoid(a0_ref[...] + _bdot(_bdot(za, a1_ref[...]), a2_ref[...]))
    og_ref[...] = _bdot(_sigmoid(_bdot(zg, g1_ref[...])), g2_ref[...])
    kk = k * kkp_ref[...]
    ss = _seg_dot(kk * kk, ones_ref)
    kk = kk * lax.rsqrt(jnp.maximum(ss, 1e-24))
    km = k * (1.0 + (a - 1.0) * kap_ref[...])
    obonus_ref[...] = _seg_dot(r * km * rkp_ref[...], ones_ref) * v

    if not grouped:
        for ref, ops in ((kr_ref, (ret_q, ret_k)), (kg_ref, (gla_q, gk_ref[...], gla_al)),
                         (kw_ref, (r, km, dec, kk, kk * a)), (va_ref, (rv_ref[...], gv_ref[...], v))):
            cw = ops[0].shape[1]
            for j, o in enumerate(ops):
                ref[j * cw:(j + 1) * cw, :] = o.T
        return

    n_grp = LANES // (HEADS * b_n)

    def emit_tiles(ref, row0, ops, n_rows):
        wt = _heads_to_lanes(ops, b_n)
        for lt in range(wt.shape[1] // LANES):
            for g, zt in enumerate(_replicate_groups(wt[:n_rows, lt * LANES:(lt + 1) * LANES], n_grp)):
                q = lt * n_grp + g
                ref[row0:row0 + n_rows, q * LANES:(q + 1) * LANES] = zt

    emit_tiles(kr_ref, 0, (ret_q, ret_k), 2 * HEAD_D)
    emit_tiles(kg_ref, 0, (gla_q, gk_ref[...], gla_al), 3 * GLA_DK)
    emit_tiles(kw_ref, 0, (r, km), 2 * HEAD_D)
    emit_tiles(kw_ref, 2 * HEAD_D, (dec, kk), 2 * HEAD_D)
    emit_tiles(kw_ref, 4 * HEAD_D, (kk * a,), HEAD_D)
    va_ref[0:2 * HEAD_D, :] = _heads_to_lanes((rv_ref[...], gv_ref[...]), b_n)
    va_ref[2 * HEAD_D:3 * HEAD_D, :] = _heads_to_lanes((v,), b_n)[:HEAD_D]


def _scan_kernel(kr_ref, kg_ref, kw_ref, va_ref, dec_ref, s0r_ref, s0g_ref, s0w_ref,
                 o_ref, sor_ref, sog_ref, sow_ref,
                 sr, sg, sw, vs, os_, *, vh_n, grouped, tc, b_n):
    ti = pl.program_id(1)

    @pl.when(ti == 0)
    def _():
        sr[...] = s0r_ref[...]
        sg[...] = s0g_ref[...]
        sw[...] = s0w_ref[...]

    n_grp = LANES // (HEADS * b_n) if grouped else 1
    grp_w = LANES // n_grp
    n_v = 3

    def k_tile(ref, n_ops, j, t):
        lanes = pl.ds(pl.multiple_of(t * LANES, LANES), LANES)
        if grouped:
            k_n = ref.shape[0] // n_ops
            return ref[j * k_n:(j + 1) * k_n, lanes]
        return ref[j, 0, :, lanes]

    if grouped:
        grp = lax.broadcasted_iota(jnp.int32, (vh_n, LANES), 1) // grp_w
        for lt in range(tc // n_grp):
            lanes = slice(lt * LANES, (lt + 1) * LANES)
            for j in range(n_v):
                ws = [va_ref[j * HEAD_D + vl * vh_n:j * HEAD_D + (vl + 1) * vh_n, lanes] for vl in range(n_grp)]
                for g in range(n_grp):
                    tile = None
                    for vl in range(n_grp):
                        r = pltpu.roll(ws[vl], ((vl - g) * grp_w) % LANES, 1)
                        tile = r if tile is None else jnp.where(grp == vl, r, tile)
                    vs[lt * n_grp + g, j] = tile
    else:
        def copy_in(t, c):
            lanes = pl.ds(pl.multiple_of(t * LANES, LANES), LANES)
            for j in range(n_v):
                vs[t, j] = va_ref[j, 0, :, lanes]
            return c

        lax.fori_loop(0, tc, copy_in, 0)

    dec_r = dec_ref[...]

    def out(t, j, vh, s, q):
        os_[t, j, pl.ds(vh, 1), :] = jnp.sum(s * q, axis=0, keepdims=True)

    def ret_step(t, c):
        q, k = k_tile(kr_ref, 2, 0, t), k_tile(kr_ref, 2, 1, t)
        for vh in range(vh_n):
            s = sr[vh] * dec_r + vs[t, 0, pl.ds(vh, 1), :] * k
            sr[vh] = s
            out(t, 0, vh, s, q)
        return c

    def gla_step(t, c):
        q, k, al = (k_tile(kg_ref, 3, j, t) for j in range(3))
        for vh in range(vh_n):
            s = sg[vh] * al + vs[t, 1, pl.ds(vh, 1), :] * k
            sg[vh] = s
            out(t, 1, vh, s, q)
        return c

    def rwkv_step(t, c):
        q, k, dec, kk, beta = (k_tile(kw_ref, 5, j, t) for j in range(5))
        for vh in range(vh_n):
            s = sw[vh]
            sk = jnp.sum(s * kk, axis=0, keepdims=True)
            s = s * dec - sk * beta + vs[t, 2, pl.ds(vh, 1), :] * k
            sw[vh] = s
            out(t, 2, vh, s, q)
        return c

    unroll = 2 if grouped else 1
    lax.fori_loop(0, tc, ret_step, 0, unroll=unroll)
    lax.fori_loop(0, tc, gla_step, 0, unroll=unroll)
    lax.fori_loop(0, tc, rwkv_step, 0, unroll=unroll)

    if grouped:
        grp = lax.broadcasted_iota(jnp.int32, (vh_n, LANES), 1) // grp_w
        for lt in range(tc // n_grp):
            lanes = slice(lt * LANES, (lt + 1) * LANES)
            for j in range(n_v):
                og = [os_[lt * n_grp + g, j] for g in range(n_grp)]
                for vl in range(n_grp):
                    wv = None
                    for g in range(n_grp):
                        r = pltpu.roll(og[g], ((g - vl) * grp_w) % LANES, 1)
                        wv = r if wv is None else jnp.where(grp == g, r, wv)
                    o_ref[j * HEAD_D + vl * vh_n:j * HEAD_D + (vl + 1) * vh_n, lanes] = wv
    else:
        def copy_out(t, c):
            lanes = pl.ds(pl.multiple_of(t * LANES, LANES), LANES)
            for j in range(n_v):
                o_ref[j, 0, :, lanes] = os_[t, j]
            return c

        lax.fori_loop(0, tc, copy_out, 0)

    @pl.when(ti == pl.num_programs(1) - 1)
    def _():
        sor_ref[...] = sr[...]
        sog_ref[...] = sg[...]
        sow_ref[...] = sw[...]


def _scan(kr, kg, kw, va, b_n, t_n, dec, s0r, s0g, s0w):
    grouped = b_n * HEADS < LANES
    vh_n = s0r.shape[0]
    if grouped:
        tc = min(t_n, 32)
        n_grp = LANES // (HEADS * b_n)
        assert tc % n_grp == 0 and t_n % tc == 0
        n_l, n_t = 1, t_n // tc
        k_specs = [pl.BlockSpec((a.shape[0], tc * LANES), lambda l, t: (0, t)) for a in (kr, kg, kw)]
        v_spec = pl.BlockSpec((va.shape[0], tc * HEADS * b_n), lambda l, t: (0, t))
        o_spec = v_spec
    else:
        assert b_n == LANES
        tc, n_l, n_t = t_n, HEADS, 1
        k_specs = [pl.BlockSpec((a.shape[0], 1, a.shape[2], tc * LANES), lambda l, t: (0, l, 0, 0))
                   for a in (kr, kg, kw)]
        v_spec = pl.BlockSpec((3, 1, HEAD_D, tc * LANES), lambda l, t: (0, l, 0, 0))
        o_spec = v_spec

    def s_spec(a):
        return pl.BlockSpec(a.shape[:2] + (LANES,), lambda l, t: (0, 0, l))

    return pl.pallas_call(
        functools.partial(_scan_kernel, vh_n=vh_n, grouped=grouped, tc=tc, b_n=b_n),
        grid=(n_l, n_t),
        in_specs=k_specs + [v_spec, pl.BlockSpec((1, LANES), lambda l, t: (0, l)),
                            s_spec(s0r), s_spec(s0g), s_spec(s0w)],
        out_specs=[o_spec, s_spec(s0r), s_spec(s0g), s_spec(s0w)],
        out_shape=[jax.ShapeDtypeStruct(va.shape, F32)] + [jax.ShapeDtypeStruct(s.shape, F32)
                                                           for s in (s0r, s0g, s0w)],
        scratch_shapes=[pltpu.VMEM(s.shape[:2] + (LANES,), F32) for s in (s0r, s0g, s0w)]
                       + [pltpu.VMEM((tc, 3, vh_n, LANES), F32), pltpu.VMEM((tc, 3, vh_n, LANES), F32)],
        compiler_params=_cparams("parallel", "arbitrary"),
        name="scan_ret_gla_rwkv",
    )(kr, kg, kw, va, dec, s0r, s0g, s0w)


def _s5_kernel(u_ref, bb_ref, cc_ref, ar_ref, ai_ref, h0r_ref, h0i_ref,
               y_ref, hr_out, hi_out, hr_scr, hi_scr, xs_scr, hs_scr, *, b_n, tc):
    ti = pl.program_id(0)

    @pl.when(ti == 0)
    def _():
        hr_scr[...] = h0r_ref[...]
        hi_scr[...] = h0i_ref[...]

    xs_scr[...] = _bdot(u_ref[...], bb_ref[...])
    ar = ar_ref[...]
    ai = ai_ref[...]

    def step(t, carry):
        hr, hi = carry
        row = pl.multiple_of(t * b_n, SUBLANES)
        x = xs_scr[pl.ds(row, b_n), :]
        nr = ar * hr - ai * hi + x[:, :S5_CH]
        ni = ar * hi + ai * hr + x[:, S5_CH:]
        hs_scr[pl.ds(row, b_n), :S5_CH] = nr
        hs_scr[pl.ds(row, b_n), S5_CH:] = ni
        return nr, ni

    hr, hi = lax.fori_loop(0, tc, step, (hr_scr[...], hi_scr[...]))
    hr_scr[...] = hr
    hi_scr[...] = hi
    y_ref[...] = _bdot(hs_scr[...], cc_ref[...])

    @pl.when(ti == pl.num_programs(0) - 1)
    def _():
        hr_out[...] = hr
        hi_out[...] = hi


def _s5_scan(cols, row_off, bb, cc, ar, ai, h0r, h0i, b_n, t_n):
    tc = min(t_n, max(1, 1024 // b_n))
    rows = tc * b_n
    assert t_n % tc == 0 and row_off % rows == 0
    blk0 = row_off // rows
    return pl.pallas_call(
        functools.partial(_s5_kernel, b_n=b_n, tc=tc),
        grid=(t_n // tc,),
        in_specs=[pl.BlockSpec((rows, BRANCH_W), lambda t: (blk0 + t, COL_S5 // BRANCH_W)),
                  _const_spec(bb.shape), _const_spec(cc.shape),
                  _const_spec(ar.shape), _const_spec(ai.shape),
                  _const_spec(h0r.shape), _const_spec(h0i.shape)],
        out_specs=[pl.BlockSpec((rows, BRANCH_W), lambda t: (t, 0)),
                   _const_spec(h0r.shape), _const_spec(h0i.shape)],
        out_shape=[jax.ShapeDtypeStruct((t_n * b_n, BRANCH_W), F32),
                   jax.ShapeDtypeStruct(h0r.shape, F32),
                   jax.ShapeDtypeStruct(h0i.shape, F32)],
        scratch_shapes=[pltpu.VMEM((b_n, S5_CH), F32), pltpu.VMEM((b_n, S5_CH), F32),
                        pltpu.VMEM((rows, 2 * S5_CH), F32), pltpu.VMEM((rows, 2 * S5_CH), F32)],
        compiler_params=_cparams("arbitrary"),
        name="scan_s5",
    )(cols, bb, cc, ar, ai, h0r, h0i)


def _post_kernel(x_ref, mixt_ref, rg_ref, gg_ref, sy_ref, su_ref, wbon_ref, wg_ref,
                 gate0_ref, gate1_ref, gate2_ref, gate3_ref,
                 avg_ref, rgn_g, rgn_b, ggn_g, s5d_ref, wglu_ref, wgn_g, wgn_b,
                 wbr_ref, wo_ref, ln_g, ln_b, o_ref, *, alpha, b_n):
    def seg_mean(v):
        return _seg_dot(v, avg_ref)

    tm = x_ref.shape[0]
    if b_n * HEADS < LANES:
        yt = mixt_ref[...].T.reshape(tm // b_n, HEADS, b_n, 3 * HEAD_D)
        per_head = [yt[:, h].reshape(tm, 3 * HEAD_D) for h in range(HEADS)]
        mix = jnp.concatenate([ph[:, j * HEAD_D:(j + 1) * HEAD_D] for j in range(3) for ph in per_head], axis=1)
    else:
        mix = mixt_ref[...].T
    ro = mix[:, 0:BRANCH_W]
    mu = seg_mean(ro)
    rc = ro - mu
    var = seg_mean(rc * rc)
    b0 = (rc * lax.rsqrt(var + LN_EPS) * rgn_g[...] + rgn_b[...]) * _silu(rg_ref[...])
    go = mix[:, BRANCH_W:2 * BRANCH_W]
    ms = seg_mean(go * go)
    b1 = go * lax.rsqrt(ms + LN_EPS) * ggn_g[...] * _silu(gg_ref[...])
    y = jax.nn.gelu(sy_ref[...] + s5d_ref[...] * su_ref[...])
    b2 = y * _sigmoid(_bdot(y, wglu_ref[...]))
    wy = mix[:, 2 * BRANCH_W:3 * BRANCH_W]
    mu = seg_mean(wy)
    wc = wy - mu
    var = seg_mean(wc * wc)
    b3 = (wc * lax.rsqrt(var + RWKV_GN_EPS) * wgn_g[...] + wgn_b[...] + wbon_ref[...]) * wg_ref[...]

    m = None
    gates = (gate0_ref, gate1_ref, gate2_ref, gate3_ref)
    for i, br in enumerate((b0, b1, b2, b3)):
        term = _bdot(br, wbr_ref[i]) * _sigmoid(gates[i][...])
        m = term if m is None else m + term
    h = _bdot(m, wo_ref[...])
    o_ref[...] = _layer_norm(alpha * x_ref[...] + h, ln_g[...], ln_b[...])


def _embed_ln2(x, f, p_ref, wpe_ref, wpg_ref, ln_g, ln_b, alpha):
    e = _bdot(p_ref[...], wpe_ref[...]) * _sigmoid(_bdot(x, wpg_ref[...]))
    return _layer_norm(alpha * x + f + e, ln_g[...], ln_b[...])


def _ffn_kernel(x_ref, p_ref, w1_ref, w3_ref, w2_ref, wpe_ref, wpg_ref, ln_g, ln_b, o_ref, *, alpha):
    x = x_ref[...]
    xb = x.astype(BF16)
    h = _silu(jnp.dot(xb, w1_ref[...], preferred_element_type=F32)) * jnp.dot(
        xb, w3_ref[...], preferred_element_type=F32)
    f = _bdot(h, w2_ref[...])
    o_ref[...] = _embed_ln2(x, f, p_ref, wpe_ref, wpg_ref, ln_g, ln_b, alpha)


def _router_kernel(x_ref, wh_ref, wl_ref, idx_ref, wgt_ref):
    xh, xl = _split_bf16(x_ref[...])
    wh = wh_ref[...]
    logits = (jnp.dot(xh, wh, preferred_element_type=F32)
              + jnp.dot(xl, wh, preferred_element_type=F32)
              + jnp.dot(xh, wl_ref[...], preferred_element_type=F32))
    col = lax.broadcasted_iota(jnp.int32, logits.shape, 1)
    neg = jnp.float32(-jnp.inf)
    lg = jnp.where(col < N_EXPERTS, logits, neg)
    m1 = jnp.max(lg, axis=1, keepdims=True)
    i1 = jnp.min(jnp.where(lg == m1, col, LANES), axis=1, keepdims=True)
    lg2 = jnp.where(col == i1, neg, lg)
    m2 = jnp.max(lg2, axis=1, keepdims=True)
    i2 = jnp.min(jnp.where(lg2 == m2, col, LANES), axis=1, keepdims=True)
    e2 = jnp.exp(m2 - m1)
    den = 1.0 + e2
    idx_ref[...] = jnp.where(col == 0, i1, jnp.where(col == 1, i2, 0))
    wgt_ref[...] = jnp.where(col == 0, 1.0 / den, jnp.where(col == 1, e2 / den, 0.0))


def _row_copies(idx_ref, n_rows, make_copy, count=None):
    def each(action):
        def body(r, c):
            action(make_copy(r, idx_ref[0, 0, r]))
            return c

        if count is None:
            lax.fori_loop(0, n_rows, body, 0, unroll=8)
            return

        @pl.when(count == n_rows)
        def _():
            lax.fori_loop(0, n_rows, body, 0, unroll=8)

        @pl.when(count < n_rows)
        def _():
            lax.fori_loop(0, count, body, 0)

    return (lambda: each(lambda cp: cp.start())), (lambda: each(lambda cp: cp.wait()))


def _moe_block_kernel(be_ref, nb_ref, bv_ref, tok_ref, tok_next_ref, dst_ref, dst_m1_ref, dst_m2_ref,
                      x_hbm, w1_ref, w3_ref, w2_ref, out_hbm, xbuf, ybuf, sem_in, sem_out):
    j = pl.program_id(0)
    last = pl.num_programs(0) - 1
    nb = nb_ref[0]
    slot = j % 2

    def gather_copy(idx_ref, half, r):
        return pltpu.make_async_copy(x_hbm.at[pl.ds(idx_ref[0, 0, r], 1)], xbuf.at[half, pl.ds(r, 1)],
                                     sem_in.at[half])

    def gather(idx_ref, half):
        return _row_copies(idx_ref, MOE_BLOCK, lambda r, row: pltpu.make_async_copy(
            x_hbm.at[pl.ds(row, 1)], xbuf.at[half, pl.ds(r, 1)], sem_in.at[half]))

    def scatter(idx_ref, half, block):
        return _row_copies(idx_ref, MOE_BLOCK, lambda r, row: pltpu.make_async_copy(
            ybuf.at[half, pl.ds(r, 1)], out_hbm.at[pl.ds(row, 1)], sem_out.at[half]),
            count=bv_ref[jnp.maximum(block, 0)])

    @pl.when(jnp.logical_and(j == 0, nb > 0))
    def _():
        gather(tok_ref, 0)[0]()

    def step(cur):
        gather(tok_ref, cur)[1]()

        @pl.when(j >= 2)
        def _():
            scatter(dst_m2_ref, cur, j - 2)[1]()

        for r in range(MOE_BLOCK):
            gather_copy(tok_next_ref, 1 - cur, r).start()
        xb = xbuf[cur].astype(BF16)
        h = _silu(jnp.dot(xb, w1_ref[0], preferred_element_type=F32)) * jnp.dot(
            xb, w3_ref[0], preferred_element_type=F32)
        ybuf[cur] = _bdot(h, w2_ref[0])
        scatter(dst_ref, cur, j)[0]()

    for par in range(2):
        pl.when(jnp.logical_and(j < nb, slot == par))(functools.partial(step, par))

    @pl.when(jnp.logical_and(j == nb, nb > 0))
    def _():
        gather(tok_ref, slot)[1]()

    @pl.when(jnp.logical_and(j >= nb, jnp.logical_and(j >= 2, j - 2 < nb)))
    def _():
        scatter(dst_m2_ref, slot, j - 2)[1]()

    @pl.when(jnp.logical_and(j == last, jnp.logical_and(j >= 1, j - 1 < nb)))
    def _():
        scatter(dst_m1_ref, 1 - slot, j - 1)[1]()

    @pl.when(jnp.logical_and(j == last, j < nb))
    def _():
        scatter(dst_ref, slot, j)[1]()


def _moe_blocks(x, block_e, nb_used, block_valid, slot_tok, slot_dst, n_out_rows, w1, w3, w2):
    n_blocks = block_e.shape[0]
    d = x.shape[1]
    dff = w1.shape[2]
    tok = slot_tok.reshape(n_blocks, 1, MOE_BLOCK)
    dst = slot_dst.reshape(n_blocks, 1, MOE_BLOCK)

    def idx_spec(shift):
        return pl.BlockSpec((1, 1, MOE_BLOCK),
                            lambda j, be, nb, bv: (jnp.clip(j + shift, 0, n_blocks - 1), 0, 0),
                            memory_space=pltpu.SMEM)

    grid_spec = pltpu.PrefetchScalarGridSpec(
        num_scalar_prefetch=3,
        grid=(n_blocks + 1,),
        in_specs=[
            idx_spec(0), idx_spec(1), idx_spec(0), idx_spec(-1), idx_spec(-2),
            pl.BlockSpec(memory_space=pl.ANY),
            pl.BlockSpec((1, d, dff), lambda j, be, nb, bv: (be[jnp.minimum(j, n_blocks - 1)], 0, 0),
                         pipeline_mode=pl.Buffered(1)),
            pl.BlockSpec((1, d, dff), lambda j, be, nb, bv: (be[jnp.minimum(j, n_blocks - 1)], 0, 0),
                         pipeline_mode=pl.Buffered(1)),
            pl.BlockSpec((1, dff, d), lambda j, be, nb, bv: (be[jnp.minimum(j, n_blocks - 1)], 0, 0),
                         pipeline_mode=pl.Buffered(1)),
        ],
        out_specs=pl.BlockSpec(memory_space=pl.ANY),
        scratch_shapes=[pltpu.VMEM((2, MOE_BLOCK, d), F32), pltpu.VMEM((2, MOE_BLOCK, d), F32),
                        pltpu.SemaphoreType.DMA((2,)), pltpu.SemaphoreType.DMA((2,))],
    )
    return pl.pallas_call(
        _moe_block_kernel,
        grid_spec=grid_spec,
        out_shape=jax.ShapeDtypeStruct((n_out_rows, d), F32),
        compiler_params=_cparams("arbitrary"),
        name="moe_blocks",
    )(block_e, nb_used, block_valid, tok, tok, dst, dst, dst, x, w1, w3, w2)


def _moe_combine_kernel(y0_ref, y1_ref, x_ref, p_ref, wgt_ref, wpe_ref, wpg_ref, ln_g, ln_b, o_ref, *, alpha):
    wgt = wgt_ref[...]
    f = y0_ref[...] * wgt[:, 0:1] + y1_ref[...] * wgt[:, 1:2]
    o_ref[...] = _embed_ln2(x_ref[...], f, p_ref, wpe_ref, wpg_ref, ln_g, ln_b, alpha)


def _lane_groups(b_n):
    bh = b_n * HEADS
    return LANES // bh if bh < LANES else 1


def _state_to_lanes(s, vl_n, value_last):
    s = s.transpose(3, 2, 1, 0) if value_last else s.transpose(2, 3, 1, 0)
    v_n, k_n, h_n, b_n = s.shape
    s = s.reshape(vl_n, v_n // vl_n, k_n, h_n, b_n).transpose(1, 2, 0, 3, 4)
    return s.reshape(v_n // vl_n, k_n, vl_n * h_n * b_n)


def _state_from_lanes(s, b_n, vl_n, value_last):
    vh_n, k_n, _ = s.shape
    s = s.reshape(vh_n, k_n, vl_n, HEADS, b_n).transpose(4, 3, 1, 2, 0).reshape(b_n, HEADS, k_n, vl_n * vh_n)
    return s if value_last else s.transpose(0, 1, 3, 2)


def _rotary_tables(pos, b_n):
    half = HEAD_D // 2
    freq = ROPE_BASE ** (-jnp.arange(half, dtype=F32) / half)
    ang = pos.astype(F32)[:, None] * freq[None, :]
    cos, sin = jnp.cos(ang), jnp.sin(ang)
    cos_h = jnp.tile(jnp.concatenate([cos, cos], axis=-1), (1, HEADS))
    sin_h = jnp.tile(jnp.concatenate([-sin, sin], axis=-1), (1, HEADS))
    return jnp.repeat(cos_h, b_n, axis=0), jnp.repeat(sin_h, b_n, axis=0)


def _block_diag_const(block, n_blocks):
    return jnp.kron(jnp.eye(n_blocks, dtype=F32), jnp.full((block, block), 1.0, F32))


def _s5_params(log_dt, a_re, a_im, b_re, b_im, c_re, c_im):
    dt = jnp.exp(log_dt)[:, None]
    mag = jnp.exp(dt * a_re)
    ang = dt * a_im
    abar_re, abar_im = mag * jnp.cos(ang), mag * jnp.sin(ang)
    den = a_re * a_re + a_im * a_im
    n_re = abar_re - 1.0
    f_re = (n_re * a_re + abar_im * a_im) / den
    f_im = (abar_im * a_re - n_re * a_im) / den
    bb_re = f_re[..., None] * b_re - f_im[..., None] * b_im
    bb_im = f_re[..., None] * b_im + f_im[..., None] * b_re
    eye = jnp.eye(S5_GROUPS, dtype=F32)

    def in_map(bb):
        return jnp.einsum("gpc,gh->gchp", bb, eye).reshape(BRANCH_W, S5_CH)

    def out_map(cm):
        return jnp.einsum("gcp,gh->gphc", cm, eye).reshape(S5_CH, BRANCH_W)

    bb = jnp.concatenate([in_map(bb_re), in_map(bb_im)], axis=1).astype(BF16)
    cc = jnp.concatenate([out_map(c_re), -out_map(c_im)], axis=0).astype(BF16)
    return bb, cc, abar_re.reshape(1, S5_CH), abar_im.reshape(1, S5_CH)


def _reorder_w_in(w_in):
    d_model = w_in.shape[0]
    main = jnp.concatenate([w_in[:, :1792], w_in[:, 1808:]], axis=1)
    tail = jnp.concatenate([w_in[:, 1792:1808],
                            jnp.zeros((d_model, D_IN_PAD - COL_GLA_R - GLA_GATE_RANK), w_in.dtype)], axis=1)
    return jnp.concatenate([main, tail], axis=1).astype(BF16)


def _pad_cols(w, n):
    return jnp.pad(w, ((0, 0), (0, n - w.shape[1])))


def _pad_rows(w, n):
    return jnp.pad(w, ((0, n - w.shape[0]), (0, 0)))


def _moe_route(idx, n):
    nk = n * TOP_K
    flat_e = idx.reshape(nk)
    onehot = (flat_e[:, None] == jnp.arange(N_EXPERTS, dtype=jnp.int32)[None, :]).astype(jnp.int32)
    incl = jnp.cumsum(onehot, axis=0)
    counts = incl[-1]
    rank = jnp.sum((incl - onehot) * onehot, axis=1)
    padded = (counts + MOE_BLOCK - 1) // MOE_BLOCK * MOE_BLOCK
    pad_end = jnp.cumsum(padded)
    slot = (pad_end - padded)[flat_e] + rank
    n_blocks = -(-(nk + N_EXPERTS * (MOE_BLOCK - 1)) // MOE_BLOCK)
    cap = n_blocks * MOE_BLOCK
    pair = jnp.full((cap,), -1, jnp.int32).at[slot].set(jnp.arange(nk, dtype=jnp.int32))
    real = pair >= 0
    slot_tok = jnp.where(real, pair // TOP_K, 0)
    slot_dst = jnp.where(real, (pair % TOP_K) * n + pair // TOP_K, 0)
    block_start = jnp.arange(n_blocks, dtype=jnp.int32) * MOE_BLOCK
    block_e = jnp.minimum(jnp.sum((pad_end[None, :] <= block_start[:, None]).astype(jnp.int32), axis=1),
                          N_EXPERTS - 1).astype(jnp.int32)
    nb_used = (pad_end[-1] // MOE_BLOCK).astype(jnp.int32).reshape(1)
    block_valid = jnp.sum(real.reshape(n_blocks, MOE_BLOCK).astype(jnp.int32), axis=1)
    return slot_tok, slot_dst, block_e, nb_used, block_valid, nk


def kernel(x_prompt, x_sample, state_ret, state_gla, state_s5_re, state_s5_im, state_rwkv, state_shift,
           p_prompt, p_sample, w_in, ret_gn_g, ret_gn_b, gla_wg2, gla_bg, gla_gn,
           s5_log_dt, s5_a_re, s5_a_im, s5_b_re, s5_b_im, s5_c_re, s5_c_im, s5_d, s5_w_glu,
           rwkv_mu, rwkv_w0, rwkv_w1, rwkv_w2, rwkv_a0, rwkv_a1, rwkv_a2, rwkv_g1, rwkv_g2,
           rwkv_kk, rwkv_ka, rwkv_rk, rwkv_gn_g, rwkv_gn_b, w_branch, w_o,
           ln1_g, ln1_b, ln2_g, ln2_b, w_pe, w_pg, ffn_w1, ffn_w3, ffn_w2,
           moe_router, moe_w1, moe_w3, moe_w2):
    depth = w_in.shape[0]
    bp, tp, d_model = x_prompt.shape
    bs, ts, _ = x_sample.shape
    n_p, n_s = bp * tp, bs * ts
    n = n_p + n_s
    tm = ROW_TILE
    assert n_p % tm == 0 and n_s % tm == 0 and tm % bp == 0 and tm % bs == 0 and n_p % bs == 0
    alpha = (2 * depth) ** 0.25
    groups = [(0, bp, tp), (n_p, bs, ts)]
    w4 = 4 * BRANCH_W

    def time_major(a_p, a_s):
        return jnp.concatenate([a_p.transpose(1, 0, 2).reshape(n_p, -1),
                                a_s.transpose(1, 0, 2).reshape(n_s, -1)], axis=0)

    x = time_major(x_prompt, x_sample)
    cos_p, sin_p = _rotary_tables(jnp.arange(tp, dtype=jnp.int32), bp)
    cos_s, sin_s = _rotary_tables(PAST_LEN + jnp.arange(ts, dtype=jnp.int32), bs)
    ones_bd = _block_diag_const(HEAD_D, HEADS).astype(BF16)
    avg_bd = (_block_diag_const(HEAD_D, HEADS) / HEAD_D).astype(BF16)
    gamma = 1.0 - jnp.exp2(-5.0 - jnp.arange(HEADS, dtype=F32))
    row = lambda v: v.reshape(1, -1)

    new = [[] for _ in range(6)]
    for i in range(depth):
        cols = _matmul(x, _reorder_w_in(w_in[i]), tm=512, tn=D_IN_PAD // 3, name="in_proj")

        prep_consts = [
            _pad_rows(_pad_cols(gla_wg2[i], LANES), LANES).astype(BF16), row(gla_bg[i]),
            rwkv_mu[i], row(rwkv_w0[i]),
            _pad_cols(rwkv_w1[i], LANES).astype(BF16), _pad_rows(rwkv_w2[i], LANES).astype(BF16),
            row(rwkv_a0[i]),
            _pad_cols(rwkv_a1[i], LANES).astype(BF16), _pad_rows(rwkv_a2[i], LANES).astype(BF16),
            _pad_cols(rwkv_g1[i], LANES).astype(BF16), _pad_rows(rwkv_g2[i], LANES).astype(BF16),
            row(rwkv_kk[i]), row(rwkv_ka[i]), row(rwkv_rk[i]), ones_bd,
        ]
        post_consts = [avg_bd, row(ret_gn_g[i]), row(ret_gn_b[i]), row(gla_gn[i]), row(s5_d[i]),
                       s5_w_glu[i].astype(BF16), row(rwkv_gn_g[i]), row(rwkv_gn_b[i]),
                       w_branch[i].astype(BF16), w_o[i].astype(BF16), row(ln1_g[i]), row(ln1_b[i])]
        s5p = _s5_params(s5_log_dt[i], s5_a_re[i], s5_a_im[i], s5_b_re[i], s5_b_im[i], s5_c_re[i], s5_c_im[i])
        rwkv_blk = COL_RWKV // w4
        x1_parts, layer_new = [], []
        for gi, (off, b_n, t_n) in enumerate(groups):
            n_g = b_n * t_n
            t0 = off // tm
            if gi == 0:
                st = [jnp.zeros((b_n,) + s.shape[2:], s.dtype)
                      for s in (state_ret, state_gla, state_s5_re, state_s5_im, state_rwkv)]
                shift0, cos_g, sin_g = jnp.zeros((b_n, w4), F32), cos_p, sin_p
            else:
                st = [state_ret[i], state_gla[i], state_s5_re[i], state_s5_im[i], state_rwkv[i]]
                shift0, cos_g, sin_g = state_shift[i], cos_s, sin_s
            vl_n = _lane_groups(b_n)
            grouped = vl_n > 1

            def cspec(width, cb, t0=t0):
                return pl.BlockSpec((tm, width), lambda r: (r + t0, cb))

            if grouped:
                t_shapes = [(2 * HEAD_D, n_g * vl_n * HEADS), (3 * GLA_DK, n_g * vl_n * HEADS),
                            (5 * HEAD_D, n_g * vl_n * HEADS), (3 * HEAD_D, n_g * HEADS)]
            else:
                t_shapes = [(2 * BRANCH_W, n_g), (3 * GLA_QK, n_g), (5 * BRANCH_W, n_g), (3 * BRANCH_W, n_g)]
            consts = [shift0] + prep_consts
            ka_ret, ka_gla, ka_rwkv, va, obonus, og = pl.pallas_call(
                functools.partial(_prep_kernel, grouped=grouped),
                grid=(n_g // tm,),
                in_specs=[cspec(BRANCH_W, 0), cspec(BRANCH_W, 1), cspec(BRANCH_W, 2),
                          _row_spec(tm, BRANCH_W, 0), _row_spec(tm, BRANCH_W, 0),
                          cspec(GLA_QK, COL_GLA_Q // GLA_QK), cspec(GLA_QK, COL_GLA_K // GLA_QK),
                          cspec(BRANCH_W, COL_GLA_V // BRANCH_W), cspec(LANES, COL_GLA_R // LANES),
                          cspec(w4, rwkv_blk),
                          pl.BlockSpec((b_n, w4), lambda r, t0=t0, b_n=b_n:
                                       (jnp.maximum((r + t0) * (tm // b_n) - 1, 0), rwkv_blk))]
                         + [_const_spec(a.shape) for a in consts],
                out_specs=[pl.BlockSpec((sh[0], sh[1] // (n_g // tm)), lambda r: (0, r)) for sh in t_shapes]
                          + [_row_spec(tm, BRANCH_W, 0), _row_spec(tm, BRANCH_W, 0)],
                out_shape=[jax.ShapeDtypeStruct(sh, F32) for sh in t_shapes]
                          + [jax.ShapeDtypeStruct((n_g, BRANCH_W), F32)] * 2,
                compiler_params=_cparams("parallel"),
                name="mixer_prep",
            )(cols, cols, cols, cos_g, sin_g, cols, cols, cols, cols, cols, cols, *consts)

            dec = jnp.tile(jnp.repeat(gamma, b_n), vl_n).reshape(1, vl_n * HEADS * b_n)
            if not grouped:
                ka_ret = ka_ret.reshape(2, HEADS, HEAD_D, n_g)
                ka_gla = ka_gla.reshape(3, HEADS, GLA_DK, n_g)
                ka_rwkv = ka_rwkv.reshape(5, HEADS, HEAD_D, n_g)
                va = va.reshape(3, HEADS, HEAD_D, n_g)
            o, s_ret, s_gla, s_rwkv = _scan(
                ka_ret, ka_gla, ka_rwkv, va, b_n, t_n, dec,
                _state_to_lanes(st[0], vl_n, True), _state_to_lanes(st[1], vl_n, True),
                _state_to_lanes(st[4], vl_n, False))
            mixt = o if grouped else o.reshape(3 * BRANCH_W, n_g)
            y, hr, hi = _s5_scan(cols, off, *s5p, st[2].reshape(b_n, S5_CH), st[3].reshape(b_n, S5_CH), b_n, t_n)
            shift_new = cols[off + (t_n - 1) * b_n:off + t_n * b_n, COL_RWKV:COL_RWKV + w4]
            layer_new.append((_state_from_lanes(s_ret, b_n, vl_n, True), _state_from_lanes(s_gla, b_n, vl_n, True),
                              hr.reshape(b_n, S5_GROUPS, S5_STATE), hi.reshape(b_n, S5_GROUPS, S5_STATE),
                              _state_from_lanes(s_rwkv, b_n, vl_n, False), shift_new))

            x1_parts.append(pl.pallas_call(
                functools.partial(_post_kernel, alpha=alpha, b_n=b_n),
                grid=(n_g // tm,),
                in_specs=[cspec(d_model, 0),
                          pl.BlockSpec((mixt.shape[0], mixt.shape[1] // (n_g // tm)), lambda r: (0, r)),
                          cspec(BRANCH_W, 3), cspec(BRANCH_W, COL_GLA_G // BRANCH_W),
                          _row_spec(tm, BRANCH_W, 0), cspec(BRANCH_W, COL_S5 // BRANCH_W),
                          _row_spec(tm, BRANCH_W, 0), _row_spec(tm, BRANCH_W, 0)]
                         + [cspec(d_model, COL_GATE // d_model + q) for q in range(N_BRANCH)]
                         + [_const_spec(a.shape) for a in post_consts],
                out_specs=_row_spec(tm, d_model, 0),
                out_shape=jax.ShapeDtypeStruct((n_g, d_model), F32),
                compiler_params=_cparams("parallel"),
                name="mixer_post",
            )(x, mixt, cols, cols, y, cols, obonus, og, cols, cols, cols, cols, *post_consts))
        for lst, pair in zip(new, zip(*layer_new)):
            lst.append(pair)
        x1 = jnp.concatenate(x1_parts, axis=0)

        p = time_major(p_prompt[i], p_sample[i])
        tail_consts = [w_pe[i].astype(BF16), w_pg[i].astype(BF16), row(ln2_g[i]), row(ln2_b[i])]
        j = i // 2
        if i % 2 == 0:
            (x,) = _rowwise(
                functools.partial(_ffn_kernel, alpha=alpha), n,
                [(x1, d_model, 0), (p, p.shape[1], 0)],
                [ffn_w1[j].astype(BF16), ffn_w3[j].astype(BF16), ffn_w2[j].astype(BF16)] + tail_consts,
                [d_model], name="ffn")
        else:
            rh, rl = _split_bf16(_pad_cols(moe_router[j], LANES))
            idx, wgt = _rowwise(_router_kernel, n, [(x1, d_model, 0)], [rh, rl], [LANES, LANES],
                                out_dtypes=[jnp.int32, F32], name="moe_router")
            slot_tok, slot_dst, block_e, nb_used, block_valid, n_out_rows = _moe_route(idx[:, :TOP_K], n)
            yk = _moe_blocks(x1, block_e, nb_used, block_valid, slot_tok, slot_dst, n_out_rows,
                             moe_w1[j].astype(BF16), moe_w3[j].astype(BF16), moe_w2[j].astype(BF16))
            (x,) = _rowwise(
                functools.partial(_moe_combine_kernel, alpha=alpha), n,
                [(yk, d_model, 0), (yk, d_model, 0, n // tm), (x1, d_model, 0), (p, p.shape[1], 0), (wgt, LANES, 0)],
                tail_consts, [d_model], name="moe_combine")

    y_prompt = x[:n_p].reshape(tp, bp, d_model).transpose(1, 0, 2)
    y_sample = x[n_p:].reshape(ts, bs, d_model).transpose(1, 0, 2)
    outs = [y_prompt, y_sample]
    for lst in new:
        outs.append(jnp.stack([pair[0] for pair in lst], 0))
        outs.append(jnp.stack([pair[1] for pair in lst], 0))
    return tuple(outs)
```

```python
import functools

import jax
import jax.numpy as jnp
from jax import lax
from jax.experimental import pallas as pl
from jax.experimental.pallas import tpu as pltpu

F32 = jnp.float32
BF16 = jnp.bfloat16

LANES = 128
SUBLANES = 8
VMEM_LIMIT = 56 * 1024 * 1024

N_BRANCH = 4
BRANCH_W = 256
HEADS = 4
HEAD_D = 64
GLA_DK = 32
GLA_QK = HEADS * GLA_DK
GLA_GATE_RANK = 16
GLA_GATE_NORM = 16.0
S5_GROUP = 16
S5_GROUPS = 16
S5_STATE = 64
S5_CH = S5_GROUPS * S5_STATE
ROPE_BASE = 10000.0
RWKV_GN_EPS = 64e-5
LN_EPS = 1e-5
N_EXPERTS = 8
TOP_K = 2
MOE_BLOCK = 256
ROW_TILE = 256
SCAN_CHUNK = 32
PAST_LEN = 16384

COL_RET = 0
COL_GLA_Q = 1024
COL_GLA_K = 1152
COL_GLA_V = 1280
COL_GLA_G = 1536
COL_S5 = 1792
COL_RWKV = 2048
COL_GATE = 3072
COL_GLA_R = 7168
D_IN_PAD = 7296


def _cparams(*sem):
    return pltpu.CompilerParams(dimension_semantics=sem, vmem_limit_bytes=VMEM_LIMIT)


def _split_bf16(x):
    hi = x.astype(BF16)
    lo = (x - hi.astype(F32)).astype(BF16)
    return hi, lo


def _seg_dot(x, m_ref):
    hi, lo = _split_bf16(x)
    m = m_ref[...]
    return (jnp.dot(hi, m, preferred_element_type=F32)
            + jnp.dot(lo, m, preferred_element_type=F32))


def _bdot(x, w):
    return jnp.dot(x.astype(BF16), w, preferred_element_type=F32)


def _sigmoid(x):
    return 1.0 / (1.0 + jnp.exp(-x))


def _silu(x):
    return x * _sigmoid(x)


def _log1p_exp_neg_abs(x):
    return jnp.log1p(jnp.exp(-jnp.abs(x)))


def _layer_norm(x, g, b):
    mu = jnp.mean(x, axis=-1, keepdims=True)
    xc = x - mu
    var = jnp.mean(xc * xc, axis=-1, keepdims=True)
    return xc * lax.rsqrt(var + LN_EPS) * g + b


def _row_spec(tm, width, col_block, row_block0=0):
    return pl.BlockSpec((tm, width), lambda i, cb=col_block, r0=row_block0: (i + r0, cb))


def _const_spec(shape):
    nd = len(shape)
    return pl.BlockSpec(shape, lambda i, nd=nd: (0,) * nd)


def _rowwise(body, n_rows, row_in, const_in, out_widths, out_dtypes=None, tm=ROW_TILE, name=None):
    assert n_rows % tm == 0
    out_dtypes = out_dtypes or [F32] * len(out_widths)
    in_specs = [_row_spec(tm, *spec[1:]) for spec in row_in] + [_const_spec(a.shape) for a in const_in]
    out_specs = [_row_spec(tm, w, 0) for w in out_widths]
    out_shape = [jax.ShapeDtypeStruct((n_rows, w), dt) for w, dt in zip(out_widths, out_dtypes)]
    return pl.pallas_call(
        body,
        grid=(n_rows // tm,),
        in_specs=in_specs,
        out_specs=out_specs,
        out_shape=out_shape,
        compiler_params=_cparams("parallel"),
        name=name,
    )(*[spec[0] for spec in row_in], *const_in)


def _matmul_kernel(x_ref, w_ref, o_ref):
    o_ref[...] = _bdot(x_ref[...], w_ref[...]).astype(o_ref.dtype)


def _matmul(x, w, tm=512, tn=None, out_dtype=F32, name=None):
    m, k = x.shape
    n = w.shape[1]
    tn = tn or n
    tm = tm if m % tm == 0 else ROW_TILE
    assert m % tm == 0 and n % tn == 0
    return pl.pallas_call(
        _matmul_kernel,
        grid=(n // tn, m // tm),
        in_specs=[pl.BlockSpec((tm, k), lambda j, i: (i, 0)),
                  pl.BlockSpec((k, tn), lambda j, i: (0, j))],
        out_specs=pl.BlockSpec((tm, tn), lambda j, i: (i, j)),
        out_shape=jax.ShapeDtypeStruct((m, n), out_dtype),
        compiler_params=_cparams("parallel", "parallel"),
        name=name,
    )(x, w)


def _replicate_groups(w, n_grp):
    if n_grp == 1:
        return [w]
    grp = lax.broadcasted_iota(jnp.int32, w.shape, 1) // (LANES // n_grp)
    parts = [w]
    span = n_grp
    while span > 1:
        half = span // 2
        shift = half * (LANES // n_grp)
        low = (grp % span) < half
        nxt = []
        for z in parts:
            rz = pltpu.roll(z, shift, 1)
            nxt.append(jnp.where(low, z, rz))
            nxt.append(jnp.where(low, rz, z))
        parts = nxt
        span = half
    return parts


def _heads_to_lanes(ops, b_n, pad_to=LANES):
    tm = ops[0].shape[0]
    c = ops[0].shape[1] // HEADS
    packed = []
    for h in range(HEADS):
        pieces = [o[:, h * c:(h + 1) * c] for o in ops]
        if len(pieces) * c < pad_to:
            pieces.append(jnp.zeros((tm, pad_to - len(pieces) * c), F32))
        packed.append(jnp.concatenate(pieces, axis=1).reshape(tm // b_n, b_n, pad_to))
    y = jnp.stack(packed, axis=1).reshape(tm * HEADS, pad_to)
    return y.T


def _prep_kernel(rq_ref, rk_ref, rv_ref, cq_ref, sq_ref, ck_ref, sk_ref,
                 gq_ref, gk_ref, gv_ref, gr_ref,
                 c_ref, tail_ref, shift_ref, wg2_ref, bg_ref,
                 mu_ref, w0_ref, w1_ref, w2_ref, a0_ref, a1_ref, a2_ref, g1_ref, g2_ref,
                 kkp_ref, kap_ref, rkp_ref, ones_ref,
                 kr_ref, kg_ref, kw_ref, va_ref, obonus_ref, og_ref, *, grouped):
    i = pl.program_id(0)
    w = BRANCH_W
    lane = lax.broadcasted_iota(jnp.int32, rq_ref.shape, 1)
    first_half = (lane % HEAD_D) < (HEAD_D // 2)

    def rot(x, cos_ref, sin_ref):
        partner = jnp.where(first_half,
                            pltpu.roll(x, BRANCH_W - HEAD_D // 2, 1),
                            pltpu.roll(x, HEAD_D // 2, 1))
        return x * cos_ref[...] + partner * sin_ref[...]

    ret_q = rot(rq_ref[...], cq_ref, sq_ref)
    ret_k = rot(rk_ref[...], ck_ref, sk_ref)

    z = _bdot(gr_ref[...], wg2_ref[...]) + bg_ref[...]
    glog = (jnp.minimum(z, 0.0) - _log1p_exp_neg_abs(z)) / GLA_GATE_NORM
    gla_q = gq_ref[...] * (GLA_DK ** -0.5)
    gla_al = jnp.exp(glog)

    c = c_ref[...]
    tm = c.shape[0]
    b_n = tail_ref.shape[0]
    tail = jnp.where(i == 0, shift_ref[...], tail_ref[...])
    d = jnp.concatenate([tail, c[:tm - b_n]], axis=0) - c
    mu = mu_ref[...]
    cr, ck, cv, cz = (c[:, j * w:(j + 1) * w] for j in range(4))
    dr, dk, dv, dz = (d[:, j * w:(j + 1) * w] for j in range(4))
    r = cr + dr * mu[0:1]
    k = ck + dk * mu[1:2]
    v = cv + dv * mu[2:3]
    zw = cz + dz * mu[3:4]
    za = cz + dz * mu[4:5]
    zg = cz + dz * mu[5:6]
    w_raw = w0_ref[...] + _bdot(jnp.tanh(_bdot(zw, w1_ref[...])), w2_ref[...])
    sp = jnp.maximum(-w_raw, 0.0) + _log1p_exp_neg_abs(w_raw)
    dec = jnp.exp(-jnp.exp(-sp - 0.5))
    a = _sigmoid(a0_ref[...] + _bdot(_bdot(za, a1_ref[...]), a2_ref[...]))
    og_ref[...] = _bdot(_sigmoid(_bdot(zg, g1_ref[...])), g2_ref[...])
    kk = k * kkp_ref[...]
    ss = _seg_dot(kk * kk, ones_ref)
    kk = kk * lax.rsqrt(jnp.maximum(ss, 1e-24))
    km = k * (1.0 + (a - 1.0) * kap_ref[...])
    obonus_ref[...] = _seg_dot(r * km * rkp_ref[...], ones_ref) * v

    if not grouped:
        for ref, ops in ((kr_ref, (ret_q, ret_k)), (kg_ref, (gla_q, gk_ref[...], gla_al)),
                         (kw_ref, (r, km, dec, kk, kk * a)), (va_ref, (rv_ref[...], gv_ref[...], v))):
            cw = ops[0].shape[1]
            for j, o in enumerate(ops):
                ref[j * cw:(j + 1) * cw, :] = o.T
        return

    n_grp = LANES // (HEADS * b_n)

    def emit_tiles(ref, row0, ops, n_rows):
        wt = _heads_to_lanes(ops, b_n)
        for lt in range(wt.shape[1] // LANES):
            for g, zt in enumerate(_replicate_groups(wt[:n_rows, lt * LANES:(lt + 1) * LANES], n_grp)):
                q = lt * n_grp + g
                ref[row0:row0 + n_rows, q * LANES:(q + 1) * LANES] = zt

    emit_tiles(kr_ref, 0, (ret_q, ret_k), 2 * HEAD_D)
    emit_tiles(kg_ref, 0, (gla_q, gk_ref[...], gla_al), 3 * GLA_DK)
    emit_tiles(kw_ref, 0, (r, km), 2 * HEAD_D)
    emit_tiles(kw_ref, 2 * HEAD_D, (dec, kk), 2 * HEAD_D)
    emit_tiles(kw_ref, 4 * HEAD_D, (kk * a,), HEAD_D)
    va_ref[0:2 * HEAD_D, :] = _heads_to_lanes((rv_ref[...], gv_ref[...]), b_n)
    va_ref[2 * HEAD_D:3 * HEAD_D, :] = _heads_to_lanes((v,), b_n)[:HEAD_D]


def _scan_kernel(kr_ref, kg_ref, kw_ref, va_ref, dec_ref, s0r_ref, s0g_ref, s0w_ref,
                 o_ref, sor_ref, sog_ref, sow_ref,
                 sr, sg, sw, vs, os_, *, vh_n, grouped, tc, b_n):
    ti = pl.program_id(1)

    @pl.when(ti == 0)
    def _():
        sr[...] = s0r_ref[...]
        sg[...] = s0g_ref[...]
        sw[...] = s0w_ref[...]

    n_grp = LANES // (HEADS * b_n) if grouped else 1
    grp_w = LANES // n_grp
    n_v = 3

    def k_tile(ref, n_ops, j, t):
        lanes = pl.ds(pl.multiple_of(t * LANES, LANES), LANES)
        if grouped:
            k_n = ref.shape[0] // n_ops
            return ref[j * k_n:(j + 1) * k_n, lanes]
        return ref[j, 0, :, lanes]

    if grouped:
        grp = lax.broadcasted_iota(jnp.int32, (vh_n, LANES), 1) // grp_w
        for lt in range(tc // n_grp):
            lanes = slice(lt * LANES, (lt + 1) * LANES)
            for j in range(n_v):
                ws = [va_ref[j * HEAD_D + vl * vh_n:j * HEAD_D + (vl + 1) * vh_n, lanes] for vl in range(n_grp)]
                for g in range(n_grp):
                    tile = None
                    for vl in range(n_grp):
                        r = pltpu.roll(ws[vl], ((vl - g) * grp_w) % LANES, 1)
                        tile = r if tile is None else jnp.where(grp == vl, r, tile)
                    vs[lt * n_grp + g, j] = tile
    else:
        def copy_in(t, c):
            lanes = pl.ds(pl.multiple_of(t * LANES, LANES), LANES)
            for j in range(n_v):
                vs[t, j] = va_ref[j, 0, :, lanes]
            return c

        lax.fori_loop(0, tc, copy_in, 0)

    dec_r = dec_ref[...]

    def out(t, j, vh, s, q):
        os_[t, j, pl.ds(vh, 1), :] = jnp.sum(s * q, axis=0, keepdims=True)

    def ret_step(t, c):
        q, k = k_tile(kr_ref, 2, 0, t), k_tile(kr_ref, 2, 1, t)
        for vh in range(vh_n):
            s = sr[vh] + vs[t, 0, pl.ds(vh, 1), :] * k
            sr[vh] = s
            out(t, 0, vh, s, q)
        return c

    def gla_step(t, c):
        q, k, al = (k_tile(kg_ref, 3, j, t) for j in range(3))
        for vh in range(vh_n):
            s = sg[vh] * al + vs[t, 1, pl.ds(vh, 1), :] * k
            sg[vh] = s
            out(t, 1, vh, s, q)
        return c

    def rwkv_step(t, c):
        q, k, dec, kk, beta = (k_tile(kw_ref, 5, j, t) for j in range(5))
        for vh in range(vh_n):
            s = sw[vh]
            sk = jnp.sum(s * kk, axis=0, keepdims=True)
            s = s * dec - sk * beta + vs[t, 2, pl.ds(vh, 1), :] * k
            sw[vh] = s
            out(t, 2, vh, s, q)
        return c

    unroll = 2 if grouped else 1
    lax.fori_loop(0, tc, ret_step, 0, unroll=unroll)
    sr[...] = sr[...] * dec_r
    lax.fori_loop(0, tc, gla_step, 0, unroll=unroll)
    lax.fori_loop(0, tc, rwkv_step, 0, unroll=unroll)

    if grouped:
        grp = lax.broadcasted_iota(jnp.int32, (vh_n, LANES), 1) // grp_w
        for lt in range(tc // n_grp):
            lanes = slice(lt * LANES, (lt + 1) * LANES)
            for j in range(n_v):
                og = [os_[lt * n_grp + g, j] for g in range(n_grp)]
                for vl in range(n_grp):
                    wv = None
                    for g in range(n_grp):
                        r = pltpu.roll(og[g], ((g - vl) * grp_w) % LANES, 1)
                        wv = r if wv is None else jnp.where(grp == g, r, wv)
                    o_ref[j * HEAD_D + vl * vh_n:j * HEAD_D + (vl + 1) * vh_n, lanes] = wv
    else:
        def copy_out(t, c):
            lanes = pl.ds(pl.multiple_of(t * LANES, LANES), LANES)
            for j in range(n_v):
                o_ref[j, 0, :, lanes] = os_[t, j]
            return c

        lax.fori_loop(0, tc, copy_out, 0)

    @pl.when(ti == pl.num_programs(1) - 1)
    def _():
        sor_ref[...] = sr[...]
        sog_ref[...] = sg[...]
        sow_ref[...] = sw[...]


def _scan(kr, kg, kw, va, b_n, t_n, dec, s0r, s0g, s0w):
    grouped = b_n * HEADS < LANES
    vh_n = s0r.shape[0]
    if grouped:
        tc = min(t_n, SCAN_CHUNK)
        n_grp = LANES // (HEADS * b_n)
        assert tc % n_grp == 0 and t_n % tc == 0
        n_l, n_t = 1, t_n // tc
        k_specs = [pl.BlockSpec((a.shape[0], tc * LANES), lambda l, t: (0, t)) for a in (kr, kg, kw)]
        v_spec = pl.BlockSpec((va.shape[0], tc * HEADS * b_n), lambda l, t: (0, t))
        o_spec = v_spec
    else:
        assert b_n == LANES
        tc, n_l, n_t = t_n, HEADS, 1
        k_specs = [pl.BlockSpec((a.shape[0], 1, a.shape[2], tc * LANES), lambda l, t: (0, l, 0, 0))
                   for a in (kr, kg, kw)]
        v_spec = pl.BlockSpec((3, 1, HEAD_D, tc * LANES), lambda l, t: (0, l, 0, 0))
        o_spec = v_spec

    def s_spec(a):
        return pl.BlockSpec(a.shape[:2] + (LANES,), lambda l, t: (0, 0, l))

    return pl.pallas_call(
        functools.partial(_scan_kernel, vh_n=vh_n, grouped=grouped, tc=tc, b_n=b_n),
        grid=(n_l, n_t),
        in_specs=k_specs + [v_spec, pl.BlockSpec((1, LANES), lambda l, t: (0, l)),
                            s_spec(s0r), s_spec(s0g), s_spec(s0w)],
        out_specs=[o_spec, s_spec(s0r), s_spec(s0g), s_spec(s0w)],
        out_shape=[jax.ShapeDtypeStruct(va.shape, F32)] + [jax.ShapeDtypeStruct(s.shape, F32)
                                                           for s in (s0r, s0g, s0w)],
        scratch_shapes=[pltpu.VMEM(s.shape[:2] + (LANES,), F32) for s in (s0r, s0g, s0w)]
                       + [pltpu.VMEM((tc, 3, vh_n, LANES), F32), pltpu.VMEM((tc, 3, vh_n, LANES), F32)],
        compiler_params=_cparams("parallel", "arbitrary"),
        name="scan_ret_gla_rwkv",
    )(kr, kg, kw, va, dec, s0r, s0g, s0w)


def _s5_kernel(u_ref, bb_ref, cc_ref, ar_ref, ai_ref, h0r_ref, h0i_ref,
               y_ref, hr_out, hi_out, hr_scr, hi_scr, xs_scr, hs_scr, *, b_n, tc):
    ti = pl.program_id(0)

    @pl.when(ti == 0)
    def _():
        hr_scr[...] = h0r_ref[...]
        hi_scr[...] = h0i_ref[...]

    xs_scr[...] = _bdot(u_ref[...], bb_ref[...])
    ar = ar_ref[...]
    ai = ai_ref[...]

    def step(t, carry):
        hr, hi = carry
        row = pl.multiple_of(t * b_n, SUBLANES)
        x = xs_scr[pl.ds(row, b_n), :]
        nr = ar * hr - ai * hi + x[:, :S5_CH]
        ni = ar * hi + ai * hr + x[:, S5_CH:]
        hs_scr[pl.ds(row, b_n), :S5_CH] = nr
        hs_scr[pl.ds(row, b_n), S5_CH:] = ni
        return nr, ni

    hr, hi = lax.fori_loop(0, tc, step, (hr_scr[...], hi_scr[...]))
    hr_scr[...] = hr
    hi_scr[...] = hi
    y_ref[...] = _bdot(hs_scr[...], cc_ref[...])

    @pl.when(ti == pl.num_programs(0) - 1)
    def _():
        hr_out[...] = hr
        hi_out[...] = hi


def _s5_scan(cols, row_off, bb, cc, ar, ai, h0r, h0i, b_n, t_n):
    tc = min(t_n, max(1, 1024 // b_n))
    rows = tc * b_n
    assert t_n % tc == 0 and row_off % rows == 0
    blk0 = row_off // rows
    return pl.pallas_call(
        functools.partial(_s5_kernel, b_n=b_n, tc=tc),
        grid=(t_n // tc,),
        in_specs=[pl.BlockSpec((rows, BRANCH_W), lambda t: (blk0 + t, COL_S5 // BRANCH_W)),
                  _const_spec(bb.shape), _const_spec(cc.shape),
                  _const_spec(ar.shape), _const_spec(ai.shape),
                  _const_spec(h0r.shape), _const_spec(h0i.shape)],
        out_specs=[pl.BlockSpec((rows, BRANCH_W), lambda t: (t, 0)),
                   _const_spec(h0r.shape), _const_spec(h0i.shape)],
        out_shape=[jax.ShapeDtypeStruct((t_n * b_n, BRANCH_W), F32),
                   jax.ShapeDtypeStruct(h0r.shape, F32),
                   jax.ShapeDtypeStruct(h0i.shape, F32)],
        scratch_shapes=[pltpu.VMEM((b_n, S5_CH), F32), pltpu.VMEM((b_n, S5_CH), F32),
                        pltpu.VMEM((rows, 2 * S5_CH), F32), pltpu.VMEM((rows, 2 * S5_CH), F32)],
        compiler_params=_cparams("arbitrary"),
        name="scan_s5",
    )(cols, bb, cc, ar, ai, h0r, h0i)


def _post_kernel(x_ref, mixt_ref, rg_ref, gg_ref, sy_ref, su_ref, wbon_ref, wg_ref,
                 gate0_ref, gate1_ref, gate2_ref, gate3_ref,
                 avg_ref, rgn_g, rgn_b, ggn_g, s5d_ref, wglu_ref, wgn_g, wgn_b,
                 wbr_ref, wo_ref, ln_g, ln_b, o_ref, *, alpha, b_n):
    def seg_mean(v):
        return _seg_dot(v, avg_ref)

    tm = x_ref.shape[0]
    if b_n * HEADS < LANES:
        yt = mixt_ref[...].T.reshape(tm // b_n, HEADS, b_n, 3 * HEAD_D)
        per_head = [yt[:, h].reshape(tm, 3 * HEAD_D) for h in range(HEADS)]
        mix = jnp.concatenate([ph[:, j * HEAD_D:(j + 1) * HEAD_D] for j in range(3) for ph in per_head], axis=1)
    else:
        mix = mixt_ref[...].T
    ro = mix[:, 0:BRANCH_W]
    mu = seg_mean(ro)
    rc = ro - mu
    var = seg_mean(rc * rc)
    b0 = (rc * lax.rsqrt(var + LN_EPS) * rgn_g[...] + rgn_b[...]) * _silu(rg_ref[...])
    go = mix[:, BRANCH_W:2 * BRANCH_W]
    ms = seg_mean(go * go)
    b1 = go * lax.rsqrt(ms + LN_EPS) * ggn_g[...] * _silu(gg_ref[...])
    y = jax.nn.gelu(sy_ref[...] + s5d_ref[...] * su_ref[...])
    b2 = y * _sigmoid(_bdot(y, wglu_ref[...]))
    wy = mix[:, 2 * BRANCH_W:3 * BRANCH_W]
    mu = seg_mean(wy)
    wc = wy - mu
    var = seg_mean(wc * wc)
    b3 = (wc * lax.rsqrt(var + RWKV_GN_EPS) * wgn_g[...] + wgn_b[...] + wbon_ref[...]) * wg_ref[...]

    m = None
    gates = (gate0_ref, gate1_ref, gate2_ref, gate3_ref)
    for i, br in enumerate((b0, b1, b2, b3)):
        term = _bdot(br, wbr_ref[i]) * _sigmoid(gates[i][...])
        m = term if m is None else m + term
    h = _bdot(m, wo_ref[...])
    o_ref[...] = _layer_norm(alpha * x_ref[...] + h, ln_g[...], ln_b[...])


def _embed_ln2(x, f, p_ref, wpe_ref, wpg_ref, ln_g, ln_b, alpha):
    e = _bdot(p_ref[...], wpe_ref[...]) * _sigmoid(_bdot(x, wpg_ref[...]))
    return _layer_norm(alpha * x + f + e, ln_g[...], ln_b[...])


def _ffn_kernel(x_ref, p_ref, w1_ref, w3_ref, w2_ref, wpe_ref, wpg_ref, ln_g, ln_b, o_ref, *, alpha):
    x = x_ref[...]
    xb = x.astype(BF16)
    h = _silu(jnp.dot(xb, w1_ref[...], preferred_element_type=F32)) * jnp.dot(
        xb, w3_ref[...], preferred_element_type=F32)
    f = _bdot(h, w2_ref[...])
    o_ref[...] = _embed_ln2(x, f, p_ref, wpe_ref, wpg_ref, ln_g, ln_b, alpha)


def _router_kernel(x_ref, wh_ref, wl_ref, idx_ref, wgt_ref):
    xh, xl = _split_bf16(x_ref[...])
    wh = wh_ref[...]
    logits = (jnp.dot(xh, wh, preferred_element_type=F32)
              + jnp.dot(xl, wh, preferred_element_type=F32)
              + jnp.dot(xh, wl_ref[...], preferred_element_type=F32))
    col = lax.broadcasted_iota(jnp.int32, logits.shape, 1)
    neg = jnp.float32(-jnp.inf)
    lg = jnp.where(col < N_EXPERTS, logits, neg)
    m1 = jnp.max(lg, axis=1, keepdims=True)
    i1 = jnp.min(jnp.where(lg == m1, col, LANES), axis=1, keepdims=True)
    lg2 = jnp.where(col == i1, neg, lg)
    m2 = jnp.max(lg2, axis=1, keepdims=True)
    i2 = jnp.min(jnp.where(lg2 == m2, col, LANES), axis=1, keepdims=True)
    e2 = jnp.exp(m2 - m1)
    den = 1.0 + e2
    idx_ref[...] = jnp.where(col == 0, i1, jnp.where(col == 1, i2, 0))
    wgt_ref[...] = jnp.where(col == 0, 1.0 / den, jnp.where(col == 1, e2 / den, 0.0))


def _row_copies(idx_ref, n_rows, make_copy, count=None):
    def each(action):
        def body(r, c):
            action(make_copy(r, idx_ref[0, 0, r]))
            return c

        if count is None:
            lax.fori_loop(0, n_rows, body, 0, unroll=8)
            return

        @pl.when(count == n_rows)
        def _():
            lax.fori_loop(0, n_rows, body, 0, unroll=8)

        @pl.when(count < n_rows)
        def _():
            lax.fori_loop(0, count, body, 0)

    return (lambda: each(lambda cp: cp.start())), (lambda: each(lambda cp: cp.wait()))


def _moe_block_kernel(be_ref, nb_ref, bv_ref, tok_ref, tok_next_ref, dst_ref, dst_m1_ref, dst_m2_ref,
                      x_hbm, w1_ref, w3_ref, w2_ref, out_hbm, xbuf, ybuf, sem_in, sem_out):
    j = pl.program_id(0)
    last = pl.num_programs(0) - 1
    nb = nb_ref[0]
    slot = j % 2

    def gather_copy(idx_ref, half, r):
        return pltpu.make_async_copy(x_hbm.at[pl.ds(idx_ref[0, 0, r], 1)], xbuf.at[half, pl.ds(r, 1)],
                                     sem_in.at[half])

    def gather(idx_ref, half):
        return _row_copies(idx_ref, MOE_BLOCK, lambda r, row: pltpu.make_async_copy(
            x_hbm.at[pl.ds(row, 1)], xbuf.at[half, pl.ds(r, 1)], sem_in.at[half]))

    def scatter(idx_ref, half, block):
        return _row_copies(idx_ref, MOE_BLOCK, lambda r, row: pltpu.make_async_copy(
            ybuf.at[half, pl.ds(r, 1)], out_hbm.at[pl.ds(row, 1)], sem_out.at[half]),
            count=bv_ref[jnp.maximum(block, 0)])

    @pl.when(jnp.logical_and(j == 0, nb > 0))
    def _():
        gather(tok_ref, 0)[0]()

    def step(cur):
        gather(tok_ref, cur)[1]()

        @pl.when(j >= 2)
        def _():
            scatter(dst_m2_ref, cur, j - 2)[1]()

        for r in range(MOE_BLOCK):
            gather_copy(tok_next_ref, 1 - cur, r).start()
        xb = xbuf[cur].astype(BF16)
        h = _silu(jnp.dot(xb, w1_ref[0], preferred_element_type=F32)) * jnp.dot(
            xb, w3_ref[0], preferred_element_type=F32)
        ybuf[cur] = _bdot(h, w2_ref[0])
        scatter(dst_ref, cur, j)[0]()

    for par in range(2):
        pl.when(jnp.logical_and(j < nb, slot == par))(functools.partial(step, par))

    @pl.when(jnp.logical_and(j == nb, nb > 0))
    def _():
        gather(tok_ref, slot)[1]()

    @pl.when(jnp.logical_and(j >= nb, jnp.logical_and(j >= 2, j - 2 < nb)))
    def _():
        scatter(dst_m2_ref, slot, j - 2)[1]()

    @pl.when(jnp.logical_and(j == last, jnp.logical_and(j >= 1, j - 1 < nb)))
    def _():
        scatter(dst_m1_ref, 1 - slot, j - 1)[1]()

    @pl.when(jnp.logical_and(j == last, j < nb))
    def _():
        scatter(dst_ref, slot, j)[1]()


def _moe_blocks(x, block_e, nb_used, block_valid, slot_tok, slot_dst, n_out_rows, w1, w3, w2):
    n_blocks = block_e.shape[0]
    d = x.shape[1]
    dff = w1.shape[2]
    tok = slot_tok.reshape(n_blocks, 1, MOE_BLOCK)
    dst = slot_dst.reshape(n_blocks, 1, MOE_BLOCK)

    def idx_spec(shift):
        return pl.BlockSpec((1, 1, MOE_BLOCK),
                            lambda j, be, nb, bv: (jnp.clip(j + shift, 0, n_blocks - 1), 0, 0),
                            memory_space=pltpu.SMEM)

    grid_spec = pltpu.PrefetchScalarGridSpec(
        num_scalar_prefetch=3,
        grid=(n_blocks + 1,),
        in_specs=[
            idx_spec(0), idx_spec(1), idx_spec(0), idx_spec(-1), idx_spec(-2),
            pl.BlockSpec(memory_space=pl.ANY),
            pl.BlockSpec((1, d, dff), lambda j, be, nb, bv: (be[jnp.minimum(j, n_blocks - 1)], 0, 0),
                         pipeline_mode=pl.Buffered(1)),
            pl.BlockSpec((1, d, dff), lambda j, be, nb, bv: (be[jnp.minimum(j, n_blocks - 1)], 0, 0),
                         pipeline_mode=pl.Buffered(1)),
            pl.BlockSpec((1, dff, d), lambda j, be, nb, bv: (be[jnp.minimum(j, n_blocks - 1)], 0, 0),
                         pipeline_mode=pl.Buffered(1)),
        ],
        out_specs=pl.BlockSpec(memory_space=pl.ANY),
        scratch_shapes=[pltpu.VMEM((2, MOE_BLOCK, d), F32), pltpu.VMEM((2, MOE_BLOCK, d), F32),
                        pltpu.SemaphoreType.DMA((2,)), pltpu.SemaphoreType.DMA((2,))],
    )
    return pl.pallas_call(
        _moe_block_kernel,
        grid_spec=grid_spec,
        out_shape=jax.ShapeDtypeStruct((n_out_rows, d), F32),
        compiler_params=_cparams("arbitrary"),
        name="moe_blocks",
    )(block_e, nb_used, block_valid, tok, tok, dst, dst, dst, x, w1, w3, w2)


def _moe_combine_kernel(y0_ref, y1_ref, x_ref, p_ref, wgt_ref, wpe_ref, wpg_ref, ln_g, ln_b, o_ref, *, alpha):
    wgt = wgt_ref[...]
    f = y0_ref[...] * wgt[:, 0:1] + y1_ref[...] * wgt[:, 1:2]
    o_ref[...] = _embed_ln2(x_ref[...], f, p_ref, wpe_ref, wpg_ref, ln_g, ln_b, alpha)


def _lane_groups(b_n):
    bh = b_n * HEADS
    return LANES // bh if bh < LANES else 1


def _state_to_lanes(s, vl_n, value_last):
    s = s.transpose(3, 2, 1, 0) if value_last else s.transpose(2, 3, 1, 0)
    v_n, k_n, h_n, b_n = s.shape
    s = s.reshape(vl_n, v_n // vl_n, k_n, h_n, b_n).transpose(1, 2, 0, 3, 4)
    return s.reshape(v_n // vl_n, k_n, vl_n * h_n * b_n)


def _state_from_lanes(s, b_n, vl_n, value_last):
    vh_n, k_n, _ = s.shape
    s = s.reshape(vh_n, k_n, vl_n, HEADS, b_n).transpose(4, 3, 1, 2, 0).reshape(b_n, HEADS, k_n, vl_n * vh_n)
    return s if value_last else s.transpose(0, 1, 3, 2)


def _rotary_tables(pos, b_n, gamma, tc):
    half = HEAD_D // 2
    freq = ROPE_BASE ** (-jnp.arange(half, dtype=F32) / half)
    ang = pos.astype(F32)[:, None] * freq[None, :]
    cos, sin = jnp.cos(ang), jnp.sin(ang)
    cos_h = jnp.tile(jnp.concatenate([cos, cos], axis=-1), (1, HEADS))
    sin_h = jnp.tile(jnp.concatenate([-sin, sin], axis=-1), (1, HEADS))
    tau1 = (jnp.arange(pos.shape[0], dtype=jnp.int32) % tc + 1).astype(F32)
    f_q = jnp.repeat(gamma[None, :] ** tau1[:, None], HEAD_D, axis=1)
    f_k = (HEAD_D ** -0.5) / f_q
    return tuple(jnp.repeat(t, b_n, axis=0) for t in (cos_h * f_q, sin_h * f_q, cos_h * f_k, sin_h * f_k))


def _block_diag_const(block, n_blocks):
    return jnp.kron(jnp.eye(n_blocks, dtype=F32), jnp.full((block, block), 1.0, F32))


def _s5_params(log_dt, a_re, a_im, b_re, b_im, c_re, c_im):
    dt = jnp.exp(log_dt)[:, None]
    mag = jnp.exp(dt * a_re)
    ang = dt * a_im
    abar_re, abar_im = mag * jnp.cos(ang), mag * jnp.sin(ang)
    den = a_re * a_re + a_im * a_im
    n_re = abar_re - 1.0
    f_re = (n_re * a_re + abar_im * a_im) / den
    f_im = (abar_im * a_re - n_re * a_im) / den
    bb_re = f_re[..., None] * b_re - f_im[..., None] * b_im
    bb_im = f_re[..., None] * b_im + f_im[..., None] * b_re
    eye = jnp.eye(S5_GROUPS, dtype=F32)

    def in_map(bb):
        return jnp.einsum("gpc,gh->gchp", bb, eye).reshape(BRANCH_W, S5_CH)

    def out_map(cm):
        return jnp.einsum("gcp,gh->gphc", cm, eye).reshape(S5_CH, BRANCH_W)

    bb = jnp.concatenate([in_map(bb_re), in_map(bb_im)], axis=1).astype(BF16)
    cc = jnp.concatenate([out_map(c_re), -out_map(c_im)], axis=0).astype(BF16)
    return bb, cc, abar_re.reshape(1, S5_CH), abar_im.reshape(1, S5_CH)


def _reorder_w_in(w_in):
    d_model = w_in.shape[0]
    main = jnp.concatenate([w_in[:, :1792], w_in[:, 1808:]], axis=1)
    tail = jnp.concatenate([w_in[:, 1792:1808],
                            jnp.zeros((d_model, D_IN_PAD - COL_GLA_R - GLA_GATE_RANK), w_in.dtype)], axis=1)
    return jnp.concatenate([main, tail], axis=1).astype(BF16)


def _pad_cols(w, n):
    return jnp.pad(w, ((0, 0), (0, n - w.shape[1])))


def _pad_rows(w, n):
    return jnp.pad(w, ((0, n - w.shape[0]), (0, 0)))


def _moe_route(idx, n):
    nk = n * TOP_K
    flat_e = idx.reshape(nk)
    onehot = (flat_e[:, None] == jnp.arange(N_EXPERTS, dtype=jnp.int32)[None, :]).astype(jnp.int32)
    incl = jnp.cumsum(onehot, axis=0)
    counts = incl[-1]
    rank = jnp.sum((incl - onehot) * onehot, axis=1)
    padded = (counts + MOE_BLOCK - 1) // MOE_BLOCK * MOE_BLOCK
    pad_end = jnp.cumsum(padded)
    slot = (pad_end - padded)[flat_e] + rank
    n_blocks = -(-(nk + N_EXPERTS * (MOE_BLOCK - 1)) // MOE_BLOCK)
    cap = n_blocks * MOE_BLOCK
    pair = jnp.full((cap,), -1, jnp.int32).at[slot].set(jnp.arange(nk, dtype=jnp.int32))
    real = pair >= 0
    slot_tok = jnp.where(real, pair // TOP_K, 0)
    slot_dst = jnp.where(real, (pair % TOP_K) * n + pair // TOP_K, 0)
    block_start = jnp.arange(n_blocks, dtype=jnp.int32) * MOE_BLOCK
    block_e = jnp.minimum(jnp.sum((pad_end[None, :] <= block_start[:, None]).astype(jnp.int32), axis=1),
                          N_EXPERTS - 1).astype(jnp.int32)
    nb_used = (pad_end[-1] // MOE_BLOCK).astype(jnp.int32).reshape(1)
    block_valid = jnp.sum(real.reshape(n_blocks, MOE_BLOCK).astype(jnp.int32), axis=1)
    return slot_tok, slot_dst, block_e, nb_used, block_valid, nk


def kernel(x_prompt, x_sample, state_ret, state_gla, state_s5_re, state_s5_im, state_rwkv, state_shift,
           p_prompt, p_sample, w_in, ret_gn_g, ret_gn_b, gla_wg2, gla_bg, gla_gn,
           s5_log_dt, s5_a_re, s5_a_im, s5_b_re, s5_b_im, s5_c_re, s5_c_im, s5_d, s5_w_glu,
           rwkv_mu, rwkv_w0, rwkv_w1, rwkv_w2, rwkv_a0, rwkv_a1, rwkv_a2, rwkv_g1, rwkv_g2,
           rwkv_kk, rwkv_ka, rwkv_rk, rwkv_gn_g, rwkv_gn_b, w_branch, w_o,
           ln1_g, ln1_b, ln2_g, ln2_b, w_pe, w_pg, ffn_w1, ffn_w3, ffn_w2,
           moe_router, moe_w1, moe_w3, moe_w2):
    depth = w_in.shape[0]
    bp, tp, d_model = x_prompt.shape
    bs, ts, _ = x_sample.shape
    n_p, n_s = bp * tp, bs * ts
    n = n_p + n_s
    tm = ROW_TILE
    assert n_p % tm == 0 and n_s % tm == 0 and tm % bp == 0 and tm % bs == 0 and n_p % bs == 0
    alpha = (2 * depth) ** 0.25
    groups = [(0, bp, tp), (n_p, bs, ts)]
    w4 = 4 * BRANCH_W

    def time_major(a_p, a_s):
        return jnp.concatenate([a_p.transpose(1, 0, 2).reshape(n_p, -1),
                                a_s.transpose(1, 0, 2).reshape(n_s, -1)], axis=0)

    x = time_major(x_prompt, x_sample)
    ones_bd = _block_diag_const(HEAD_D, HEADS).astype(BF16)
    avg_bd = (_block_diag_const(HEAD_D, HEADS) / HEAD_D).astype(BF16)
    gamma = 1.0 - jnp.exp2(-5.0 - jnp.arange(HEADS, dtype=F32))
    tc_p, tc_s = min(tp, SCAN_CHUNK), ts
    rot_p = _rotary_tables(jnp.arange(tp, dtype=jnp.int32), bp, gamma, tc_p)
    rot_s = _rotary_tables(PAST_LEN + jnp.arange(ts, dtype=jnp.int32), bs, gamma, tc_s)
    row = lambda v: v.reshape(1, -1)

    new = [[] for _ in range(6)]
    for i in range(depth):
        cols = _matmul(x, _reorder_w_in(w_in[i]), tm=512, tn=D_IN_PAD // 3, name="in_proj")

        prep_consts = [
            _pad_rows(_pad_cols(gla_wg2[i], LANES), LANES).astype(BF16), row(gla_bg[i]),
            rwkv_mu[i], row(rwkv_w0[i]),
            _pad_cols(rwkv_w1[i], LANES).astype(BF16), _pad_rows(rwkv_w2[i], LANES).astype(BF16),
            row(rwkv_a0[i]),
            _pad_cols(rwkv_a1[i], LANES).astype(BF16), _pad_rows(rwkv_a2[i], LANES).astype(BF16),
            _pad_cols(rwkv_g1[i], LANES).astype(BF16), _pad_rows(rwkv_g2[i], LANES).astype(BF16),
            row(rwkv_kk[i]), row(rwkv_ka[i]), row(rwkv_rk[i]), ones_bd,
        ]
        post_consts = [avg_bd, row(ret_gn_g[i]), row(ret_gn_b[i]), row(gla_gn[i]), row(s5_d[i]),
                       s5_w_glu[i].astype(BF16), row(rwkv_gn_g[i]), row(rwkv_gn_b[i]),
                       w_branch[i].astype(BF16), w_o[i].astype(BF16), row(ln1_g[i]), row(ln1_b[i])]
        s5p = _s5_params(s5_log_dt[i], s5_a_re[i], s5_a_im[i], s5_b_re[i], s5_b_im[i], s5_c_re[i], s5_c_im[i])
        rwkv_blk = COL_RWKV // w4
        x1_parts, layer_new = [], []
        for gi, (off, b_n, t_n) in enumerate(groups):
            n_g = b_n * t_n
            t0 = off // tm
            if gi == 0:
                st = [jnp.zeros((b_n,) + s.shape[2:], s.dtype)
                      for s in (state_ret, state_gla, state_s5_re, state_s5_im, state_rwkv)]
                shift0, rot_g, tc_g = jnp.zeros((b_n, w4), F32), rot_p, tc_p
            else:
                st = [state_ret[i], state_gla[i], state_s5_re[i], state_s5_im[i], state_rwkv[i]]
                shift0, rot_g, tc_g = state_shift[i], rot_s, tc_s
            vl_n = _lane_groups(b_n)
            grouped = vl_n > 1

            def cspec(width, cb, t0=t0):
                return pl.BlockSpec((tm, width), lambda r: (r + t0, cb))

            if grouped:
                t_shapes = [(2 * HEAD_D, n_g * vl_n * HEADS), (3 * GLA_DK, n_g * vl_n * HEADS),
                            (5 * HEAD_D, n_g * vl_n * HEADS), (3 * HEAD_D, n_g * HEADS)]
            else:
                t_shapes = [(2 * BRANCH_W, n_g), (3 * GLA_QK, n_g), (5 * BRANCH_W, n_g), (3 * BRANCH_W, n_g)]
            consts = [shift0] + prep_consts
            ka_ret, ka_gla, ka_rwkv, va, obonus, og = pl.pallas_call(
                functools.partial(_prep_kernel, grouped=grouped),
                grid=(n_g // tm,),
                in_specs=[cspec(BRANCH_W, 0), cspec(BRANCH_W, 1), cspec(BRANCH_W, 2),
                          _row_spec(tm, BRANCH_W, 0), _row_spec(tm, BRANCH_W, 0),
                          _row_spec(tm, BRANCH_W, 0), _row_spec(tm, BRANCH_W, 0),
                          cspec(GLA_QK, COL_GLA_Q // GLA_QK), cspec(GLA_QK, COL_GLA_K // GLA_QK),
                          cspec(BRANCH_W, COL_GLA_V // BRANCH_W), cspec(LANES, COL_GLA_R // LANES),
                          cspec(w4, rwkv_blk),
                          pl.BlockSpec((b_n, w4), lambda r, t0=t0, b_n=b_n:
                                       (jnp.maximum((r + t0) * (tm // b_n) - 1, 0), rwkv_blk))]
                         + [_const_spec(a.shape) for a in consts],
                out_specs=[pl.BlockSpec((sh[0], sh[1] // (n_g // tm)), lambda r: (0, r)) for sh in t_shapes]
                          + [_row_spec(tm, BRANCH_W, 0), _row_spec(tm, BRANCH_W, 0)],
                out_shape=[jax.ShapeDtypeStruct(sh, F32) for sh in t_shapes]
                          + [jax.ShapeDtypeStruct((n_g, BRANCH_W), F32)] * 2,
                compiler_params=_cparams("parallel"),
                name="mixer_prep",
            )(cols, cols, cols, *rot_g, cols, cols, cols, cols, cols, cols, *consts)

            dec = jnp.tile(jnp.repeat(gamma ** tc_g, b_n), vl_n).reshape(1, vl_n * HEADS * b_n)
            if not grouped:
                ka_ret = ka_ret.reshape(2, HEADS, HEAD_D, n_g)
                ka_gla = ka_gla.reshape(3, HEADS, GLA_DK, n_g)
                ka_rwkv = ka_rwkv.reshape(5, HEADS, HEAD_D, n_g)
                va = va.reshape(3, HEADS, HEAD_D, n_g)
            o, s_ret, s_gla, s_rwkv = _scan(
                ka_ret, ka_gla, ka_rwkv, va, b_n, t_n, dec,
                _state_to_lanes(st[0], vl_n, True), _state_to_lanes(st[1], vl_n, True),
                _state_to_lanes(st[4], vl_n, False))
            mixt = o if grouped else o.reshape(3 * BRANCH_W, n_g)
            y, hr, hi = _s5_scan(cols, off, *s5p, st[2].reshape(b_n, S5_CH), st[3].reshape(b_n, S5_CH), b_n, t_n)
            shift_new = cols[off + (t_n - 1) * b_n:off + t_n * b_n, COL_RWKV:COL_RWKV + w4]
            layer_new.append((_state_from_lanes(s_ret, b_n, vl_n, True), _state_from_lanes(s_gla, b_n, vl_n, True),
                              hr.reshape(b_n, S5_GROUPS, S5_STATE), hi.reshape(b_n, S5_GROUPS, S5_STATE),
                              _state_from_lanes(s_rwkv, b_n, vl_n, False), shift_new))

            x1_parts.append(pl.pallas_call(
                functools.partial(_post_kernel, alpha=alpha, b_n=b_n),
                grid=(n_g // tm,),
                in_specs=[cspec(d_model, 0),
                          pl.BlockSpec((mixt.shape[0], mixt.shape[1] // (n_g // tm)), lambda r: (0, r)),
                          cspec(BRANCH_W, 3), cspec(BRANCH_W, COL_GLA_G // BRANCH_W),
                          _row_spec(tm, BRANCH_W, 0), cspec(BRANCH_W, COL_S5 // BRANCH_W),
                          _row_spec(tm, BRANCH_W, 0), _row_spec(tm, BRANCH_W, 0)]
                         + [cspec(d_model, COL_GATE // d_model + q) for q in range(N_BRANCH)]
                         + [_const_spec(a.shape) for a in post_consts],
                out_specs=_row_spec(tm, d_model, 0),
                out_shape=jax.ShapeDtypeStruct((n_g, d_model), F32),
                compiler_params=_cparams("parallel"),
                name="mixer_post",
            )(x, mixt, cols, cols, y, cols, obonus, og, cols, cols, cols, cols, *post_consts))
        for lst, pair in zip(new, zip(*layer_new)):
            lst.append(pair)
        x1 = jnp.concatenate(x1_parts, axis=0)

        p = time_major(p_prompt[i], p_sample[i])
        tail_consts = [w_pe[i].astype(BF16), w_pg[i].astype(BF16), row(ln2_g[i]), row(ln2_b[i])]
        j = i // 2
        if i % 2 == 0:
            (x,) = _rowwise(
                functools.partial(_ffn_kernel, alpha=alpha), n,
                [(x1, d_model, 0), (p, p.shape[1], 0)],
                [ffn_w1[j].astype(BF16), ffn_w3[j].astype(BF16), ffn_w2[j].astype(BF16)] + tail_consts,
                [d_model], name="ffn")
        else:
            rh, rl = _split_bf16(_pad_cols(moe_router[j], LANES))
            idx, wgt = _rowwise(_router_kernel, n, [(x1, d_model, 0)], [rh, rl], [LANES, LANES],
                                out_dtypes=[jnp.int32, F32], name="moe_router")
            slot_tok, slot_dst, block_e, nb_used, block_valid, n_out_rows = _moe_route(idx[:, :TOP_K], n)
            yk = _moe_blocks(x1, block_e, nb_used, block_valid, slot_tok, slot_dst, n_out_rows,
                             moe_w1[j].astype(BF16), moe_w3[j].astype(BF16), moe_w2[j].astype(BF16))
            (x,) = _rowwise(
                functools.partial(_moe_combine_kernel, alpha=alpha), n,
                [(yk, d_model, 0), (yk, d_model, 0, n // tm), (x1, d_model, 0), (p, p.shape[1], 0), (wgt, LANES, 0)],
                tail_consts, [d_model], name="moe_combine")

    y_prompt = x[:n_p].reshape(tp, bp, d_model).transpose(1, 0, 2)
    y_sample = x[n_p:].reshape(ts, bs, d_model).transpose(1, 0, 2)
    outs = [y_prompt, y_sample]
    for lst in new:
        outs.append(jnp.stack([pair[0] for pair in lst], 0))
        outs.append(jnp.stack([pair[1] for pair in lst], 0))
    return tuple(outs)
```

```python
import functools

import jax
import jax.numpy as jnp
from jax import lax
from jax.experimental import pallas as pl
from jax.experimental.pallas import tpu as pltpu

F32 = jnp.float32
BF16 = jnp.bfloat16

LANES = 128
SUBLANES = 8
VMEM_LIMIT = 56 * 1024 * 1024

N_BRANCH = 4
BRANCH_W = 256
HEADS = 4
HEAD_D = 64
GLA_DK = 32
GLA_QK = HEADS * GLA_DK
GLA_GATE_RANK = 16
GLA_GATE_NORM = 16.0
S5_GROUP = 16
S5_GROUPS = 16
S5_STATE = 64
S5_CH = S5_GROUPS * S5_STATE
ROPE_BASE = 10000.0
RWKV_GN_EPS = 64e-5
LN_EPS = 1e-5
N_EXPERTS = 8
TOP_K = 2
MOE_BLOCK = 256
ROW_TILE = 256
SCAN_CHUNK = 32
PAST_LEN = 16384

COL_RET = 0
COL_GLA_Q = 1024
COL_GLA_K = 1152
COL_GLA_V = 1280
COL_GLA_G = 1536
COL_S5 = 1792
COL_RWKV = 2048
COL_GATE = 3072
COL_GLA_R = 7168
D_IN_PAD = 7296


def _cparams(*sem):
    return pltpu.CompilerParams(dimension_semantics=sem, vmem_limit_bytes=VMEM_LIMIT)


def _split_bf16(x):
    hi = x.astype(BF16)
    lo = (x - hi.astype(F32)).astype(BF16)
    return hi, lo


def _seg_dot(x, m_ref):
    hi, lo = _split_bf16(x)
    m = m_ref[...]
    return (jnp.dot(hi, m, preferred_element_type=F32)
            + jnp.dot(lo, m, preferred_element_type=F32))


def _bdot(x, w):
    return jnp.dot(x.astype(BF16), w, preferred_element_type=F32)


def _sigmoid(x):
    return 1.0 / (1.0 + jnp.exp(-x))


def _silu(x):
    return x * _sigmoid(x)


def _log1p_exp_neg_abs(x):
    return jnp.log1p(jnp.exp(-jnp.abs(x)))


def _layer_norm(x, g, b):
    mu = jnp.mean(x, axis=-1, keepdims=True)
    xc = x - mu
    var = jnp.mean(xc * xc, axis=-1, keepdims=True)
    return xc * lax.rsqrt(var + LN_EPS) * g + b


def _row_spec(tm, width, col_block, row_block0=0):
    return pl.BlockSpec((tm, width), lambda i, cb=col_block, r0=row_block0: (i + r0, cb))


def _const_spec(shape):
    nd = len(shape)
    return pl.BlockSpec(shape, lambda i, nd=nd: (0,) * nd)


def _rowwise(body, n_rows, row_in, const_in, out_widths, out_dtypes=None, tm=ROW_TILE, name=None):
    assert n_rows % tm == 0
    out_dtypes = out_dtypes or [F32] * len(out_widths)
    in_specs = [_row_spec(tm, *spec[1:]) for spec in row_in] + [_const_spec(a.shape) for a in const_in]
    out_specs = [_row_spec(tm, w, 0) for w in out_widths]
    out_shape = [jax.ShapeDtypeStruct((n_rows, w), dt) for w, dt in zip(out_widths, out_dtypes)]
    return pl.pallas_call(
        body,
        grid=(n_rows // tm,),
        in_specs=in_specs,
        out_specs=out_specs,
        out_shape=out_shape,
        compiler_params=_cparams("parallel"),
        name=name,
    )(*[spec[0] for spec in row_in], *const_in)


def _matmul_kernel(x_ref, w_ref, o_ref):
    o_ref[...] = _bdot(x_ref[...], w_ref[...]).astype(o_ref.dtype)


def _matmul(x, w, tm=512, tn=None, out_dtype=F32, name=None):
    m, k = x.shape
    n = w.shape[1]
    tn = tn or n
    tm = tm if m % tm == 0 else ROW_TILE
    assert m % tm == 0 and n % tn == 0
    return pl.pallas_call(
        _matmul_kernel,
        grid=(n // tn, m // tm),
        in_specs=[pl.BlockSpec((tm, k), lambda j, i: (i, 0)),
                  pl.BlockSpec((k, tn), lambda j, i: (0, j))],
        out_specs=pl.BlockSpec((tm, tn), lambda j, i: (i, j)),
        out_shape=jax.ShapeDtypeStruct((m, n), out_dtype),
        compiler_params=_cparams("parallel", "parallel"),
        name=name,
    )(x, w)


def _replicate_groups(w, n_grp):
    if n_grp == 1:
        return [w]
    grp = lax.broadcasted_iota(jnp.int32, w.shape, 1) // (LANES // n_grp)
    parts = [w]
    span = n_grp
    while span > 1:
        half = span // 2
        shift = half * (LANES // n_grp)
        low = (grp % span) < half
        nxt = []
        for z in parts:
            rz = pltpu.roll(z, shift, 1)
            nxt.append(jnp.where(low, z, rz))
            nxt.append(jnp.where(low, rz, z))
        parts = nxt
        span = half
    return parts


def _heads_to_lanes(ops, b_n, pad_to=LANES):
    tm = ops[0].shape[0]
    c = ops[0].shape[1] // HEADS
    packed = []
    for h in range(HEADS):
        pieces = [o[:, h * c:(h + 1) * c] for o in ops]
        if len(pieces) * c < pad_to:
            pieces.append(jnp.zeros((tm, pad_to - len(pieces) * c), F32))
        packed.append(jnp.concatenate(pieces, axis=1).reshape(tm // b_n, b_n, pad_to))
    y = jnp.stack(packed, axis=1).reshape(tm * HEADS, pad_to)
    return y.T


def _prep_kernel(rq_ref, rk_ref, rv_ref, cq_ref, sq_ref, ck_ref, sk_ref,
                 gq_ref, gk_ref, gv_ref, gr_ref,
                 c_ref, tail_ref, shift_ref, wg2_ref, bg_ref,
                 mu_ref, w0_ref, w1_ref, w2_ref, a0_ref, a1_ref, a2_ref, g1_ref, g2_ref,
                 kkp_ref, kap_ref, rkp_ref, ones_ref,
                 kr_ref, kg_ref, kw_ref, va_ref, obonus_ref, og_ref, *, grouped):
    i = pl.program_id(0)
    w = BRANCH_W
    lane = lax.broadcasted_iota(jnp.int32, rq_ref.shape, 1)
    first_half = (lane % HEAD_D) < (HEAD_D // 2)

    def per_row(tab_ref):
        tab = tab_ref[...]
        rows, reps = tab.shape[0], rq_ref.shape[0] // tab.shape[0]
        if reps == 1:
            return tab
        return jnp.broadcast_to(tab[:, None, :], (rows, reps, w)).reshape(rows * reps, w)

    def rot(x, cos_ref, sin_ref):
        partner = jnp.where(first_half,
                            pltpu.roll(x, BRANCH_W - HEAD_D // 2, 1),
                            pltpu.roll(x, HEAD_D // 2, 1))
        return x * per_row(cos_ref) + partner * per_row(sin_ref)

    ret_q = rot(rq_ref[...], cq_ref, sq_ref)
    ret_k = rot(rk_ref[...], ck_ref, sk_ref)

    z = _bdot(gr_ref[...], wg2_ref[...]) + bg_ref[...]
    glog = (jnp.minimum(z, 0.0) - _log1p_exp_neg_abs(z)) / GLA_GATE_NORM
    gla_q = gq_ref[...] * (GLA_DK ** -0.5)
    gla_al = jnp.exp(glog)

    c = c_ref[...]
    tm = c.shape[0]
    b_n = tail_ref.shape[0]
    tail = jnp.where(i == 0, shift_ref[...], tail_ref[...])
    d = jnp.concatenate([tail, c[:tm - b_n]], axis=0) - c
    mu = mu_ref[...]
    cr, ck, cv, cz = (c[:, j * w:(j + 1) * w] for j in range(4))
    dr, dk, dv, dz = (d[:, j * w:(j + 1) * w] for j in range(4))
    r = cr + dr * mu[0:1]
    k = ck + dk * mu[1:2]
    v = cv + dv * mu[2:3]
    zw = cz + dz * mu[3:4]
    za = cz + dz * mu[4:5]
    zg = cz + dz * mu[5:6]
    w_raw = w0_ref[...] + _bdot(jnp.tanh(_bdot(zw, w1_ref[...])), w2_ref[...])
    sp = jnp.maximum(-w_raw, 0.0) + _log1p_exp_neg_abs(w_raw)
    dec = jnp.exp(-jnp.exp(-sp - 0.5))
    a = _sigmoid(a0_ref[...] + _bdot(_bdot(za, a1_ref[...]), a2_ref[...]))
    og_ref[...] = _bdot(_sigmoid(_bdot(zg, g1_ref[...])), g2_ref[...])
    kk = k * kkp_ref[...]
    ss = _seg_dot(kk * kk, ones_ref)
    kk = kk * lax.rsqrt(jnp.maximum(ss, 1e-24))
    km = k * (1.0 + (a - 1.0) * kap_ref[...])
    obonus_ref[...] = _seg_dot(r * km * rkp_ref[...], ones_ref) * v

    if not grouped:
        for ref, ops in ((kr_ref, (ret_q, ret_k)), (kg_ref, (gla_q, gk_ref[...], gla_al)),
                         (kw_ref, (r, km, dec, kk, kk * a)), (va_ref, (rv_ref[...], gv_ref[...], v))):
            cw = ops[0].shape[1]
            for j, o in enumerate(ops):
                ref[j * cw:(j + 1) * cw, :] = o.T
        return

    n_grp = LANES // (HEADS * b_n)

    def emit_tiles(ref, row0, ops, n_rows):
        wt = _heads_to_lanes(ops, b_n)
        for lt in range(wt.shape[1] // LANES):
            for g, zt in enumerate(_replicate_groups(wt[:n_rows, lt * LANES:(lt + 1) * LANES], n_grp)):
                q = lt * n_grp + g
                ref[row0:row0 + n_rows, q * LANES:(q + 1) * LANES] = zt

    emit_tiles(kr_ref, 0, (ret_q, ret_k), 2 * HEAD_D)
    emit_tiles(kg_ref, 0, (gla_q, gk_ref[...], gla_al), 3 * GLA_DK)
    emit_tiles(kw_ref, 0, (r, km), 2 * HEAD_D)
    emit_tiles(kw_ref, 2 * HEAD_D, (dec, kk), 2 * HEAD_D)
    emit_tiles(kw_ref, 4 * HEAD_D, (kk * a,), HEAD_D)
    va_ref[0:2 * HEAD_D, :] = _heads_to_lanes((rv_ref[...], gv_ref[...]), b_n)
    va_ref[2 * HEAD_D:3 * HEAD_D, :] = _heads_to_lanes((v,), b_n)[:HEAD_D]


def _scan_kernel(kr_ref, kg_ref, kw_ref, va_ref, dec_ref, s0r_ref, s0g_ref, s0w_ref,
                 o_ref, sor_ref, sog_ref, sow_ref,
                 sr, sg, sw, vs, os_, *, vh_n, grouped, tc, b_n):
    ti = pl.program_id(1)

    @pl.when(ti == 0)
    def _():
        sr[...] = s0r_ref[...]
        sg[...] = s0g_ref[...]
        sw[...] = s0w_ref[...]

    n_grp = LANES // (HEADS * b_n) if grouped else 1
    grp_w = LANES // n_grp
    n_v = 3

    def k_tile(ref, n_ops, j, t):
        lanes = pl.ds(pl.multiple_of(t * LANES, LANES), LANES)
        if grouped:
            k_n = ref.shape[0] // n_ops
            return ref[j * k_n:(j + 1) * k_n, lanes]
        return ref[j, 0, :, lanes]

    if grouped:
        grp = lax.broadcasted_iota(jnp.int32, (vh_n, LANES), 1) // grp_w
        for lt in range(tc // n_grp):
            lanes = slice(lt * LANES, (lt + 1) * LANES)
            for j in range(n_v):
                ws = [va_ref[j * HEAD_D + vl * vh_n:j * HEAD_D + (vl + 1) * vh_n, lanes] for vl in range(n_grp)]
                for g in range(n_grp):
                    tile = None
                    for vl in range(n_grp):
                        r = pltpu.roll(ws[vl], ((vl - g) * grp_w) % LANES, 1)
                        tile = r if tile is None else jnp.where(grp == vl, r, tile)
                    vs[lt * n_grp + g, j] = tile
    else:
        def copy_in(t, c):
            lanes = pl.ds(pl.multiple_of(t * LANES, LANES), LANES)
            for j in range(n_v):
                vs[t, j] = va_ref[j, 0, :, lanes]
            return c

        lax.fori_loop(0, tc, copy_in, 0)

    dec_r = dec_ref[...]

    def out(t, j, vh, s, q):
        os_[t, j, pl.ds(vh, 1), :] = jnp.sum(s * q, axis=0, keepdims=True)

    def ret_step(t, c):
        q, k = k_tile(kr_ref, 2, 0, t), k_tile(kr_ref, 2, 1, t)
        for vh in range(vh_n):
            s = sr[vh] + vs[t, 0, pl.ds(vh, 1), :] * k
            sr[vh] = s
            out(t, 0, vh, s, q)
        return c

    def gla_step(t, c):
        q, k, al = (k_tile(kg_ref, 3, j, t) for j in range(3))
        for vh in range(vh_n):
            s = sg[vh] * al + vs[t, 1, pl.ds(vh, 1), :] * k
            sg[vh] = s
            out(t, 1, vh, s, q)
        return c

    def rwkv_step(t, c):
        q, k, dec, kk, beta = (k_tile(kw_ref, 5, j, t) for j in range(5))
        for vh in range(vh_n):
            s = sw[vh]
            sk = jnp.sum(s * kk, axis=0, keepdims=True)
            s = s * dec - sk * beta + vs[t, 2, pl.ds(vh, 1), :] * k
            sw[vh] = s
            out(t, 2, vh, s, q)
        return c

    unroll = 2 if grouped else 1
    lax.fori_loop(0, tc, ret_step, 0, unroll=unroll)
    sr[...] = sr[...] * dec_r
    lax.fori_loop(0, tc, gla_step, 0, unroll=unroll)
    lax.fori_loop(0, tc, rwkv_step, 0, unroll=unroll)

    if grouped:
        grp = lax.broadcasted_iota(jnp.int32, (vh_n, LANES), 1) // grp_w
        for lt in range(tc // n_grp):
            lanes = slice(lt * LANES, (lt + 1) * LANES)
            for j in range(n_v):
                og = [os_[lt * n_grp + g, j] for g in range(n_grp)]
                for vl in range(n_grp):
                    wv = None
                    for g in range(n_grp):
                        r = pltpu.roll(og[g], ((g - vl) * grp_w) % LANES, 1)
                        wv = r if wv is None else jnp.where(grp == g, r, wv)
                    o_ref[j * HEAD_D + vl * vh_n:j * HEAD_D + (vl + 1) * vh_n, lanes] = wv
    else:
        def copy_out(t, c):
            lanes = pl.ds(pl.multiple_of(t * LANES, LANES), LANES)
            for j in range(n_v):
                o_ref[j, 0, :, lanes] = os_[t, j]
            return c

        lax.fori_loop(0, tc, copy_out, 0)

    @pl.when(ti == pl.num_programs(1) - 1)
    def _():
        sor_ref[...] = sr[...]
        sog_ref[...] = sg[...]
        sow_ref[...] = sw[...]


def _scan(kr, kg, kw, va, b_n, t_n, dec, s0r, s0g, s0w):
    grouped = b_n * HEADS < LANES
    vh_n = s0r.shape[0]
    if grouped:
        tc = min(t_n, SCAN_CHUNK)
        n_grp = LANES // (HEADS * b_n)
        assert tc % n_grp == 0 and t_n % tc == 0
        n_l, n_t = 1, t_n // tc
        k_specs = [pl.BlockSpec((a.shape[0], tc * LANES), lambda l, t: (0, t)) for a in (kr, kg, kw)]
        v_spec = pl.BlockSpec((va.shape[0], tc * HEADS * b_n), lambda l, t: (0, t))
        o_spec = v_spec
    else:
        assert b_n == LANES
        tc, n_l, n_t = t_n, HEADS, 1
        k_specs = [pl.BlockSpec((a.shape[0], 1, a.shape[2], tc * LANES), lambda l, t: (0, l, 0, 0))
                   for a in (kr, kg, kw)]
        v_spec = pl.BlockSpec((3, 1, HEAD_D, tc * LANES), lambda l, t: (0, l, 0, 0))
        o_spec = v_spec

    def s_spec(a):
        return pl.BlockSpec(a.shape[:2] + (LANES,), lambda l, t: (0, 0, l))

    return pl.pallas_call(
        functools.partial(_scan_kernel, vh_n=vh_n, grouped=grouped, tc=tc, b_n=b_n),
        grid=(n_l, n_t),
        in_specs=k_specs + [v_spec, pl.BlockSpec((1, LANES), lambda l, t: (0, l)),
                            s_spec(s0r), s_spec(s0g), s_spec(s0w)],
        out_specs=[o_spec, s_spec(s0r), s_spec(s0g), s_spec(s0w)],
        out_shape=[jax.ShapeDtypeStruct(va.shape, F32)] + [jax.ShapeDtypeStruct(s.shape, F32)
                                                           for s in (s0r, s0g, s0w)],
        scratch_shapes=[pltpu.VMEM(s.shape[:2] + (LANES,), F32) for s in (s0r, s0g, s0w)]
                       + [pltpu.VMEM((tc, 3, vh_n, LANES), F32), pltpu.VMEM((tc, 3, vh_n, LANES), F32)],
        compiler_params=_cparams("parallel", "arbitrary"),
        name="scan_ret_gla_rwkv",
    )(kr, kg, kw, va, dec, s0r, s0g, s0w)


def _s5_kernel(u_ref, bb_ref, cc_ref, ar_ref, ai_ref, h0r_ref, h0i_ref,
               y_ref, hr_out, hi_out, hr_scr, hi_scr, xs_scr, hs_scr, *, b_n, tc):
    ti = pl.program_id(0)

    @pl.when(ti == 0)
    def _():
        hr_scr[...] = h0r_ref[...]
        hi_scr[...] = h0i_ref[...]

    xs_scr[...] = _bdot(u_ref[...], bb_ref[...])
    ar = ar_ref[...]
    ai = ai_ref[...]

    def step(t, carry):
        hr, hi = carry
        row = pl.multiple_of(t * b_n, SUBLANES)
        x = xs_scr[pl.ds(row, b_n), :]
        nr = ar * hr - ai * hi + x[:, :S5_CH]
        ni = ar * hi + ai * hr + x[:, S5_CH:]
        hs_scr[pl.ds(row, b_n), :S5_CH] = nr
        hs_scr[pl.ds(row, b_n), S5_CH:] = ni
        return nr, ni

    hr, hi = lax.fori_loop(0, tc, step, (hr_scr[...], hi_scr[...]))
    hr_scr[...] = hr
    hi_scr[...] = hi
    y_ref[...] = _bdot(hs_scr[...], cc_ref[...])

    @pl.when(ti == pl.num_programs(0) - 1)
    def _():
        hr_out[...] = hr
        hi_out[...] = hi


def _s5_scan(cols, row_off, bb, cc, ar, ai, h0r, h0i, b_n, t_n):
    tc = min(t_n, max(1, 1024 // b_n))
    rows = tc * b_n
    assert t_n % tc == 0 and row_off % rows == 0
    blk0 = row_off // rows
    return pl.pallas_call(
        functools.partial(_s5_kernel, b_n=b_n, tc=tc),
        grid=(t_n // tc,),
        in_specs=[pl.BlockSpec((rows, BRANCH_W), lambda t: (blk0 + t, COL_S5 // BRANCH_W)),
                  _const_spec(bb.shape), _const_spec(cc.shape),
                  _const_spec(ar.shape), _const_spec(ai.shape),
                  _const_spec(h0r.shape), _const_spec(h0i.shape)],
        out_specs=[pl.BlockSpec((rows, BRANCH_W), lambda t: (t, 0)),
                   _const_spec(h0r.shape), _const_spec(h0i.shape)],
        out_shape=[jax.ShapeDtypeStruct((t_n * b_n, BRANCH_W), F32),
                   jax.ShapeDtypeStruct(h0r.shape, F32),
                   jax.ShapeDtypeStruct(h0i.shape, F32)],
        scratch_shapes=[pltpu.VMEM((b_n, S5_CH), F32), pltpu.VMEM((b_n, S5_CH), F32),
                        pltpu.VMEM((rows, 2 * S5_CH), F32), pltpu.VMEM((rows, 2 * S5_CH), F32)],
        compiler_params=_cparams("arbitrary"),
        name="scan_s5",
    )(cols, bb, cc, ar, ai, h0r, h0i)


def _post_kernel(x_ref, mixt_ref, rg_ref, gg_ref, sy_ref, su_ref, wbon_ref, wg_ref,
                 gate0_ref, gate1_ref, gate2_ref, gate3_ref,
                 avg_ref, rgn_g, rgn_b, ggn_g, s5d_ref, wglu_ref, wgn_g, wgn_b,
                 wbr_ref, wo_ref, ln_g, ln_b, o_ref, *, alpha, b_n):
    def seg_mean(v):
        return _seg_dot(v, avg_ref)

    tm = x_ref.shape[0]
    if b_n * HEADS < LANES:
        yt = mixt_ref[...].T.reshape(tm // b_n, HEADS, b_n, 3 * HEAD_D)
        per_head = [yt[:, h].reshape(tm, 3 * HEAD_D) for h in range(HEADS)]
        mix = jnp.concatenate([ph[:, j * HEAD_D:(j + 1) * HEAD_D] for j in range(3) for ph in per_head], axis=1)
    else:
        mix = mixt_ref[...].T
    ro = mix[:, 0:BRANCH_W]
    mu = seg_mean(ro)
    rc = ro - mu
    var = seg_mean(rc * rc)
    b0 = (rc * lax.rsqrt(var + LN_EPS) * rgn_g[...] + rgn_b[...]) * _silu(rg_ref[...])
    go = mix[:, BRANCH_W:2 * BRANCH_W]
    ms = seg_mean(go * go)
    b1 = go * lax.rsqrt(ms + LN_EPS) * ggn_g[...] * _silu(gg_ref[...])
    y = jax.nn.gelu(sy_ref[...] + s5d_ref[...] * su_ref[...])
    b2 = y * _sigmoid(_bdot(y, wglu_ref[...]))
    wy = mix[:, 2 * BRANCH_W:3 * BRANCH_W]
    mu = seg_mean(wy)
    wc = wy - mu
    var = seg_mean(wc * wc)
    b3 = (wc * lax.rsqrt(var + RWKV_GN_EPS) * wgn_g[...] + wgn_b[...] + wbon_ref[...]) * wg_ref[...]

    m = None
    gates = (gate0_ref, gate1_ref, gate2_ref, gate3_ref)
    for i, br in enumerate((b0, b1, b2, b3)):
        term = _bdot(br, wbr_ref[i]) * _sigmoid(gates[i][...])
        m = term if m is None else m + term
    h = _bdot(m, wo_ref[...])
    o_ref[...] = _layer_norm(alpha * x_ref[...] + h, ln_g[...], ln_b[...])


def _embed_ln2(x, f, p_ref, wpe_ref, wpg_ref, ln_g, ln_b, alpha):
    e = _bdot(p_ref[...], wpe_ref[...]) * _sigmoid(_bdot(x, wpg_ref[...]))
    return _layer_norm(alpha * x + f + e, ln_g[...], ln_b[...])


def _ffn_kernel(x_ref, p_ref, w1_ref, w3_ref, w2_ref, wpe_ref, wpg_ref, ln_g, ln_b, o_ref, *, alpha):
    x = x_ref[...]
    xb = x.astype(BF16)
    h = _silu(jnp.dot(xb, w1_ref[...], preferred_element_type=F32)) * jnp.dot(
        xb, w3_ref[...], preferred_element_type=F32)
    f = _bdot(h, w2_ref[...])
    o_ref[...] = _embed_ln2(x, f, p_ref, wpe_ref, wpg_ref, ln_g, ln_b, alpha)


def _router_kernel(x_ref, wh_ref, wl_ref, idx_ref, wgt_ref):
    xh, xl = _split_bf16(x_ref[...])
    wh = wh_ref[...]
    logits = (jnp.dot(xh, wh, preferred_element_type=F32)
              + jnp.dot(xl, wh, preferred_element_type=F32)
              + jnp.dot(xh, wl_ref[...], preferred_element_type=F32))
    col = lax.broadcasted_iota(jnp.int32, logits.shape, 1)
    neg = jnp.float32(-jnp.inf)
    lg = jnp.where(col < N_EXPERTS, logits, neg)
    m1 = jnp.max(lg, axis=1, keepdims=True)
    i1 = jnp.min(jnp.where(lg == m1, col, LANES), axis=1, keepdims=True)
    lg2 = jnp.where(col == i1, neg, lg)
    m2 = jnp.max(lg2, axis=1, keepdims=True)
    i2 = jnp.min(jnp.where(lg2 == m2, col, LANES), axis=1, keepdims=True)
    e2 = jnp.exp(m2 - m1)
    den = 1.0 + e2
    idx_ref[...] = jnp.where(col == 0, i1, jnp.where(col == 1, i2, 0))
    wgt_ref[...] = jnp.where(col == 0, 1.0 / den, jnp.where(col == 1, e2 / den, 0.0))


def _row_copies(idx_ref, n_rows, make_copy, count=None):
    def each(action):
        def body(r, c):
            action(make_copy(r, idx_ref[0, 0, r]))
            return c

        if count is None:
            lax.fori_loop(0, n_rows, body, 0, unroll=8)
            return

        @pl.when(count == n_rows)
        def _():
            lax.fori_loop(0, n_rows, body, 0, unroll=8)

        @pl.when(count < n_rows)
        def _():
            lax.fori_loop(0, count, body, 0)

    return (lambda: each(lambda cp: cp.start())), (lambda: each(lambda cp: cp.wait()))


def _moe_block_kernel(be_ref, nb_ref, bv_ref, tok_ref, tok_next_ref, dst_ref, dst_m1_ref, dst_m2_ref,
                      x_hbm, w1_ref, w3_ref, w2_ref, out_hbm, xbuf, ybuf, sem_in, sem_out):
    j = pl.program_id(0)
    last = pl.num_programs(0) - 1
    nb = nb_ref[0]
    slot = j % 2

    def gather_copy(idx_ref, half, r):
        return pltpu.make_async_copy(x_hbm.at[pl.ds(idx_ref[0, 0, r], 1)], xbuf.at[half, pl.ds(r, 1)],
                                     sem_in.at[half])

    def gather(idx_ref, half):
        return _row_copies(idx_ref, MOE_BLOCK, lambda r, row: pltpu.make_async_copy(
            x_hbm.at[pl.ds(row, 1)], xbuf.at[half, pl.ds(r, 1)], sem_in.at[half]))

    def scatter(idx_ref, half, block):
        return _row_copies(idx_ref, MOE_BLOCK, lambda r, row: pltpu.make_async_copy(
            ybuf.at[half, pl.ds(r, 1)], out_hbm.at[pl.ds(row, 1)], sem_out.at[half]),
            count=bv_ref[jnp.maximum(block, 0)])

    @pl.when(jnp.logical_and(j == 0, nb > 0))
    def _():
        gather(tok_ref, 0)[0]()

    def step(cur):
        gather(tok_ref, cur)[1]()

        @pl.when(j >= 2)
        def _():
            scatter(dst_m2_ref, cur, j - 2)[1]()

        for r in range(MOE_BLOCK):
            gather_copy(tok_next_ref, 1 - cur, r).start()
        xb = xbuf[cur].astype(BF16)
        h = _silu(jnp.dot(xb, w1_ref[0], preferred_element_type=F32)) * jnp.dot(
            xb, w3_ref[0], preferred_element_type=F32)
        ybuf[cur] = _bdot(h, w2_ref[0])
        scatter(dst_ref, cur, j)[0]()

    for par in range(2):
        pl.when(jnp.logical_and(j < nb, slot == par))(functools.partial(step, par))

    @pl.when(jnp.logical_and(j == nb, nb > 0))
    def _():
        gather(tok_ref, slot)[1]()

    @pl.when(jnp.logical_and(j >= nb, jnp.logical_and(j >= 2, j - 2 < nb)))
    def _():
        scatter(dst_m2_ref, slot, j - 2)[1]()

    @pl.when(jnp.logical_and(j == last, jnp.logical_and(j >= 1, j - 1 < nb)))
    def _():
        scatter(dst_m1_ref, 1 - slot, j - 1)[1]()

    @pl.when(jnp.logical_and(j == last, j < nb))
    def _():
        scatter(dst_ref, slot, j)[1]()


def _moe_blocks(x, block_e, nb_used, block_valid, slot_tok, slot_dst, n_out_rows, w1, w3, w2):
    n_blocks = block_e.shape[0]
    d = x.shape[1]
    dff = w1.shape[2]
    tok = slot_tok.reshape(n_blocks, 1, MOE_BLOCK)
    dst = slot_dst.reshape(n_blocks, 1, MOE_BLOCK)

    def idx_spec(shift):
        return pl.BlockSpec((1, 1, MOE_BLOCK),
                            lambda j, be, nb, bv: (jnp.clip(j + shift, 0, n_blocks - 1), 0, 0),
                            memory_space=pltpu.SMEM)

    grid_spec = pltpu.PrefetchScalarGridSpec(
        num_scalar_prefetch=3,
        grid=(n_blocks + 1,),
        in_specs=[
            idx_spec(0), idx_spec(1), idx_spec(0), idx_spec(-1), idx_spec(-2),
            pl.BlockSpec(memory_space=pl.ANY),
            pl.BlockSpec((1, d, dff), lambda j, be, nb, bv: (be[jnp.minimum(j, n_blocks - 1)], 0, 0),
                         pipeline_mode=pl.Buffered(1)),
            pl.BlockSpec((1, d, dff), lambda j, be, nb, bv: (be[jnp.minimum(j, n_blocks - 1)], 0, 0),
                         pipeline_mode=pl.Buffered(1)),
            pl.BlockSpec((1, dff, d), lambda j, be, nb, bv: (be[jnp.minimum(j, n_blocks - 1)], 0, 0),
                         pipeline_mode=pl.Buffered(1)),
        ],
        out_specs=pl.BlockSpec(memory_space=pl.ANY),
        scratch_shapes=[pltpu.VMEM((2, MOE_BLOCK, d), F32), pltpu.VMEM((2, MOE_BLOCK, d), F32),
                        pltpu.SemaphoreType.DMA((2,)), pltpu.SemaphoreType.DMA((2,))],
    )
    return pl.pallas_call(
        _moe_block_kernel,
        grid_spec=grid_spec,
        out_shape=jax.ShapeDtypeStruct((n_out_rows, d), F32),
        compiler_params=_cparams("arbitrary"),
        name="moe_blocks",
    )(block_e, nb_used, block_valid, tok, tok, dst, dst, dst, x, w1, w3, w2)


def _moe_combine_kernel(y0_ref, y1_ref, x_ref, p_ref, wgt_ref, wpe_ref, wpg_ref, ln_g, ln_b, o_ref, *, alpha):
    wgt = wgt_ref[...]
    f = y0_ref[...] * wgt[:, 0:1] + y1_ref[...] * wgt[:, 1:2]
    o_ref[...] = _embed_ln2(x_ref[...], f, p_ref, wpe_ref, wpg_ref, ln_g, ln_b, alpha)


def _lane_groups(b_n):
    bh = b_n * HEADS
    return LANES // bh if bh < LANES else 1


def _state_to_lanes(s, vl_n, value_last):
    s = s.transpose(3, 2, 1, 0) if value_last else s.transpose(2, 3, 1, 0)
    v_n, k_n, h_n, b_n = s.shape
    s = s.reshape(vl_n, v_n // vl_n, k_n, h_n, b_n).transpose(1, 2, 0, 3, 4)
    return s.reshape(v_n // vl_n, k_n, vl_n * h_n * b_n)


def _state_from_lanes(s, b_n, vl_n, value_last):
    vh_n, k_n, _ = s.shape
    s = s.reshape(vh_n, k_n, vl_n, HEADS, b_n).transpose(4, 3, 1, 2, 0).reshape(b_n, HEADS, k_n, vl_n * vh_n)
    return s if value_last else s.transpose(0, 1, 3, 2)


def _rotary_tables(pos, row_repeat, gamma, tc):
    half = HEAD_D // 2
    freq = ROPE_BASE ** (-jnp.arange(half, dtype=F32) / half)
    ang = pos.astype(F32)[:, None] * freq[None, :]
    cos, sin = jnp.cos(ang), jnp.sin(ang)
    cos_h = jnp.tile(jnp.concatenate([cos, cos], axis=-1), (1, HEADS))
    sin_h = jnp.tile(jnp.concatenate([-sin, sin], axis=-1), (1, HEADS))
    tau1 = (jnp.arange(pos.shape[0], dtype=jnp.int32) % tc + 1).astype(F32)
    f_q = jnp.repeat(gamma[None, :] ** tau1[:, None], HEAD_D, axis=1)
    f_k = (HEAD_D ** -0.5) / f_q
    tabs = (cos_h * f_q, sin_h * f_q, cos_h * f_k, sin_h * f_k)
    return tuple(jnp.repeat(t, row_repeat, axis=0) for t in tabs) if row_repeat > 1 else tabs


def _block_diag_const(block, n_blocks):
    return jnp.kron(jnp.eye(n_blocks, dtype=F32), jnp.full((block, block), 1.0, F32))


def _s5_params(log_dt, a_re, a_im, b_re, b_im, c_re, c_im):
    dt = jnp.exp(log_dt)[:, None]
    mag = jnp.exp(dt * a_re)
    ang = dt * a_im
    abar_re, abar_im = mag * jnp.cos(ang), mag * jnp.sin(ang)
    den = a_re * a_re + a_im * a_im
    n_re = abar_re - 1.0
    f_re = (n_re * a_re + abar_im * a_im) / den
    f_im = (abar_im * a_re - n_re * a_im) / den
    bb_re = f_re[..., None] * b_re - f_im[..., None] * b_im
    bb_im = f_re[..., None] * b_im + f_im[..., None] * b_re
    eye = jnp.eye(S5_GROUPS, dtype=F32)

    def in_map(bb):
        return jnp.einsum("gpc,gh->gchp", bb, eye).reshape(BRANCH_W, S5_CH)

    def out_map(cm):
        return jnp.einsum("gcp,gh->gphc", cm, eye).reshape(S5_CH, BRANCH_W)

    bb = jnp.concatenate([in_map(bb_re), in_map(bb_im)], axis=1).astype(BF16)
    cc = jnp.concatenate([out_map(c_re), -out_map(c_im)], axis=0).astype(BF16)
    return bb, cc, abar_re.reshape(1, S5_CH), abar_im.reshape(1, S5_CH)


def _reorder_w_in(w_in):
    d_model = w_in.shape[0]
    main = jnp.concatenate([w_in[:, :1792], w_in[:, 1808:]], axis=1)
    tail = jnp.concatenate([w_in[:, 1792:1808],
                            jnp.zeros((d_model, D_IN_PAD - COL_GLA_R - GLA_GATE_RANK), w_in.dtype)], axis=1)
    return jnp.concatenate([main, tail], axis=1).astype(BF16)


def _pad_cols(w, n):
    return jnp.pad(w, ((0, 0), (0, n - w.shape[1])))


def _pad_rows(w, n):
    return jnp.pad(w, ((0, n - w.shape[0]), (0, 0)))


def _moe_route(idx, n):
    nk = n * TOP_K
    flat_e = idx.reshape(nk)
    onehot = (flat_e[:, None] == jnp.arange(N_EXPERTS, dtype=jnp.int32)[None, :]).astype(jnp.int32)
    incl = jnp.cumsum(onehot, axis=0)
    counts = incl[-1]
    rank = jnp.sum((incl - onehot) * onehot, axis=1)
    padded = (counts + MOE_BLOCK - 1) // MOE_BLOCK * MOE_BLOCK
    pad_end = jnp.cumsum(padded)
    slot = (pad_end - padded)[flat_e] + rank
    n_blocks = -(-(nk + N_EXPERTS * (MOE_BLOCK - 1)) // MOE_BLOCK)
    cap = n_blocks * MOE_BLOCK
    pair = jnp.full((cap,), -1, jnp.int32).at[slot].set(jnp.arange(nk, dtype=jnp.int32))
    real = pair >= 0
    slot_tok = jnp.where(real, pair // TOP_K, 0)
    slot_dst = jnp.where(real, (pair % TOP_K) * n + pair // TOP_K, 0)
    block_start = jnp.arange(n_blocks, dtype=jnp.int32) * MOE_BLOCK
    block_e = jnp.minimum(jnp.sum((pad_end[None, :] <= block_start[:, None]).astype(jnp.int32), axis=1),
                          N_EXPERTS - 1).astype(jnp.int32)
    nb_used = (pad_end[-1] // MOE_BLOCK).astype(jnp.int32).reshape(1)
    block_valid = jnp.sum(real.reshape(n_blocks, MOE_BLOCK).astype(jnp.int32), axis=1)
    return slot_tok, slot_dst, block_e, nb_used, block_valid, nk


def kernel(x_prompt, x_sample, state_ret, state_gla, state_s5_re, state_s5_im, state_rwkv, state_shift,
           p_prompt, p_sample, w_in, ret_gn_g, ret_gn_b, gla_wg2, gla_bg, gla_gn,
           s5_log_dt, s5_a_re, s5_a_im, s5_b_re, s5_b_im, s5_c_re, s5_c_im, s5_d, s5_w_glu,
           rwkv_mu, rwkv_w0, rwkv_w1, rwkv_w2, rwkv_a0, rwkv_a1, rwkv_a2, rwkv_g1, rwkv_g2,
           rwkv_kk, rwkv_ka, rwkv_rk, rwkv_gn_g, rwkv_gn_b, w_branch, w_o,
           ln1_g, ln1_b, ln2_g, ln2_b, w_pe, w_pg, ffn_w1, ffn_w3, ffn_w2,
           moe_router, moe_w1, moe_w3, moe_w2):
    depth = w_in.shape[0]
    bp, tp, d_model = x_prompt.shape
    bs, ts, _ = x_sample.shape
    n_p, n_s = bp * tp, bs * ts
    n = n_p + n_s
    tm = ROW_TILE
    assert n_p % tm == 0 and n_s % tm == 0 and tm % bp == 0 and tm % bs == 0 and n_p % bs == 0
    alpha = (2 * depth) ** 0.25
    groups = [(0, bp, tp), (n_p, bs, ts)]
    w4 = 4 * BRANCH_W

    def time_major(a_p, a_s):
        return jnp.concatenate([a_p.transpose(1, 0, 2).reshape(n_p, -1),
                                a_s.transpose(1, 0, 2).reshape(n_s, -1)], axis=0)

    x = time_major(x_prompt, x_sample)
    ones_bd = _block_diag_const(HEAD_D, HEADS).astype(BF16)
    avg_bd = (_block_diag_const(HEAD_D, HEADS) / HEAD_D).astype(BF16)
    gamma = 1.0 - jnp.exp2(-5.0 - jnp.arange(HEADS, dtype=F32))
    tc_p, tc_s = min(tp, SCAN_CHUNK), ts
    rot_p = _rotary_tables(jnp.arange(tp, dtype=jnp.int32), 1 if tm // bp >= SUBLANES else bp, gamma, tc_p)
    rot_s = _rotary_tables(PAST_LEN + jnp.arange(ts, dtype=jnp.int32), 1 if tm // bs >= SUBLANES else bs,
                           gamma, tc_s)
    row = lambda v: v.reshape(1, -1)

    new = [[] for _ in range(6)]
    for i in range(depth):
        cols = _matmul(x, _reorder_w_in(w_in[i]), tm=512, tn=D_IN_PAD // 3, name="in_proj")

        prep_consts = [
            _pad_rows(_pad_cols(gla_wg2[i], LANES), LANES).astype(BF16), row(gla_bg[i]),
            rwkv_mu[i], row(rwkv_w0[i]),
            _pad_cols(rwkv_w1[i], LANES).astype(BF16), _pad_rows(rwkv_w2[i], LANES).astype(BF16),
            row(rwkv_a0[i]),
            _pad_cols(rwkv_a1[i], LANES).astype(BF16), _pad_rows(rwkv_a2[i], LANES).astype(BF16),
            _pad_cols(rwkv_g1[i], LANES).astype(BF16), _pad_rows(rwkv_g2[i], LANES).astype(BF16),
            row(rwkv_kk[i]), row(rwkv_ka[i]), row(rwkv_rk[i]), ones_bd,
        ]
        post_consts = [avg_bd, row(ret_gn_g[i]), row(ret_gn_b[i]), row(gla_gn[i]), row(s5_d[i]),
                       s5_w_glu[i].astype(BF16), row(rwkv_gn_g[i]), row(rwkv_gn_b[i]),
                       w_branch[i].astype(BF16), w_o[i].astype(BF16), row(ln1_g[i]), row(ln1_b[i])]
        s5p = _s5_params(s5_log_dt[i], s5_a_re[i], s5_a_im[i], s5_b_re[i], s5_b_im[i], s5_c_re[i], s5_c_im[i])
        rwkv_blk = COL_RWKV // w4
        x1_parts, layer_new = [], []
        for gi, (off, b_n, t_n) in enumerate(groups):
            n_g = b_n * t_n
            t0 = off // tm
            if gi == 0:
                st = [jnp.zeros((b_n,) + s.shape[2:], s.dtype)
                      for s in (state_ret, state_gla, state_s5_re, state_s5_im, state_rwkv)]
                shift0, rot_g, tc_g = jnp.zeros((b_n, w4), F32), rot_p, tc_p
            else:
                st = [state_ret[i], state_gla[i], state_s5_re[i], state_s5_im[i], state_rwkv[i]]
                shift0, rot_g, tc_g = state_shift[i], rot_s, tc_s
            vl_n = _lane_groups(b_n)
            grouped = vl_n > 1

            def cspec(width, cb, t0=t0):
                return pl.BlockSpec((tm, width), lambda r: (r + t0, cb))

            if grouped:
                t_shapes = [(2 * HEAD_D, n_g * vl_n * HEADS), (3 * GLA_DK, n_g * vl_n * HEADS),
                            (5 * HEAD_D, n_g * vl_n * HEADS), (3 * HEAD_D, n_g * HEADS)]
            else:
                t_shapes = [(2 * BRANCH_W, n_g), (3 * GLA_QK, n_g), (5 * BRANCH_W, n_g), (3 * BRANCH_W, n_g)]
            consts = [shift0] + prep_consts
            ka_ret, ka_gla, ka_rwkv, va, obonus, og = pl.pallas_call(
                functools.partial(_prep_kernel, grouped=grouped),
                grid=(n_g // tm,),
                in_specs=[cspec(BRANCH_W, 0), cspec(BRANCH_W, 1), cspec(BRANCH_W, 2),
                          ] + [_row_spec(rot_g[0].shape[0] // (n_g // tm), BRANCH_W, 0)] * 4 + [
                          cspec(GLA_QK, COL_GLA_Q // GLA_QK), cspec(GLA_QK, COL_GLA_K // GLA_QK),
                          cspec(BRANCH_W, COL_GLA_V // BRANCH_W), cspec(LANES, COL_GLA_R // LANES),
                          cspec(w4, rwkv_blk),
                          pl.BlockSpec((b_n, w4), lambda r, t0=t0, b_n=b_n:
                                       (jnp.maximum((r + t0) * (tm // b_n) - 1, 0), rwkv_blk))]
                         + [_const_spec(a.shape) for a in consts],
                out_specs=[pl.BlockSpec((sh[0], sh[1] // (n_g // tm)), lambda r: (0, r)) for sh in t_shapes]
                          + [_row_spec(tm, BRANCH_W, 0), _row_spec(tm, BRANCH_W, 0)],
                out_shape=[jax.ShapeDtypeStruct(sh, F32) for sh in t_shapes]
                          + [jax.ShapeDtypeStruct((n_g, BRANCH_W), F32)] * 2,
                compiler_params=_cparams("parallel"),
                name="mixer_prep",
            )(cols, cols, cols, *rot_g, cols, cols, cols, cols, cols, cols, *consts)

            dec = jnp.tile(jnp.repeat(gamma ** tc_g, b_n), vl_n).reshape(1, vl_n * HEADS * b_n)
            if not grouped:
                ka_ret = ka_ret.reshape(2, HEADS, HEAD_D, n_g)
                ka_gla = ka_gla.reshape(3, HEADS, GLA_DK, n_g)
                ka_rwkv = ka_rwkv.reshape(5, HEADS, HEAD_D, n_g)
                va = va.reshape(3, HEADS, HEAD_D, n_g)
            o, s_ret, s_gla, s_rwkv = _scan(
                ka_ret, ka_gla, ka_rwkv, va, b_n, t_n, dec,
                _state_to_lanes(st[0], vl_n, True), _state_to_lanes(st[1], vl_n, True),
                _state_to_lanes(st[4], vl_n, False))
            mixt = o if grouped else o.reshape(3 * BRANCH_W, n_g)
            y, hr, hi = _s5_scan(cols, off, *s5p, st[2].reshape(b_n, S5_CH), st[3].reshape(b_n, S5_CH), b_n, t_n)
            shift_new = cols[off + (t_n - 1) * b_n:off + t_n * b_n, COL_RWKV:COL_RWKV + w4]
            layer_new.append((_state_from_lanes(s_ret, b_n, vl_n, True), _state_from_lanes(s_gla, b_n, vl_n, True),
                              hr.reshape(b_n, S5_GROUPS, S5_STATE), hi.reshape(b_n, S5_GROUPS, S5_STATE),
                              _state_from_lanes(s_rwkv, b_n, vl_n, False), shift_new))

            x1_parts.append(pl.pallas_call(
                functools.partial(_post_kernel, alpha=alpha, b_n=b_n),
                grid=(n_g // tm,),
                in_specs=[cspec(d_model, 0),
                          pl.BlockSpec((mixt.shape[0], mixt.shape[1] // (n_g // tm)), lambda r: (0, r)),
                          cspec(BRANCH_W, 3), cspec(BRANCH_W, COL_GLA_G // BRANCH_W),
                          _row_spec(tm, BRANCH_W, 0), cspec(BRANCH_W, COL_S5 // BRANCH_W),
                          _row_spec(tm, BRANCH_W, 0), _row_spec(tm, BRANCH_W, 0)]
                         + [cspec(d_model, COL_GATE // d_model + q) for q in range(N_BRANCH)]
                         + [_const_spec(a.shape) for a in post_consts],
                out_specs=_row_spec(tm, d_model, 0),
                out_shape=jax.ShapeDtypeStruct((n_g, d_model), F32),
                compiler_params=_cparams("parallel"),
                name="mixer_post",
            )(x, mixt, cols, cols, y, cols, obonus, og, cols, cols, cols, cols, *post_consts))
        for lst, pair in zip(new, zip(*layer_new)):
            lst.append(pair)
        x1 = jnp.concatenate(x1_parts, axis=0)

        p = time_major(p_prompt[i], p_sample[i])
        tail_consts = [w_pe[i].astype(BF16), w_pg[i].astype(BF16), row(ln2_g[i]), row(ln2_b[i])]
        j = i // 2
        if i % 2 == 0:
            (x,) = _rowwise(
                functools.partial(_ffn_kernel, alpha=alpha), n,
                [(x1, d_model, 0), (p, p.shape[1], 0)],
                [ffn_w1[j].astype(BF16), ffn_w3[j].astype(BF16), ffn_w2[j].astype(BF16)] + tail_consts,
                [d_model], name="ffn")
        else:
            rh, rl = _split_bf16(_pad_cols(moe_router[j], LANES))
            idx, wgt = _rowwise(_router_kernel, n, [(x1, d_model, 0)], [rh, rl], [LANES, LANES],
                                out_dtypes=[jnp.int32, F32], name="moe_router")
            slot_tok, slot_dst, block_e, nb_used, block_valid, n_out_rows = _moe_route(idx[:, :TOP_K], n)
            yk = _moe_blocks(x1, block_e, nb_used, block_valid, slot_tok, slot_dst, n_out_rows,
                             moe_w1[j].astype(BF16), moe_w3[j].astype(BF16), moe_w2[j].astype(BF16))
            (x,) = _rowwise(
                functools.partial(_moe_combine_kernel, alpha=alpha), n,
                [(yk, d_model, 0), (yk, d_model, 0, n // tm), (x1, d_model, 0), (p, p.shape[1], 0), (wgt, LANES, 0)],
                tail_consts, [d_model], name="moe_combine")

    y_prompt = x[:n_p].reshape(tp, bp, d_model).transpose(1, 0, 2)
    y_sample = x[n_p:].reshape(ts, bs, d_model).transpose(1, 0, 2)
    outs = [y_prompt, y_sample]
    for lst in new:
        outs.append(jnp.stack([pair[0] for pair in lst], 0))
        outs.append(jnp.stack([pair[1] for pair in lst], 0))
    return tuple(outs)
```

```python
import functools

import jax
import jax.numpy as jnp
from jax import lax
from jax.experimental import pallas as pl
from jax.experimental.pallas import tpu as pltpu

F32 = jnp.float32
BF16 = jnp.bfloat16

LANES = 128
SUBLANES = 8
VMEM_LIMIT = 56 * 1024 * 1024

N_BRANCH = 4
BRANCH_W = 256
HEADS = 4
HEAD_D = 64
GLA_DK = 32
GLA_QK = HEADS * GLA_DK
GLA_GATE_RANK = 16
GLA_GATE_NORM = 16.0
S5_GROUP = 16
S5_GROUPS = 16
S5_STATE = 64
S5_CH = S5_GROUPS * S5_STATE
ROPE_BASE = 10000.0
RWKV_GN_EPS = 64e-5
LN_EPS = 1e-5
N_EXPERTS = 8
TOP_K = 2
MOE_BLOCK = 256
ROW_TILE = 256
SCAN_CHUNK = 32
PAST_LEN = 16384

COL_RET = 0
COL_GLA_Q = 1024
COL_GLA_K = 1152
COL_GLA_V = 1280
COL_GLA_G = 1536
COL_S5 = 1792
COL_RWKV = 2048
COL_GATE = 3072
COL_GLA_R = 7168
D_IN_PAD = 7296


def _cparams(*sem):
    return pltpu.CompilerParams(dimension_semantics=sem, vmem_limit_bytes=VMEM_LIMIT)


def _split_bf16(x):
    hi = x.astype(BF16)
    lo = (x - hi.astype(F32)).astype(BF16)
    return hi, lo


def _seg_dot(x, m_ref):
    hi, lo = _split_bf16(x)
    m = m_ref[...]
    return (jnp.dot(hi, m, preferred_element_type=F32)
            + jnp.dot(lo, m, preferred_element_type=F32))


def _bdot(x, w):
    return jnp.dot(x.astype(BF16), w, preferred_element_type=F32)


def _sigmoid(x):
    return 1.0 / (1.0 + jnp.exp(-x))


def _silu(x):
    return x * _sigmoid(x)


def _log1p_exp_neg_abs(x):
    return jnp.log1p(jnp.exp(-jnp.abs(x)))


def _layer_norm(x, g, b):
    mu = jnp.mean(x, axis=-1, keepdims=True)
    xc = x - mu
    var = jnp.mean(xc * xc, axis=-1, keepdims=True)
    return xc * lax.rsqrt(var + LN_EPS) * g + b


def _row_spec(tm, width, col_block, row_block0=0):
    return pl.BlockSpec((tm, width), lambda i, cb=col_block, r0=row_block0: (i + r0, cb))


def _const_spec(shape):
    nd = len(shape)
    return pl.BlockSpec(shape, lambda i, nd=nd: (0,) * nd)


def _rowwise(body, n_rows, row_in, const_in, out_widths, out_dtypes=None, tm=ROW_TILE, name=None):
    assert n_rows % tm == 0
    out_dtypes = out_dtypes or [F32] * len(out_widths)
    in_specs = [_row_spec(tm, *spec[1:]) for spec in row_in] + [_const_spec(a.shape) for a in const_in]
    out_specs = [_row_spec(tm, w, 0) for w in out_widths]
    out_shape = [jax.ShapeDtypeStruct((n_rows, w), dt) for w, dt in zip(out_widths, out_dtypes)]
    return pl.pallas_call(
        body,
        grid=(n_rows // tm,),
        in_specs=in_specs,
        out_specs=out_specs,
        out_shape=out_shape,
        compiler_params=_cparams("parallel"),
        name=name,
    )(*[spec[0] for spec in row_in], *const_in)


def _matmul_kernel(x_ref, w_ref, o_ref):
    o_ref[...] = _bdot(x_ref[...], w_ref[...]).astype(o_ref.dtype)


def _matmul(x, w, tm=512, tn=None, out_dtype=F32, name=None):
    m, k = x.shape
    n = w.shape[1]
    tn = tn or n
    tm = tm if m % tm == 0 else ROW_TILE
    assert m % tm == 0 and n % tn == 0
    return pl.pallas_call(
        _matmul_kernel,
        grid=(n // tn, m // tm),
        in_specs=[pl.BlockSpec((tm, k), lambda j, i: (i, 0)),
                  pl.BlockSpec((k, tn), lambda j, i: (0, j))],
        out_specs=pl.BlockSpec((tm, tn), lambda j, i: (i, j)),
        out_shape=jax.ShapeDtypeStruct((m, n), out_dtype),
        compiler_params=_cparams("parallel", "parallel"),
        name=name,
    )(x, w)


def _replicate_groups(w, n_grp):
    if n_grp == 1:
        return [w]
    grp = lax.broadcasted_iota(jnp.int32, w.shape, 1) // (LANES // n_grp)
    parts = [w]
    span = n_grp
    while span > 1:
        half = span // 2
        shift = half * (LANES // n_grp)
        low = (grp % span) < half
        nxt = []
        for z in parts:
            rz = pltpu.roll(z, shift, 1)
            nxt.append(jnp.where(low, z, rz))
            nxt.append(jnp.where(low, rz, z))
        parts = nxt
        span = half
    return parts


def _heads_to_lanes(ops, b_n, pad_to=LANES):
    tm = ops[0].shape[0]
    c = ops[0].shape[1] // HEADS
    packed = []
    for h in range(HEADS):
        pieces = [o[:, h * c:(h + 1) * c] for o in ops]
        if len(pieces) * c < pad_to:
            pieces.append(jnp.zeros((tm, pad_to - len(pieces) * c), F32))
        packed.append(jnp.concatenate(pieces, axis=1).reshape(tm // b_n, b_n, pad_to))
    y = jnp.stack(packed, axis=1).reshape(tm * HEADS, pad_to)
    return y.T


def _prep_kernel(rq_ref, rk_ref, rv_ref, cq_ref, sq_ref, ck_ref, sk_ref,
                 gq_ref, gk_ref, gv_ref, gr_ref,
                 c_ref, tail_ref, shift_ref, wg2_ref, bg_ref,
                 mu_ref, w0_ref, w1_ref, w2_ref, a0_ref, a1_ref, a2_ref, g1_ref, g2_ref,
                 kkp_ref, kap_ref, rkp_ref, ones_ref,
                 kr_ref, kg_ref, kw_ref, va_ref, obonus_ref, og_ref, *, grouped):
    i = pl.program_id(0)
    w = BRANCH_W
    lane = lax.broadcasted_iota(jnp.int32, rq_ref.shape, 1)
    first_half = (lane % HEAD_D) < (HEAD_D // 2)

    def per_row(tab_ref):
        tab = tab_ref[...]
        rows, reps = tab.shape[0], rq_ref.shape[0] // tab.shape[0]
        if reps == 1:
            return tab
        return jnp.broadcast_to(tab[:, None, :], (rows, reps, w)).reshape(rows * reps, w)

    def rot(x, cos_ref, sin_ref):
        partner = jnp.where(first_half,
                            pltpu.roll(x, BRANCH_W - HEAD_D // 2, 1),
                            pltpu.roll(x, HEAD_D // 2, 1))
        return x * per_row(cos_ref) + partner * per_row(sin_ref)

    ret_q = rot(rq_ref[...], cq_ref, sq_ref)
    ret_k = rot(rk_ref[...], ck_ref, sk_ref)

    z = _bdot(gr_ref[...], wg2_ref[...]) + bg_ref[...]
    glog = (jnp.minimum(z, 0.0) - _log1p_exp_neg_abs(z)) / GLA_GATE_NORM
    gla_q = gq_ref[...] * (GLA_DK ** -0.5)
    gla_al = jnp.exp(glog)

    c = c_ref[...]
    tm = c.shape[0]
    b_n = tail_ref.shape[0]
    tail = jnp.where(i == 0, shift_ref[...], tail_ref[...])
    d = jnp.concatenate([tail, c[:tm - b_n]], axis=0) - c
    mu = mu_ref[...]
    cr, ck, cv, cz = (c[:, j * w:(j + 1) * w] for j in range(4))
    dr, dk, dv, dz = (d[:, j * w:(j + 1) * w] for j in range(4))
    r = cr + dr * mu[0:1]
    k = ck + dk * mu[1:2]
    v = cv + dv * mu[2:3]
    zw = cz + dz * mu[3:4]
    za = cz + dz * mu[4:5]
    zg = cz + dz * mu[5:6]
    w_raw = w0_ref[...] + _bdot(jnp.tanh(_bdot(zw, w1_ref[...])), w2_ref[...])
    sp = jnp.maximum(-w_raw, 0.0) + _log1p_exp_neg_abs(w_raw)
    dec = jnp.exp(-jnp.exp(-sp - 0.5))
    a = _sigmoid(a0_ref[...] + _bdot(_bdot(za, a1_ref[...]), a2_ref[...]))
    og_ref[...] = _bdot(_sigmoid(_bdot(zg, g1_ref[...])), g2_ref[...])
    kk = k * kkp_ref[...]
    ss = _seg_dot(kk * kk, ones_ref)
    kk = kk * lax.rsqrt(jnp.maximum(ss, 1e-24))
    km = k * (1.0 + (a - 1.0) * kap_ref[...])
    obonus_ref[...] = _seg_dot(r * km * rkp_ref[...], ones_ref) * v

    if not grouped:
        for ref, ops in ((kr_ref, (ret_q, ret_k)), (kg_ref, (gla_q, gk_ref[...], gla_al)),
                         (kw_ref, (r, km, dec, kk, kk * a)), (va_ref, (rv_ref[...], gv_ref[...], v))):
            cw = ops[0].shape[1]
            for j, o in enumerate(ops):
                ref[j * cw:(j + 1) * cw, :] = o.T
        return

    n_grp = LANES // (HEADS * b_n)

    def emit_tiles(ref, row0, ops, n_rows):
        wt = _heads_to_lanes(ops, b_n)
        for lt in range(wt.shape[1] // LANES):
            for g, zt in enumerate(_replicate_groups(wt[:n_rows, lt * LANES:(lt + 1) * LANES], n_grp)):
                q = lt * n_grp + g
                ref[row0:row0 + n_rows, q * LANES:(q + 1) * LANES] = zt

    emit_tiles(kr_ref, 0, (ret_q, ret_k), 2 * HEAD_D)
    emit_tiles(kg_ref, 0, (gla_q, gk_ref[...], gla_al), 3 * GLA_DK)
    emit_tiles(kw_ref, 0, (r, km), 2 * HEAD_D)
    emit_tiles(kw_ref, 2 * HEAD_D, (dec, kk), 2 * HEAD_D)
    emit_tiles(kw_ref, 4 * HEAD_D, (kk * a,), HEAD_D)
    va_ref[0:2 * HEAD_D, :] = _heads_to_lanes((rv_ref[...], gv_ref[...]), b_n)
    va_ref[2 * HEAD_D:3 * HEAD_D, :] = _heads_to_lanes((v,), b_n)[:HEAD_D]


def _scan_kernel(kr_ref, kg_ref, kw_ref, va_ref, dec_ref, s0r_ref, s0g_ref, s0w_ref,
                 o_ref, sor_ref, sog_ref, sow_ref,
                 sr, sg, sw, vs, os_, *, vh_n, grouped, tc, b_n):
    ti = pl.program_id(1)

    @pl.when(ti == 0)
    def _():
        sr[...] = s0r_ref[...]
        sg[...] = s0g_ref[...]
        sw[...] = s0w_ref[...]

    n_grp = LANES // (HEADS * b_n) if grouped else 1
    grp_w = LANES // n_grp
    n_v = 3

    def k_tile(ref, n_ops, j, t):
        lanes = pl.ds(pl.multiple_of(t * LANES, LANES), LANES)
        if grouped:
            k_n = ref.shape[0] // n_ops
            return ref[j * k_n:(j + 1) * k_n, lanes]
        return ref[j, 0, :, lanes]

    if grouped:
        grp = lax.broadcasted_iota(jnp.int32, (vh_n, LANES), 1) // grp_w
        for lt in range(tc // n_grp):
            lanes = slice(lt * LANES, (lt + 1) * LANES)
            for j in range(n_v):
                ws = [va_ref[j * HEAD_D + vl * vh_n:j * HEAD_D + (vl + 1) * vh_n, lanes] for vl in range(n_grp)]
                for g in range(n_grp):
                    tile = None
                    for vl in range(n_grp):
                        r = pltpu.roll(ws[vl], ((vl - g) * grp_w) % LANES, 1)
                        tile = r if tile is None else jnp.where(grp == vl, r, tile)
                    vs[lt * n_grp + g, j] = tile
    else:
        def copy_in(t, c):
            lanes = pl.ds(pl.multiple_of(t * LANES, LANES), LANES)
            for j in range(n_v):
                vs[t, j] = va_ref[j, 0, :, lanes]
            return c

        lax.fori_loop(0, tc, copy_in, 0)

    dec_r = dec_ref[...]

    def out(t, j, vh, s, q):
        os_[t, j, pl.ds(vh, 1), :] = jnp.sum(s * q, axis=0, keepdims=True)

    def ret_step(t, c):
        q, k = k_tile(kr_ref, 2, 0, t), k_tile(kr_ref, 2, 1, t)
        for vh in range(vh_n):
            s = sr[vh] + vs[t, 0, pl.ds(vh, 1), :] * k
            sr[vh] = s
            out(t, 0, vh, s, q)
        return c

    def gla_step(t, c):
        q, k, al = (k_tile(kg_ref, 3, j, t) for j in range(3))
        for vh in range(vh_n):
            s = sg[vh] * al + vs[t, 1, pl.ds(vh, 1), :] * k
            sg[vh] = s
            out(t, 1, vh, s, q)
        return c

    def rwkv_step(t, c):
        q, k, dec, kk, beta = (k_tile(kw_ref, 5, j, t) for j in range(5))
        for vh in range(vh_n):
            s = sw[vh]
            sk = jnp.sum(s * kk, axis=0, keepdims=True)
            s = s * dec - sk * beta + vs[t, 2, pl.ds(vh, 1), :] * k
            sw[vh] = s
            out(t, 2, vh, s, q)
        return c

    unroll = 2 if grouped else 1
    lax.fori_loop(0, tc, ret_step, 0, unroll=unroll)
    sr[...] = sr[...] * dec_r
    lax.fori_loop(0, tc, gla_step, 0, unroll=unroll)
    lax.fori_loop(0, tc, rwkv_step, 0, unroll=unroll)

    if grouped:
        grp = lax.broadcasted_iota(jnp.int32, (vh_n, LANES), 1) // grp_w
        for lt in range(tc // n_grp):
            lanes = slice(lt * LANES, (lt + 1) * LANES)
            for j in range(n_v):
                og = [os_[lt * n_grp + g, j] for g in range(n_grp)]
                for vl in range(n_grp):
                    wv = None
                    for g in range(n_grp):
                        r = pltpu.roll(og[g], ((g - vl) * grp_w) % LANES, 1)
                        wv = r if wv is None else jnp.where(grp == g, r, wv)
                    o_ref[j * HEAD_D + vl * vh_n:j * HEAD_D + (vl + 1) * vh_n, lanes] = wv
    else:
        def copy_out(t, c):
            lanes = pl.ds(pl.multiple_of(t * LANES, LANES), LANES)
            for j in range(n_v):
                o_ref[j, 0, :, lanes] = os_[t, j]
            return c

        lax.fori_loop(0, tc, copy_out, 0)

    @pl.when(ti == pl.num_programs(1) - 1)
    def _():
        sor_ref[...] = sr[...]
        sog_ref[...] = sg[...]
        sow_ref[...] = sw[...]


def _scan(kr, kg, kw, va, b_n, t_n, dec, s0r, s0g, s0w):
    grouped = b_n * HEADS < LANES
    vh_n = s0r.shape[0]
    if grouped:
        tc = min(t_n, SCAN_CHUNK)
        n_grp = LANES // (HEADS * b_n)
        assert tc % n_grp == 0 and t_n % tc == 0
        n_l, n_t = 1, t_n // tc
        k_specs = [pl.BlockSpec((a.shape[0], tc * LANES), lambda l, t: (0, t)) for a in (kr, kg, kw)]
        v_spec = pl.BlockSpec((va.shape[0], tc * HEADS * b_n), lambda l, t: (0, t))
        o_spec = v_spec
    else:
        assert b_n == LANES
        tc, n_l, n_t = t_n, HEADS, 1
        k_specs = [pl.BlockSpec((a.shape[0], 1, a.shape[2], tc * LANES), lambda l, t: (0, l, 0, 0))
                   for a in (kr, kg, kw)]
        v_spec = pl.BlockSpec((3, 1, HEAD_D, tc * LANES), lambda l, t: (0, l, 0, 0))
        o_spec = v_spec

    def s_spec(a):
        return pl.BlockSpec(a.shape[:2] + (LANES,), lambda l, t: (0, 0, l))

    return pl.pallas_call(
        functools.partial(_scan_kernel, vh_n=vh_n, grouped=grouped, tc=tc, b_n=b_n),
        grid=(n_l, n_t),
        in_specs=k_specs + [v_spec, pl.BlockSpec((1, LANES), lambda l, t: (0, l)),
                            s_spec(s0r), s_spec(s0g), s_spec(s0w)],
        out_specs=[o_spec, s_spec(s0r), s_spec(s0g), s_spec(s0w)],
        out_shape=[jax.ShapeDtypeStruct(va.shape, F32)] + [jax.ShapeDtypeStruct(s.shape, F32)
                                                           for s in (s0r, s0g, s0w)],
        scratch_shapes=[pltpu.VMEM(s.shape[:2] + (LANES,), F32) for s in (s0r, s0g, s0w)]
                       + [pltpu.VMEM((tc, 3, vh_n, LANES), F32), pltpu.VMEM((tc, 3, vh_n, LANES), F32)],
        compiler_params=_cparams("parallel", "arbitrary"),
        name="scan_ret_gla_rwkv",
    )(kr, kg, kw, va, dec, s0r, s0g, s0w)


def _s5_kernel(u_ref, bb_ref, cc_ref, ar_ref, ai_ref, h0r_ref, h0i_ref,
               y_ref, hr_out, hi_out, hr_scr, hi_scr, xs_scr, hs_scr, *, b_n, tc):
    ti = pl.program_id(0)

    @pl.when(ti == 0)
    def _():
        hr_scr[...] = h0r_ref[...]
        hi_scr[...] = h0i_ref[...]

    xs_scr[...] = _bdot(u_ref[...], bb_ref[...])
    ar = ar_ref[...]
    ai = ai_ref[...]

    def step(t, carry):
        hr, hi = carry
        row = pl.multiple_of(t * b_n, SUBLANES)
        x = xs_scr[pl.ds(row, b_n), :]
        nr = ar * hr - ai * hi + x[:, :S5_CH]
        ni = ar * hi + ai * hr + x[:, S5_CH:]
        hs_scr[pl.ds(row, b_n), :S5_CH] = nr
        hs_scr[pl.ds(row, b_n), S5_CH:] = ni
        return nr, ni

    hr, hi = lax.fori_loop(0, tc, step, (hr_scr[...], hi_scr[...]))
    hr_scr[...] = hr
    hi_scr[...] = hi
    y_ref[...] = _bdot(hs_scr[...], cc_ref[...])

    @pl.when(ti == pl.num_programs(0) - 1)
    def _():
        hr_out[...] = hr
        hi_out[...] = hi


def _s5_scan(cols, row_off, bb, cc, ar, ai, h0r, h0i, b_n, t_n):
    tc = min(t_n, max(1, 1024 // b_n))
    rows = tc * b_n
    assert t_n % tc == 0 and row_off % rows == 0
    blk0 = row_off // rows
    return pl.pallas_call(
        functools.partial(_s5_kernel, b_n=b_n, tc=tc),
        grid=(t_n // tc,),
        in_specs=[pl.BlockSpec((rows, BRANCH_W), lambda t: (blk0 + t, COL_S5 // BRANCH_W)),
                  _const_spec(bb.shape), _const_spec(cc.shape),
                  _const_spec(ar.shape), _const_spec(ai.shape),
                  _const_spec(h0r.shape), _const_spec(h0i.shape)],
        out_specs=[pl.BlockSpec((rows, BRANCH_W), lambda t: (t, 0)),
                   _const_spec(h0r.shape), _const_spec(h0i.shape)],
        out_shape=[jax.ShapeDtypeStruct((t_n * b_n, BRANCH_W), F32),
                   jax.ShapeDtypeStruct(h0r.shape, F32),
                   jax.ShapeDtypeStruct(h0i.shape, F32)],
        scratch_shapes=[pltpu.VMEM((b_n, S5_CH), F32), pltpu.VMEM((b_n, S5_CH), F32),
                        pltpu.VMEM((rows, 2 * S5_CH), F32), pltpu.VMEM((rows, 2 * S5_CH), F32)],
        compiler_params=_cparams("arbitrary"),
        name="scan_s5",
    )(cols, bb, cc, ar, ai, h0r, h0i)


def _post_kernel(x_ref, mixt_ref, rg_ref, gg_ref, sy_ref, su_ref, wbon_ref, wg_ref,
                 gate0_ref, gate1_ref, gate2_ref, gate3_ref,
                 avg_ref, rgn_g, rgn_b, ggn_g, s5d_ref, wglu_ref, wgn_g, wgn_b,
                 wbr_ref, wo_ref, ln_g, ln_b, o_ref, *, alpha, b_n):
    def seg_mean(v):
        return _seg_dot(v, avg_ref)

    tm = x_ref.shape[0]
    if b_n * HEADS < LANES:
        yt = mixt_ref[...].T.reshape(tm // b_n, HEADS, b_n, 3 * HEAD_D)
        per_head = [yt[:, h].reshape(tm, 3 * HEAD_D) for h in range(HEADS)]
        mix = jnp.concatenate([ph[:, j * HEAD_D:(j + 1) * HEAD_D] for j in range(3) for ph in per_head], axis=1)
    else:
        mix = mixt_ref[...].T
    ro = mix[:, 0:BRANCH_W]
    mu = seg_mean(ro)
    rc = ro - mu
    var = seg_mean(rc * rc)
    b0 = (rc * lax.rsqrt(var + LN_EPS) * rgn_g[...] + rgn_b[...]) * _silu(rg_ref[...])
    go = mix[:, BRANCH_W:2 * BRANCH_W]
    ms = seg_mean(go * go)
    b1 = go * lax.rsqrt(ms + LN_EPS) * ggn_g[...] * _silu(gg_ref[...])
    y = jax.nn.gelu(sy_ref[...] + s5d_ref[...] * su_ref[...])
    b2 = y * _sigmoid(_bdot(y, wglu_ref[...]))
    wy = mix[:, 2 * BRANCH_W:3 * BRANCH_W]
    mu = seg_mean(wy)
    wc = wy - mu
    var = seg_mean(wc * wc)
    b3 = (wc * lax.rsqrt(var + RWKV_GN_EPS) * wgn_g[...] + wgn_b[...] + wbon_ref[...]) * wg_ref[...]

    m = None
    gates = (gate0_ref, gate1_ref, gate2_ref, gate3_ref)
    for i, br in enumerate((b0, b1, b2, b3)):
        term = _bdot(br, wbr_ref[i]) * _sigmoid(gates[i][...])
        m = term if m is None else m + term
    h = _bdot(m, wo_ref[...])
    o_ref[...] = _layer_norm(alpha * x_ref[...] + h, ln_g[...], ln_b[...])


def _embed_ln2(x, f, p_ref, wpe_ref, wpg_ref, ln_g, ln_b, alpha):
    e = _bdot(p_ref[...], wpe_ref[...]) * _sigmoid(_bdot(x, wpg_ref[...]))
    return _layer_norm(alpha * x + f + e, ln_g[...], ln_b[...])


def _ffn_kernel(x_ref, p_ref, w1_ref, w3_ref, w2_ref, wpe_ref, wpg_ref, ln_g, ln_b, o_ref, *, alpha):
    x = x_ref[...]
    xb = x.astype(BF16)
    h = _silu(jnp.dot(xb, w1_ref[...], preferred_element_type=F32)) * jnp.dot(
        xb, w3_ref[...], preferred_element_type=F32)
    f = _bdot(h, w2_ref[...])
    o_ref[...] = _embed_ln2(x, f, p_ref, wpe_ref, wpg_ref, ln_g, ln_b, alpha)


def _router_kernel(x_ref, wh_ref, wl_ref, idx_ref, wgt_ref):
    xh, xl = _split_bf16(x_ref[...])
    wh = wh_ref[...]
    logits = (jnp.dot(xh, wh, preferred_element_type=F32)
              + jnp.dot(xl, wh, preferred_element_type=F32)
              + jnp.dot(xh, wl_ref[...], preferred_element_type=F32))
    col = lax.broadcasted_iota(jnp.int32, logits.shape, 1)
    neg = jnp.float32(-jnp.inf)
    lg = jnp.where(col < N_EXPERTS, logits, neg)
    m1 = jnp.max(lg, axis=1, keepdims=True)
    i1 = jnp.min(jnp.where(lg == m1, col, LANES), axis=1, keepdims=True)
    lg2 = jnp.where(col == i1, neg, lg)
    m2 = jnp.max(lg2, axis=1, keepdims=True)
    i2 = jnp.min(jnp.where(lg2 == m2, col, LANES), axis=1, keepdims=True)
    e2 = jnp.exp(m2 - m1)
    den = 1.0 + e2
    idx_ref[...] = jnp.where(col == 0, i1, jnp.where(col == 1, i2, 0))
    wgt_ref[...] = jnp.where(col == 0, 1.0 / den, jnp.where(col == 1, e2 / den, 0.0))


def _row_copies(idx_ref, n_rows, make_copy, count=None):
    def each(action):
        def body(r, c):
            action(make_copy(r, idx_ref[0, 0, r]))
            return c

        if count is None:
            lax.fori_loop(0, n_rows, body, 0, unroll=8)
            return

        @pl.when(count == n_rows)
        def _():
            lax.fori_loop(0, n_rows, body, 0, unroll=8)

        @pl.when(count < n_rows)
        def _():
            lax.fori_loop(0, count, body, 0)

    return (lambda: each(lambda cp: cp.start())), (lambda: each(lambda cp: cp.wait()))


def _moe_block_kernel(be_ref, nb_ref, bv_ref, tok_ref, tok_next_ref, dst_ref, dst_m1_ref, dst_m2_ref,
                      x_hbm, w1_ref, w3_ref, w2_ref, out_hbm, xbuf, ybuf, sem_in, sem_out):
    j = pl.program_id(0)
    last = pl.num_programs(0) - 1
    nb = nb_ref[0]
    slot = j % 2

    def gather_copy(idx_ref, half, r):
        return pltpu.make_async_copy(x_hbm.at[pl.ds(idx_ref[0, 0, r], 1)], xbuf.at[half, pl.ds(r, 1)],
                                     sem_in.at[half])

    def gather(idx_ref, half):
        return _row_copies(idx_ref, MOE_BLOCK, lambda r, row: pltpu.make_async_copy(
            x_hbm.at[pl.ds(row, 1)], xbuf.at[half, pl.ds(r, 1)], sem_in.at[half]))

    def scatter(idx_ref, half, block):
        return _row_copies(idx_ref, MOE_BLOCK, lambda r, row: pltpu.make_async_copy(
            ybuf.at[half, pl.ds(r, 1)], out_hbm.at[pl.ds(row, 1)], sem_out.at[half]),
            count=bv_ref[jnp.maximum(block, 0)])

    @pl.when(jnp.logical_and(j == 0, nb > 0))
    def _():
        gather(tok_ref, 0)[0]()

    def step(cur):
        gather(tok_ref, cur)[1]()

        @pl.when(j >= 2)
        def _():
            scatter(dst_m2_ref, cur, j - 2)[1]()

        for r in range(MOE_BLOCK):
            gather_copy(tok_next_ref, 1 - cur, r).start()
        xb = xbuf[cur].astype(BF16)
        h = _silu(jnp.dot(xb, w1_ref[0], preferred_element_type=F32)) * jnp.dot(
            xb, w3_ref[0], preferred_element_type=F32)
        ybuf[cur] = _bdot(h, w2_ref[0])
        scatter(dst_ref, cur, j)[0]()

    for par in range(2):
        pl.when(jnp.logical_and(j < nb, slot == par))(functools.partial(step, par))

    @pl.when(jnp.logical_and(j == nb, nb > 0))
    def _():
        gather(tok_ref, slot)[1]()

    @pl.when(jnp.logical_and(j >= nb, jnp.logical_and(j >= 2, j - 2 < nb)))
    def _():
        scatter(dst_m2_ref, slot, j - 2)[1]()

    @pl.when(jnp.logical_and(j == last, jnp.logical_and(j >= 1, j - 1 < nb)))
    def _():
        scatter(dst_m1_ref, 1 - slot, j - 1)[1]()

    @pl.when(jnp.logical_and(j == last, j < nb))
    def _():
        scatter(dst_ref, slot, j)[1]()


def _moe_blocks(x, block_e, nb_used, block_valid, slot_tok, slot_dst, n_out_rows, w1, w3, w2):
    n_blocks = block_e.shape[0]
    d = x.shape[1]
    dff = w1.shape[2]
    tok = slot_tok.reshape(n_blocks, 1, MOE_BLOCK)
    dst = slot_dst.reshape(n_blocks, 1, MOE_BLOCK)

    def idx_spec(shift):
        return pl.BlockSpec((1, 1, MOE_BLOCK),
                            lambda j, be, nb, bv: (jnp.clip(j + shift, 0, n_blocks - 1), 0, 0),
                            memory_space=pltpu.SMEM)

    grid_spec = pltpu.PrefetchScalarGridSpec(
        num_scalar_prefetch=3,
        grid=(n_blocks + 1,),
        in_specs=[
            idx_spec(0), idx_spec(1), idx_spec(0), idx_spec(-1), idx_spec(-2),
            pl.BlockSpec(memory_space=pl.ANY),
            pl.BlockSpec((1, d, dff), lambda j, be, nb, bv: (be[jnp.minimum(j, n_blocks - 1)], 0, 0),
                         pipeline_mode=pl.Buffered(1)),
            pl.BlockSpec((1, d, dff), lambda j, be, nb, bv: (be[jnp.minimum(j, n_blocks - 1)], 0, 0),
                         pipeline_mode=pl.Buffered(1)),
            pl.BlockSpec((1, dff, d), lambda j, be, nb, bv: (be[jnp.minimum(j, n_blocks - 1)], 0, 0),
                         pipeline_mode=pl.Buffered(1)),
        ],
        out_specs=pl.BlockSpec(memory_space=pl.ANY),
        scratch_shapes=[pltpu.VMEM((2, MOE_BLOCK, d), F32), pltpu.VMEM((2, MOE_BLOCK, d), F32),
                        pltpu.SemaphoreType.DMA((2,)), pltpu.SemaphoreType.DMA((2,))],
    )
    return pl.pallas_call(
        _moe_block_kernel,
        grid_spec=grid_spec,
        out_shape=jax.ShapeDtypeStruct((n_out_rows, d), F32),
        compiler_params=_cparams("arbitrary"),
        name="moe_blocks",
    )(block_e, nb_used, block_valid, tok, tok, dst, dst, dst, x, w1, w3, w2)


def _moe_combine_kernel(y0_ref, y1_ref, x_ref, p_ref, wgt_ref, wpe_ref, wpg_ref, ln_g, ln_b, o_ref, *, alpha):
    wgt = wgt_ref[...]
    f = y0_ref[...] * wgt[:, 0:1] + y1_ref[...] * wgt[:, 1:2]
    o_ref[...] = _embed_ln2(x_ref[...], f, p_ref, wpe_ref, wpg_ref, ln_g, ln_b, alpha)


def _lane_groups(b_n):
    bh = b_n * HEADS
    return LANES // bh if bh < LANES else 1


def _state_to_lanes(s, vl_n, value_last):
    s = s.transpose(3, 2, 1, 0) if value_last else s.transpose(2, 3, 1, 0)
    v_n, k_n, h_n, b_n = s.shape
    s = s.reshape(vl_n, v_n // vl_n, k_n, h_n, b_n).transpose(1, 2, 0, 3, 4)
    return s.reshape(v_n // vl_n, k_n, vl_n * h_n * b_n)


def _state_from_lanes(s, b_n, vl_n, value_last):
    vh_n, k_n, _ = s.shape
    s = s.reshape(vh_n, k_n, vl_n, HEADS, b_n).transpose(4, 3, 1, 2, 0).reshape(b_n, HEADS, k_n, vl_n * vh_n)
    return s if value_last else s.transpose(0, 1, 3, 2)


def _rotary_tables(pos, row_repeat, gamma, tc):
    half = HEAD_D // 2
    freq = ROPE_BASE ** (-jnp.arange(half, dtype=F32) / half)
    ang = pos.astype(F32)[:, None] * freq[None, :]
    cos, sin = jnp.cos(ang), jnp.sin(ang)
    cos_h = jnp.tile(jnp.concatenate([cos, cos], axis=-1), (1, HEADS))
    sin_h = jnp.tile(jnp.concatenate([-sin, sin], axis=-1), (1, HEADS))
    tau1 = (jnp.arange(pos.shape[0], dtype=jnp.int32) % tc + 1).astype(F32)
    f_q = jnp.repeat(gamma[None, :] ** tau1[:, None], HEAD_D, axis=1)
    f_k = (HEAD_D ** -0.5) / f_q
    tabs = (cos_h * f_q, sin_h * f_q, cos_h * f_k, sin_h * f_k)
    return tuple(jnp.repeat(t, row_repeat, axis=0) for t in tabs) if row_repeat > 1 else tabs


def _block_diag_const(block, n_blocks):
    return jnp.kron(jnp.eye(n_blocks, dtype=F32), jnp.full((block, block), 1.0, F32))


def _s5_params(log_dt, a_re, a_im, b_re, b_im, c_re, c_im):
    dt = jnp.exp(log_dt)[:, None]
    mag = jnp.exp(dt * a_re)
    ang = dt * a_im
    abar_re, abar_im = mag * jnp.cos(ang), mag * jnp.sin(ang)
    den = a_re * a_re + a_im * a_im
    n_re = abar_re - 1.0
    f_re = (n_re * a_re + abar_im * a_im) / den
    f_im = (abar_im * a_re - n_re * a_im) / den
    bb_re = f_re[..., None] * b_re - f_im[..., None] * b_im
    bb_im = f_re[..., None] * b_im + f_im[..., None] * b_re
    eye = jnp.eye(S5_GROUPS, dtype=F32)

    def in_map(bb):
        return jnp.einsum("gpc,gh->gchp", bb, eye).reshape(BRANCH_W, S5_CH)

    def out_map(cm):
        return jnp.einsum("gcp,gh->gphc", cm, eye).reshape(S5_CH, BRANCH_W)

    bb = jnp.concatenate([in_map(bb_re), in_map(bb_im)], axis=1).astype(BF16)
    cc = jnp.concatenate([out_map(c_re), -out_map(c_im)], axis=0).astype(BF16)
    return bb, cc, abar_re.reshape(1, S5_CH), abar_im.reshape(1, S5_CH)


def _reorder_w_in(w_in):
    d_model = w_in.shape[0]
    main = jnp.concatenate([w_in[:, :1792], w_in[:, 1808:]], axis=1)
    tail = jnp.concatenate([w_in[:, 1792:1808],
                            jnp.zeros((d_model, D_IN_PAD - COL_GLA_R - GLA_GATE_RANK), w_in.dtype)], axis=1)
    return jnp.concatenate([main, tail], axis=1).astype(BF16)


def _pad_cols(w, n):
    return jnp.pad(w, ((0, 0), (0, n - w.shape[1])))


def _pad_rows(w, n):
    return jnp.pad(w, ((0, n - w.shape[0]), (0, 0)))


def _moe_route(idx, n):
    nk = n * TOP_K
    flat_e = idx.reshape(nk)
    onehot = (flat_e[:, None] == jnp.arange(N_EXPERTS, dtype=jnp.int32)[None, :]).astype(jnp.int32)
    incl = jnp.cumsum(onehot, axis=0)
    counts = incl[-1]
    rank = jnp.sum((incl - onehot) * onehot, axis=1)
    padded = (counts + MOE_BLOCK - 1) // MOE_BLOCK * MOE_BLOCK
    pad_end = jnp.cumsum(padded)
    slot = (pad_end - padded)[flat_e] + rank
    n_blocks = -(-(nk + N_EXPERTS * (MOE_BLOCK - 1)) // MOE_BLOCK)
    cap = n_blocks * MOE_BLOCK
    pair = jnp.full((cap,), -1, jnp.int32).at[slot].set(jnp.arange(nk, dtype=jnp.int32))
    real = pair >= 0
    slot_tok = jnp.where(real, pair // TOP_K, 0)
    slot_dst = jnp.where(real, (pair % TOP_K) * n + pair // TOP_K, 0)
    block_start = jnp.arange(n_blocks, dtype=jnp.int32) * MOE_BLOCK
    block_e = jnp.minimum(jnp.sum((pad_end[None, :] <= block_start[:, None]).astype(jnp.int32), axis=1),
                          N_EXPERTS - 1).astype(jnp.int32)
    nb_used = (pad_end[-1] // MOE_BLOCK).astype(jnp.int32).reshape(1)
    block_valid = jnp.sum(real.reshape(n_blocks, MOE_BLOCK).astype(jnp.int32), axis=1)
    return slot_tok, slot_dst, block_e, nb_used, block_valid, nk


def kernel(x_prompt, x_sample, state_ret, state_gla, state_s5_re, state_s5_im, state_rwkv, state_shift,
           p_prompt, p_sample, w_in, ret_gn_g, ret_gn_b, gla_wg2, gla_bg, gla_gn,
           s5_log_dt, s5_a_re, s5_a_im, s5_b_re, s5_b_im, s5_c_re, s5_c_im, s5_d, s5_w_glu,
           rwkv_mu, rwkv_w0, rwkv_w1, rwkv_w2, rwkv_a0, rwkv_a1, rwkv_a2, rwkv_g1, rwkv_g2,
           rwkv_kk, rwkv_ka, rwkv_rk, rwkv_gn_g, rwkv_gn_b, w_branch, w_o,
           ln1_g, ln1_b, ln2_g, ln2_b, w_pe, w_pg, ffn_w1, ffn_w3, ffn_w2,
           moe_router, moe_w1, moe_w3, moe_w2):
    depth = w_in.shape[0]
    bp, tp, d_model = x_prompt.shape
    bs, ts, _ = x_sample.shape
    n_p, n_s = bp * tp, bs * ts
    n = n_p + n_s
    tm = ROW_TILE
    assert n_p % tm == 0 and n_s % tm == 0 and tm % bp == 0 and tm % bs == 0 and n_p % bs == 0
    alpha = (2 * depth) ** 0.25
    groups = [(0, bp, tp), (n_p, bs, ts)]
    w4 = 4 * BRANCH_W

    def time_major(a_p, a_s):
        return jnp.concatenate([a_p.transpose(1, 0, 2).reshape(n_p, -1),
                                a_s.transpose(1, 0, 2).reshape(n_s, -1)], axis=0)

    x = time_major(x_prompt, x_sample)
    ones_bd = _block_diag_const(HEAD_D, HEADS).astype(BF16)
    avg_bd = (_block_diag_const(HEAD_D, HEADS) / HEAD_D).astype(BF16)
    gamma = 1.0 - jnp.exp2(-5.0 - jnp.arange(HEADS, dtype=F32))
    tc_p, tc_s = min(tp, SCAN_CHUNK), ts
    rot_p = _rotary_tables(jnp.arange(tp, dtype=jnp.int32), 1 if tm // bp >= SUBLANES else bp, gamma, tc_p)
    rot_s = _rotary_tables(PAST_LEN + jnp.arange(ts, dtype=jnp.int32), 1 if tm // bs >= SUBLANES else bs,
                           gamma, tc_s)
    row = lambda v: v.reshape(1, -1)

    new = [[] for _ in range(6)]
    for i in range(depth):
        cols = _matmul(x, _reorder_w_in(w_in[i]), tm=512, tn=D_IN_PAD // 3, name="in_proj")

        prep_consts = [
            _pad_rows(_pad_cols(gla_wg2[i], LANES), LANES).astype(BF16), row(gla_bg[i]),
            rwkv_mu[i], row(rwkv_w0[i]),
            _pad_cols(rwkv_w1[i], LANES).astype(BF16), _pad_rows(rwkv_w2[i], LANES).astype(BF16),
            row(rwkv_a0[i]),
            _pad_cols(rwkv_a1[i], LANES).astype(BF16), _pad_rows(rwkv_a2[i], LANES).astype(BF16),
            _pad_cols(rwkv_g1[i], LANES).astype(BF16), _pad_rows(rwkv_g2[i], LANES).astype(BF16),
            row(rwkv_kk[i]), row(rwkv_ka[i]), row(rwkv_rk[i]), ones_bd,
        ]
        post_consts = [avg_bd, row(ret_gn_g[i]), row(ret_gn_b[i]), row(gla_gn[i]), row(s5_d[i]),
                       s5_w_glu[i].astype(BF16), row(rwkv_gn_g[i]), row(rwkv_gn_b[i]),
                       w_branch[i].astype(BF16), w_o[i].astype(BF16), row(ln1_g[i]), row(ln1_b[i])]
        s5p = _s5_params(s5_log_dt[i], s5_a_re[i], s5_a_im[i], s5_b_re[i], s5_b_im[i], s5_c_re[i], s5_c_im[i])
        rwkv_blk = COL_RWKV // w4
        x1, layer_new = x, []
        for gi, (off, b_n, t_n) in enumerate(groups):
            n_g = b_n * t_n
            t0 = off // tm
            if gi == 0:
                st = [jnp.zeros((b_n,) + s.shape[2:], s.dtype)
                      for s in (state_ret, state_gla, state_s5_re, state_s5_im, state_rwkv)]
                shift0, rot_g, tc_g = jnp.zeros((b_n, w4), F32), rot_p, tc_p
            else:
                st = [state_ret[i], state_gla[i], state_s5_re[i], state_s5_im[i], state_rwkv[i]]
                shift0, rot_g, tc_g = state_shift[i], rot_s, tc_s
            vl_n = _lane_groups(b_n)
            grouped = vl_n > 1

            def cspec(width, cb, t0=t0):
                return pl.BlockSpec((tm, width), lambda r: (r + t0, cb))

            if grouped:
                t_shapes = [(2 * HEAD_D, n_g * vl_n * HEADS), (3 * GLA_DK, n_g * vl_n * HEADS),
                            (5 * HEAD_D, n_g * vl_n * HEADS), (3 * HEAD_D, n_g * HEADS)]
            else:
                t_shapes = [(2 * BRANCH_W, n_g), (3 * GLA_QK, n_g), (5 * BRANCH_W, n_g), (3 * BRANCH_W, n_g)]
            consts = [shift0] + prep_consts
            ka_ret, ka_gla, ka_rwkv, va, obonus, og = pl.pallas_call(
                functools.partial(_prep_kernel, grouped=grouped),
                grid=(n_g // tm,),
                in_specs=[cspec(BRANCH_W, 0), cspec(BRANCH_W, 1), cspec(BRANCH_W, 2),
                          ] + [_row_spec(rot_g[0].shape[0] // (n_g // tm), BRANCH_W, 0)] * 4 + [
                          cspec(GLA_QK, COL_GLA_Q // GLA_QK), cspec(GLA_QK, COL_GLA_K // GLA_QK),
                          cspec(BRANCH_W, COL_GLA_V // BRANCH_W), cspec(LANES, COL_GLA_R // LANES),
                          cspec(w4, rwkv_blk),
                          pl.BlockSpec((b_n, w4), lambda r, t0=t0, b_n=b_n:
                                       (jnp.maximum((r + t0) * (tm // b_n) - 1, 0), rwkv_blk))]
                         + [_const_spec(a.shape) for a in consts],
                out_specs=[pl.BlockSpec((sh[0], sh[1] // (n_g // tm)), lambda r: (0, r)) for sh in t_shapes]
                          + [_row_spec(tm, BRANCH_W, 0), _row_spec(tm, BRANCH_W, 0)],
                out_shape=[jax.ShapeDtypeStruct(sh, F32) for sh in t_shapes]
                          + [jax.ShapeDtypeStruct((n_g, BRANCH_W), F32)] * 2,
                compiler_params=_cparams("parallel"),
                name="mixer_prep",
            )(cols, cols, cols, *rot_g, cols, cols, cols, cols, cols, cols, *consts)

            dec = jnp.tile(jnp.repeat(gamma ** tc_g, b_n), vl_n).reshape(1, vl_n * HEADS * b_n)
            if not grouped:
                ka_ret = ka_ret.reshape(2, HEADS, HEAD_D, n_g)
                ka_gla = ka_gla.reshape(3, HEADS, GLA_DK, n_g)
                ka_rwkv = ka_rwkv.reshape(5, HEADS, HEAD_D, n_g)
                va = va.reshape(3, HEADS, HEAD_D, n_g)
            o, s_ret, s_gla, s_rwkv = _scan(
                ka_ret, ka_gla, ka_rwkv, va, b_n, t_n, dec,
                _state_to_lanes(st[0], vl_n, True), _state_to_lanes(st[1], vl_n, True),
                _state_to_lanes(st[4], vl_n, False))
            mixt = o if grouped else o.reshape(3 * BRANCH_W, n_g)
            y, hr, hi = _s5_scan(cols, off, *s5p, st[2].reshape(b_n, S5_CH), st[3].reshape(b_n, S5_CH), b_n, t_n)
            shift_new = cols[off + (t_n - 1) * b_n:off + t_n * b_n, COL_RWKV:COL_RWKV + w4]
            layer_new.append((_state_from_lanes(s_ret, b_n, vl_n, True), _state_from_lanes(s_gla, b_n, vl_n, True),
                              hr.reshape(b_n, S5_GROUPS, S5_STATE), hi.reshape(b_n, S5_GROUPS, S5_STATE),
                              _state_from_lanes(s_rwkv, b_n, vl_n, False), shift_new))

            x1 = pl.pallas_call(
                functools.partial(_post_kernel, alpha=alpha, b_n=b_n),
                grid=(n_g // tm,),
                in_specs=[cspec(d_model, 0),
                          pl.BlockSpec((mixt.shape[0], mixt.shape[1] // (n_g // tm)), lambda r: (0, r)),
                          cspec(BRANCH_W, 3), cspec(BRANCH_W, COL_GLA_G // BRANCH_W),
                          _row_spec(tm, BRANCH_W, 0), cspec(BRANCH_W, COL_S5 // BRANCH_W),
                          _row_spec(tm, BRANCH_W, 0), _row_spec(tm, BRANCH_W, 0)]
                         + [cspec(d_model, COL_GATE // d_model + q) for q in range(N_BRANCH)]
                         + [_const_spec(a.shape) for a in post_consts],
                out_specs=cspec(d_model, 0),
                out_shape=jax.ShapeDtypeStruct((n, d_model), F32),
                input_output_aliases={0: 0},
                compiler_params=_cparams("parallel"),
                name="mixer_post",
            )(x1, mixt, cols, cols, y, cols, obonus, og, cols, cols, cols, cols, *post_consts)
        for lst, pair in zip(new, zip(*layer_new)):
            lst.append(pair)

        p = time_major(p_prompt[i], p_sample[i])
        tail_consts = [w_pe[i].astype(BF16), w_pg[i].astype(BF16), row(ln2_g[i]), row(ln2_b[i])]
        j = i // 2
        if i % 2 == 0:
            (x,) = _rowwise(
                functools.partial(_ffn_kernel, alpha=alpha), n,
                [(x1, d_model, 0), (p, p.shape[1], 0)],
                [ffn_w1[j].astype(BF16), ffn_w3[j].astype(BF16), ffn_w2[j].astype(BF16)] + tail_consts,
                [d_model], name="ffn")
        else:
            rh, rl = _split_bf16(_pad_cols(moe_router[j], LANES))
            idx, wgt = _rowwise(_router_kernel, n, [(x1, d_model, 0)], [rh, rl], [LANES, LANES],
                                out_dtypes=[jnp.int32, F32], name="moe_router")
            slot_tok, slot_dst, block_e, nb_used, block_valid, n_out_rows = _moe_route(idx[:, :TOP_K], n)
            yk = _moe_blocks(x1, block_e, nb_used, block_valid, slot_tok, slot_dst, n_out_rows,
                             moe_w1[j].astype(BF16), moe_w3[j].astype(BF16), moe_w2[j].astype(BF16))
            (x,) = _rowwise(
                functools.partial(_moe_combine_kernel, alpha=alpha), n,
                [(yk, d_model, 0), (yk, d_model, 0, n // tm), (x1, d_model, 0), (p, p.shape[1], 0), (wgt, LANES, 0)],
                tail_consts, [d_model], name="moe_combine")

    y_prompt = x[:n_p].reshape(tp, bp, d_model).transpose(1, 0, 2)
    y_sample = x[n_p:].reshape(ts, bs, d_model).transpose(1, 0, 2)
    outs = [y_prompt, y_sample]
    for lst in new:
        outs.append(jnp.stack([pair[0] for pair in lst], 0))
        outs.append(jnp.stack([pair[1] for pair in lst], 0))
    return tuple(outs)
```

```python
import functools

import jax
import jax.numpy as jnp
from jax import lax
from jax.experimental import pallas as pl
from jax.experimental.pallas import tpu as pltpu

F32 = jnp.float32
BF16 = jnp.bfloat16

LANES = 128
SUBLANES = 8
VMEM_LIMIT = 56 * 1024 * 1024

N_BRANCH = 4
BRANCH_W = 256
HEADS = 4
HEAD_D = 64
GLA_DK = 32
GLA_QK = HEADS * GLA_DK
GLA_GATE_RANK = 16
GLA_GATE_NORM = 16.0
S5_GROUP = 16
S5_GROUPS = 16
S5_STATE = 64
S5_CH = S5_GROUPS * S5_STATE
ROPE_BASE = 10000.0
RWKV_GN_EPS = 64e-5
LN_EPS = 1e-5
N_EXPERTS = 8
TOP_K = 2
MOE_BLOCK = 256
ROW_TILE = 256
SCAN_CHUNK = 32
PAST_LEN = 16384

COL_RET = 0
COL_GLA_Q = 1024
COL_GLA_K = 1152
COL_GLA_V = 1280
COL_GLA_G = 1536
COL_S5 = 1792
COL_RWKV = 2048
COL_GATE = 3072
COL_GLA_R = 7168
D_IN_PAD = 7296


def _cparams(*sem):
    return pltpu.CompilerParams(dimension_semantics=sem, vmem_limit_bytes=VMEM_LIMIT)


def _split_bf16(x):
    hi = x.astype(BF16)
    lo = (x - hi.astype(F32)).astype(BF16)
    return hi, lo


def _seg_dot(x, m_ref):
    hi, lo = _split_bf16(x)
    m = m_ref[...]
    return (jnp.dot(hi, m, preferred_element_type=F32)
            + jnp.dot(lo, m, preferred_element_type=F32))


def _bdot(x, w):
    return jnp.dot(x.astype(BF16), w, preferred_element_type=F32)


def _sigmoid(x):
    return 1.0 / (1.0 + jnp.exp(-x))


def _silu(x):
    return x * _sigmoid(x)


def _log1p_exp_neg_abs(x):
    return jnp.log1p(jnp.exp(-jnp.abs(x)))


def _layer_norm(x, g, b):
    mu = jnp.mean(x, axis=-1, keepdims=True)
    xc = x - mu
    var = jnp.mean(xc * xc, axis=-1, keepdims=True)
    return xc * lax.rsqrt(var + LN_EPS) * g + b


def _row_spec(tm, width, col_block, row_block0=0):
    return pl.BlockSpec((tm, width), lambda i, cb=col_block, r0=row_block0: (i + r0, cb))


def _const_spec(shape):
    nd = len(shape)
    return pl.BlockSpec(shape, lambda i, nd=nd: (0,) * nd)


def _rowwise(body, n_rows, row_in, const_in, out_widths, out_dtypes=None, tm=ROW_TILE, name=None):
    assert n_rows % tm == 0
    out_dtypes = out_dtypes or [F32] * len(out_widths)
    in_specs = [_row_spec(tm, *spec[1:]) for spec in row_in] + [_const_spec(a.shape) for a in const_in]
    out_specs = [_row_spec(tm, w, 0) for w in out_widths]
    out_shape = [jax.ShapeDtypeStruct((n_rows, w), dt) for w, dt in zip(out_widths, out_dtypes)]
    return pl.pallas_call(
        body,
        grid=(n_rows // tm,),
        in_specs=in_specs,
        out_specs=out_specs,
        out_shape=out_shape,
        compiler_params=_cparams("parallel"),
        name=name,
    )(*[spec[0] for spec in row_in], *const_in)


def _matmul_kernel(x_ref, w_ref, o_ref):
    o_ref[...] = _bdot(x_ref[...], w_ref[...]).astype(o_ref.dtype)


def _matmul(x, w, tm=512, tn=None, out_dtype=F32, name=None):
    m, k = x.shape
    n = w.shape[1]
    tn = tn or n
    tm = tm if m % tm == 0 else ROW_TILE
    assert m % tm == 0 and n % tn == 0
    return pl.pallas_call(
        _matmul_kernel,
        grid=(n // tn, m // tm),
        in_specs=[pl.BlockSpec((tm, k), lambda j, i: (i, 0)),
                  pl.BlockSpec((k, tn), lambda j, i: (0, j))],
        out_specs=pl.BlockSpec((tm, tn), lambda j, i: (i, j)),
        out_shape=jax.ShapeDtypeStruct((m, n), out_dtype),
        compiler_params=_cparams("parallel", "parallel"),
        name=name,
    )(x, w)


def _replicate_groups(w, n_grp):
    if n_grp == 1:
        return [w]
    grp = lax.broadcasted_iota(jnp.int32, w.shape, 1) // (LANES // n_grp)
    parts = [w]
    span = n_grp
    while span > 1:
        half = span // 2
        shift = half * (LANES // n_grp)
        low = (grp % span) < half
        nxt = []
        for z in parts:
            rz = pltpu.roll(z, shift, 1)
            nxt.append(jnp.where(low, z, rz))
            nxt.append(jnp.where(low, rz, z))
        parts = nxt
        span = half
    return parts


def _heads_to_lanes(ops, b_n, pad_to=LANES):
    tm = ops[0].shape[0]
    c = ops[0].shape[1] // HEADS
    packed = []
    for h in range(HEADS):
        pieces = [o[:, h * c:(h + 1) * c] for o in ops]
        if len(pieces) * c < pad_to:
            pieces.append(jnp.zeros((tm, pad_to - len(pieces) * c), F32))
        packed.append(jnp.concatenate(pieces, axis=1).reshape(tm // b_n, b_n, pad_to))
    y = jnp.stack(packed, axis=1).reshape(tm * HEADS, pad_to)
    return y.T


def _prep_kernel(rq_ref, rk_ref, rv_ref, cq_ref, sq_ref, ck_ref, sk_ref,
                 gq_ref, gk_ref, gv_ref, gr_ref,
                 c_ref, tail_ref, shift_ref, wg2_ref, bg_ref,
                 mu_ref, w0_ref, w1_ref, w2_ref, a0_ref, a1_ref, a2_ref, g1_ref, g2_ref,
                 kkp_ref, kap_ref, rkp_ref, ones_ref,
                 kr_ref, kg_ref, kw_ref, va_ref, obonus_ref, og_ref, *, grouped):
    i = pl.program_id(0)
    w = BRANCH_W
    lane = lax.broadcasted_iota(jnp.int32, rq_ref.shape, 1)
    first_half = (lane % HEAD_D) < (HEAD_D // 2)

    def per_row(tab_ref):
        tab = tab_ref[...]
        rows, reps = tab.shape[0], rq_ref.shape[0] // tab.shape[0]
        if reps == 1:
            return tab
        return jnp.broadcast_to(tab[:, None, :], (rows, reps, w)).reshape(rows * reps, w)

    def rot(x, cos_ref, sin_ref):
        partner = jnp.where(first_half,
                            pltpu.roll(x, BRANCH_W - HEAD_D // 2, 1),
                            pltpu.roll(x, HEAD_D // 2, 1))
        return x * per_row(cos_ref) + partner * per_row(sin_ref)

    ret_q = rot(rq_ref[...], cq_ref, sq_ref)
    ret_k = rot(rk_ref[...], ck_ref, sk_ref)

    z = _bdot(gr_ref[...], wg2_ref[...]) + bg_ref[...]
    glog = (jnp.minimum(z, 0.0) - _log1p_exp_neg_abs(z)) / GLA_GATE_NORM
    gla_q = gq_ref[...] * (GLA_DK ** -0.5)
    gla_al = jnp.exp(glog)

    c = c_ref[...]
    tm = c.shape[0]
    b_n = tail_ref.shape[0]
    tail = jnp.where(i == 0, shift_ref[...], tail_ref[...])
    d = jnp.concatenate([tail, c[:tm - b_n]], axis=0) - c
    mu = mu_ref[...]
    cr, ck, cv, cz = (c[:, j * w:(j + 1) * w] for j in range(4))
    dr, dk, dv, dz = (d[:, j * w:(j + 1) * w] for j in range(4))
    r = cr + dr * mu[0:1]
    k = ck + dk * mu[1:2]
    v = cv + dv * mu[2:3]
    zw = cz + dz * mu[3:4]
    za = cz + dz * mu[4:5]
    zg = cz + dz * mu[5:6]
    w_raw = w0_ref[...] + _bdot(jnp.tanh(_bdot(zw, w1_ref[...])), w2_ref[...])
    sp = jnp.maximum(-w_raw, 0.0) + _log1p_exp_neg_abs(w_raw)
    dec = jnp.exp(-jnp.exp(-sp - 0.5))
    a = _sigmoid(a0_ref[...] + _bdot(_bdot(za, a1_ref[...]), a2_ref[...]))
    og_ref[...] = _bdot(_sigmoid(_bdot(zg, g1_ref[...])), g2_ref[...])
    kk = k * kkp_ref[...]
    ss = _seg_dot(kk * kk, ones_ref)
    kk = kk * lax.rsqrt(jnp.maximum(ss, 1e-24))
    km = k * (1.0 + (a - 1.0) * kap_ref[...])
    obonus_ref[...] = _seg_dot(r * km * rkp_ref[...], ones_ref) * v

    if not grouped:
        for ref, ops in ((kr_ref, (ret_q, ret_k)), (kg_ref, (gla_q, gk_ref[...], gla_al)),
                         (kw_ref, (r, km, dec, kk, kk * a)), (va_ref, (rv_ref[...], gv_ref[...], v))):
            cw = ops[0].shape[1]
            for j, o in enumerate(ops):
                ref[j * cw:(j + 1) * cw, :] = o.T
        return

    n_grp = LANES // (HEADS * b_n)

    def emit_tiles(ref, row0, ops, n_rows):
        wt = _heads_to_lanes(ops, b_n)
        for lt in range(wt.shape[1] // LANES):
            for g, zt in enumerate(_replicate_groups(wt[:n_rows, lt * LANES:(lt + 1) * LANES], n_grp)):
                q = lt * n_grp + g
                ref[row0:row0 + n_rows, q * LANES:(q + 1) * LANES] = zt

    emit_tiles(kr_ref, 0, (ret_q, ret_k), 2 * HEAD_D)
    emit_tiles(kg_ref, 0, (gla_q, gk_ref[...], gla_al), 3 * GLA_DK)
    emit_tiles(kw_ref, 0, (r, km), 2 * HEAD_D)
    emit_tiles(kw_ref, 2 * HEAD_D, (dec, kk), 2 * HEAD_D)
    emit_tiles(kw_ref, 4 * HEAD_D, (kk * a,), HEAD_D)
    va_ref[0:2 * HEAD_D, :] = _heads_to_lanes((rv_ref[...], gv_ref[...]), b_n)
    va_ref[2 * HEAD_D:3 * HEAD_D, :] = _heads_to_lanes((v,), b_n)[:HEAD_D]


def _scan_kernel(kr_ref, kg_ref, kw_ref, va_ref, dec_ref, s0r_ref, s0g_ref, s0w_ref,
                 o_ref, sor_ref, sog_ref, sow_ref,
                 sr, sg, sw, vs, os_, *, vh_n, grouped, tc, b_n):
    ti = pl.program_id(1)

    @pl.when(ti == 0)
    def _():
        sr[...] = s0r_ref[...]
        sg[...] = s0g_ref[...]
        sw[...] = s0w_ref[...]

    n_grp = LANES // (HEADS * b_n) if grouped else 1
    grp_w = LANES // n_grp
    n_v = 3

    def k_tile(ref, n_ops, j, t):
        lanes = pl.ds(pl.multiple_of(t * LANES, LANES), LANES)
        if grouped:
            k_n = ref.shape[0] // n_ops
            return ref[j * k_n:(j + 1) * k_n, lanes]
        return ref[j, 0, :, lanes]

    if grouped:
        grp = lax.broadcasted_iota(jnp.int32, (vh_n, LANES), 1) // grp_w
        for lt in range(tc // n_grp):
            lanes = slice(lt * LANES, (lt + 1) * LANES)
            for j in range(n_v):
                ws = [va_ref[j * HEAD_D + vl * vh_n:j * HEAD_D + (vl + 1) * vh_n, lanes] for vl in range(n_grp)]
                for g in range(n_grp):
                    tile = None
                    for vl in range(n_grp):
                        r = pltpu.roll(ws[vl], ((vl - g) * grp_w) % LANES, 1)
                        tile = r if tile is None else jnp.where(grp == vl, r, tile)
                    vs[lt * n_grp + g, j] = tile
    else:
        def copy_in(t, c):
            lanes = pl.ds(pl.multiple_of(t * LANES, LANES), LANES)
            for j in range(n_v):
                vs[t, j] = va_ref[j, 0, :, lanes]
            return c

        lax.fori_loop(0, tc, copy_in, 0)

    dec_r = dec_ref[...]

    def out(t, j, vh, s, q):
        os_[t, j, pl.ds(vh, 1), :] = jnp.sum(s * q, axis=0, keepdims=True)

    def ret_step(t, c):
        q, k = k_tile(kr_ref, 2, 0, t), k_tile(kr_ref, 2, 1, t)
        for vh in range(vh_n):
            s = sr[vh] + vs[t, 0, pl.ds(vh, 1), :] * k
            sr[vh] = s
            out(t, 0, vh, s, q)
        return c

    def gla_step(t, c):
        q, k, al = (k_tile(kg_ref, 3, j, t) for j in range(3))
        for vh in range(vh_n):
            s = sg[vh] * al + vs[t, 1, pl.ds(vh, 1), :] * k
            sg[vh] = s
            out(t, 1, vh, s, q)
        return c

    def rwkv_step(t, c):
        q, k, dec, kk, beta = (k_tile(kw_ref, 5, j, t) for j in range(5))
        for vh in range(vh_n):
            s = sw[vh]
            sk = jnp.sum(s * kk, axis=0, keepdims=True)
            s = s * dec - sk * beta + vs[t, 2, pl.ds(vh, 1), :] * k
            sw[vh] = s
            out(t, 2, vh, s, q)
        return c

    unroll = 2 if grouped else 1
    lax.fori_loop(0, tc, ret_step, 0, unroll=unroll)
    sr[...] = sr[...] * dec_r
    lax.fori_loop(0, tc, gla_step, 0, unroll=unroll)
    lax.fori_loop(0, tc, rwkv_step, 0, unroll=unroll)

    if grouped:
        grp = lax.broadcasted_iota(jnp.int32, (vh_n, LANES), 1) // grp_w
        for lt in range(tc // n_grp):
            lanes = slice(lt * LANES, (lt + 1) * LANES)
            for j in range(n_v):
                og = [os_[lt * n_grp + g, j] for g in range(n_grp)]
                for vl in range(n_grp):
                    wv = None
                    for g in range(n_grp):
                        r = pltpu.roll(og[g], ((g - vl) * grp_w) % LANES, 1)
                        wv = r if wv is None else jnp.where(grp == g, r, wv)
                    o_ref[j * HEAD_D + vl * vh_n:j * HEAD_D + (vl + 1) * vh_n, lanes] = wv
    else:
        def copy_out(t, c):
            lanes = pl.ds(pl.multiple_of(t * LANES, LANES), LANES)
            for j in range(n_v):
                o_ref[j, 0, :, lanes] = os_[t, j]
            return c

        lax.fori_loop(0, tc, copy_out, 0)

    @pl.when(ti == pl.num_programs(1) - 1)
    def _():
        sor_ref[...] = sr[...]
        sog_ref[...] = sg[...]
        sow_ref[...] = sw[...]


def _scan(kr, kg, kw, va, b_n, t_n, dec, s0r, s0g, s0w):
    grouped = b_n * HEADS < LANES
    vh_n = s0r.shape[0]
    if grouped:
        tc = min(t_n, SCAN_CHUNK)
        n_grp = LANES // (HEADS * b_n)
        assert tc % n_grp == 0 and t_n % tc == 0
        n_l, n_t = 1, t_n // tc
        k_specs = [pl.BlockSpec((a.shape[0], tc * LANES), lambda l, t: (0, t)) for a in (kr, kg, kw)]
        v_spec = pl.BlockSpec((va.shape[0], tc * HEADS * b_n), lambda l, t: (0, t))
        o_spec = v_spec
    else:
        assert b_n == LANES
        tc, n_l, n_t = t_n, HEADS, 1
        k_specs = [pl.BlockSpec((a.shape[0], 1, a.shape[2], tc * LANES), lambda l, t: (0, l, 0, 0))
                   for a in (kr, kg, kw)]
        v_spec = pl.BlockSpec((3, 1, HEAD_D, tc * LANES), lambda l, t: (0, l, 0, 0))
        o_spec = v_spec

    def s_spec(a):
        return pl.BlockSpec(a.shape[:2] + (LANES,), lambda l, t: (0, 0, l))

    return pl.pallas_call(
        functools.partial(_scan_kernel, vh_n=vh_n, grouped=grouped, tc=tc, b_n=b_n),
        grid=(n_l, n_t),
        in_specs=k_specs + [v_spec, pl.BlockSpec((1, LANES), lambda l, t: (0, l)),
                            s_spec(s0r), s_spec(s0g), s_spec(s0w)],
        out_specs=[o_spec, s_spec(s0r), s_spec(s0g), s_spec(s0w)],
        out_shape=[jax.ShapeDtypeStruct(va.shape, F32)] + [jax.ShapeDtypeStruct(s.shape, F32)
                                                           for s in (s0r, s0g, s0w)],
        scratch_shapes=[pltpu.VMEM(s.shape[:2] + (LANES,), F32) for s in (s0r, s0g, s0w)]
                       + [pltpu.VMEM((tc, 3, vh_n, LANES), F32), pltpu.VMEM((tc, 3, vh_n, LANES), F32)],
        compiler_params=_cparams("parallel", "arbitrary"),
        name="scan_ret_gla_rwkv",
    )(kr, kg, kw, va, dec, s0r, s0g, s0w)


def _s5_kernel(u_ref, bb_ref, cc_ref, ar_ref, ai_ref, h0r_ref, h0i_ref,
               y_ref, hr_out, hi_out, hr_scr, hi_scr, xs_scr, hs_scr, *, b_n, tc):
    ti = pl.program_id(0)

    @pl.when(ti == 0)
    def _():
        hr_scr[...] = h0r_ref[...]
        hi_scr[...] = h0i_ref[...]

    xs_scr[...] = _bdot(u_ref[...], bb_ref[...])
    ar = ar_ref[...]
    ai = ai_ref[...]

    def step(t, carry):
        hr, hi = carry
        row = pl.multiple_of(t * b_n, SUBLANES)
        x = xs_scr[pl.ds(row, b_n), :]
        nr = ar * hr - ai * hi + x[:, :S5_CH]
        ni = ar * hi + ai * hr + x[:, S5_CH:]
        hs_scr[pl.ds(row, b_n), :S5_CH] = nr
        hs_scr[pl.ds(row, b_n), S5_CH:] = ni
        return nr, ni

    hr, hi = lax.fori_loop(0, tc, step, (hr_scr[...], hi_scr[...]))
    hr_scr[...] = hr
    hi_scr[...] = hi
    y_ref[...] = _bdot(hs_scr[...], cc_ref[...])

    @pl.when(ti == pl.num_programs(0) - 1)
    def _():
        hr_out[...] = hr
        hi_out[...] = hi


def _s5_scan(cols, row_off, bb, cc, ar, ai, h0r, h0i, b_n, t_n):
    tc = min(t_n, max(1, 1024 // b_n))
    rows = tc * b_n
    assert t_n % tc == 0 and row_off % rows == 0
    blk0 = row_off // rows
    return pl.pallas_call(
        functools.partial(_s5_kernel, b_n=b_n, tc=tc),
        grid=(t_n // tc,),
        in_specs=[pl.BlockSpec((rows, BRANCH_W), lambda t: (blk0 + t, COL_S5 // BRANCH_W)),
                  _const_spec(bb.shape), _const_spec(cc.shape),
                  _const_spec(ar.shape), _const_spec(ai.shape),
                  _const_spec(h0r.shape), _const_spec(h0i.shape)],
        out_specs=[pl.BlockSpec((rows, BRANCH_W), lambda t: (t, 0)),
                   _const_spec(h0r.shape), _const_spec(h0i.shape)],
        out_shape=[jax.ShapeDtypeStruct((t_n * b_n, BRANCH_W), F32),
                   jax.ShapeDtypeStruct(h0r.shape, F32),
                   jax.ShapeDtypeStruct(h0i.shape, F32)],
        scratch_shapes=[pltpu.VMEM((b_n, S5_CH), F32), pltpu.VMEM((b_n, S5_CH), F32),
                        pltpu.VMEM((rows, 2 * S5_CH), F32), pltpu.VMEM((rows, 2 * S5_CH), F32)],
        compiler_params=_cparams("arbitrary"),
        name="scan_s5",
    )(cols, bb, cc, ar, ai, h0r, h0i)


def _post_kernel(x_ref, mixt_ref, rg_ref, gg_ref, sy_ref, su_ref, wbon_ref, wg_ref,
                 gate0_ref, gate1_ref, gate2_ref, gate3_ref,
                 avg_ref, rgn_g, rgn_b, ggn_g, s5d_ref, wglu_ref, wgn_g, wgn_b,
                 wbr_ref, wo_ref, ln_g, ln_b, *rest, alpha, b_n):
    o_ref = rest[-1] if len(rest) == 1 else rest[4]

    def seg_mean(v):
        return _seg_dot(v, avg_ref)

    tm = x_ref.shape[0]
    if b_n * HEADS < LANES:
        yt = mixt_ref[...].T.reshape(tm // b_n, HEADS, b_n, 3 * HEAD_D)
        per_head = [yt[:, h].reshape(tm, 3 * HEAD_D) for h in range(HEADS)]
        mix = jnp.concatenate([ph[:, j * HEAD_D:(j + 1) * HEAD_D] for j in range(3) for ph in per_head], axis=1)
    else:
        mix = mixt_ref[...].T
    ro = mix[:, 0:BRANCH_W]
    mu = seg_mean(ro)
    rc = ro - mu
    var = seg_mean(rc * rc)
    b0 = (rc * lax.rsqrt(var + LN_EPS) * rgn_g[...] + rgn_b[...]) * _silu(rg_ref[...])
    go = mix[:, BRANCH_W:2 * BRANCH_W]
    ms = seg_mean(go * go)
    b1 = go * lax.rsqrt(ms + LN_EPS) * ggn_g[...] * _silu(gg_ref[...])
    y = jax.nn.gelu(sy_ref[...] + s5d_ref[...] * su_ref[...])
    b2 = y * _sigmoid(_bdot(y, wglu_ref[...]))
    wy = mix[:, 2 * BRANCH_W:3 * BRANCH_W]
    mu = seg_mean(wy)
    wc = wy - mu
    var = seg_mean(wc * wc)
    b3 = (wc * lax.rsqrt(var + RWKV_GN_EPS) * wgn_g[...] + wgn_b[...] + wbon_ref[...]) * wg_ref[...]

    m = None
    gates = (gate0_ref, gate1_ref, gate2_ref, gate3_ref)
    for i, br in enumerate((b0, b1, b2, b3)):
        term = _bdot(br, wbr_ref[i]) * _sigmoid(gates[i][...])
        m = term if m is None else m + term
    h = _bdot(m, wo_ref[...])
    x1 = _layer_norm(alpha * x_ref[...] + h, ln_g[...], ln_b[...])
    o_ref[...] = x1
    if len(rest) > 1:
        _route_top2(x1, rest[0], rest[1], rest[5], rest[6])


def _embed_ln2(x, f, p_ref, wpe_ref, wpg_ref, ln_g, ln_b, alpha):
    e = _bdot(p_ref[...], wpe_ref[...]) * _sigmoid(_bdot(x, wpg_ref[...]))
    return _layer_norm(alpha * x + f + e, ln_g[...], ln_b[...])


def _ffn_kernel(x_ref, p_ref, w1_ref, w3_ref, w2_ref, wpe_ref, wpg_ref, ln_g, ln_b, o_ref, *, alpha):
    x = x_ref[...]
    xb = x.astype(BF16)
    h = _silu(jnp.dot(xb, w1_ref[...], preferred_element_type=F32)) * jnp.dot(
        xb, w3_ref[...], preferred_element_type=F32)
    f = _bdot(h, w2_ref[...])
    o_ref[...] = _embed_ln2(x, f, p_ref, wpe_ref, wpg_ref, ln_g, ln_b, alpha)


def _route_top2(x, wh_ref, wl_ref, idx_ref, wgt_ref):
    xh, xl = _split_bf16(x)
    wh = wh_ref[...]
    logits = (jnp.dot(xh, wh, preferred_element_type=F32)
              + jnp.dot(xl, wh, preferred_element_type=F32)
              + jnp.dot(xh, wl_ref[...], preferred_element_type=F32))
    col = lax.broadcasted_iota(jnp.int32, logits.shape, 1)
    neg = jnp.float32(-jnp.inf)
    lg = jnp.where(col < N_EXPERTS, logits, neg)
    m1 = jnp.max(lg, axis=1, keepdims=True)
    i1 = jnp.min(jnp.where(lg == m1, col, LANES), axis=1, keepdims=True)
    lg2 = jnp.where(col == i1, neg, lg)
    m2 = jnp.max(lg2, axis=1, keepdims=True)
    i2 = jnp.min(jnp.where(lg2 == m2, col, LANES), axis=1, keepdims=True)
    e2 = jnp.exp(m2 - m1)
    den = 1.0 + e2
    idx_ref[...] = jnp.where(col == 0, i1, jnp.where(col == 1, i2, 0))
    wgt_ref[...] = jnp.where(col == 0, 1.0 / den, jnp.where(col == 1, e2 / den, 0.0))


def _row_copies(idx_ref, n_rows, make_copy, count=None):
    def each(action):
        def body(r, c):
            action(make_copy(r, idx_ref[0, 0, r]))
            return c

        if count is None:
            lax.fori_loop(0, n_rows, body, 0, unroll=8)
            return

        @pl.when(count == n_rows)
        def _():
            lax.fori_loop(0, n_rows, body, 0, unroll=8)

        @pl.when(count < n_rows)
        def _():
            lax.fori_loop(0, count, body, 0)

    return (lambda: each(lambda cp: cp.start())), (lambda: each(lambda cp: cp.wait()))


def _moe_block_kernel(be_ref, nb_ref, bv_ref, tok_ref, tok_next_ref, dst_ref, dst_m1_ref, dst_m2_ref,
                      x_hbm, w1_ref, w3_ref, w2_ref, out_hbm, xbuf, ybuf, sem_in, sem_out):
    j = pl.program_id(0)
    last = pl.num_programs(0) - 1
    nb = nb_ref[0]
    slot = j % 2

    def gather_copy(idx_ref, half, r):
        return pltpu.make_async_copy(x_hbm.at[pl.ds(idx_ref[0, 0, r], 1)], xbuf.at[half, pl.ds(r, 1)],
                                     sem_in.at[half])

    def gather(idx_ref, half):
        return _row_copies(idx_ref, MOE_BLOCK, lambda r, row: pltpu.make_async_copy(
            x_hbm.at[pl.ds(row, 1)], xbuf.at[half, pl.ds(r, 1)], sem_in.at[half]))

    def scatter(idx_ref, half, block):
        return _row_copies(idx_ref, MOE_BLOCK, lambda r, row: pltpu.make_async_copy(
            ybuf.at[half, pl.ds(r, 1)], out_hbm.at[pl.ds(row, 1)], sem_out.at[half]),
            count=bv_ref[jnp.maximum(block, 0)])

    @pl.when(jnp.logical_and(j == 0, nb > 0))
    def _():
        gather(tok_ref, 0)[0]()

    def step(cur):
        gather(tok_ref, cur)[1]()

        @pl.when(j >= 2)
        def _():
            scatter(dst_m2_ref, cur, j - 2)[1]()

        for r in range(MOE_BLOCK):
            gather_copy(tok_next_ref, 1 - cur, r).start()
        xb = xbuf[cur].astype(BF16)
        h = _silu(jnp.dot(xb, w1_ref[0], preferred_element_type=F32)) * jnp.dot(
            xb, w3_ref[0], preferred_element_type=F32)
        ybuf[cur] = _bdot(h, w2_ref[0])
        scatter(dst_ref, cur, j)[0]()

    for par in range(2):
        pl.when(jnp.logical_and(j < nb, slot == par))(functools.partial(step, par))

    @pl.when(jnp.logical_and(j == nb, nb > 0))
    def _():
        gather(tok_ref, slot)[1]()

    @pl.when(jnp.logical_and(j >= nb, jnp.logical_and(j >= 2, j - 2 < nb)))
    def _():
        scatter(dst_m2_ref, slot, j - 2)[1]()

    @pl.when(jnp.logical_and(j == last, jnp.logical_and(j >= 1, j - 1 < nb)))
    def _():
        scatter(dst_m1_ref, 1 - slot, j - 1)[1]()

    @pl.when(jnp.logical_and(j == last, j < nb))
    def _():
        scatter(dst_ref, slot, j)[1]()


def _moe_blocks(x, block_e, nb_used, block_valid, slot_tok, slot_dst, n_out_rows, w1, w3, w2):
    n_blocks = block_e.shape[0]
    d = x.shape[1]
    dff = w1.shape[2]
    tok = slot_tok.reshape(n_blocks, 1, MOE_BLOCK)
    dst = slot_dst.reshape(n_blocks, 1, MOE_BLOCK)

    def idx_spec(shift):
        return pl.BlockSpec((1, 1, MOE_BLOCK),
                            lambda j, be, nb, bv: (jnp.clip(j + shift, 0, n_blocks - 1), 0, 0),
                            memory_space=pltpu.SMEM)

    grid_spec = pltpu.PrefetchScalarGridSpec(
        num_scalar_prefetch=3,
        grid=(n_blocks + 1,),
        in_specs=[
            idx_spec(0), idx_spec(1), idx_spec(0), idx_spec(-1), idx_spec(-2),
            pl.BlockSpec(memory_space=pl.ANY),
            pl.BlockSpec((1, d, dff), lambda j, be, nb, bv: (be[jnp.minimum(j, n_blocks - 1)], 0, 0),
                         pipeline_mode=pl.Buffered(1)),
            pl.BlockSpec((1, d, dff), lambda j, be, nb, bv: (be[jnp.minimum(j, n_blocks - 1)], 0, 0),
                         pipeline_mode=pl.Buffered(1)),
            pl.BlockSpec((1, dff, d), lambda j, be, nb, bv: (be[jnp.minimum(j, n_blocks - 1)], 0, 0),
                         pipeline_mode=pl.Buffered(1)),
        ],
        out_specs=pl.BlockSpec(memory_space=pl.ANY),
        scratch_shapes=[pltpu.VMEM((2, MOE_BLOCK, d), F32), pltpu.VMEM((2, MOE_BLOCK, d), F32),
                        pltpu.SemaphoreType.DMA((2,)), pltpu.SemaphoreType.DMA((2,))],
    )
    return pl.pallas_call(
        _moe_block_kernel,
        grid_spec=grid_spec,
        out_shape=jax.ShapeDtypeStruct((n_out_rows, d), F32),
        compiler_params=_cparams("arbitrary"),
        name="moe_blocks",
    )(block_e, nb_used, block_valid, tok, tok, dst, dst, dst, x, w1, w3, w2)


def _moe_combine_kernel(y0_ref, y1_ref, x_ref, p_ref, wgt_ref, wpe_ref, wpg_ref, ln_g, ln_b, o_ref, *, alpha):
    wgt = wgt_ref[...]
    f = y0_ref[...] * wgt[:, 0:1] + y1_ref[...] * wgt[:, 1:2]
    o_ref[...] = _embed_ln2(x_ref[...], f, p_ref, wpe_ref, wpg_ref, ln_g, ln_b, alpha)


def _lane_groups(b_n):
    bh = b_n * HEADS
    return LANES // bh if bh < LANES else 1


def _state_to_lanes(s, vl_n, value_last):
    s = s.transpose(3, 2, 1, 0) if value_last else s.transpose(2, 3, 1, 0)
    v_n, k_n, h_n, b_n = s.shape
    s = s.reshape(vl_n, v_n // vl_n, k_n, h_n, b_n).transpose(1, 2, 0, 3, 4)
    return s.reshape(v_n // vl_n, k_n, vl_n * h_n * b_n)


def _state_from_lanes(s, b_n, vl_n, value_last):
    vh_n, k_n, _ = s.shape
    s = s.reshape(vh_n, k_n, vl_n, HEADS, b_n).transpose(4, 3, 1, 2, 0).reshape(b_n, HEADS, k_n, vl_n * vh_n)
    return s if value_last else s.transpose(0, 1, 3, 2)


def _rotary_tables(pos, row_repeat, gamma, tc):
    half = HEAD_D // 2
    freq = ROPE_BASE ** (-jnp.arange(half, dtype=F32) / half)
    ang = pos.astype(F32)[:, None] * freq[None, :]
    cos, sin = jnp.cos(ang), jnp.sin(ang)
    cos_h = jnp.tile(jnp.concatenate([cos, cos], axis=-1), (1, HEADS))
    sin_h = jnp.tile(jnp.concatenate([-sin, sin], axis=-1), (1, HEADS))
    tau1 = (jnp.arange(pos.shape[0], dtype=jnp.int32) % tc + 1).astype(F32)
    f_q = jnp.repeat(gamma[None, :] ** tau1[:, None], HEAD_D, axis=1)
    f_k = (HEAD_D ** -0.5) / f_q
    tabs = (cos_h * f_q, sin_h * f_q, cos_h * f_k, sin_h * f_k)
    return tuple(jnp.repeat(t, row_repeat, axis=0) for t in tabs) if row_repeat > 1 else tabs


def _block_diag_const(block, n_blocks):
    return jnp.kron(jnp.eye(n_blocks, dtype=F32), jnp.full((block, block), 1.0, F32))


def _s5_params(log_dt, a_re, a_im, b_re, b_im, c_re, c_im):
    dt = jnp.exp(log_dt)[:, None]
    mag = jnp.exp(dt * a_re)
    ang = dt * a_im
    abar_re, abar_im = mag * jnp.cos(ang), mag * jnp.sin(ang)
    den = a_re * a_re + a_im * a_im
    n_re = abar_re - 1.0
    f_re = (n_re * a_re + abar_im * a_im) / den
    f_im = (abar_im * a_re - n_re * a_im) / den
    bb_re = f_re[..., None] * b_re - f_im[..., None] * b_im
    bb_im = f_re[..., None] * b_im + f_im[..., None] * b_re
    eye = jnp.eye(S5_GROUPS, dtype=F32)

    def in_map(bb):
        return jnp.einsum("gpc,gh->gchp", bb, eye).reshape(BRANCH_W, S5_CH)

    def out_map(cm):
        return jnp.einsum("gcp,gh->gphc", cm, eye).reshape(S5_CH, BRANCH_W)

    bb = jnp.concatenate([in_map(bb_re), in_map(bb_im)], axis=1).astype(BF16)
    cc = jnp.concatenate([out_map(c_re), -out_map(c_im)], axis=0).astype(BF16)
    return bb, cc, abar_re.reshape(1, S5_CH), abar_im.reshape(1, S5_CH)


def _reorder_w_in(w_in):
    d_model = w_in.shape[0]
    main = jnp.concatenate([w_in[:, :1792], w_in[:, 1808:]], axis=1)
    tail = jnp.concatenate([w_in[:, 1792:1808],
                            jnp.zeros((d_model, D_IN_PAD - COL_GLA_R - GLA_GATE_RANK), w_in.dtype)], axis=1)
    return jnp.concatenate([main, tail], axis=1).astype(BF16)


def _pad_cols(w, n):
    return jnp.pad(w, ((0, 0), (0, n - w.shape[1])))


def _pad_rows(w, n):
    return jnp.pad(w, ((0, n - w.shape[0]), (0, 0)))


def _moe_route(idx, n):
    nk = n * TOP_K
    flat_e = idx.reshape(nk)
    onehot = (flat_e[:, None] == jnp.arange(N_EXPERTS, dtype=jnp.int32)[None, :]).astype(jnp.int32)
    incl = jnp.cumsum(onehot, axis=0)
    counts = incl[-1]
    rank = jnp.sum((incl - onehot) * onehot, axis=1)
    padded = (counts + MOE_BLOCK - 1) // MOE_BLOCK * MOE_BLOCK
    pad_end = jnp.cumsum(padded)
    slot = (pad_end - padded)[flat_e] + rank
    n_blocks = -(-(nk + N_EXPERTS * (MOE_BLOCK - 1)) // MOE_BLOCK)
    cap = n_blocks * MOE_BLOCK
    pair = jnp.full((cap,), -1, jnp.int32).at[slot].set(jnp.arange(nk, dtype=jnp.int32))
    real = pair >= 0
    slot_tok = jnp.where(real, pair // TOP_K, 0)
    slot_dst = jnp.where(real, (pair % TOP_K) * n + pair // TOP_K, 0)
    block_start = jnp.arange(n_blocks, dtype=jnp.int32) * MOE_BLOCK
    block_e = jnp.minimum(jnp.sum((pad_end[None, :] <= block_start[:, None]).astype(jnp.int32), axis=1),
                          N_EXPERTS - 1).astype(jnp.int32)
    nb_used = (pad_end[-1] // MOE_BLOCK).astype(jnp.int32).reshape(1)
    block_valid = jnp.sum(real.reshape(n_blocks, MOE_BLOCK).astype(jnp.int32), axis=1)
    return slot_tok, slot_dst, block_e, nb_used, block_valid, nk


def kernel(x_prompt, x_sample, state_ret, state_gla, state_s5_re, state_s5_im, state_rwkv, state_shift,
           p_prompt, p_sample, w_in, ret_gn_g, ret_gn_b, gla_wg2, gla_bg, gla_gn,
           s5_log_dt, s5_a_re, s5_a_im, s5_b_re, s5_b_im, s5_c_re, s5_c_im, s5_d, s5_w_glu,
           rwkv_mu, rwkv_w0, rwkv_w1, rwkv_w2, rwkv_a0, rwkv_a1, rwkv_a2, rwkv_g1, rwkv_g2,
           rwkv_kk, rwkv_ka, rwkv_rk, rwkv_gn_g, rwkv_gn_b, w_branch, w_o,
           ln1_g, ln1_b, ln2_g, ln2_b, w_pe, w_pg, ffn_w1, ffn_w3, ffn_w2,
           moe_router, moe_w1, moe_w3, moe_w2):
    depth = w_in.shape[0]
    bp, tp, d_model = x_prompt.shape
    bs, ts, _ = x_sample.shape
    n_p, n_s = bp * tp, bs * ts
    n = n_p + n_s
    tm = ROW_TILE
    assert n_p % tm == 0 and n_s % tm == 0 and tm % bp == 0 and tm % bs == 0 and n_p % bs == 0
    alpha = (2 * depth) ** 0.25
    groups = [(0, bp, tp), (n_p, bs, ts)]
    w4 = 4 * BRANCH_W

    def time_major(a_p, a_s):
        return jnp.concatenate([a_p.transpose(1, 0, 2).reshape(n_p, -1),
                                a_s.transpose(1, 0, 2).reshape(n_s, -1)], axis=0)

    x = time_major(x_prompt, x_sample)
    ones_bd = _block_diag_const(HEAD_D, HEADS).astype(BF16)
    avg_bd = (_block_diag_const(HEAD_D, HEADS) / HEAD_D).astype(BF16)
    gamma = 1.0 - jnp.exp2(-5.0 - jnp.arange(HEADS, dtype=F32))
    tc_p, tc_s = min(tp, SCAN_CHUNK), ts
    rot_p = _rotary_tables(jnp.arange(tp, dtype=jnp.int32), 1 if tm // bp >= SUBLANES else bp, gamma, tc_p)
    rot_s = _rotary_tables(PAST_LEN + jnp.arange(ts, dtype=jnp.int32), 1 if tm // bs >= SUBLANES else bs,
                           gamma, tc_s)
    row = lambda v: v.reshape(1, -1)

    new = [[] for _ in range(6)]
    for i in range(depth):
        cols = _matmul(x, _reorder_w_in(w_in[i]), tm=512, tn=D_IN_PAD // 3, name="in_proj")

        prep_consts = [
            _pad_rows(_pad_cols(gla_wg2[i], LANES), LANES).astype(BF16), row(gla_bg[i]),
            rwkv_mu[i], row(rwkv_w0[i]),
            _pad_cols(rwkv_w1[i], LANES).astype(BF16), _pad_rows(rwkv_w2[i], LANES).astype(BF16),
            row(rwkv_a0[i]),
            _pad_cols(rwkv_a1[i], LANES).astype(BF16), _pad_rows(rwkv_a2[i], LANES).astype(BF16),
            _pad_cols(rwkv_g1[i], LANES).astype(BF16), _pad_rows(rwkv_g2[i], LANES).astype(BF16),
            row(rwkv_kk[i]), row(rwkv_ka[i]), row(rwkv_rk[i]), ones_bd,
        ]
        post_consts = [avg_bd, row(ret_gn_g[i]), row(ret_gn_b[i]), row(gla_gn[i]), row(s5_d[i]),
                       s5_w_glu[i].astype(BF16), row(rwkv_gn_g[i]), row(rwkv_gn_b[i]),
                       w_branch[i].astype(BF16), w_o[i].astype(BF16), row(ln1_g[i]), row(ln1_b[i])]
        s5p = _s5_params(s5_log_dt[i], s5_a_re[i], s5_a_im[i], s5_b_re[i], s5_b_im[i], s5_c_re[i], s5_c_im[i])
        rwkv_blk = COL_RWKV // w4
        x1, layer_new = x, []
        routed = i % 2 == 1
        if routed:
            router_hi, router_lo = _split_bf16(_pad_cols(moe_router[i // 2], LANES))
            idx, wgt = jnp.zeros((n, LANES), jnp.int32), jnp.zeros((n, LANES), F32)
        for gi, (off, b_n, t_n) in enumerate(groups):
            n_g = b_n * t_n
            t0 = off // tm
            if gi == 0:
                st = [jnp.zeros((b_n,) + s.shape[2:], s.dtype)
                      for s in (state_ret, state_gla, state_s5_re, state_s5_im, state_rwkv)]
                shift0, rot_g, tc_g = jnp.zeros((b_n, w4), F32), rot_p, tc_p
            else:
                st = [state_ret[i], state_gla[i], state_s5_re[i], state_s5_im[i], state_rwkv[i]]
                shift0, rot_g, tc_g = state_shift[i], rot_s, tc_s
            vl_n = _lane_groups(b_n)
            grouped = vl_n > 1

            def cspec(width, cb, t0=t0):
                return pl.BlockSpec((tm, width), lambda r: (r + t0, cb))

            if grouped:
                t_shapes = [(2 * HEAD_D, n_g * vl_n * HEADS), (3 * GLA_DK, n_g * vl_n * HEADS),
                            (5 * HEAD_D, n_g * vl_n * HEADS), (3 * HEAD_D, n_g * HEADS)]
            else:
                t_shapes = [(2 * BRANCH_W, n_g), (3 * GLA_QK, n_g), (5 * BRANCH_W, n_g), (3 * BRANCH_W, n_g)]
            consts = [shift0] + prep_consts
            ka_ret, ka_gla, ka_rwkv, va, obonus, og = pl.pallas_call(
                functools.partial(_prep_kernel, grouped=grouped),
                grid=(n_g // tm,),
                in_specs=[cspec(BRANCH_W, 0), cspec(BRANCH_W, 1), cspec(BRANCH_W, 2),
                          ] + [_row_spec(rot_g[0].shape[0] // (n_g // tm), BRANCH_W, 0)] * 4 + [
                          cspec(GLA_QK, COL_GLA_Q // GLA_QK), cspec(GLA_QK, COL_GLA_K // GLA_QK),
                          cspec(BRANCH_W, COL_GLA_V // BRANCH_W), cspec(LANES, COL_GLA_R // LANES),
                          cspec(w4, rwkv_blk),
                          pl.BlockSpec((b_n, w4), lambda r, t0=t0, b_n=b_n:
                                       (jnp.maximum((r + t0) * (tm // b_n) - 1, 0), rwkv_blk))]
                         + [_const_spec(a.shape) for a in consts],
                out_specs=[pl.BlockSpec((sh[0], sh[1] // (n_g // tm)), lambda r: (0, r)) for sh in t_shapes]
                          + [_row_spec(tm, BRANCH_W, 0), _row_spec(tm, BRANCH_W, 0)],
                out_shape=[jax.ShapeDtypeStruct(sh, F32) for sh in t_shapes]
                          + [jax.ShapeDtypeStruct((n_g, BRANCH_W), F32)] * 2,
                compiler_params=_cparams("parallel"),
                name="mixer_prep",
            )(cols, cols, cols, *rot_g, cols, cols, cols, cols, cols, cols, *consts)

            dec = jnp.tile(jnp.repeat(gamma ** tc_g, b_n), vl_n).reshape(1, vl_n * HEADS * b_n)
            if not grouped:
                ka_ret = ka_ret.reshape(2, HEADS, HEAD_D, n_g)
                ka_gla = ka_gla.reshape(3, HEADS, GLA_DK, n_g)
                ka_rwkv = ka_rwkv.reshape(5, HEADS, HEAD_D, n_g)
                va = va.reshape(3, HEADS, HEAD_D, n_g)
            o, s_ret, s_gla, s_rwkv = _scan(
                ka_ret, ka_gla, ka_rwkv, va, b_n, t_n, dec,
                _state_to_lanes(st[0], vl_n, True), _state_to_lanes(st[1], vl_n, True),
                _state_to_lanes(st[4], vl_n, False))
            mixt = o if grouped else o.reshape(3 * BRANCH_W, n_g)
            y, hr, hi = _s5_scan(cols, off, *s5p, st[2].reshape(b_n, S5_CH), st[3].reshape(b_n, S5_CH), b_n, t_n)
            shift_new = cols[off + (t_n - 1) * b_n:off + t_n * b_n, COL_RWKV:COL_RWKV + w4]
            layer_new.append((_state_from_lanes(s_ret, b_n, vl_n, True), _state_from_lanes(s_gla, b_n, vl_n, True),
                              hr.reshape(b_n, S5_GROUPS, S5_STATE), hi.reshape(b_n, S5_GROUPS, S5_STATE),
                              _state_from_lanes(s_rwkv, b_n, vl_n, False), shift_new))

            post_in = [x1, mixt, cols, cols, y, cols, obonus, og, cols, cols, cols, cols, *post_consts]
            post_specs = ([cspec(d_model, 0),
                           pl.BlockSpec((mixt.shape[0], mixt.shape[1] // (n_g // tm)), lambda r: (0, r)),
                           cspec(BRANCH_W, 3), cspec(BRANCH_W, COL_GLA_G // BRANCH_W),
                           _row_spec(tm, BRANCH_W, 0), cspec(BRANCH_W, COL_S5 // BRANCH_W),
                           _row_spec(tm, BRANCH_W, 0), _row_spec(tm, BRANCH_W, 0)]
                          + [cspec(d_model, COL_GATE // d_model + q) for q in range(N_BRANCH)]
                          + [_const_spec(a.shape) for a in post_consts])
            out_specs, out_shape, aliases = [cspec(d_model, 0)], [jax.ShapeDtypeStruct((n, d_model), F32)], {0: 0}
            if routed:
                aliases.update({len(post_in) + 2: 1, len(post_in) + 3: 2})
                post_in += [router_hi, router_lo, idx, wgt]
                post_specs += [_const_spec(router_hi.shape), _const_spec(router_lo.shape),
                               cspec(LANES, 0), cspec(LANES, 0)]
                out_specs += [cspec(LANES, 0), cspec(LANES, 0)]
                out_shape += [jax.ShapeDtypeStruct((n, LANES), jnp.int32), jax.ShapeDtypeStruct((n, LANES), F32)]
            res = pl.pallas_call(
                functools.partial(_post_kernel, alpha=alpha, b_n=b_n),
                grid=(n_g // tm,),
                in_specs=post_specs,
                out_specs=out_specs,
                out_shape=out_shape,
                input_output_aliases=aliases,
                compiler_params=_cparams("parallel"),
                name="mixer_post",
            )(*post_in)
            x1 = res[0]
            if routed:
                idx, wgt = res[1], res[2]
        for lst, pair in zip(new, zip(*layer_new)):
            lst.append(pair)

        p = time_major(p_prompt[i], p_sample[i])
        tail_consts = [w_pe[i].astype(BF16), w_pg[i].astype(BF16), row(ln2_g[i]), row(ln2_b[i])]
        j = i // 2
        if i % 2 == 0:
            (x,) = _rowwise(
                functools.partial(_ffn_kernel, alpha=alpha), n,
                [(x1, d_model, 0), (p, p.shape[1], 0)],
                [ffn_w1[j].astype(BF16), ffn_w3[j].astype(BF16), ffn_w2[j].astype(BF16)] + tail_consts,
                [d_model], name="ffn")
        else:
            slot_tok, slot_dst, block_e, nb_used, block_valid, n_out_rows = _moe_route(idx[:, :TOP_K], n)
            yk = _moe_blocks(x1, block_e, nb_used, block_valid, slot_tok, slot_dst, n_out_rows,
                             moe_w1[j].astype(BF16), moe_w3[j].astype(BF16), moe_w2[j].astype(BF16))
            (x,) = _rowwise(
                functools.partial(_moe_combine_kernel, alpha=alpha), n,
                [(yk, d_model, 0), (yk, d_model, 0, n // tm), (x1, d_model, 0), (p, p.shape[1], 0), (wgt, LANES, 0)],
                tail_consts, [d_model], name="moe_combine")

    y_prompt = x[:n_p].reshape(tp, bp, d_model).transpose(1, 0, 2)
    y_sample = x[n_p:].reshape(ts, bs, d_model).transpose(1, 0, 2)
    outs = [y_prompt, y_sample]
    for lst in new:
        outs.append(jnp.stack([pair[0] for pair in lst], 0))
        outs.append(jnp.stack([pair[1] for pair in lst], 0))
    return tuple(outs)
```

```python
import functools

import jax
import jax.numpy as jnp
from jax import lax
from jax.experimental import pallas as pl
from jax.experimental.pallas import tpu as pltpu

F32 = jnp.float32
BF16 = jnp.bfloat16

LANES = 128
SUBLANES = 8
VMEM_LIMIT = 56 * 1024 * 1024

N_BRANCH = 4
BRANCH_W = 256
HEADS = 4
HEAD_D = 64
GLA_DK = 32
GLA_QK = HEADS * GLA_DK
GLA_GATE_RANK = 16
GLA_GATE_NORM = 16.0
S5_GROUP = 16
S5_GROUPS = 16
S5_STATE = 64
S5_CH = S5_GROUPS * S5_STATE
ROPE_BASE = 10000.0
RWKV_GN_EPS = 64e-5
LN_EPS = 1e-5
N_EXPERTS = 8
TOP_K = 2
MOE_BLOCK = 256
ROW_TILE = 256
SCAN_CHUNK = 32
PAST_LEN = 16384

COL_RET = 0
COL_GLA_Q = 1024
COL_GLA_K = 1152
COL_GLA_V = 1280
COL_GLA_G = 1536
COL_S5 = 1792
COL_RWKV = 2048
COL_GATE = 3072
COL_GLA_R = 7168
D_IN_PAD = 7296


def _cparams(*sem):
    return pltpu.CompilerParams(dimension_semantics=sem, vmem_limit_bytes=VMEM_LIMIT)


def _split_bf16(x):
    hi = x.astype(BF16)
    lo = (x - hi.astype(F32)).astype(BF16)
    return hi, lo


def _seg_dot(x, m_ref):
    hi, lo = _split_bf16(x)
    m = m_ref[...]
    return (jnp.dot(hi, m, preferred_element_type=F32)
            + jnp.dot(lo, m, preferred_element_type=F32))


def _bdot(x, w):
    return jnp.dot(x.astype(BF16), w, preferred_element_type=F32)


def _sigmoid(x):
    return 1.0 / (1.0 + jnp.exp(-x))


def _silu(x):
    return x * _sigmoid(x)


def _log1p_exp_neg_abs(x):
    return jnp.log1p(jnp.exp(-jnp.abs(x)))


def _layer_norm(x, g, b):
    mu = jnp.mean(x, axis=-1, keepdims=True)
    xc = x - mu
    var = jnp.mean(xc * xc, axis=-1, keepdims=True)
    return xc * lax.rsqrt(var + LN_EPS) * g + b


def _row_spec(tm, width, col_block, row_block0=0):
    return pl.BlockSpec((tm, width), lambda i, cb=col_block, r0=row_block0: (i + r0, cb))


def _const_spec(shape):
    nd = len(shape)
    return pl.BlockSpec(shape, lambda i, nd=nd: (0,) * nd)


def _rowwise(body, n_rows, row_in, const_in, out_widths, out_dtypes=None, tm=ROW_TILE, name=None):
    assert n_rows % tm == 0
    out_dtypes = out_dtypes or [F32] * len(out_widths)
    in_specs = [_row_spec(tm, *spec[1:]) for spec in row_in] + [_const_spec(a.shape) for a in const_in]
    out_specs = [_row_spec(tm, w, 0) for w in out_widths]
    out_shape = [jax.ShapeDtypeStruct((n_rows, w), dt) for w, dt in zip(out_widths, out_dtypes)]
    return pl.pallas_call(
        body,
        grid=(n_rows // tm,),
        in_specs=in_specs,
        out_specs=out_specs,
        out_shape=out_shape,
        compiler_params=_cparams("parallel"),
        name=name,
    )(*[spec[0] for spec in row_in], *const_in)


def _matmul_kernel(x_ref, w_ref, o_ref):
    o_ref[...] = _bdot(x_ref[...], w_ref[...]).astype(o_ref.dtype)


def _matmul(x, w, tm=512, tn=None, out_dtype=F32, name=None):
    m, k = x.shape
    n = w.shape[1]
    tn = tn or n
    tm = tm if m % tm == 0 else ROW_TILE
    assert m % tm == 0 and n % tn == 0
    return pl.pallas_call(
        _matmul_kernel,
        grid=(n // tn, m // tm),
        in_specs=[pl.BlockSpec((tm, k), lambda j, i: (i, 0)),
                  pl.BlockSpec((k, tn), lambda j, i: (0, j))],
        out_specs=pl.BlockSpec((tm, tn), lambda j, i: (i, j)),
        out_shape=jax.ShapeDtypeStruct((m, n), out_dtype),
        compiler_params=_cparams("parallel", "parallel"),
        name=name,
    )(x, w)


def _replicate_groups(w, n_grp):
    if n_grp == 1:
        return [w]
    grp = lax.broadcasted_iota(jnp.int32, w.shape, 1) // (LANES // n_grp)
    parts = [w]
    span = n_grp
    while span > 1:
        half = span // 2
        shift = half * (LANES // n_grp)
        low = (grp % span) < half
        nxt = []
        for z in parts:
            rz = pltpu.roll(z, shift, 1)
            nxt.append(jnp.where(low, z, rz))
            nxt.append(jnp.where(low, rz, z))
        parts = nxt
        span = half
    return parts


def _heads_to_lanes(ops, b_n, pad_to=LANES):
    tm = ops[0].shape[0]
    c = ops[0].shape[1] // HEADS
    packed = []
    for h in range(HEADS):
        pieces = [o[:, h * c:(h + 1) * c] for o in ops]
        if len(pieces) * c < pad_to:
            pieces.append(jnp.zeros((tm, pad_to - len(pieces) * c), F32))
        packed.append(jnp.concatenate(pieces, axis=1).reshape(tm // b_n, b_n, pad_to))
    y = jnp.stack(packed, axis=1).reshape(tm * HEADS, pad_to)
    return y.T


def _prep_kernel(rq_ref, rk_ref, rv_ref, cq_ref, sq_ref, ck_ref, sk_ref,
                 gq_ref, gk_ref, gv_ref, gr_ref,
                 c_ref, tail_ref, shift_ref, wg2_ref, bg_ref,
                 mu_ref, w0_ref, w1_ref, w2_ref, a0_ref, a1_ref, a2_ref, g1_ref, g2_ref,
                 kkp_ref, kap_ref, rkp_ref, ones_ref,
                 kr_ref, kg_ref, kw_ref, va_ref, obonus_ref, og_ref, *, grouped):
    i = pl.program_id(0)
    w = BRANCH_W
    lane = lax.broadcasted_iota(jnp.int32, rq_ref.shape, 1)
    first_half = (lane % HEAD_D) < (HEAD_D // 2)

    def per_row(tab_ref):
        tab = tab_ref[...]
        rows, reps = tab.shape[0], rq_ref.shape[0] // tab.shape[0]
        if reps == 1:
            return tab
        return jnp.broadcast_to(tab[:, None, :], (rows, reps, w)).reshape(rows * reps, w)

    def rot(x, cos_ref, sin_ref):
        partner = jnp.where(first_half,
                            pltpu.roll(x, BRANCH_W - HEAD_D // 2, 1),
                            pltpu.roll(x, HEAD_D // 2, 1))
        return x * per_row(cos_ref) + partner * per_row(sin_ref)

    ret_q = rot(rq_ref[...], cq_ref, sq_ref)
    ret_k = rot(rk_ref[...], ck_ref, sk_ref)

    z = _bdot(gr_ref[...], wg2_ref[...]) + bg_ref[...]
    glog = (jnp.minimum(z, 0.0) - _log1p_exp_neg_abs(z)) / GLA_GATE_NORM
    gla_q = gq_ref[...] * (GLA_DK ** -0.5)
    gla_al = jnp.exp(glog)

    c = c_ref[...]
    tm = c.shape[0]
    b_n = tail_ref.shape[0]
    tail = jnp.where(i == 0, shift_ref[...], tail_ref[...])
    d = jnp.concatenate([tail, c[:tm - b_n]], axis=0) - c
    mu = mu_ref[...]
    cr, ck, cv, cz = (c[:, j * w:(j + 1) * w] for j in range(4))
    dr, dk, dv, dz = (d[:, j * w:(j + 1) * w] for j in range(4))
    r = cr + dr * mu[0:1]
    k = ck + dk * mu[1:2]
    v = cv + dv * mu[2:3]
    zw = cz + dz * mu[3:4]
    za = cz + dz * mu[4:5]
    zg = cz + dz * mu[5:6]
    w_raw = w0_ref[...] + _bdot(jnp.tanh(_bdot(zw, w1_ref[...])), w2_ref[...])
    sp = jnp.maximum(-w_raw, 0.0) + _log1p_exp_neg_abs(w_raw)
    dec = jnp.exp(-jnp.exp(-sp - 0.5))
    a = _sigmoid(a0_ref[...] + _bdot(_bdot(za, a1_ref[...]), a2_ref[...]))
    og_ref[...] = _bdot(_sigmoid(_bdot(zg, g1_ref[...])), g2_ref[...])
    kk = k * kkp_ref[...]
    ss = _seg_dot(kk * kk, ones_ref)
    kk = kk * lax.rsqrt(jnp.maximum(ss, 1e-24))
    km = k * (1.0 + (a - 1.0) * kap_ref[...])
    obonus_ref[...] = _seg_dot(r * km * rkp_ref[...], ones_ref) * v

    if not grouped:
        for ref, ops in ((kr_ref, (ret_q, ret_k)), (kg_ref, (gla_q, gk_ref[...], gla_al)),
                         (kw_ref, (r, km, dec, kk, kk * a)), (va_ref, (rv_ref[...], gv_ref[...], v))):
            cw = ops[0].shape[1]
            for j, o in enumerate(ops):
                ref[j * cw:(j + 1) * cw, :] = o.T
        return

    n_grp = LANES // (HEADS * b_n)

    def emit_tiles(ref, row0, ops, n_rows):
        wt = _heads_to_lanes(ops, b_n)
        for lt in range(wt.shape[1] // LANES):
            for g, zt in enumerate(_replicate_groups(wt[:n_rows, lt * LANES:(lt + 1) * LANES], n_grp)):
                q = lt * n_grp + g
                ref[row0:row0 + n_rows, q * LANES:(q + 1) * LANES] = zt

    emit_tiles(kr_ref, 0, (ret_q, ret_k), 2 * HEAD_D)
    emit_tiles(kg_ref, 0, (gla_q, gk_ref[...], gla_al), 3 * GLA_DK)
    emit_tiles(kw_ref, 0, (r, km), 2 * HEAD_D)
    emit_tiles(kw_ref, 2 * HEAD_D, (dec, kk), 2 * HEAD_D)
    emit_tiles(kw_ref, 4 * HEAD_D, (kk * a,), HEAD_D)
    va_ref[0:2 * HEAD_D, :] = _heads_to_lanes((rv_ref[...], gv_ref[...]), b_n)
    va_ref[2 * HEAD_D:3 * HEAD_D, :] = _heads_to_lanes((v,), b_n)[:HEAD_D]


def _scan_kernel(kr_ref, kg_ref, kw_ref, va_ref, dec_ref, s0r_ref, s0g_ref, s0w_ref,
                 o_ref, sor_ref, sog_ref, sow_ref,
                 sr, sg, sw, vs, os_, *, vh_n, grouped, tc, b_n):
    ti = pl.program_id(1)

    @pl.when(ti == 0)
    def _():
        sr[...] = s0r_ref[...]
        sg[...] = s0g_ref[...]
        sw[...] = s0w_ref[...]

    n_grp = LANES // (HEADS * b_n) if grouped else 1
    grp_w = LANES // n_grp
    n_v = 3

    def k_tile(ref, n_ops, j, t):
        lanes = pl.ds(pl.multiple_of(t * LANES, LANES), LANES)
        if grouped:
            k_n = ref.shape[0] // n_ops
            return ref[j * k_n:(j + 1) * k_n, lanes]
        return ref[j, 0, :, lanes]

    if grouped:
        grp = lax.broadcasted_iota(jnp.int32, (vh_n, LANES), 1) // grp_w
        for lt in range(tc // n_grp):
            lanes = slice(lt * LANES, (lt + 1) * LANES)
            for j in range(n_v):
                ws = [va_ref[j * HEAD_D + vl * vh_n:j * HEAD_D + (vl + 1) * vh_n, lanes] for vl in range(n_grp)]
                for g in range(n_grp):
                    tile = None
                    for vl in range(n_grp):
                        r = pltpu.roll(ws[vl], ((vl - g) * grp_w) % LANES, 1)
                        tile = r if tile is None else jnp.where(grp == vl, r, tile)
                    vs[lt * n_grp + g, j] = tile
    else:
        def copy_in(t, c):
            lanes = pl.ds(pl.multiple_of(t * LANES, LANES), LANES)
            for j in range(n_v):
                vs[t, j] = va_ref[j, 0, :, lanes]
            return c

        lax.fori_loop(0, tc, copy_in, 0)

    dec_r = dec_ref[...]

    def out(t, j, vh, s, q):
        os_[t, j, pl.ds(vh, 1), :] = jnp.sum(s * q, axis=0, keepdims=True)

    def ret_step(t, c):
        q, k = k_tile(kr_ref, 2, 0, t), k_tile(kr_ref, 2, 1, t)
        for vh in range(vh_n):
            s = sr[vh] + vs[t, 0, pl.ds(vh, 1), :] * k
            sr[vh] = s
            out(t, 0, vh, s, q)
        return c

    def gla_step(t, c):
        q, k, al = (k_tile(kg_ref, 3, j, t) for j in range(3))
        for vh in range(vh_n):
            s = sg[vh] * al + vs[t, 1, pl.ds(vh, 1), :] * k
            sg[vh] = s
            out(t, 1, vh, s, q)
        return c

    def rwkv_step(t, c):
        q, k, dec, kk, beta = (k_tile(kw_ref, 5, j, t) for j in range(5))
        for vh in range(vh_n):
            s = sw[vh]
            sk = jnp.sum(s * kk, axis=0, keepdims=True)
            s = s * dec - sk * beta + vs[t, 2, pl.ds(vh, 1), :] * k
            sw[vh] = s
            out(t, 2, vh, s, q)
        return c

    unroll = 2 if grouped else 1
    lax.fori_loop(0, tc, ret_step, 0, unroll=unroll)
    sr[...] = sr[...] * dec_r
    lax.fori_loop(0, tc, gla_step, 0, unroll=unroll)
    lax.fori_loop(0, tc, rwkv_step, 0, unroll=unroll)

    if grouped:
        grp = lax.broadcasted_iota(jnp.int32, (vh_n, LANES), 1) // grp_w
        for lt in range(tc // n_grp):
            lanes = slice(lt * LANES, (lt + 1) * LANES)
            for j in range(n_v):
                og = [os_[lt * n_grp + g, j] for g in range(n_grp)]
                for vl in range(n_grp):
                    wv = None
                    for g in range(n_grp):
                        r = pltpu.roll(og[g], ((g - vl) * grp_w) % LANES, 1)
                        wv = r if wv is None else jnp.where(grp == g, r, wv)
                    o_ref[j * HEAD_D + vl * vh_n:j * HEAD_D + (vl + 1) * vh_n, lanes] = wv
    else:
        def copy_out(t, c):
            lanes = pl.ds(pl.multiple_of(t * LANES, LANES), LANES)
            for j in range(n_v):
                o_ref[j, 0, :, lanes] = os_[t, j]
            return c

        lax.fori_loop(0, tc, copy_out, 0)

    @pl.when(ti == pl.num_programs(1) - 1)
    def _():
        sor_ref[...] = sr[...]
        sog_ref[...] = sg[...]
        sow_ref[...] = sw[...]


def _scan(kr, kg, kw, va, b_n, t_n, dec, s0r, s0g, s0w):
    grouped = b_n * HEADS < LANES
    vh_n = s0r.shape[0]
    if grouped:
        tc = min(t_n, SCAN_CHUNK)
        n_grp = LANES // (HEADS * b_n)
        assert tc % n_grp == 0 and t_n % tc == 0
        n_l, n_t = 1, t_n // tc
        k_specs = [pl.BlockSpec((a.shape[0], tc * LANES), lambda l, t: (0, t)) for a in (kr, kg, kw)]
        v_spec = pl.BlockSpec((va.shape[0], tc * HEADS * b_n), lambda l, t: (0, t))
        o_spec = v_spec
    else:
        assert b_n == LANES
        tc, n_l, n_t = t_n, HEADS, 1
        k_specs = [pl.BlockSpec((a.shape[0], 1, a.shape[2], tc * LANES), lambda l, t: (0, l, 0, 0))
                   for a in (kr, kg, kw)]
        v_spec = pl.BlockSpec((3, 1, HEAD_D, tc * LANES), lambda l, t: (0, l, 0, 0))
        o_spec = v_spec

    def s_spec(a):
        return pl.BlockSpec(a.shape[:2] + (LANES,), lambda l, t: (0, 0, l))

    return pl.pallas_call(
        functools.partial(_scan_kernel, vh_n=vh_n, grouped=grouped, tc=tc, b_n=b_n),
        grid=(n_l, n_t),
        in_specs=k_specs + [v_spec, pl.BlockSpec((1, LANES), lambda l, t: (0, l)),
                            s_spec(s0r), s_spec(s0g), s_spec(s0w)],
        out_specs=[o_spec, s_spec(s0r), s_spec(s0g), s_spec(s0w)],
        out_shape=[jax.ShapeDtypeStruct(va.shape, F32)] + [jax.ShapeDtypeStruct(s.shape, F32)
                                                           for s in (s0r, s0g, s0w)],
        scratch_shapes=[pltpu.VMEM(s.shape[:2] + (LANES,), F32) for s in (s0r, s0g, s0w)]
                       + [pltpu.VMEM((tc, 3, vh_n, LANES), F32), pltpu.VMEM((tc, 3, vh_n, LANES), F32)],
        compiler_params=_cparams("parallel", "arbitrary"),
        name="scan_ret_gla_rwkv",
    )(kr, kg, kw, va, dec, s0r, s0g, s0w)


def _s5_kernel(u_ref, bb_ref, cc_ref, ar_ref, ai_ref, h0r_ref, h0i_ref,
               y_ref, hr_out, hi_out, hr_scr, hi_scr, xs_scr, hs_scr, *, b_n, tc):
    ti = pl.program_id(0)

    @pl.when(ti == 0)
    def _():
        hr_scr[...] = h0r_ref[...]
        hi_scr[...] = h0i_ref[...]

    xs_scr[...] = _bdot(u_ref[...], bb_ref[...])
    ar = ar_ref[...]
    ai = ai_ref[...]

    def step(t, carry):
        hr, hi = carry
        row = pl.multiple_of(t * b_n, SUBLANES)
        x = xs_scr[pl.ds(row, b_n), :]
        nr = ar * hr - ai * hi + x[:, :S5_CH]
        ni = ar * hi + ai * hr + x[:, S5_CH:]
        hs_scr[pl.ds(row, b_n), :S5_CH] = nr
        hs_scr[pl.ds(row, b_n), S5_CH:] = ni
        return nr, ni

    hr, hi = lax.fori_loop(0, tc, step, (hr_scr[...], hi_scr[...]))
    hr_scr[...] = hr
    hi_scr[...] = hi
    y_ref[...] = _bdot(hs_scr[...], cc_ref[...])

    @pl.when(ti == pl.num_programs(0) - 1)
    def _():
        hr_out[...] = hr
        hi_out[...] = hi


def _s5_scan(cols, row_off, bb, cc, ar, ai, h0r, h0i, b_n, t_n):
    tc = min(t_n, max(1, 1024 // b_n))
    rows = tc * b_n
    assert t_n % tc == 0 and row_off % rows == 0
    blk0 = row_off // rows
    return pl.pallas_call(
        functools.partial(_s5_kernel, b_n=b_n, tc=tc),
        grid=(t_n // tc,),
        in_specs=[pl.BlockSpec((rows, BRANCH_W), lambda t: (blk0 + t, COL_S5 // BRANCH_W)),
                  _const_spec(bb.shape), _const_spec(cc.shape),
                  _const_spec(ar.shape), _const_spec(ai.shape),
                  _const_spec(h0r.shape), _const_spec(h0i.shape)],
        out_specs=[pl.BlockSpec((rows, BRANCH_W), lambda t: (t, 0)),
                   _const_spec(h0r.shape), _const_spec(h0i.shape)],
        out_shape=[jax.ShapeDtypeStruct((t_n * b_n, BRANCH_W), F32),
                   jax.ShapeDtypeStruct(h0r.shape, F32),
                   jax.ShapeDtypeStruct(h0i.shape, F32)],
        scratch_shapes=[pltpu.VMEM((b_n, S5_CH), F32), pltpu.VMEM((b_n, S5_CH), F32),
                        pltpu.VMEM((rows, 2 * S5_CH), F32), pltpu.VMEM((rows, 2 * S5_CH), F32)],
        compiler_params=_cparams("arbitrary"),
        name="scan_s5",
    )(cols, bb, cc, ar, ai, h0r, h0i)


def _post_kernel(x_ref, mixt_ref, rg_ref, gg_ref, sy_ref, su_ref, wbon_ref, wg_ref,
                 gate0_ref, gate1_ref, gate2_ref, gate3_ref,
                 avg_ref, rgn_g, rgn_b, ggn_g, s5d_ref, wglu_ref, wgn_g, wgn_b,
                 wbr_ref, wo_ref, ln_g, ln_b, *rest, alpha, b_n):
    o_ref = rest[-1] if len(rest) == 1 else rest[4]

    def seg_mean(v):
        return _seg_dot(v, avg_ref)

    tm = x_ref.shape[0]
    if b_n * HEADS < LANES:
        yt = mixt_ref[...].T.reshape(tm // b_n, HEADS, b_n, 3 * HEAD_D)
        per_head = [yt[:, h].reshape(tm, 3 * HEAD_D) for h in range(HEADS)]
        mix = jnp.concatenate([ph[:, j * HEAD_D:(j + 1) * HEAD_D] for j in range(3) for ph in per_head], axis=1)
    else:
        mix = mixt_ref[...].T
    ro = mix[:, 0:BRANCH_W]
    mu = seg_mean(ro)
    rc = ro - mu
    var = seg_mean(rc * rc)
    b0 = (rc * lax.rsqrt(var + LN_EPS) * rgn_g[...] + rgn_b[...]) * _silu(rg_ref[...])
    go = mix[:, BRANCH_W:2 * BRANCH_W]
    ms = seg_mean(go * go)
    b1 = go * lax.rsqrt(ms + LN_EPS) * ggn_g[...] * _silu(gg_ref[...])
    y = jax.nn.gelu(sy_ref[...] + s5d_ref[...] * su_ref[...])
    b2 = y * _sigmoid(_bdot(y, wglu_ref[...]))
    wy = mix[:, 2 * BRANCH_W:3 * BRANCH_W]
    mu = seg_mean(wy)
    wc = wy - mu
    var = seg_mean(wc * wc)
    b3 = (wc * lax.rsqrt(var + RWKV_GN_EPS) * wgn_g[...] + wgn_b[...] + wbon_ref[...]) * wg_ref[...]

    m = None
    gates = (gate0_ref, gate1_ref, gate2_ref, gate3_ref)
    for i, br in enumerate((b0, b1, b2, b3)):
        term = _bdot(br, wbr_ref[i]) * _sigmoid(gates[i][...])
        m = term if m is None else m + term
    h = _bdot(m, wo_ref[...])
    x1 = _layer_norm(alpha * x_ref[...] + h, ln_g[...], ln_b[...])
    o_ref[...] = x1
    if len(rest) > 1:
        _route_top2(x1, rest[0], rest[1], rest[5], rest[6])


def _embed_ln2(x, f, p_ref, wpe_ref, wpg_ref, ln_g, ln_b, alpha):
    e = _bdot(p_ref[...], wpe_ref[...]) * _sigmoid(_bdot(x, wpg_ref[...]))
    return _layer_norm(alpha * x + f + e, ln_g[...], ln_b[...])


def _ffn_kernel(x_ref, p_ref, w1_ref, w3_ref, w2_ref, wpe_ref, wpg_ref, ln_g, ln_b, o_ref, *, alpha):
    x = x_ref[...]
    xb = x.astype(BF16)
    h = _silu(jnp.dot(xb, w1_ref[...], preferred_element_type=F32)) * jnp.dot(
        xb, w3_ref[...], preferred_element_type=F32)
    f = _bdot(h, w2_ref[...])
    o_ref[...] = _embed_ln2(x, f, p_ref, wpe_ref, wpg_ref, ln_g, ln_b, alpha)


def _route_top2(x, wh_ref, wl_ref, idx_ref, wgt_ref):
    xh, xl = _split_bf16(x)
    wh = wh_ref[...]
    logits = (jnp.dot(xh, wh, preferred_element_type=F32)
              + jnp.dot(xl, wh, preferred_element_type=F32)
              + jnp.dot(xh, wl_ref[...], preferred_element_type=F32))
    col = lax.broadcasted_iota(jnp.int32, logits.shape, 1)
    neg = jnp.float32(-jnp.inf)
    lg = jnp.where(col < N_EXPERTS, logits, neg)
    m1 = jnp.max(lg, axis=1, keepdims=True)
    i1 = jnp.min(jnp.where(lg == m1, col, LANES), axis=1, keepdims=True)
    lg2 = jnp.where(col == i1, neg, lg)
    m2 = jnp.max(lg2, axis=1, keepdims=True)
    i2 = jnp.min(jnp.where(lg2 == m2, col, LANES), axis=1, keepdims=True)
    e2 = jnp.exp(m2 - m1)
    den = 1.0 + e2
    idx_ref[...] = jnp.where(col == 0, i1, jnp.where(col == 1, i2, 0))
    wgt_ref[...] = jnp.where(col == 0, 1.0 / den, jnp.where(col == 1, e2 / den, 0.0))


def _row_copies(idx_ref, n_rows, make_copy, count=None):
    def each(action):
        def body(r, c):
            action(make_copy(r, idx_ref[0, 0, r]))
            return c

        if count is None:
            lax.fori_loop(0, n_rows, body, 0, unroll=8)
            return

        @pl.when(count == n_rows)
        def _():
            lax.fori_loop(0, n_rows, body, 0, unroll=8)

        @pl.when(count < n_rows)
        def _():
            lax.fori_loop(0, count, body, 0)

    return (lambda: each(lambda cp: cp.start())), (lambda: each(lambda cp: cp.wait()))


def _moe_block_kernel(be_ref, nb_ref, bv_ref, tok_ref, tok_next_ref, dst_ref, dst_m1_ref, dst_m2_ref,
                      x_hbm, w1_ref, w3_ref, w2_ref, out_hbm, xbuf, ybuf, sem_in, sem_out):
    j = pl.program_id(0)
    last = pl.num_programs(0) - 1
    nb = nb_ref[0]
    slot = j % 2

    def gather_copy(idx_ref, half, r):
        return pltpu.make_async_copy(x_hbm.at[pl.ds(idx_ref[0, 0, r], 1)], xbuf.at[half, pl.ds(r, 1)],
                                     sem_in.at[half])

    def gather(idx_ref, half):
        return _row_copies(idx_ref, MOE_BLOCK, lambda r, row: pltpu.make_async_copy(
            x_hbm.at[pl.ds(row, 1)], xbuf.at[half, pl.ds(r, 1)], sem_in.at[half]))

    def scatter(idx_ref, half, block):
        return _row_copies(idx_ref, MOE_BLOCK, lambda r, row: pltpu.make_async_copy(
            ybuf.at[half, pl.ds(r, 1)], out_hbm.at[pl.ds(row, 1)], sem_out.at[half]),
            count=bv_ref[jnp.maximum(block, 0)])

    @pl.when(jnp.logical_and(j == 0, nb > 0))
    def _():
        gather(tok_ref, 0)[0]()

    def step(cur):
        gather(tok_ref, cur)[1]()

        @pl.when(j >= 2)
        def _():
            scatter(dst_m2_ref, cur, j - 2)[1]()

        for r in range(MOE_BLOCK):
            gather_copy(tok_next_ref, 1 - cur, r).start()
        xb = xbuf[cur].astype(BF16)
        h = _silu(jnp.dot(xb, w1_ref[0], preferred_element_type=F32)) * jnp.dot(
            xb, w3_ref[0], preferred_element_type=F32)
        ybuf[cur] = _bdot(h, w2_ref[0])
        scatter(dst_ref, cur, j)[0]()

    for par in range(2):
        pl.when(jnp.logical_and(j < nb, slot == par))(functools.partial(step, par))

    @pl.when(jnp.logical_and(j == nb, nb > 0))
    def _():
        gather(tok_ref, slot)[1]()

    @pl.when(jnp.logical_and(j >= nb, jnp.logical_and(j >= 2, j - 2 < nb)))
    def _():
        scatter(dst_m2_ref, slot, j - 2)[1]()

    @pl.when(jnp.logical_and(j == last, jnp.logical_and(j >= 1, j - 1 < nb)))
    def _():
        scatter(dst_m1_ref, 1 - slot, j - 1)[1]()

    @pl.when(jnp.logical_and(j == last, j < nb))
    def _():
        scatter(dst_ref, slot, j)[1]()


def _moe_blocks(x, block_e, nb_used, block_valid, slot_tok, slot_dst, n_out_rows, w1, w3, w2):
    n_blocks = block_e.shape[0]
    d = x.shape[1]
    dff = w1.shape[2]
    tok = slot_tok.reshape(n_blocks, 1, MOE_BLOCK)
    dst = slot_dst.reshape(n_blocks, 1, MOE_BLOCK)

    def idx_spec(shift):
        return pl.BlockSpec((1, 1, MOE_BLOCK),
                            lambda j, be, nb, bv: (jnp.clip(j + shift, 0, n_blocks - 1), 0, 0),
                            memory_space=pltpu.SMEM)

    grid_spec = pltpu.PrefetchScalarGridSpec(
        num_scalar_prefetch=3,
        grid=(n_blocks + 1,),
        in_specs=[
            idx_spec(0), idx_spec(1), idx_spec(0), idx_spec(-1), idx_spec(-2),
            pl.BlockSpec(memory_space=pl.ANY),
            pl.BlockSpec((1, d, dff), lambda j, be, nb, bv: (be[jnp.minimum(j, n_blocks - 1)], 0, 0),
                         pipeline_mode=pl.Buffered(1)),
            pl.BlockSpec((1, d, dff), lambda j, be, nb, bv: (be[jnp.minimum(j, n_blocks - 1)], 0, 0),
                         pipeline_mode=pl.Buffered(1)),
            pl.BlockSpec((1, dff, d), lambda j, be, nb, bv: (be[jnp.minimum(j, n_blocks - 1)], 0, 0),
                         pipeline_mode=pl.Buffered(1)),
        ],
        out_specs=pl.BlockSpec(memory_space=pl.ANY),
        scratch_shapes=[pltpu.VMEM((2, MOE_BLOCK, d), F32), pltpu.VMEM((2, MOE_BLOCK, d), F32),
                        pltpu.SemaphoreType.DMA((2,)), pltpu.SemaphoreType.DMA((2,))],
    )
    return pl.pallas_call(
        _moe_block_kernel,
        grid_spec=grid_spec,
        out_shape=jax.ShapeDtypeStruct((n_out_rows, d), F32),
        compiler_params=_cparams("arbitrary"),
        name="moe_blocks",
    )(block_e, nb_used, block_valid, tok, tok, dst, dst, dst, x, w1, w3, w2)


def _moe_combine_kernel(y0_ref, y1_ref, x_ref, p_ref, wgt_ref, wpe_ref, wpg_ref, ln_g, ln_b, o_ref, *, alpha):
    wgt = wgt_ref[...]
    f = y0_ref[...] * wgt[:, 0:1] + y1_ref[...] * wgt[:, 1:2]
    o_ref[...] = _embed_ln2(x_ref[...], f, p_ref, wpe_ref, wpg_ref, ln_g, ln_b, alpha)


def _lane_groups(b_n):
    bh = b_n * HEADS
    return LANES // bh if bh < LANES else 1


def _state_to_lanes(s, vl_n, value_last):
    s = s.transpose(3, 2, 1, 0) if value_last else s.transpose(2, 3, 1, 0)
    v_n, k_n, h_n, b_n = s.shape
    s = s.reshape(vl_n, v_n // vl_n, k_n, h_n, b_n).transpose(1, 2, 0, 3, 4)
    return s.reshape(v_n // vl_n, k_n, vl_n * h_n * b_n)


def _state_from_lanes(s, b_n, vl_n, value_last):
    vh_n, k_n, _ = s.shape
    s = s.reshape(vh_n, k_n, vl_n, HEADS, b_n).transpose(4, 3, 1, 2, 0).reshape(b_n, HEADS, k_n, vl_n * vh_n)
    return s if value_last else s.transpose(0, 1, 3, 2)


def _rotary_tables(pos, row_repeat, gamma, tc):
    half = HEAD_D // 2
    freq = ROPE_BASE ** (-jnp.arange(half, dtype=F32) / half)
    ang = pos.astype(F32)[:, None] * freq[None, :]
    cos, sin = jnp.cos(ang), jnp.sin(ang)
    cos_h = jnp.tile(jnp.concatenate([cos, cos], axis=-1), (1, HEADS))
    sin_h = jnp.tile(jnp.concatenate([-sin, sin], axis=-1), (1, HEADS))
    tau1 = (jnp.arange(pos.shape[0], dtype=jnp.int32) % tc + 1).astype(F32)
    f_q = jnp.repeat(gamma[None, :] ** tau1[:, None], HEAD_D, axis=1)
    f_k = (HEAD_D ** -0.5) / f_q
    tabs = (cos_h * f_q, sin_h * f_q, cos_h * f_k, sin_h * f_k)
    return tuple(jnp.repeat(t, row_repeat, axis=0) for t in tabs) if row_repeat > 1 else tabs


def _block_diag_const(block, n_blocks):
    return jnp.kron(jnp.eye(n_blocks, dtype=F32), jnp.full((block, block), 1.0, F32))


def _s5_params(log_dt, a_re, a_im, b_re, b_im, c_re, c_im):
    dt = jnp.exp(log_dt)[:, None]
    mag = jnp.exp(dt * a_re)
    ang = dt * a_im
    abar_re, abar_im = mag * jnp.cos(ang), mag * jnp.sin(ang)
    den = a_re * a_re + a_im * a_im
    n_re = abar_re - 1.0
    f_re = (n_re * a_re + abar_im * a_im) / den
    f_im = (abar_im * a_re - n_re * a_im) / den
    bb_re = f_re[..., None] * b_re - f_im[..., None] * b_im
    bb_im = f_re[..., None] * b_im + f_im[..., None] * b_re
    eye = jnp.eye(S5_GROUPS, dtype=F32)

    def in_map(bb):
        return jnp.einsum("gpc,gh->gchp", bb, eye).reshape(BRANCH_W, S5_CH)

    def out_map(cm):
        return jnp.einsum("gcp,gh->gphc", cm, eye).reshape(S5_CH, BRANCH_W)

    bb = jnp.concatenate([in_map(bb_re), in_map(bb_im)], axis=1).astype(BF16)
    cc = jnp.concatenate([out_map(c_re), -out_map(c_im)], axis=0).astype(BF16)
    return bb, cc, abar_re.reshape(1, S5_CH), abar_im.reshape(1, S5_CH)


def _reorder_w_in(w_in):
    d_model = w_in.shape[0]
    main = jnp.concatenate([w_in[:, :1792], w_in[:, 1808:]], axis=1)
    tail = jnp.concatenate([w_in[:, 1792:1808],
                            jnp.zeros((d_model, D_IN_PAD - COL_GLA_R - GLA_GATE_RANK), w_in.dtype)], axis=1)
    return jnp.concatenate([main, tail], axis=1).astype(BF16)


def _pad_cols(w, n):
    return jnp.pad(w, ((0, 0), (0, n - w.shape[1])))


def _pad_rows(w, n):
    return jnp.pad(w, ((0, n - w.shape[0]), (0, 0)))


def _moe_route(idx, n):
    nk = n * TOP_K
    flat_e = idx.reshape(nk)
    onehot = (flat_e[:, None] == jnp.arange(N_EXPERTS, dtype=jnp.int32)[None, :]).astype(jnp.int32)
    incl = jnp.cumsum(onehot, axis=0)
    counts = incl[-1]
    rank = jnp.sum((incl - onehot) * onehot, axis=1)
    padded = (counts + MOE_BLOCK - 1) // MOE_BLOCK * MOE_BLOCK
    pad_end = jnp.cumsum(padded)
    slot = (pad_end - padded)[flat_e] + rank
    n_blocks = -(-(nk + N_EXPERTS * (MOE_BLOCK - 1)) // MOE_BLOCK)
    cap = n_blocks * MOE_BLOCK
    pair = jnp.full((cap,), -1, jnp.int32).at[slot].set(jnp.arange(nk, dtype=jnp.int32))
    real = pair >= 0
    slot_tok = jnp.where(real, pair // TOP_K, 0)
    slot_dst = jnp.where(real, (pair % TOP_K) * n + pair // TOP_K, 0)
    block_start = jnp.arange(n_blocks, dtype=jnp.int32) * MOE_BLOCK
    block_e = jnp.minimum(jnp.sum((pad_end[None, :] <= block_start[:, None]).astype(jnp.int32), axis=1),
                          N_EXPERTS - 1).astype(jnp.int32)
    nb_used = (pad_end[-1] // MOE_BLOCK).astype(jnp.int32).reshape(1)
    block_valid = jnp.sum(real.reshape(n_blocks, MOE_BLOCK).astype(jnp.int32), axis=1)
    return slot_tok, slot_dst, block_e, nb_used, block_valid, nk


def kernel(x_prompt, x_sample, state_ret, state_gla, state_s5_re, state_s5_im, state_rwkv, state_shift,
           p_prompt, p_sample, w_in, ret_gn_g, ret_gn_b, gla_wg2, gla_bg, gla_gn,
           s5_log_dt, s5_a_re, s5_a_im, s5_b_re, s5_b_im, s5_c_re, s5_c_im, s5_d, s5_w_glu,
           rwkv_mu, rwkv_w0, rwkv_w1, rwkv_w2, rwkv_a0, rwkv_a1, rwkv_a2, rwkv_g1, rwkv_g2,
           rwkv_kk, rwkv_ka, rwkv_rk, rwkv_gn_g, rwkv_gn_b, w_branch, w_o,
           ln1_g, ln1_b, ln2_g, ln2_b, w_pe, w_pg, ffn_w1, ffn_w3, ffn_w2,
           moe_router, moe_w1, moe_w3, moe_w2):
    depth = w_in.shape[0]
    bp, tp, d_model = x_prompt.shape
    bs, ts, _ = x_sample.shape
    n_p, n_s = bp * tp, bs * ts
    n = n_p + n_s
    tm = ROW_TILE
    assert n_p % tm == 0 and n_s % tm == 0 and tm % bp == 0 and tm % bs == 0 and n_p % bs == 0
    alpha = (2 * depth) ** 0.25
    groups = [(0, bp, tp), (n_p, bs, ts)]
    w4 = 4 * BRANCH_W

    def time_major(a_p, a_s):
        return jnp.concatenate([a_p.transpose(1, 0, 2).reshape(n_p, -1),
                                a_s.transpose(1, 0, 2).reshape(n_s, -1)], axis=0)

    x = time_major(x_prompt, x_sample)
    ones_bd = _block_diag_const(HEAD_D, HEADS).astype(BF16)
    avg_bd = (_block_diag_const(HEAD_D, HEADS) / HEAD_D).astype(BF16)
    gamma = 1.0 - jnp.exp2(-5.0 - jnp.arange(HEADS, dtype=F32))
    tc_p, tc_s = min(tp, SCAN_CHUNK), ts
    rot_p = _rotary_tables(jnp.arange(tp, dtype=jnp.int32), 1 if tm // bp >= SUBLANES else bp, gamma, tc_p)
    rot_s = _rotary_tables(PAST_LEN + jnp.arange(ts, dtype=jnp.int32), 1 if tm // bs >= SUBLANES else bs,
                           gamma, tc_s)
    row = lambda v: v.reshape(1, -1)

    new = [[] for _ in range(6)]
    for i in range(depth):
        cols = _matmul(x, _reorder_w_in(w_in[i]), tm=512, tn=D_IN_PAD // 3, name="in_proj")

        prep_consts = [
            _pad_rows(_pad_cols(gla_wg2[i], LANES), LANES).astype(BF16), row(gla_bg[i]),
            rwkv_mu[i], row(rwkv_w0[i]),
            _pad_cols(rwkv_w1[i], LANES).astype(BF16), _pad_rows(rwkv_w2[i], LANES).astype(BF16),
            row(rwkv_a0[i]),
            _pad_cols(rwkv_a1[i], LANES).astype(BF16), _pad_rows(rwkv_a2[i], LANES).astype(BF16),
            _pad_cols(rwkv_g1[i], LANES).astype(BF16), _pad_rows(rwkv_g2[i], LANES).astype(BF16),
            row(rwkv_kk[i]), row(rwkv_ka[i]), row(rwkv_rk[i]), ones_bd,
        ]
        post_consts = [avg_bd, row(ret_gn_g[i]), row(ret_gn_b[i]), row(gla_gn[i]), row(s5_d[i]),
                       s5_w_glu[i].astype(BF16), row(rwkv_gn_g[i]), row(rwkv_gn_b[i]),
                       w_branch[i].astype(BF16), w_o[i].astype(BF16), row(ln1_g[i]), row(ln1_b[i])]
        s5p = _s5_params(s5_log_dt[i], s5_a_re[i], s5_a_im[i], s5_b_re[i], s5_b_im[i], s5_c_re[i], s5_c_im[i])
        rwkv_blk = COL_RWKV // w4
        x1, layer_new = x, []
        routed = i % 2 == 1
        if routed:
            router_hi, router_lo = _split_bf16(_pad_cols(moe_router[i // 2], LANES))
            idx, wgt = jnp.zeros((n, LANES), jnp.int32), jnp.zeros((n, LANES), F32)
        for gi, (off, b_n, t_n) in enumerate(groups):
            n_g = b_n * t_n
            t0 = off // tm
            if gi == 0:
                st = [jnp.zeros((b_n,) + s.shape[2:], s.dtype)
                      for s in (state_ret, state_gla, state_s5_re, state_s5_im, state_rwkv)]
                shift0, rot_g, tc_g = jnp.zeros((b_n, w4), F32), rot_p, tc_p
            else:
                st = [state_ret[i], state_gla[i], state_s5_re[i], state_s5_im[i], state_rwkv[i]]
                shift0, rot_g, tc_g = state_shift[i], rot_s, tc_s
            vl_n = _lane_groups(b_n)
            grouped = vl_n > 1

            def cspec(width, cb, t0=t0):
                return pl.BlockSpec((tm, width), lambda r: (r + t0, cb))

            if grouped:
                t_shapes = [(2 * HEAD_D, n_g * vl_n * HEADS), (3 * GLA_DK, n_g * vl_n * HEADS),
                            (5 * HEAD_D, n_g * vl_n * HEADS), (3 * HEAD_D, n_g * HEADS)]
            else:
                t_shapes = [(2 * BRANCH_W, n_g), (3 * GLA_QK, n_g), (5 * BRANCH_W, n_g), (3 * BRANCH_W, n_g)]
            consts = [shift0] + prep_consts
            ka_ret, ka_gla, ka_rwkv, va, obonus, og = pl.pallas_call(
                functools.partial(_prep_kernel, grouped=grouped),
                grid=(n_g // tm,),
                in_specs=[cspec(BRANCH_W, 0), cspec(BRANCH_W, 1), cspec(BRANCH_W, 2),
                          ] + [_row_spec(rot_g[0].shape[0] // (n_g // tm), BRANCH_W, 0)] * 4 + [
                          cspec(GLA_QK, COL_GLA_Q // GLA_QK), cspec(GLA_QK, COL_GLA_K // GLA_QK),
                          cspec(BRANCH_W, COL_GLA_V // BRANCH_W), cspec(LANES, COL_GLA_R // LANES),
                          cspec(w4, rwkv_blk),
                          pl.BlockSpec((b_n, w4), lambda r, t0=t0, b_n=b_n:
                                       (jnp.maximum((r + t0) * (tm // b_n) - 1, 0), rwkv_blk))]
                         + [_const_spec(a.shape) for a in consts],
                out_specs=[pl.BlockSpec((sh[0], sh[1] // (n_g // tm)), lambda r: (0, r)) for sh in t_shapes]
                          + [_row_spec(tm, BRANCH_W, 0), _row_spec(tm, BRANCH_W, 0)],
                out_shape=[jax.ShapeDtypeStruct(sh, F32) for sh in t_shapes]
                          + [jax.ShapeDtypeStruct((n_g, BRANCH_W), F32)] * 2,
                compiler_params=_cparams("parallel"),
                name="mixer_prep",
            )(cols, cols, cols, *rot_g, cols, cols, cols, cols, cols, cols, *consts)

            dec = jnp.tile(jnp.repeat(gamma ** tc_g, b_n), vl_n).reshape(1, vl_n * HEADS * b_n)
            if not grouped:
                ka_ret = ka_ret.reshape(2, HEADS, HEAD_D, n_g)
                ka_gla = ka_gla.reshape(3, HEADS, GLA_DK, n_g)
                ka_rwkv = ka_rwkv.reshape(5, HEADS, HEAD_D, n_g)
                va = va.reshape(3, HEADS, HEAD_D, n_g)
            o, s_ret, s_gla, s_rwkv = _scan(
                ka_ret, ka_gla, ka_rwkv, va, b_n, t_n, dec,
                _state_to_lanes(st[0], vl_n, True), _state_to_lanes(st[1], vl_n, True),
                _state_to_lanes(st[4], vl_n, False))
            mixt = o if grouped else o.reshape(3 * BRANCH_W, n_g)
            y, hr, hi = _s5_scan(cols, off, *s5p, st[2].reshape(b_n, S5_CH), st[3].reshape(b_n, S5_CH), b_n, t_n)
            shift_new = cols[off + (t_n - 1) * b_n:off + t_n * b_n, COL_RWKV:COL_RWKV + w4]
            layer_new.append((_state_from_lanes(s_ret, b_n, vl_n, True), _state_from_lanes(s_gla, b_n, vl_n, True),
                              hr.reshape(b_n, S5_GROUPS, S5_STATE), hi.reshape(b_n, S5_GROUPS, S5_STATE),
                              _state_from_lanes(s_rwkv, b_n, vl_n, False), shift_new))

            post_in = [x1, mixt, cols, cols, y, cols, obonus, og, cols, cols, cols, cols, *post_consts]
            post_specs = ([cspec(d_model, 0),
                           pl.BlockSpec((mixt.shape[0], mixt.shape[1] // (n_g // tm)), lambda r: (0, r)),
                           cspec(BRANCH_W, 3), cspec(BRANCH_W, COL_GLA_G // BRANCH_W),
                           _row_spec(tm, BRANCH_W, 0), cspec(BRANCH_W, COL_S5 // BRANCH_W),
                           _row_spec(tm, BRANCH_W, 0), _row_spec(tm, BRANCH_W, 0)]
                          + [cspec(d_model, COL_GATE // d_model + q) for q in range(N_BRANCH)]
                          + [_const_spec(a.shape) for a in post_consts])
            out_specs, out_shape, aliases = [cspec(d_model, 0)], [jax.ShapeDtypeStruct((n, d_model), F32)], {0: 0}
            if routed:
                aliases.update({len(post_in) + 2: 1, len(post_in) + 3: 2})
                post_in += [router_hi, router_lo, idx, wgt]
                post_specs += [_const_spec(router_hi.shape), _const_spec(router_lo.shape),
                               cspec(LANES, 0), cspec(LANES, 0)]
                out_specs += [cspec(LANES, 0), cspec(LANES, 0)]
                out_shape += [jax.ShapeDtypeStruct((n, LANES), jnp.int32), jax.ShapeDtypeStruct((n, LANES), F32)]
            res = pl.pallas_call(
                functools.partial(_post_kernel, alpha=alpha, b_n=b_n),
                grid=(n_g // tm,),
                in_specs=post_specs,
                out_specs=out_specs,
                out_shape=out_shape,
                input_output_aliases=aliases,
                compiler_params=_cparams("parallel"),
                name="mixer_post",
            )(*post_in)
            x1 = res[0]
            if routed:
                idx, wgt = res[1], res[2]
        for lst, pair in zip(new, zip(*layer_new)):
            lst.append(pair)

        if i + 1 < depth:
            p_in = [time_major(p_prompt[i], p_sample[i])]
        else:
            p_in = [p_prompt[i].transpose(1, 0, 2).reshape(n_p, -1), p_sample[i].transpose(1, 0, 2).reshape(n_s, -1)]
        tail_consts = [w_pe[i].astype(BF16), w_pg[i].astype(BF16), row(ln2_g[i]), row(ln2_b[i])]
        j = i // 2
        spans = [(0, n, 0)] if i + 1 < depth else [(0, n_p, 0), (n_p // tm, n_s, 0)]
        if i % 2 == 0:
            ffn_consts = [ffn_w1[j].astype(BF16), ffn_w3[j].astype(BF16), ffn_w2[j].astype(BF16)] + tail_consts
            xs = [_rowwise(functools.partial(_ffn_kernel, alpha=alpha), rows,
                           [(x1, d_model, 0, r0), (p, p.shape[1], 0, p0)],
                           ffn_consts, [d_model], name="ffn")[0] for (r0, rows, p0), p in zip(spans, p_in)]
        else:
            slot_tok, slot_dst, block_e, nb_used, block_valid, n_out_rows = _moe_route(idx[:, :TOP_K], n)
            yk = _moe_blocks(x1, block_e, nb_used, block_valid, slot_tok, slot_dst, n_out_rows,
                             moe_w1[j].astype(BF16), moe_w3[j].astype(BF16), moe_w2[j].astype(BF16))
            xs = [_rowwise(functools.partial(_moe_combine_kernel, alpha=alpha), rows,
                           [(yk, d_model, 0, r0), (yk, d_model, 0, n // tm + r0), (x1, d_model, 0, r0),
                            (p, p.shape[1], 0, p0), (wgt, LANES, 0, r0)],
                           tail_consts, [d_model], name="moe_combine")[0] for (r0, rows, p0), p in zip(spans, p_in)]
        x = xs[0]

    y_prompt = xs[0].reshape(tp, bp, d_model).transpose(1, 0, 2)
    y_sample = xs[1].reshape(ts, bs, d_model).transpose(1, 0, 2)
    outs = [y_prompt, y_sample]
    for lst in new:
        outs.append(jnp.stack([pair[0] for pair in lst], 0))
        outs.append(jnp.stack([pair[1] for pair in lst], 0))
    return tuple(outs)
```

```python
import functools

import jax
import jax.numpy as jnp
from jax import lax
from jax.experimental import pallas as pl
from jax.experimental.pallas import tpu as pltpu

F32 = jnp.float32
BF16 = jnp.bfloat16

LANES = 128
SUBLANES = 8
VMEM_LIMIT = 56 * 1024 * 1024

N_BRANCH = 4
BRANCH_W = 256
HEADS = 4
HEAD_D = 64
GLA_DK = 32
GLA_QK = HEADS * GLA_DK
GLA_GATE_RANK = 16
GLA_GATE_NORM = 16.0
S5_GROUP = 16
S5_GROUPS = 16
S5_STATE = 64
S5_CH = S5_GROUPS * S5_STATE
ROPE_BASE = 10000.0
RWKV_GN_EPS = 64e-5
LN_EPS = 1e-5
N_EXPERTS = 8
TOP_K = 2
MOE_BLOCK = 256
ROW_TILE = 256
SCAN_CHUNK = 32
PAST_LEN = 16384

COL_RET = 0
COL_GLA_Q = 1024
COL_GLA_K = 1152
COL_GLA_V = 1280
COL_GLA_G = 1536
COL_S5 = 1792
COL_RWKV = 2048
COL_GATE = 3072
COL_GLA_R = 7168
D_IN_PAD = 7296


def _cparams(*sem):
    return pltpu.CompilerParams(dimension_semantics=sem, vmem_limit_bytes=VMEM_LIMIT)


def _split_bf16(x):
    hi = x.astype(BF16)
    lo = (x - hi.astype(F32)).astype(BF16)
    return hi, lo


def _seg_dot(x, m_ref):
    hi, lo = _split_bf16(x)
    m = m_ref[...]
    return (jnp.dot(hi, m, preferred_element_type=F32)
            + jnp.dot(lo, m, preferred_element_type=F32))


def _bdot(x, w):
    return jnp.dot(x.astype(BF16), w, preferred_element_type=F32)


def _sigmoid(x):
    return 1.0 / (1.0 + jnp.exp(-x))


def _silu(x):
    return x * _sigmoid(x)


def _log1p_exp_neg_abs(x):
    return jnp.log1p(jnp.exp(-jnp.abs(x)))


def _layer_norm(x, g, b):
    mu = jnp.mean(x, axis=-1, keepdims=True)
    xc = x - mu
    var = jnp.mean(xc * xc, axis=-1, keepdims=True)
    return xc * lax.rsqrt(var + LN_EPS) * g + b


def _row_spec(tm, width, col_block, row_block0=0):
    return pl.BlockSpec((tm, width), lambda i, cb=col_block, r0=row_block0: (i + r0, cb))


def _const_spec(shape):
    nd = len(shape)
    return pl.BlockSpec(shape, lambda i, nd=nd: (0,) * nd)


def _rowwise(body, n_rows, row_in, const_in, out_widths, out_dtypes=None, tm=ROW_TILE, name=None):
    assert n_rows % tm == 0
    out_dtypes = out_dtypes or [F32] * len(out_widths)
    in_specs = [_row_spec(tm, *spec[1:]) for spec in row_in] + [_const_spec(a.shape) for a in const_in]
    out_specs = [_row_spec(tm, w, 0) for w in out_widths]
    out_shape = [jax.ShapeDtypeStruct((n_rows, w), dt) for w, dt in zip(out_widths, out_dtypes)]
    return pl.pallas_call(
        body,
        grid=(n_rows // tm,),
        in_specs=in_specs,
        out_specs=out_specs,
        out_shape=out_shape,
        compiler_params=_cparams("parallel"),
        name=name,
    )(*[spec[0] for spec in row_in], *const_in)


def _matmul_kernel(x_ref, w_ref, o_ref):
    o_ref[...] = _bdot(x_ref[...], w_ref[...]).astype(o_ref.dtype)


def _matmul(x, w, tm=512, tn=None, out_dtype=F32, name=None):
    m, k = x.shape
    n = w.shape[1]
    tn = tn or n
    tm = tm if m % tm == 0 else ROW_TILE
    assert m % tm == 0 and n % tn == 0
    return pl.pallas_call(
        _matmul_kernel,
        grid=(n // tn, m // tm),
        in_specs=[pl.BlockSpec((tm, k), lambda j, i: (i, 0)),
                  pl.BlockSpec((k, tn), lambda j, i: (0, j))],
        out_specs=pl.BlockSpec((tm, tn), lambda j, i: (i, j)),
        out_shape=jax.ShapeDtypeStruct((m, n), out_dtype),
        compiler_params=_cparams("parallel", "parallel"),
        name=name,
    )(x, w)


def _replicate_groups(w, n_grp):
    if n_grp == 1:
        return [w]
    grp = lax.broadcasted_iota(jnp.int32, w.shape, 1) // (LANES // n_grp)
    parts = [w]
    span = n_grp
    while span > 1:
        half = span // 2
        shift = half * (LANES // n_grp)
        low = (grp % span) < half
        nxt = []
        for z in parts:
            rz = pltpu.roll(z, shift, 1)
            nxt.append(jnp.where(low, z, rz))
            nxt.append(jnp.where(low, rz, z))
        parts = nxt
        span = half
    return parts


def _heads_to_lanes(ops, b_n, pad_to=LANES):
    tm = ops[0].shape[0]
    c = ops[0].shape[1] // HEADS
    packed = []
    for h in range(HEADS):
        pieces = [o[:, h * c:(h + 1) * c] for o in ops]
        if len(pieces) * c < pad_to:
            pieces.append(jnp.zeros((tm, pad_to - len(pieces) * c), F32))
        packed.append(jnp.concatenate(pieces, axis=1).reshape(tm // b_n, b_n, pad_to))
    y = jnp.stack(packed, axis=1).reshape(tm * HEADS, pad_to)
    return y.T


def _prep_kernel(rq_ref, rk_ref, rv_ref, cq_ref, sq_ref, ck_ref, sk_ref,
                 gq_ref, gk_ref, gv_ref, gr_ref,
                 c_ref, tail_ref, shift_ref, wg2_ref, bg_ref,
                 mu_ref, w0_ref, w1_ref, w2_ref, a0_ref, a1_ref, a2_ref, g1_ref, g2_ref,
                 kkp_ref, kap_ref, rkp_ref, ones_ref,
                 kr_ref, kg_ref, kw_ref, va_ref, obonus_ref, og_ref, *, grouped):
    i = pl.program_id(0)
    w = BRANCH_W
    lane = lax.broadcasted_iota(jnp.int32, rq_ref.shape, 1)
    first_half = (lane % HEAD_D) < (HEAD_D // 2)

    def per_row(tab_ref):
        tab = tab_ref[...]
        rows, reps = tab.shape[0], rq_ref.shape[0] // tab.shape[0]
        if reps == 1:
            return tab
        return jnp.broadcast_to(tab[:, None, :], (rows, reps, w)).reshape(rows * reps, w)

    def rot(x, cos_ref, sin_ref):
        partner = jnp.where(first_half,
                            pltpu.roll(x, BRANCH_W - HEAD_D // 2, 1),
                            pltpu.roll(x, HEAD_D // 2, 1))
        return x * per_row(cos_ref) + partner * per_row(sin_ref)

    ret_q = rot(rq_ref[...], cq_ref, sq_ref)
    ret_k = rot(rk_ref[...], ck_ref, sk_ref)

    z = _bdot(gr_ref[...], wg2_ref[...]) + bg_ref[...]
    glog = (jnp.minimum(z, 0.0) - _log1p_exp_neg_abs(z)) / GLA_GATE_NORM
    gla_q = gq_ref[...] * (GLA_DK ** -0.5)
    gla_al = jnp.exp(glog)

    c = c_ref[...]
    tm = c.shape[0]
    b_n = tail_ref.shape[0]
    tail = jnp.where(i == 0, shift_ref[...], tail_ref[...])
    d = jnp.concatenate([tail, c[:tm - b_n]], axis=0) - c
    mu = mu_ref[...]
    cr, ck, cv, cz = (c[:, j * w:(j + 1) * w] for j in range(4))
    dr, dk, dv, dz = (d[:, j * w:(j + 1) * w] for j in range(4))
    r = cr + dr * mu[0:1]
    k = ck + dk * mu[1:2]
    v = cv + dv * mu[2:3]
    zw = cz + dz * mu[3:4]
    za = cz + dz * mu[4:5]
    zg = cz + dz * mu[5:6]
    w_raw = w0_ref[...] + _bdot(jnp.tanh(_bdot(zw, w1_ref[...])), w2_ref[...])
    sp = jnp.maximum(-w_raw, 0.0) + _log1p_exp_neg_abs(w_raw)
    dec = jnp.exp(-jnp.exp(-sp - 0.5))
    a = _sigmoid(a0_ref[...] + _bdot(_bdot(za, a1_ref[...]), a2_ref[...]))
    og_ref[...] = _bdot(_sigmoid(_bdot(zg, g1_ref[...])), g2_ref[...])
    kk = k * kkp_ref[...]
    ss = _seg_dot(kk * kk, ones_ref)
    kk = kk * lax.rsqrt(jnp.maximum(ss, 1e-24))
    km = k * (1.0 + (a - 1.0) * kap_ref[...])
    obonus_ref[...] = _seg_dot(r * km * rkp_ref[...], ones_ref) * v

    if not grouped:
        for ref, ops in ((kr_ref, (ret_q, ret_k)), (kg_ref, (gla_q, gk_ref[...], gla_al)),
                         (kw_ref, (r, km, dec, kk, kk * a)), (va_ref, (rv_ref[...], gv_ref[...], v))):
            cw = ops[0].shape[1]
            for j, o in enumerate(ops):
                ref[j * cw:(j + 1) * cw, :] = o.T
        return

    n_grp = LANES // (HEADS * b_n)

    def emit_tiles(ref, row0, ops, n_rows, wt=None):
        wt = _heads_to_lanes(ops, b_n) if wt is None else wt
        for lt in range(wt.shape[1] // LANES):
            for g, zt in enumerate(_replicate_groups(wt[:n_rows, lt * LANES:(lt + 1) * LANES], n_grp)):
                q = lt * n_grp + g
                ref[row0:row0 + n_rows, q * LANES:(q + 1) * LANES] = zt

    emit_tiles(kr_ref, 0, (ret_q, ret_k), 2 * HEAD_D)
    emit_tiles(kg_ref, 0, (gla_q, gk_ref[...], gla_al), 3 * GLA_DK)
    emit_tiles(kw_ref, 0, (r, km), 2 * HEAD_D)
    emit_tiles(kw_ref, 2 * HEAD_D, (dec, kk), 2 * HEAD_D)
    beta_v = _heads_to_lanes((kk * a, v), b_n)
    emit_tiles(kw_ref, 4 * HEAD_D, None, HEAD_D, wt=beta_v)
    va_ref[0:2 * HEAD_D, :] = _heads_to_lanes((rv_ref[...], gv_ref[...]), b_n)
    va_ref[2 * HEAD_D:3 * HEAD_D, :] = beta_v[HEAD_D:2 * HEAD_D]


def _scan_kernel(kr_ref, kg_ref, kw_ref, va_ref, dec_ref, s0r_ref, s0g_ref, s0w_ref,
                 o_ref, sor_ref, sog_ref, sow_ref,
                 sr, sg, sw, vs, os_, *, vh_n, grouped, tc, b_n):
    ti = pl.program_id(1)

    @pl.when(ti == 0)
    def _():
        sr[...] = s0r_ref[...]
        sg[...] = s0g_ref[...]
        sw[...] = s0w_ref[...]

    n_grp = LANES // (HEADS * b_n) if grouped else 1
    grp_w = LANES // n_grp
    n_v = 3

    def k_tile(ref, n_ops, j, t):
        lanes = pl.ds(pl.multiple_of(t * LANES, LANES), LANES)
        if grouped:
            k_n = ref.shape[0] // n_ops
            return ref[j * k_n:(j + 1) * k_n, lanes]
        return ref[j, 0, :, lanes]

    if grouped:
        grp = lax.broadcasted_iota(jnp.int32, (vh_n, LANES), 1) // grp_w
        for lt in range(tc // n_grp):
            lanes = slice(lt * LANES, (lt + 1) * LANES)
            for j in range(n_v):
                ws = [va_ref[j * HEAD_D + vl * vh_n:j * HEAD_D + (vl + 1) * vh_n, lanes] for vl in range(n_grp)]
                for g in range(n_grp):
                    tile = None
                    for vl in range(n_grp):
                        r = pltpu.roll(ws[vl], ((vl - g) * grp_w) % LANES, 1)
                        tile = r if tile is None else jnp.where(grp == vl, r, tile)
                    vs[lt * n_grp + g, j] = tile
    else:
        def copy_in(t, c):
            lanes = pl.ds(pl.multiple_of(t * LANES, LANES), LANES)
            for j in range(n_v):
                vs[t, j] = va_ref[j, 0, :, lanes]
            return c

        lax.fori_loop(0, tc, copy_in, 0)

    dec_r = dec_ref[...]

    def out(t, j, vh, s, q):
        os_[t, j, pl.ds(vh, 1), :] = jnp.sum(s * q, axis=0, keepdims=True)

    def ret_step(t, c):
        q, k = k_tile(kr_ref, 2, 0, t), k_tile(kr_ref, 2, 1, t)
        for vh in range(vh_n):
            s = sr[vh] + vs[t, 0, pl.ds(vh, 1), :] * k
            sr[vh] = s
            out(t, 0, vh, s, q)
        return c

    def gla_step(t, c):
        q, k, al = (k_tile(kg_ref, 3, j, t) for j in range(3))
        for vh in range(vh_n):
            s = sg[vh] * al + vs[t, 1, pl.ds(vh, 1), :] * k
            sg[vh] = s
            out(t, 1, vh, s, q)
        return c

    def rwkv_step(t, c):
        q, k, dec, kk, beta = (k_tile(kw_ref, 5, j, t) for j in range(5))
        for vh in range(vh_n):
            s = sw[vh]
            sk = jnp.sum(s * kk, axis=0, keepdims=True)
            s = s * dec - sk * beta + vs[t, 2, pl.ds(vh, 1), :] * k
            sw[vh] = s
            out(t, 2, vh, s, q)
        return c

    unroll = 2 if grouped else 1
    lax.fori_loop(0, tc, ret_step, 0, unroll=unroll)
    sr[...] = sr[...] * dec_r
    lax.fori_loop(0, tc, gla_step, 0, unroll=unroll)
    lax.fori_loop(0, tc, rwkv_step, 0, unroll=unroll)

    if grouped:
        grp = lax.broadcasted_iota(jnp.int32, (vh_n, LANES), 1) // grp_w
        for lt in range(tc // n_grp):
            lanes = slice(lt * LANES, (lt + 1) * LANES)
            for j in range(n_v):
                og = [os_[lt * n_grp + g, j] for g in range(n_grp)]
                for vl in range(n_grp):
                    wv = None
                    for g in range(n_grp):
                        r = pltpu.roll(og[g], ((g - vl) * grp_w) % LANES, 1)
                        wv = r if wv is None else jnp.where(grp == g, r, wv)
                    o_ref[j * HEAD_D + vl * vh_n:j * HEAD_D + (vl + 1) * vh_n, lanes] = wv
    else:
        def copy_out(t, c):
            lanes = pl.ds(pl.multiple_of(t * LANES, LANES), LANES)
            for j in range(n_v):
                o_ref[j, 0, :, lanes] = os_[t, j]
            return c

        lax.fori_loop(0, tc, copy_out, 0)

    @pl.when(ti == pl.num_programs(1) - 1)
    def _():
        sor_ref[...] = sr[...]
        sog_ref[...] = sg[...]
        sow_ref[...] = sw[...]


def _scan(kr, kg, kw, va, b_n, t_n, dec, s0r, s0g, s0w):
    grouped = b_n * HEADS < LANES
    vh_n = s0r.shape[0]
    if grouped:
        tc = min(t_n, SCAN_CHUNK)
        n_grp = LANES // (HEADS * b_n)
        assert tc % n_grp == 0 and t_n % tc == 0
        n_l, n_t = 1, t_n // tc
        k_specs = [pl.BlockSpec((a.shape[0], tc * LANES), lambda l, t: (0, t)) for a in (kr, kg, kw)]
        v_spec = pl.BlockSpec((va.shape[0], tc * HEADS * b_n), lambda l, t: (0, t))
        o_spec = v_spec
    else:
        assert b_n == LANES
        tc, n_l, n_t = t_n, HEADS, 1
        k_specs = [pl.BlockSpec((a.shape[0], 1, a.shape[2], tc * LANES), lambda l, t: (0, l, 0, 0))
                   for a in (kr, kg, kw)]
        v_spec = pl.BlockSpec((3, 1, HEAD_D, tc * LANES), lambda l, t: (0, l, 0, 0))
        o_spec = v_spec

    def s_spec(a):
        return pl.BlockSpec(a.shape[:2] + (LANES,), lambda l, t: (0, 0, l))

    return pl.pallas_call(
        functools.partial(_scan_kernel, vh_n=vh_n, grouped=grouped, tc=tc, b_n=b_n),
        grid=(n_l, n_t),
        in_specs=k_specs + [v_spec, pl.BlockSpec((1, LANES), lambda l, t: (0, l)),
                            s_spec(s0r), s_spec(s0g), s_spec(s0w)],
        out_specs=[o_spec, s_spec(s0r), s_spec(s0g), s_spec(s0w)],
        out_shape=[jax.ShapeDtypeStruct(va.shape, F32)] + [jax.ShapeDtypeStruct(s.shape, F32)
                                                           for s in (s0r, s0g, s0w)],
        scratch_shapes=[pltpu.VMEM(s.shape[:2] + (LANES,), F32) for s in (s0r, s0g, s0w)]
                       + [pltpu.VMEM((tc, 3, vh_n, LANES), F32), pltpu.VMEM((tc, 3, vh_n, LANES), F32)],
        compiler_params=_cparams("parallel", "arbitrary"),
        name="scan_ret_gla_rwkv",
    )(kr, kg, kw, va, dec, s0r, s0g, s0w)


def _s5_kernel(u_ref, bb_ref, cc_ref, ar_ref, ai_ref, h0r_ref, h0i_ref,
               y_ref, hr_out, hi_out, hr_scr, hi_scr, xs_scr, hs_scr, *, b_n, tc):
    ti = pl.program_id(0)

    @pl.when(ti == 0)
    def _():
        hr_scr[...] = h0r_ref[...]
        hi_scr[...] = h0i_ref[...]

    xs_scr[...] = _bdot(u_ref[...], bb_ref[...])
    ar = ar_ref[...]
    ai = ai_ref[...]

    def step(t, carry):
        hr, hi = carry
        row = pl.multiple_of(t * b_n, SUBLANES)
        x = xs_scr[pl.ds(row, b_n), :]
        nr = ar * hr - ai * hi + x[:, :S5_CH]
        ni = ar * hi + ai * hr + x[:, S5_CH:]
        hs_scr[pl.ds(row, b_n), :S5_CH] = nr
        hs_scr[pl.ds(row, b_n), S5_CH:] = ni
        return nr, ni

    hr, hi = lax.fori_loop(0, tc, step, (hr_scr[...], hi_scr[...]))
    hr_scr[...] = hr
    hi_scr[...] = hi
    y_ref[...] = _bdot(hs_scr[...], cc_ref[...])

    @pl.when(ti == pl.num_programs(0) - 1)
    def _():
        hr_out[...] = hr
        hi_out[...] = hi


def _s5_scan(cols, row_off, bb, cc, ar, ai, h0r, h0i, b_n, t_n):
    tc = min(t_n, max(1, 1024 // b_n))
    rows = tc * b_n
    assert t_n % tc == 0 and row_off % rows == 0
    blk0 = row_off // rows
    return pl.pallas_call(
        functools.partial(_s5_kernel, b_n=b_n, tc=tc),
        grid=(t_n // tc,),
        in_specs=[pl.BlockSpec((rows, BRANCH_W), lambda t: (blk0 + t, COL_S5 // BRANCH_W)),
                  _const_spec(bb.shape), _const_spec(cc.shape),
                  _const_spec(ar.shape), _const_spec(ai.shape),
                  _const_spec(h0r.shape), _const_spec(h0i.shape)],
        out_specs=[pl.BlockSpec((rows, BRANCH_W), lambda t: (t, 0)),
                   _const_spec(h0r.shape), _const_spec(h0i.shape)],
        out_shape=[jax.ShapeDtypeStruct((t_n * b_n, BRANCH_W), F32),
                   jax.ShapeDtypeStruct(h0r.shape, F32),
                   jax.ShapeDtypeStruct(h0i.shape, F32)],
        scratch_shapes=[pltpu.VMEM((b_n, S5_CH), F32), pltpu.VMEM((b_n, S5_CH), F32),
                        pltpu.VMEM((rows, 2 * S5_CH), F32), pltpu.VMEM((rows, 2 * S5_CH), F32)],
        compiler_params=_cparams("arbitrary"),
        name="scan_s5",
    )(cols, bb, cc, ar, ai, h0r, h0i)


def _post_kernel(x_ref, mixt_ref, rg_ref, gg_ref, sy_ref, su_ref, wbon_ref, wg_ref,
                 gate0_ref, gate1_ref, gate2_ref, gate3_ref,
                 avg_ref, rgn_g, rgn_b, ggn_g, s5d_ref, wglu_ref, wgn_g, wgn_b,
                 wbr_ref, wo_ref, ln_g, ln_b, *rest, alpha, b_n):
    o_ref = rest[-1] if len(rest) == 1 else rest[4]

    def seg_mean(v):
        return _seg_dot(v, avg_ref)

    tm = x_ref.shape[0]
    if b_n * HEADS < LANES:
        yt = mixt_ref[...].T.reshape(tm // b_n, HEADS, b_n, 3 * HEAD_D)
        per_head = [yt[:, h].reshape(tm, 3 * HEAD_D) for h in range(HEADS)]
        mix = jnp.concatenate([ph[:, j * HEAD_D:(j + 1) * HEAD_D] for j in range(3) for ph in per_head], axis=1)
    else:
        mix = mixt_ref[...].T
    ro = mix[:, 0:BRANCH_W]
    mu = seg_mean(ro)
    rc = ro - mu
    var = seg_mean(rc * rc)
    b0 = (rc * lax.rsqrt(var + LN_EPS) * rgn_g[...] + rgn_b[...]) * _silu(rg_ref[...])
    go = mix[:, BRANCH_W:2 * BRANCH_W]
    ms = seg_mean(go * go)
    b1 = go * lax.rsqrt(ms + LN_EPS) * ggn_g[...] * _silu(gg_ref[...])
    y = jax.nn.gelu(sy_ref[...] + s5d_ref[...] * su_ref[...])
    b2 = y * _sigmoid(_bdot(y, wglu_ref[...]))
    wy = mix[:, 2 * BRANCH_W:3 * BRANCH_W]
    mu = seg_mean(wy)
    wc = wy - mu
    var = seg_mean(wc * wc)
    b3 = (wc * lax.rsqrt(var + RWKV_GN_EPS) * wgn_g[...] + wgn_b[...] + wbon_ref[...]) * wg_ref[...]

    m = None
    gates = (gate0_ref, gate1_ref, gate2_ref, gate3_ref)
    for i, br in enumerate((b0, b1, b2, b3)):
        term = _bdot(br, wbr_ref[i]) * _sigmoid(gates[i][...])
        m = term if m is None else m + term
    h = _bdot(m, wo_ref[...])
    x1 = _layer_norm(alpha * x_ref[...] + h, ln_g[...], ln_b[...])
    o_ref[...] = x1
    if len(rest) > 1:
        _route_top2(x1, rest[0], rest[1], rest[5], rest[6])


def _embed_ln2(x, f, p_ref, wpe_ref, wpg_ref, ln_g, ln_b, alpha):
    e = _bdot(p_ref[...], wpe_ref[...]) * _sigmoid(_bdot(x, wpg_ref[...]))
    return _layer_norm(alpha * x + f + e, ln_g[...], ln_b[...])


def _ffn_kernel(x_ref, p_ref, w1_ref, w3_ref, w2_ref, wpe_ref, wpg_ref, ln_g, ln_b, o_ref, *, alpha):
    x = x_ref[...]
    xb = x.astype(BF16)
    h = _silu(jnp.dot(xb, w1_ref[...], preferred_element_type=F32)) * jnp.dot(
        xb, w3_ref[...], preferred_element_type=F32)
    f = _bdot(h, w2_ref[...])
    o_ref[...] = _embed_ln2(x, f, p_ref, wpe_ref, wpg_ref, ln_g, ln_b, alpha)


def _route_top2(x, wh_ref, wl_ref, idx_ref, wgt_ref):
    xh, xl = _split_bf16(x)
    wh = wh_ref[...]
    logits = (jnp.dot(xh, wh, preferred_element_type=F32)
              + jnp.dot(xl, wh, preferred_element_type=F32)
              + jnp.dot(xh, wl_ref[...], preferred_element_type=F32))
    col = lax.broadcasted_iota(jnp.int32, logits.shape, 1)
    neg = jnp.float32(-jnp.inf)
    lg = jnp.where(col < N_EXPERTS, logits, neg)
    m1 = jnp.max(lg, axis=1, keepdims=True)
    i1 = jnp.min(jnp.where(lg == m1, col, LANES), axis=1, keepdims=True)
    lg2 = jnp.where(col == i1, neg, lg)
    m2 = jnp.max(lg2, axis=1, keepdims=True)
    i2 = jnp.min(jnp.where(lg2 == m2, col, LANES), axis=1, keepdims=True)
    e2 = jnp.exp(m2 - m1)
    den = 1.0 + e2
    idx_ref[...] = jnp.where(col == 0, i1, jnp.where(col == 1, i2, 0))
    wgt_ref[...] = jnp.where(col == 0, 1.0 / den, jnp.where(col == 1, e2 / den, 0.0))


def _row_copies(idx_ref, n_rows, make_copy, count=None):
    def each(action):
        def body(r, c):
            action(make_copy(r, idx_ref[0, 0, r]))
            return c

        if count is None:
            lax.fori_loop(0, n_rows, body, 0, unroll=8)
            return

        @pl.when(count == n_rows)
        def _():
            lax.fori_loop(0, n_rows, body, 0, unroll=8)

        @pl.when(count < n_rows)
        def _():
            lax.fori_loop(0, count, body, 0)

    return (lambda: each(lambda cp: cp.start())), (lambda: each(lambda cp: cp.wait()))


def _moe_block_kernel(be_ref, nb_ref, bv_ref, tok_ref, tok_next_ref, dst_ref, dst_m1_ref, dst_m2_ref,
                      x_hbm, w1_ref, w3_ref, w2_ref, out_hbm, xbuf, ybuf, sem_in, sem_out):
    j = pl.program_id(0)
    last = pl.num_programs(0) - 1
    nb = nb_ref[0]
    slot = j % 2

    def gather_copy(idx_ref, half, r):
        return pltpu.make_async_copy(x_hbm.at[pl.ds(idx_ref[0, 0, r], 1)], xbuf.at[half, pl.ds(r, 1)],
                                     sem_in.at[half])

    def gather(idx_ref, half):
        return _row_copies(idx_ref, MOE_BLOCK, lambda r, row: pltpu.make_async_copy(
            x_hbm.at[pl.ds(row, 1)], xbuf.at[half, pl.ds(r, 1)], sem_in.at[half]))

    def scatter(idx_ref, half, block):
        return _row_copies(idx_ref, MOE_BLOCK, lambda r, row: pltpu.make_async_copy(
            ybuf.at[half, pl.ds(r, 1)], out_hbm.at[pl.ds(row, 1)], sem_out.at[half]),
            count=bv_ref[jnp.maximum(block, 0)])

    @pl.when(jnp.logical_and(j == 0, nb > 0))
    def _():
        gather(tok_ref, 0)[0]()

    def step(cur):
        gather(tok_ref, cur)[1]()

        @pl.when(j >= 2)
        def _():
            scatter(dst_m2_ref, cur, j - 2)[1]()

        for r in range(MOE_BLOCK):
            gather_copy(tok_next_ref, 1 - cur, r).start()
        xb = xbuf[cur].astype(BF16)
        h = _silu(jnp.dot(xb, w1_ref[0], preferred_element_type=F32)) * jnp.dot(
            xb, w3_ref[0], preferred_element_type=F32)
        ybuf[cur] = _bdot(h, w2_ref[0])
        scatter(dst_ref, cur, j)[0]()

    for par in range(2):
        pl.when(jnp.logical_and(j < nb, slot == par))(functools.partial(step, par))

    @pl.when(jnp.logical_and(j == nb, nb > 0))
    def _():
        gather(tok_ref, slot)[1]()

    @pl.when(jnp.logical_and(j >= nb, jnp.logical_and(j >= 2, j - 2 < nb)))
    def _():
        scatter(dst_m2_ref, slot, j - 2)[1]()

    @pl.when(jnp.logical_and(j == last, jnp.logical_and(j >= 1, j - 1 < nb)))
    def _():
        scatter(dst_m1_ref, 1 - slot, j - 1)[1]()

    @pl.when(jnp.logical_and(j == last, j < nb))
    def _():
        scatter(dst_ref, slot, j)[1]()


def _moe_blocks(x, block_e, nb_used, block_valid, slot_tok, slot_dst, n_out_rows, w1, w3, w2):
    n_blocks = block_e.shape[0]
    d = x.shape[1]
    dff = w1.shape[2]
    tok = slot_tok.reshape(n_blocks, 1, MOE_BLOCK)
    dst = slot_dst.reshape(n_blocks, 1, MOE_BLOCK)

    def idx_spec(shift):
        return pl.BlockSpec((1, 1, MOE_BLOCK),
                            lambda j, be, nb, bv: (jnp.clip(j + shift, 0, n_blocks - 1), 0, 0),
                            memory_space=pltpu.SMEM)

    grid_spec = pltpu.PrefetchScalarGridSpec(
        num_scalar_prefetch=3,
        grid=(n_blocks + 1,),
        in_specs=[
            idx_spec(0), idx_spec(1), idx_spec(0), idx_spec(-1), idx_spec(-2),
            pl.BlockSpec(memory_space=pl.ANY),
            pl.BlockSpec((1, d, dff), lambda j, be, nb, bv: (be[jnp.minimum(j, n_blocks - 1)], 0, 0),
                         pipeline_mode=pl.Buffered(1)),
            pl.BlockSpec((1, d, dff), lambda j, be, nb, bv: (be[jnp.minimum(j, n_blocks - 1)], 0, 0),
                         pipeline_mode=pl.Buffered(1)),
            pl.BlockSpec((1, dff, d), lambda j, be, nb, bv: (be[jnp.minimum(j, n_blocks - 1)], 0, 0),
                         pipeline_mode=pl.Buffered(1)),
        ],
        out_specs=pl.BlockSpec(memory_space=pl.ANY),
        scratch_shapes=[pltpu.VMEM((2, MOE_BLOCK, d), F32), pltpu.VMEM((2, MOE_BLOCK, d), F32),
                        pltpu.SemaphoreType.DMA((2,)), pltpu.SemaphoreType.DMA((2,))],
    )
    return pl.pallas_call(
        _moe_block_kernel,
        grid_spec=grid_spec,
        out_shape=jax.ShapeDtypeStruct((n_out_rows, d), F32),
        compiler_params=_cparams("arbitrary"),
        name="moe_blocks",
    )(block_e, nb_used, block_valid, tok, tok, dst, dst, dst, x, w1, w3, w2)


def _moe_combine_kernel(y0_ref, y1_ref, x_ref, p_ref, wgt_ref, wpe_ref, wpg_ref, ln_g, ln_b, o_ref, *, alpha):
    wgt = wgt_ref[...]
    f = y0_ref[...] * wgt[:, 0:1] + y1_ref[...] * wgt[:, 1:2]
    o_ref[...] = _embed_ln2(x_ref[...], f, p_ref, wpe_ref, wpg_ref, ln_g, ln_b, alpha)


def _lane_groups(b_n):
    bh = b_n * HEADS
    return LANES // bh if bh < LANES else 1


def _state_to_lanes(s, vl_n, value_last):
    s = s.transpose(3, 2, 1, 0) if value_last else s.transpose(2, 3, 1, 0)
    v_n, k_n, h_n, b_n = s.shape
    s = s.reshape(vl_n, v_n // vl_n, k_n, h_n, b_n).transpose(1, 2, 0, 3, 4)
    return s.reshape(v_n // vl_n, k_n, vl_n * h_n * b_n)


def _state_from_lanes(s, b_n, vl_n, value_last):
    vh_n, k_n, _ = s.shape
    s = s.reshape(vh_n, k_n, vl_n, HEADS, b_n).transpose(4, 3, 1, 2, 0).reshape(b_n, HEADS, k_n, vl_n * vh_n)
    return s if value_last else s.transpose(0, 1, 3, 2)


def _rotary_tables(pos, row_repeat, gamma, tc):
    half = HEAD_D // 2
    freq = ROPE_BASE ** (-jnp.arange(half, dtype=F32) / half)
    ang = pos.astype(F32)[:, None] * freq[None, :]
    cos, sin = jnp.cos(ang), jnp.sin(ang)
    cos_h = jnp.tile(jnp.concatenate([cos, cos], axis=-1), (1, HEADS))
    sin_h = jnp.tile(jnp.concatenate([-sin, sin], axis=-1), (1, HEADS))
    tau1 = (jnp.arange(pos.shape[0], dtype=jnp.int32) % tc + 1).astype(F32)
    f_q = jnp.repeat(gamma[None, :] ** tau1[:, None], HEAD_D, axis=1)
    f_k = (HEAD_D ** -0.5) / f_q
    tabs = (cos_h * f_q, sin_h * f_q, cos_h * f_k, sin_h * f_k)
    return tuple(jnp.repeat(t, row_repeat, axis=0) for t in tabs) if row_repeat > 1 else tabs


def _block_diag_const(block, n_blocks):
    return jnp.kron(jnp.eye(n_blocks, dtype=F32), jnp.full((block, block), 1.0, F32))


def _s5_params(log_dt, a_re, a_im, b_re, b_im, c_re, c_im):
    dt = jnp.exp(log_dt)[:, None]
    mag = jnp.exp(dt * a_re)
    ang = dt * a_im
    abar_re, abar_im = mag * jnp.cos(ang), mag * jnp.sin(ang)
    den = a_re * a_re + a_im * a_im
    n_re = abar_re - 1.0
    f_re = (n_re * a_re + abar_im * a_im) / den
    f_im = (abar_im * a_re - n_re * a_im) / den
    bb_re = f_re[..., None] * b_re - f_im[..., None] * b_im
    bb_im = f_re[..., None] * b_im + f_im[..., None] * b_re
    eye = jnp.eye(S5_GROUPS, dtype=F32)

    def in_map(bb):
        return jnp.einsum("gpc,gh->gchp", bb, eye).reshape(BRANCH_W, S5_CH)

    def out_map(cm):
        return jnp.einsum("gcp,gh->gphc", cm, eye).reshape(S5_CH, BRANCH_W)

    bb = jnp.concatenate([in_map(bb_re), in_map(bb_im)], axis=1).astype(BF16)
    cc = jnp.concatenate([out_map(c_re), -out_map(c_im)], axis=0).astype(BF16)
    return bb, cc, abar_re.reshape(1, S5_CH), abar_im.reshape(1, S5_CH)


def _reorder_w_in(w_in):
    d_model = w_in.shape[0]
    main = jnp.concatenate([w_in[:, :1792], w_in[:, 1808:]], axis=1)
    tail = jnp.concatenate([w_in[:, 1792:1808],
                            jnp.zeros((d_model, D_IN_PAD - COL_GLA_R - GLA_GATE_RANK), w_in.dtype)], axis=1)
    return jnp.concatenate([main, tail], axis=1).astype(BF16)


def _pad_cols(w, n):
    return jnp.pad(w, ((0, 0), (0, n - w.shape[1])))


def _pad_rows(w, n):
    return jnp.pad(w, ((0, n - w.shape[0]), (0, 0)))


def _moe_route(idx, n):
    nk = n * TOP_K
    flat_e = idx.reshape(nk)
    onehot = (flat_e[:, None] == jnp.arange(N_EXPERTS, dtype=jnp.int32)[None, :]).astype(jnp.int32)
    incl = jnp.cumsum(onehot, axis=0)
    counts = incl[-1]
    rank = jnp.sum((incl - onehot) * onehot, axis=1)
    padded = (counts + MOE_BLOCK - 1) // MOE_BLOCK * MOE_BLOCK
    pad_end = jnp.cumsum(padded)
    slot = (pad_end - padded)[flat_e] + rank
    n_blocks = -(-(nk + N_EXPERTS * (MOE_BLOCK - 1)) // MOE_BLOCK)
    cap = n_blocks * MOE_BLOCK
    pair = jnp.full((cap,), -1, jnp.int32).at[slot].set(jnp.arange(nk, dtype=jnp.int32))
    real = pair >= 0
    slot_tok = jnp.where(real, pair // TOP_K, 0)
    slot_dst = jnp.where(real, (pair % TOP_K) * n + pair // TOP_K, 0)
    block_start = jnp.arange(n_blocks, dtype=jnp.int32) * MOE_BLOCK
    block_e = jnp.minimum(jnp.sum((pad_end[None, :] <= block_start[:, None]).astype(jnp.int32), axis=1),
                          N_EXPERTS - 1).astype(jnp.int32)
    nb_used = (pad_end[-1] // MOE_BLOCK).astype(jnp.int32).reshape(1)
    block_valid = jnp.sum(real.reshape(n_blocks, MOE_BLOCK).astype(jnp.int32), axis=1)
    return slot_tok, slot_dst, block_e, nb_used, block_valid, nk


def kernel(x_prompt, x_sample, state_ret, state_gla, state_s5_re, state_s5_im, state_rwkv, state_shift,
           p_prompt, p_sample, w_in, ret_gn_g, ret_gn_b, gla_wg2, gla_bg, gla_gn,
           s5_log_dt, s5_a_re, s5_a_im, s5_b_re, s5_b_im, s5_c_re, s5_c_im, s5_d, s5_w_glu,
           rwkv_mu, rwkv_w0, rwkv_w1, rwkv_w2, rwkv_a0, rwkv_a1, rwkv_a2, rwkv_g1, rwkv_g2,
           rwkv_kk, rwkv_ka, rwkv_rk, rwkv_gn_g, rwkv_gn_b, w_branch, w_o,
           ln1_g, ln1_b, ln2_g, ln2_b, w_pe, w_pg, ffn_w1, ffn_w3, ffn_w2,
           moe_router, moe_w1, moe_w3, moe_w2):
    depth = w_in.shape[0]
    bp, tp, d_model = x_prompt.shape
    bs, ts, _ = x_sample.shape
    n_p, n_s = bp * tp, bs * ts
    n = n_p + n_s
    tm = ROW_TILE
    assert n_p % tm == 0 and n_s % tm == 0 and tm % bp == 0 and tm % bs == 0 and n_p % bs == 0
    alpha = (2 * depth) ** 0.25
    groups = [(0, bp, tp), (n_p, bs, ts)]
    w4 = 4 * BRANCH_W

    def time_major(a_p, a_s):
        return jnp.concatenate([a_p.transpose(1, 0, 2).reshape(n_p, -1),
                                a_s.transpose(1, 0, 2).reshape(n_s, -1)], axis=0)

    x = time_major(x_prompt, x_sample)
    ones_bd = _block_diag_const(HEAD_D, HEADS).astype(BF16)
    avg_bd = (_block_diag_const(HEAD_D, HEADS) / HEAD_D).astype(BF16)
    gamma = 1.0 - jnp.exp2(-5.0 - jnp.arange(HEADS, dtype=F32))
    tc_p, tc_s = min(tp, SCAN_CHUNK), ts
    rot_p = _rotary_tables(jnp.arange(tp, dtype=jnp.int32), 1 if tm // bp >= SUBLANES else bp, gamma, tc_p)
    rot_s = _rotary_tables(PAST_LEN + jnp.arange(ts, dtype=jnp.int32), 1 if tm // bs >= SUBLANES else bs,
                           gamma, tc_s)
    row = lambda v: v.reshape(1, -1)

    new = [[] for _ in range(6)]
    for i in range(depth):
        cols = _matmul(x, _reorder_w_in(w_in[i]), tm=512, tn=D_IN_PAD // 3, name="in_proj")

        prep_consts = [
            _pad_rows(_pad_cols(gla_wg2[i], LANES), LANES).astype(BF16), row(gla_bg[i]),
            rwkv_mu[i], row(rwkv_w0[i]),
            _pad_cols(rwkv_w1[i], LANES).astype(BF16), _pad_rows(rwkv_w2[i], LANES).astype(BF16),
            row(rwkv_a0[i]),
            _pad_cols(rwkv_a1[i], LANES).astype(BF16), _pad_rows(rwkv_a2[i], LANES).astype(BF16),
            _pad_cols(rwkv_g1[i], LANES).astype(BF16), _pad_rows(rwkv_g2[i], LANES).astype(BF16),
            row(rwkv_kk[i]), row(rwkv_ka[i]), row(rwkv_rk[i]), ones_bd,
        ]
        post_consts = [avg_bd, row(ret_gn_g[i]), row(ret_gn_b[i]), row(gla_gn[i]), row(s5_d[i]),
                       s5_w_glu[i].astype(BF16), row(rwkv_gn_g[i]), row(rwkv_gn_b[i]),
                       w_branch[i].astype(BF16), w_o[i].astype(BF16), row(ln1_g[i]), row(ln1_b[i])]
        s5p = _s5_params(s5_log_dt[i], s5_a_re[i], s5_a_im[i], s5_b_re[i], s5_b_im[i], s5_c_re[i], s5_c_im[i])
        rwkv_blk = COL_RWKV // w4
        x1, layer_new = x, []
        routed = i % 2 == 1
        if routed:
            router_hi, router_lo = _split_bf16(_pad_cols(moe_router[i // 2], LANES))
            idx, wgt = jnp.zeros((n, LANES), jnp.int32), jnp.zeros((n, LANES), F32)
        for gi, (off, b_n, t_n) in enumerate(groups):
            n_g = b_n * t_n
            t0 = off // tm
            if gi == 0:
                st = [jnp.zeros((b_n,) + s.shape[2:], s.dtype)
                      for s in (state_ret, state_gla, state_s5_re, state_s5_im, state_rwkv)]
                shift0, rot_g, tc_g = jnp.zeros((b_n, w4), F32), rot_p, tc_p
            else:
                st = [state_ret[i], state_gla[i], state_s5_re[i], state_s5_im[i], state_rwkv[i]]
                shift0, rot_g, tc_g = state_shift[i], rot_s, tc_s
            vl_n = _lane_groups(b_n)
            grouped = vl_n > 1

            def cspec(width, cb, t0=t0):
                return pl.BlockSpec((tm, width), lambda r: (r + t0, cb))

            if grouped:
                t_shapes = [(2 * HEAD_D, n_g * vl_n * HEADS), (3 * GLA_DK, n_g * vl_n * HEADS),
                            (5 * HEAD_D, n_g * vl_n * HEADS), (3 * HEAD_D, n_g * HEADS)]
            else:
                t_shapes = [(2 * BRANCH_W, n_g), (3 * GLA_QK, n_g), (5 * BRANCH_W, n_g), (3 * BRANCH_W, n_g)]
            consts = [shift0] + prep_consts
            ka_ret, ka_gla, ka_rwkv, va, obonus, og = pl.pallas_call(
                functools.partial(_prep_kernel, grouped=grouped),
                grid=(n_g // tm,),
                in_specs=[cspec(BRANCH_W, 0), cspec(BRANCH_W, 1), cspec(BRANCH_W, 2),
                          ] + [_row_spec(rot_g[0].shape[0] // (n_g // tm), BRANCH_W, 0)] * 4 + [
                          cspec(GLA_QK, COL_GLA_Q // GLA_QK), cspec(GLA_QK, COL_GLA_K // GLA_QK),
                          cspec(BRANCH_W, COL_GLA_V // BRANCH_W), cspec(LANES, COL_GLA_R // LANES),
                          cspec(w4, rwkv_blk),
                          pl.BlockSpec((b_n, w4), lambda r, t0=t0, b_n=b_n:
                                       (jnp.maximum((r + t0) * (tm // b_n) - 1, 0), rwkv_blk))]
                         + [_const_spec(a.shape) for a in consts],
                out_specs=[pl.BlockSpec((sh[0], sh[1] // (n_g // tm)), lambda r: (0, r)) for sh in t_shapes]
                          + [_row_spec(tm, BRANCH_W, 0), _row_spec(tm, BRANCH_W, 0)],
                out_shape=[jax.ShapeDtypeStruct(sh, F32) for sh in t_shapes]
                          + [jax.ShapeDtypeStruct((n_g, BRANCH_W), F32)] * 2,
                compiler_params=_cparams("parallel"),
                name="mixer_prep",
            )(cols, cols, cols, *rot_g, cols, cols, cols, cols, cols, cols, *consts)

            dec = jnp.tile(jnp.repeat(gamma ** tc_g, b_n), vl_n).reshape(1, vl_n * HEADS * b_n)
            if not grouped:
                ka_ret = ka_ret.reshape(2, HEADS, HEAD_D, n_g)
                ka_gla = ka_gla.reshape(3, HEADS, GLA_DK, n_g)
                ka_rwkv = ka_rwkv.reshape(5, HEADS, HEAD_D, n_g)
                va = va.reshape(3, HEADS, HEAD_D, n_g)
            o, s_ret, s_gla, s_rwkv = _scan(
                ka_ret, ka_gla, ka_rwkv, va, b_n, t_n, dec,
                _state_to_lanes(st[0], vl_n, True), _state_to_lanes(st[1], vl_n, True),
                _state_to_lanes(st[4], vl_n, False))
            mixt = o if grouped else o.reshape(3 * BRANCH_W, n_g)
            y, hr, hi = _s5_scan(cols, off, *s5p, st[2].reshape(b_n, S5_CH), st[3].reshape(b_n, S5_CH), b_n, t_n)
            shift_new = cols[off + (t_n - 1) * b_n:off + t_n * b_n, COL_RWKV:COL_RWKV + w4]
            layer_new.append((_state_from_lanes(s_ret, b_n, vl_n, True), _state_from_lanes(s_gla, b_n, vl_n, True),
                              hr.reshape(b_n, S5_GROUPS, S5_STATE), hi.reshape(b_n, S5_GROUPS, S5_STATE),
                              _state_from_lanes(s_rwkv, b_n, vl_n, False), shift_new))

            post_in = [x1, mixt, cols, cols, y, cols, obonus, og, cols, cols, cols, cols, *post_consts]
            post_specs = ([cspec(d_model, 0),
                           pl.BlockSpec((mixt.shape[0], mixt.shape[1] // (n_g // tm)), lambda r: (0, r)),
                           cspec(BRANCH_W, 3), cspec(BRANCH_W, COL_GLA_G // BRANCH_W),
                           _row_spec(tm, BRANCH_W, 0), cspec(BRANCH_W, COL_S5 // BRANCH_W),
                           _row_spec(tm, BRANCH_W, 0), _row_spec(tm, BRANCH_W, 0)]
                          + [cspec(d_model, COL_GATE // d_model + q) for q in range(N_BRANCH)]
                          + [_const_spec(a.shape) for a in post_consts])
            out_specs, out_shape, aliases = [cspec(d_model, 0)], [jax.ShapeDtypeStruct((n, d_model), F32)], {0: 0}
            if routed:
                aliases.update({len(post_in) + 2: 1, len(post_in) + 3: 2})
                post_in += [router_hi, router_lo, idx, wgt]
                post_specs += [_const_spec(router_hi.shape), _const_spec(router_lo.shape),
                               cspec(LANES, 0), cspec(LANES, 0)]
                out_specs += [cspec(LANES, 0), cspec(LANES, 0)]
                out_shape += [jax.ShapeDtypeStruct((n, LANES), jnp.int32), jax.ShapeDtypeStruct((n, LANES), F32)]
            res = pl.pallas_call(
                functools.partial(_post_kernel, alpha=alpha, b_n=b_n),
                grid=(n_g // tm,),
                in_specs=post_specs,
                out_specs=out_specs,
                out_shape=out_shape,
                input_output_aliases=aliases,
                compiler_params=_cparams("parallel"),
                name="mixer_post",
            )(*post_in)
            x1 = res[0]
            if routed:
                idx, wgt = res[1], res[2]
        for lst, pair in zip(new, zip(*layer_new)):
            lst.append(pair)

        if i + 1 < depth:
            p_in = [time_major(p_prompt[i], p_sample[i])]
        else:
            p_in = [p_prompt[i].transpose(1, 0, 2).reshape(n_p, -1), p_sample[i].transpose(1, 0, 2).reshape(n_s, -1)]
        tail_consts = [w_pe[i].astype(BF16), w_pg[i].astype(BF16), row(ln2_g[i]), row(ln2_b[i])]
        j = i // 2
        spans = [(0, n, 0)] if i + 1 < depth else [(0, n_p, 0), (n_p // tm, n_s, 0)]
        if i % 2 == 0:
            ffn_consts = [ffn_w1[j].astype(BF16), ffn_w3[j].astype(BF16), ffn_w2[j].astype(BF16)] + tail_consts
            xs = [_rowwise(functools.partial(_ffn_kernel, alpha=alpha), rows,
                           [(x1, d_model, 0, r0), (p, p.shape[1], 0, p0)],
                           ffn_consts, [d_model], name="ffn")[0] for (r0, rows, p0), p in zip(spans, p_in)]
        else:
            slot_tok, slot_dst, block_e, nb_used, block_valid, n_out_rows = _moe_route(idx[:, :TOP_K], n)
            yk = _moe_blocks(x1, block_e, nb_used, block_valid, slot_tok, slot_dst, n_out_rows,
                             moe_w1[j].astype(BF16), moe_w3[j].astype(BF16), moe_w2[j].astype(BF16))
            xs = [_rowwise(functools.partial(_moe_combine_kernel, alpha=alpha), rows,
                           [(yk, d_model, 0, r0), (yk, d_model, 0, n // tm + r0), (x1, d_model, 0, r0),
                            (p, p.shape[1], 0, p0), (wgt, LANES, 0, r0)],
                           tail_consts, [d_model], name="moe_combine")[0] for (r0, rows, p0), p in zip(spans, p_in)]
        x = xs[0]

    y_prompt = xs[0].reshape(tp, bp, d_model).transpose(1, 0, 2)
    y_sample = xs[1].reshape(ts, bs, d_model).transpose(1, 0, 2)
    outs = [y_prompt, y_sample]
    for lst in new:
        outs.append(jnp.stack([pair[0] for pair in lst], 0))
        outs.append(jnp.stack([pair[1] for pair in lst], 0))
    return tuple(outs)
```

```python
import functools

import jax
import jax.numpy as jnp
from jax import lax
from jax.experimental import pallas as pl
from jax.experimental.pallas import tpu as pltpu

F32 = jnp.float32
BF16 = jnp.bfloat16

LANES = 128
SUBLANES = 8
VMEM_LIMIT = 56 * 1024 * 1024

N_BRANCH = 4
BRANCH_W = 256
HEADS = 4
HEAD_D = 64
GLA_DK = 32
GLA_QK = HEADS * GLA_DK
GLA_GATE_RANK = 16
GLA_GATE_NORM = 16.0
S5_GROUP = 16
S5_GROUPS = 16
S5_STATE = 64
S5_CH = S5_GROUPS * S5_STATE
ROPE_BASE = 10000.0
RWKV_GN_EPS = 64e-5
LN_EPS = 1e-5
N_EXPERTS = 8
TOP_K = 2
MOE_BLOCK = 256
ROW_TILE = 256
SCAN_CHUNK = 32
PAST_LEN = 16384

COL_RET = 0
COL_GLA_Q = 1024
COL_GLA_K = 1152
COL_GLA_V = 1280
COL_GLA_G = 1536
COL_S5 = 1792
COL_RWKV = 2048
COL_GATE = 3072
COL_GLA_R = 7168
D_IN_PAD = 7296


def _cparams(*sem):
    return pltpu.CompilerParams(dimension_semantics=sem, vmem_limit_bytes=VMEM_LIMIT)


def _split_bf16(x):
    hi = x.astype(BF16)
    lo = (x - hi.astype(F32)).astype(BF16)
    return hi, lo


def _seg_dot(x, m_ref):
    hi, lo = _split_bf16(x)
    m = m_ref[...]
    return (jnp.dot(hi, m, preferred_element_type=F32)
            + jnp.dot(lo, m, preferred_element_type=F32))


def _bdot(x, w):
    return jnp.dot(x.astype(BF16), w, preferred_element_type=F32)


def _sigmoid(x):
    return 1.0 / (1.0 + jnp.exp(-x))


def _silu(x):
    return x * _sigmoid(x)


def _log1p_exp_neg_abs(x):
    return jnp.log1p(jnp.exp(-jnp.abs(x)))


def _layer_norm(x, g, b):
    mu = jnp.mean(x, axis=-1, keepdims=True)
    xc = x - mu
    var = jnp.mean(xc * xc, axis=-1, keepdims=True)
    return xc * lax.rsqrt(var + LN_EPS) * g + b


def _row_spec(tm, width, col_block, row_block0=0):
    return pl.BlockSpec((tm, width), lambda i, cb=col_block, r0=row_block0: (i + r0, cb))


def _const_spec(shape):
    nd = len(shape)
    return pl.BlockSpec(shape, lambda i, nd=nd: (0,) * nd)


def _rowwise(body, n_rows, row_in, const_in, out_widths, out_dtypes=None, tm=ROW_TILE, name=None):
    assert n_rows % tm == 0
    out_dtypes = out_dtypes or [F32] * len(out_widths)
    in_specs = [_row_spec(tm, *spec[1:]) for spec in row_in] + [_const_spec(a.shape) for a in const_in]
    out_specs = [_row_spec(tm, w, 0) for w in out_widths]
    out_shape = [jax.ShapeDtypeStruct((n_rows, w), dt) for w, dt in zip(out_widths, out_dtypes)]
    return pl.pallas_call(
        body,
        grid=(n_rows // tm,),
        in_specs=in_specs,
        out_specs=out_specs,
        out_shape=out_shape,
        compiler_params=_cparams("parallel"),
        name=name,
    )(*[spec[0] for spec in row_in], *const_in)


def _matmul_kernel(x_ref, w_ref, o_ref):
    o_ref[...] = _bdot(x_ref[...], w_ref[...]).astype(o_ref.dtype)


def _matmul(x, w, tm=512, tn=None, out_dtype=F32, name=None):
    m, k = x.shape
    n = w.shape[1]
    tn = tn or n
    tm = tm if m % tm == 0 else ROW_TILE
    assert m % tm == 0 and n % tn == 0
    return pl.pallas_call(
        _matmul_kernel,
        grid=(n // tn, m // tm),
        in_specs=[pl.BlockSpec((tm, k), lambda j, i: (i, 0)),
                  pl.BlockSpec((k, tn), lambda j, i: (0, j))],
        out_specs=pl.BlockSpec((tm, tn), lambda j, i: (i, j)),
        out_shape=jax.ShapeDtypeStruct((m, n), out_dtype),
        compiler_params=_cparams("parallel", "parallel"),
        name=name,
    )(x, w)


def _replicate_groups(w, n_grp):
    if n_grp == 1:
        return [w]
    grp = lax.broadcasted_iota(jnp.int32, w.shape, 1) // (LANES // n_grp)
    parts = [w]
    span = n_grp
    while span > 1:
        half = span // 2
        shift = half * (LANES // n_grp)
        low = (grp % span) < half
        nxt = []
        for z in parts:
            rz = pltpu.roll(z, shift, 1)
            nxt.append(jnp.where(low, z, rz))
            nxt.append(jnp.where(low, rz, z))
        parts = nxt
        span = half
    return parts


def _heads_to_lanes(ops, b_n, pad_to=LANES):
    tm = ops[0].shape[0]
    c = ops[0].shape[1] // HEADS
    packed = []
    for h in range(HEADS):
        pieces = [o[:, h * c:(h + 1) * c] for o in ops]
        if len(pieces) * c < pad_to:
            pieces.append(jnp.zeros((tm, pad_to - len(pieces) * c), F32))
        packed.append(jnp.concatenate(pieces, axis=1).reshape(tm // b_n, b_n, pad_to))
    y = jnp.stack(packed, axis=1).reshape(tm * HEADS, pad_to)
    return y.T


def _prep_kernel(rq_ref, rk_ref, rv_ref, cq_ref, sq_ref, ck_ref, sk_ref,
                 gq_ref, gk_ref, gv_ref, gr_ref,
                 c_ref, tail_ref, shift_ref, wg2_ref, bg_ref,
                 mu_ref, w0_ref, w1_ref, w2_ref, a0_ref, a1_ref, a2_ref, g1_ref, g2_ref,
                 kkp_ref, kap_ref, rkp_ref, ones_ref,
                 kr_ref, kg_ref, kw_ref, va_ref, obonus_ref, og_ref, *, grouped):
    i = pl.program_id(0)
    w = BRANCH_W
    lane = lax.broadcasted_iota(jnp.int32, rq_ref.shape, 1)
    first_half = (lane % HEAD_D) < (HEAD_D // 2)

    def per_row(tab_ref):
        tab = tab_ref[...]
        rows, reps = tab.shape[0], rq_ref.shape[0] // tab.shape[0]
        if reps == 1:
            return tab
        return jnp.broadcast_to(tab[:, None, :], (rows, reps, w)).reshape(rows * reps, w)

    def rot(x, cos_ref, sin_ref):
        partner = jnp.where(first_half,
                            pltpu.roll(x, BRANCH_W - HEAD_D // 2, 1),
                            pltpu.roll(x, HEAD_D // 2, 1))
        return x * per_row(cos_ref) + partner * per_row(sin_ref)

    ret_q = rot(rq_ref[...], cq_ref, sq_ref)
    ret_k = rot(rk_ref[...], ck_ref, sk_ref)

    z = _bdot(gr_ref[...], wg2_ref[...]) + bg_ref[...]
    glog = (jnp.minimum(z, 0.0) - _log1p_exp_neg_abs(z)) / GLA_GATE_NORM
    gla_q = gq_ref[...] * (GLA_DK ** -0.5)
    gla_al = jnp.exp(glog)

    c = c_ref[...]
    tm = c.shape[0]
    b_n = tail_ref.shape[0]
    tail = jnp.where(i == 0, shift_ref[...], tail_ref[...])
    d = jnp.concatenate([tail, c[:tm - b_n]], axis=0) - c
    mu = mu_ref[...]
    cr, ck, cv, cz = (c[:, j * w:(j + 1) * w] for j in range(4))
    dr, dk, dv, dz = (d[:, j * w:(j + 1) * w] for j in range(4))
    r = cr + dr * mu[0:1]
    k = ck + dk * mu[1:2]
    v = cv + dv * mu[2:3]
    zw = cz + dz * mu[3:4]
    za = cz + dz * mu[4:5]
    zg = cz + dz * mu[5:6]
    w_raw = w0_ref[...] + _bdot(jnp.tanh(_bdot(zw, w1_ref[...])), w2_ref[...])
    sp = jnp.maximum(-w_raw, 0.0) + _log1p_exp_neg_abs(w_raw)
    dec = jnp.exp(-jnp.exp(-sp - 0.5))
    a = _sigmoid(a0_ref[...] + _bdot(_bdot(za, a1_ref[...]), a2_ref[...]))
    og_ref[...] = _bdot(_sigmoid(_bdot(zg, g1_ref[...])), g2_ref[...])
    kk = k * kkp_ref[...]
    ss = _seg_dot(kk * kk, ones_ref)
    kk = kk * lax.rsqrt(jnp.maximum(ss, 1e-24))
    km = k * (1.0 + (a - 1.0) * kap_ref[...])
    obonus_ref[...] = _seg_dot(r * km * rkp_ref[...], ones_ref) * v

    if not grouped:
        for ref, ops in ((kr_ref, (ret_q, ret_k)), (kg_ref, (gla_q, gk_ref[...], gla_al)),
                         (kw_ref, (r, km, dec, kk, kk * a)), (va_ref, (rv_ref[...], gv_ref[...], v))):
            cw = ops[0].shape[1]
            for j, o in enumerate(ops):
                ref[j * cw:(j + 1) * cw, :] = o.T
        return

    n_grp = LANES // (HEADS * b_n)

    def emit_tiles(ref, row0, ops, n_rows, wt=None):
        wt = _heads_to_lanes(ops, b_n) if wt is None else wt
        for lt in range(wt.shape[1] // LANES):
            for g, zt in enumerate(_replicate_groups(wt[:n_rows, lt * LANES:(lt + 1) * LANES], n_grp)):
                q = lt * n_grp + g
                ref[row0:row0 + n_rows, q * LANES:(q + 1) * LANES] = zt

    emit_tiles(kr_ref, 0, (ret_q, ret_k), 2 * HEAD_D)
    emit_tiles(kg_ref, 0, (gla_q, gk_ref[...], gla_al), 3 * GLA_DK)
    emit_tiles(kw_ref, 0, (r, km), 2 * HEAD_D)
    emit_tiles(kw_ref, 2 * HEAD_D, (dec, kk), 2 * HEAD_D)
    beta_v = _heads_to_lanes((kk * a, v), b_n)
    emit_tiles(kw_ref, 4 * HEAD_D, None, HEAD_D, wt=beta_v)
    va_ref[0:2 * HEAD_D, :] = _heads_to_lanes((rv_ref[...], gv_ref[...]), b_n)
    va_ref[2 * HEAD_D:3 * HEAD_D, :] = beta_v[HEAD_D:2 * HEAD_D]


def _scan_kernel(kr_ref, kg_ref, kw_ref, va_ref, dec_ref, s0r_ref, s0g_ref, s0w_ref,
                 o_ref, sor_ref, sog_ref, sow_ref,
                 sr, sg, sw, vs, os_, *, vh_n, grouped, tc, b_n):
    ti = pl.program_id(1)

    @pl.when(ti == 0)
    def _():
        sr[...] = s0r_ref[...]
        sg[...] = s0g_ref[...]
        sw[...] = s0w_ref[...]

    n_grp = LANES // (HEADS * b_n) if grouped else 1
    grp_w = LANES // n_grp
    n_v = 3

    def k_tile(ref, n_ops, j, t):
        lanes = pl.ds(pl.multiple_of(t * LANES, LANES), LANES)
        if grouped:
            k_n = ref.shape[0] // n_ops
            return ref[j * k_n:(j + 1) * k_n, lanes]
        return ref[j, 0, :, lanes]

    if grouped:
        grp = lax.broadcasted_iota(jnp.int32, (vh_n, LANES), 1) // grp_w
        for lt in range(tc // n_grp):
            lanes = slice(lt * LANES, (lt + 1) * LANES)
            for j in range(n_v):
                ws = [va_ref[j * HEAD_D + vl * vh_n:j * HEAD_D + (vl + 1) * vh_n, lanes] for vl in range(n_grp)]
                for g in range(n_grp):
                    tile = None
                    for vl in range(n_grp):
                        r = pltpu.roll(ws[vl], ((vl - g) * grp_w) % LANES, 1)
                        tile = r if tile is None else jnp.where(grp == vl, r, tile)
                    vs[lt * n_grp + g, j] = tile
    else:
        def copy_in(t, c):
            lanes = pl.ds(pl.multiple_of(t * LANES, LANES), LANES)
            for j in range(n_v):
                vs[t, j] = va_ref[j, 0, :, lanes]
            return c

        lax.fori_loop(0, tc, copy_in, 0)

    dec_r = dec_ref[...]

    def out(t, j, vh, s, q):
        os_[t, j, pl.ds(vh, 1), :] = jnp.sum(s * q, axis=0, keepdims=True)

    def ret_step(t, c):
        q, k = k_tile(kr_ref, 2, 0, t), k_tile(kr_ref, 2, 1, t)
        for vh in range(vh_n):
            s = sr[vh] + vs[t, 0, pl.ds(vh, 1), :] * k
            sr[vh] = s
            out(t, 0, vh, s, q)
        return c

    def gla_step(t, c):
        q, k, al = (k_tile(kg_ref, 3, j, t) for j in range(3))
        for vh in range(vh_n):
            s = sg[vh] * al + vs[t, 1, pl.ds(vh, 1), :] * k
            sg[vh] = s
            out(t, 1, vh, s, q)
        return c

    def rwkv_step(t, c):
        q, k, dec, kk, beta = (k_tile(kw_ref, 5, j, t) for j in range(5))
        for vh in range(vh_n):
            s = sw[vh]
            sk = jnp.sum(s * kk, axis=0, keepdims=True)
            s = s * dec - sk * beta + vs[t, 2, pl.ds(vh, 1), :] * k
            sw[vh] = s
            out(t, 2, vh, s, q)
        return c

    unroll = 2 if grouped else 1
    lax.fori_loop(0, tc, ret_step, 0, unroll=unroll)
    sr[...] = sr[...] * dec_r
    lax.fori_loop(0, tc, gla_step, 0, unroll=unroll)
    lax.fori_loop(0, tc, rwkv_step, 0, unroll=unroll)

    if grouped:
        grp = lax.broadcasted_iota(jnp.int32, (vh_n, LANES), 1) // grp_w
        for lt in range(tc // n_grp):
            lanes = slice(lt * LANES, (lt + 1) * LANES)
            for j in range(n_v):
                og = [os_[lt * n_grp + g, j] for g in range(n_grp)]
                for vl in range(n_grp):
                    wv = None
                    for g in range(n_grp):
                        r = pltpu.roll(og[g], ((g - vl) * grp_w) % LANES, 1)
                        wv = r if wv is None else jnp.where(grp == g, r, wv)
                    o_ref[j * HEAD_D + vl * vh_n:j * HEAD_D + (vl + 1) * vh_n, lanes] = wv
    else:
        def copy_out(t, c):
            lanes = pl.ds(pl.multiple_of(t * LANES, LANES), LANES)
            for j in range(n_v):
                o_ref[j, 0, :, lanes] = os_[t, j]
            return c

        lax.fori_loop(0, tc, copy_out, 0)

    @pl.when(ti == pl.num_programs(1) - 1)
    def _():
        sor_ref[...] = sr[...]
        sog_ref[...] = sg[...]
        sow_ref[...] = sw[...]


def _scan(kr, kg, kw, va, b_n, t_n, dec, s0r, s0g, s0w):
    grouped = b_n * HEADS < LANES
    vh_n = s0r.shape[0]
    if grouped:
        tc = min(t_n, SCAN_CHUNK)
        n_grp = LANES // (HEADS * b_n)
        assert tc % n_grp == 0 and t_n % tc == 0
        n_l, n_t = 1, t_n // tc
        k_specs = [pl.BlockSpec((a.shape[0], tc * LANES), lambda l, t: (0, t)) for a in (kr, kg, kw)]
        v_spec = pl.BlockSpec((va.shape[0], tc * HEADS * b_n), lambda l, t: (0, t))
        o_spec = v_spec
    else:
        assert b_n == LANES
        tc, n_l, n_t = t_n, HEADS, 1
        k_specs = [pl.BlockSpec((a.shape[0], 1, a.shape[2], tc * LANES), lambda l, t: (0, l, 0, 0))
                   for a in (kr, kg, kw)]
        v_spec = pl.BlockSpec((3, 1, HEAD_D, tc * LANES), lambda l, t: (0, l, 0, 0))
        o_spec = v_spec

    def s_spec(a):
        return pl.BlockSpec(a.shape[:2] + (LANES,), lambda l, t: (0, 0, l))

    return pl.pallas_call(
        functools.partial(_scan_kernel, vh_n=vh_n, grouped=grouped, tc=tc, b_n=b_n),
        grid=(n_l, n_t),
        in_specs=k_specs + [v_spec, pl.BlockSpec((1, LANES), lambda l, t: (0, l)),
                            s_spec(s0r), s_spec(s0g), s_spec(s0w)],
        out_specs=[o_spec, s_spec(s0r), s_spec(s0g), s_spec(s0w)],
        out_shape=[jax.ShapeDtypeStruct(va.shape, F32)] + [jax.ShapeDtypeStruct(s.shape, F32)
                                                           for s in (s0r, s0g, s0w)],
        scratch_shapes=[pltpu.VMEM(s.shape[:2] + (LANES,), F32) for s in (s0r, s0g, s0w)]
                       + [pltpu.VMEM((tc, 3, vh_n, LANES), F32), pltpu.VMEM((tc, 3, vh_n, LANES), F32)],
        compiler_params=_cparams("parallel", "arbitrary"),
        name="scan_ret_gla_rwkv",
    )(kr, kg, kw, va, dec, s0r, s0g, s0w)


def _s5_kernel(u_ref, bb_ref, cc_ref, ar_ref, ai_ref, h0r_ref, h0i_ref,
               y_ref, hr_out, hi_out, hr_scr, hi_scr, xs_scr, hs_scr, *, b_n, tc):
    ti = pl.program_id(0)

    @pl.when(ti == 0)
    def _():
        hr_scr[...] = h0r_ref[...]
        hi_scr[...] = h0i_ref[...]

    xs_scr[...] = _bdot(u_ref[...], bb_ref[...])
    ar = ar_ref[...]
    ai = ai_ref[...]

    def step(t, carry):
        hr, hi = carry
        row = pl.multiple_of(t * b_n, SUBLANES)
        x = xs_scr[pl.ds(row, b_n), :]
        nr = ar * hr - ai * hi + x[:, :S5_CH]
        ni = ar * hi + ai * hr + x[:, S5_CH:]
        hs_scr[pl.ds(row, b_n), :S5_CH] = nr
        hs_scr[pl.ds(row, b_n), S5_CH:] = ni
        return nr, ni

    hr, hi = lax.fori_loop(0, tc, step, (hr_scr[...], hi_scr[...]))
    hr_scr[...] = hr
    hi_scr[...] = hi
    y_ref[...] = _bdot(hs_scr[...], cc_ref[...])

    @pl.when(ti == pl.num_programs(0) - 1)
    def _():
        hr_out[...] = hr
        hi_out[...] = hi


def _s5_scan(cols, row_off, bb, cc, ar, ai, h0r, h0i, b_n, t_n):
    tc = min(t_n, max(1, 1024 // b_n))
    rows = tc * b_n
    assert t_n % tc == 0 and row_off % rows == 0
    blk0 = row_off // rows
    return pl.pallas_call(
        functools.partial(_s5_kernel, b_n=b_n, tc=tc),
        grid=(t_n // tc,),
        in_specs=[pl.BlockSpec((rows, BRANCH_W), lambda t: (blk0 + t, COL_S5 // BRANCH_W)),
                  _const_spec(bb.shape), _const_spec(cc.shape),
                  _const_spec(ar.shape), _const_spec(ai.shape),
                  _const_spec(h0r.shape), _const_spec(h0i.shape)],
        out_specs=[pl.BlockSpec((rows, BRANCH_W), lambda t: (t, 0)),
                   _const_spec(h0r.shape), _const_spec(h0i.shape)],
        out_shape=[jax.ShapeDtypeStruct((t_n * b_n, BRANCH_W), F32),
                   jax.ShapeDtypeStruct(h0r.shape, F32),
                   jax.ShapeDtypeStruct(h0i.shape, F32)],
        scratch_shapes=[pltpu.VMEM((b_n, S5_CH), F32), pltpu.VMEM((b_n, S5_CH), F32),
                        pltpu.VMEM((rows, 2 * S5_CH), F32), pltpu.VMEM((rows, 2 * S5_CH), F32)],
        compiler_params=_cparams("arbitrary"),
        name="scan_s5",
    )(cols, bb, cc, ar, ai, h0r, h0i)


def _post_kernel(x_ref, mixt_ref, rg_ref, gg_ref, sy_ref, su_ref, wbon_ref, wg_ref,
                 gate0_ref, gate1_ref, gate2_ref, gate3_ref,
                 avg_ref, rgn_g, rgn_b, ggn_g, s5d_ref, wglu_ref, wgn_g, wgn_b,
                 wbr_ref, wo_ref, ln_g, ln_b, *rest, alpha, b_n):
    o_ref = rest[-1] if len(rest) == 1 else rest[4]

    def seg_mean(v):
        return _seg_dot(v, avg_ref)

    tm = x_ref.shape[0]
    if b_n * HEADS < LANES:
        yt = mixt_ref[...].T.reshape(tm // b_n, HEADS, b_n, 3 * HEAD_D)
        per_head = [yt[:, h].reshape(tm, 3 * HEAD_D) for h in range(HEADS)]
        mix = jnp.concatenate([ph[:, j * HEAD_D:(j + 1) * HEAD_D] for j in range(3) for ph in per_head], axis=1)
    else:
        mix = mixt_ref[...].T
    ro = mix[:, 0:BRANCH_W]
    mu = seg_mean(ro)
    rc = ro - mu
    var = seg_mean(rc * rc)
    b0 = (rc * lax.rsqrt(var + LN_EPS) * rgn_g[...] + rgn_b[...]) * _silu(rg_ref[...])
    go = mix[:, BRANCH_W:2 * BRANCH_W]
    ms = seg_mean(go * go)
    b1 = go * lax.rsqrt(ms + LN_EPS) * ggn_g[...] * _silu(gg_ref[...])
    y = jax.nn.gelu(sy_ref[...] + s5d_ref[...] * su_ref[...])
    b2 = y * _sigmoid(_bdot(y, wglu_ref[...]))
    wy = mix[:, 2 * BRANCH_W:3 * BRANCH_W]
    mu = seg_mean(wy)
    wc = wy - mu
    var = seg_mean(wc * wc)
    b3 = (wc * lax.rsqrt(var + RWKV_GN_EPS) * wgn_g[...] + wgn_b[...] + wbon_ref[...]) * wg_ref[...]

    m = None
    gates = (gate0_ref, gate1_ref, gate2_ref, gate3_ref)
    for i, br in enumerate((b0, b1, b2, b3)):
        term = _bdot(br, wbr_ref[i]) * _sigmoid(gates[i][...])
        m = term if m is None else m + term
    h = _bdot(m, wo_ref[...])
    x1 = _layer_norm(alpha * x_ref[...] + h, ln_g[...], ln_b[...])
    o_ref[...] = x1
    if len(rest) > 1:
        _route_top2(x1, rest[0], rest[1], rest[5], rest[6])


def _embed_ln2(x, f, p_ref, wpe_ref, wpg_ref, ln_g, ln_b, alpha):
    e = _bdot(p_ref[...], wpe_ref[...]) * _sigmoid(_bdot(x, wpg_ref[...]))
    return _layer_norm(alpha * x + f + e, ln_g[...], ln_b[...])


def _ffn_kernel(x_ref, p_ref, w1_ref, w3_ref, w2_ref, wpe_ref, wpg_ref, ln_g, ln_b, o_ref, *, alpha):
    x = x_ref[...]
    xb = x.astype(BF16)
    h = _silu(jnp.dot(xb, w1_ref[...], preferred_element_type=F32)) * jnp.dot(
        xb, w3_ref[...], preferred_element_type=F32)
    f = _bdot(h, w2_ref[...])
    o_ref[...] = _embed_ln2(x, f, p_ref, wpe_ref, wpg_ref, ln_g, ln_b, alpha)


def _route_top2(x, wh_ref, wl_ref, idx_ref, wgt_ref):
    xh, xl = _split_bf16(x)
    wh = wh_ref[...]
    logits = (jnp.dot(xh, wh, preferred_element_type=F32)
              + jnp.dot(xl, wh, preferred_element_type=F32)
              + jnp.dot(xh, wl_ref[...], preferred_element_type=F32))
    col = lax.broadcasted_iota(jnp.int32, logits.shape, 1)
    neg = jnp.float32(-jnp.inf)
    lg = jnp.where(col < N_EXPERTS, logits, neg)
    m1 = jnp.max(lg, axis=1, keepdims=True)
    i1 = jnp.min(jnp.where(lg == m1, col, LANES), axis=1, keepdims=True)
    lg2 = jnp.where(col == i1, neg, lg)
    m2 = jnp.max(lg2, axis=1, keepdims=True)
    i2 = jnp.min(jnp.where(lg2 == m2, col, LANES), axis=1, keepdims=True)
    e2 = jnp.exp(m2 - m1)
    den = 1.0 + e2
    idx_ref[...] = jnp.where(col == 0, i1, jnp.where(col == 1, i2, 0))
    wgt_ref[...] = jnp.where(col == 0, 1.0 / den, jnp.where(col == 1, e2 / den, 0.0))


def _row_copies(idx_ref, n_rows, make_copy, count=None):
    def each(action):
        def body(r, c):
            action(make_copy(r, idx_ref[0, 0, r]))
            return c

        if count is None:
            lax.fori_loop(0, n_rows, body, 0, unroll=8)
            return

        @pl.when(count == n_rows)
        def _():
            lax.fori_loop(0, n_rows, body, 0, unroll=8)

        @pl.when(count < n_rows)
        def _():
            lax.fori_loop(0, count, body, 0)

    return (lambda: each(lambda cp: cp.start())), (lambda: each(lambda cp: cp.wait()))


def _moe_block_kernel(be_ref, nb_ref, bv_ref, tok_ref, tok_next_ref, dst_ref, dst_m1_ref, dst_m2_ref,
                      x_hbm, w1_ref, w3_ref, w2_ref, out_hbm, xbuf, ybuf, sem_in, sem_out):
    j = pl.program_id(0)
    last = pl.num_programs(0) - 1
    nb = nb_ref[0]
    slot = j % 2

    def gather_copy(idx_ref, half, r):
        return pltpu.make_async_copy(x_hbm.at[pl.ds(idx_ref[0, 0, r], 1)], xbuf.at[half, pl.ds(r, 1)],
                                     sem_in.at[half])

    def gather(idx_ref, half):
        return _row_copies(idx_ref, MOE_BLOCK, lambda r, row: pltpu.make_async_copy(
            x_hbm.at[pl.ds(row, 1)], xbuf.at[half, pl.ds(r, 1)], sem_in.at[half]))

    def scatter(idx_ref, half, block):
        return _row_copies(idx_ref, MOE_BLOCK, lambda r, row: pltpu.make_async_copy(
            ybuf.at[half, pl.ds(r, 1)], out_hbm.at[pl.ds(row, 1)], sem_out.at[half]),
            count=bv_ref[jnp.maximum(block, 0)])

    @pl.when(jnp.logical_and(j == 0, nb > 0))
    def _():
        gather(tok_ref, 0)[0]()

    def step(cur):
        gather(tok_ref, cur)[1]()

        @pl.when(j >= 2)
        def _():
            scatter(dst_m2_ref, cur, j - 2)[1]()

        for r in range(MOE_BLOCK):
            gather_copy(tok_next_ref, 1 - cur, r).start()
        xb = xbuf[cur].astype(BF16)
        h = _silu(jnp.dot(xb, w1_ref[0], preferred_element_type=F32)) * jnp.dot(
            xb, w3_ref[0], preferred_element_type=F32)
        ybuf[cur] = _bdot(h, w2_ref[0])
        scatter(dst_ref, cur, j)[0]()

    for par in range(2):
        pl.when(jnp.logical_and(j < nb, slot == par))(functools.partial(step, par))

    @pl.when(jnp.logical_and(j == nb, nb > 0))
    def _():
        gather(tok_ref, slot)[1]()

    @pl.when(jnp.logical_and(j >= nb, jnp.logical_and(j >= 2, j - 2 < nb)))
    def _():
        scatter(dst_m2_ref, slot, j - 2)[1]()

    @pl.when(jnp.logical_and(j == last, jnp.logical_and(j >= 1, j - 1 < nb)))
    def _():
        scatter(dst_m1_ref, 1 - slot, j - 1)[1]()

    @pl.when(jnp.logical_and(j == last, j < nb))
    def _():
        scatter(dst_ref, slot, j)[1]()


def _moe_blocks(x, block_e, nb_used, block_valid, slot_tok, slot_dst, n_out_rows, w1, w3, w2):
    n_blocks = block_e.shape[0]
    d = x.shape[1]
    dff = w1.shape[2]
    tok = slot_tok.reshape(n_blocks, 1, MOE_BLOCK)
    dst = slot_dst.reshape(n_blocks, 1, MOE_BLOCK)

    def idx_spec(shift):
        return pl.BlockSpec((1, 1, MOE_BLOCK),
                            lambda j, be, nb, bv: (jnp.clip(j + shift, 0, n_blocks - 1), 0, 0),
                            memory_space=pltpu.SMEM)

    grid_spec = pltpu.PrefetchScalarGridSpec(
        num_scalar_prefetch=3,
        grid=(n_blocks + 1,),
        in_specs=[
            idx_spec(0), idx_spec(1), idx_spec(0), idx_spec(-1), idx_spec(-2),
            pl.BlockSpec(memory_space=pl.ANY),
            pl.BlockSpec((1, d, dff), lambda j, be, nb, bv: (be[jnp.minimum(j, n_blocks - 1)], 0, 0),
                         pipeline_mode=pl.Buffered(1)),
            pl.BlockSpec((1, d, dff), lambda j, be, nb, bv: (be[jnp.minimum(j, n_blocks - 1)], 0, 0),
                         pipeline_mode=pl.Buffered(1)),
            pl.BlockSpec((1, dff, d), lambda j, be, nb, bv: (be[jnp.minimum(j, n_blocks - 1)], 0, 0),
                         pipeline_mode=pl.Buffered(1)),
        ],
        out_specs=pl.BlockSpec(memory_space=pl.ANY),
        scratch_shapes=[pltpu.VMEM((2, MOE_BLOCK, d), F32), pltpu.VMEM((2, MOE_BLOCK, d), F32),
                        pltpu.SemaphoreType.DMA((2,)), pltpu.SemaphoreType.DMA((2,))],
    )
    return pl.pallas_call(
        _moe_block_kernel,
        grid_spec=grid_spec,
        out_shape=jax.ShapeDtypeStruct((n_out_rows, d), F32),
        compiler_params=_cparams("arbitrary"),
        name="moe_blocks",
    )(block_e, nb_used, block_valid, tok, tok, dst, dst, dst, x, w1, w3, w2)


def _moe_combine_kernel(y0_ref, y1_ref, x_ref, p_ref, wgt_ref, wpe_ref, wpg_ref, ln_g, ln_b, o_ref, *, alpha):
    wgt = wgt_ref[...]
    f = y0_ref[...] * wgt[:, 0:1] + y1_ref[...] * wgt[:, 1:2]
    o_ref[...] = _embed_ln2(x_ref[...], f, p_ref, wpe_ref, wpg_ref, ln_g, ln_b, alpha)


def _lane_groups(b_n):
    bh = b_n * HEADS
    return LANES // bh if bh < LANES else 1


def _state_to_lanes(s, vl_n, value_last):
    s = s.transpose(3, 2, 1, 0) if value_last else s.transpose(2, 3, 1, 0)
    v_n, k_n, h_n, b_n = s.shape
    s = s.reshape(vl_n, v_n // vl_n, k_n, h_n, b_n).transpose(1, 2, 0, 3, 4)
    return s.reshape(v_n // vl_n, k_n, vl_n * h_n * b_n)


def _state_from_lanes(s, b_n, vl_n, value_last):
    vh_n, k_n, _ = s.shape
    s = s.reshape(vh_n, k_n, vl_n, HEADS, b_n).transpose(4, 3, 1, 2, 0).reshape(b_n, HEADS, k_n, vl_n * vh_n)
    return s if value_last else s.transpose(0, 1, 3, 2)


def _rotary_tables(pos, row_repeat, gamma, tc):
    half = HEAD_D // 2
    freq = ROPE_BASE ** (-jnp.arange(half, dtype=F32) / half)
    ang = pos.astype(F32)[:, None] * freq[None, :]
    cos, sin = jnp.cos(ang), jnp.sin(ang)
    cos_h = jnp.tile(jnp.concatenate([cos, cos], axis=-1), (1, HEADS))
    sin_h = jnp.tile(jnp.concatenate([-sin, sin], axis=-1), (1, HEADS))
    tau1 = (jnp.arange(pos.shape[0], dtype=jnp.int32) % tc + 1).astype(F32)
    f_q = jnp.repeat(gamma[None, :] ** tau1[:, None], HEAD_D, axis=1)
    f_k = (HEAD_D ** -0.5) / f_q
    tabs = (cos_h * f_q, sin_h * f_q, cos_h * f_k, sin_h * f_k)
    return tuple(jnp.repeat(t, row_repeat, axis=0) for t in tabs) if row_repeat > 1 else tabs


def _block_diag_const(block, n_blocks):
    return jnp.kron(jnp.eye(n_blocks, dtype=F32), jnp.full((block, block), 1.0, F32))


def _s5_params(log_dt, a_re, a_im, b_re, b_im, c_re, c_im):
    dt = jnp.exp(log_dt)[:, None]
    mag = jnp.exp(dt * a_re)
    ang = dt * a_im
    abar_re, abar_im = mag * jnp.cos(ang), mag * jnp.sin(ang)
    den = a_re * a_re + a_im * a_im
    n_re = abar_re - 1.0
    f_re = (n_re * a_re + abar_im * a_im) / den
    f_im = (abar_im * a_re - n_re * a_im) / den
    bb_re = f_re[..., None] * b_re - f_im[..., None] * b_im
    bb_im = f_re[..., None] * b_im + f_im[..., None] * b_re
    eye = jnp.eye(S5_GROUPS, dtype=F32)

    def in_map(bb):
        return jnp.einsum("gpc,gh->gchp", bb, eye).reshape(BRANCH_W, S5_CH)

    def out_map(cm):
        return jnp.einsum("gcp,gh->gphc", cm, eye).reshape(S5_CH, BRANCH_W)

    bb = jnp.concatenate([in_map(bb_re), in_map(bb_im)], axis=1).astype(BF16)
    cc = jnp.concatenate([out_map(c_re), -out_map(c_im)], axis=0).astype(BF16)
    return bb, cc, abar_re.reshape(1, S5_CH), abar_im.reshape(1, S5_CH)


def _reorder_w_in(w_in):
    d_model = w_in.shape[0]
    main = jnp.concatenate([w_in[:, :1792], w_in[:, 1808:]], axis=1)
    tail = jnp.concatenate([w_in[:, 1792:1808],
                            jnp.zeros((d_model, D_IN_PAD - COL_GLA_R - GLA_GATE_RANK), w_in.dtype)], axis=1)
    return jnp.concatenate([main, tail], axis=1).astype(BF16)


def _pad_cols(w, n):
    return jnp.pad(w, ((0, 0), (0, n - w.shape[1])))


def _pad_rows(w, n):
    return jnp.pad(w, ((0, n - w.shape[0]), (0, 0)))


def _moe_route(idx, n):
    nk = n * TOP_K
    flat_e = idx.reshape(nk)
    onehot = (flat_e[:, None] == jnp.arange(N_EXPERTS, dtype=jnp.int32)[None, :]).astype(jnp.int32)
    incl = jnp.cumsum(onehot, axis=0)
    counts = incl[-1]
    rank = jnp.sum((incl - onehot) * onehot, axis=1)
    padded = (counts + MOE_BLOCK - 1) // MOE_BLOCK * MOE_BLOCK
    pad_end = jnp.cumsum(padded)
    slot = (pad_end - padded)[flat_e] + rank
    n_blocks = -(-(nk + N_EXPERTS * (MOE_BLOCK - 1)) // MOE_BLOCK)
    cap = n_blocks * MOE_BLOCK
    pair = jnp.full((cap,), -1, jnp.int32).at[slot].set(jnp.arange(nk, dtype=jnp.int32))
    real = pair >= 0
    slot_tok = jnp.where(real, pair // TOP_K, 0)
    slot_dst = jnp.where(real, (pair % TOP_K) * n + pair // TOP_K, 0)
    block_start = jnp.arange(n_blocks, dtype=jnp.int32) * MOE_BLOCK
    block_e = jnp.minimum(jnp.sum((pad_end[None, :] <= block_start[:, None]).astype(jnp.int32), axis=1),
                          N_EXPERTS - 1).astype(jnp.int32)
    nb_used = (pad_end[-1] // MOE_BLOCK).astype(jnp.int32).reshape(1)
    block_valid = jnp.sum(real.reshape(n_blocks, MOE_BLOCK).astype(jnp.int32), axis=1)
    return slot_tok, slot_dst, block_e, nb_used, block_valid, nk


def kernel(x_prompt, x_sample, state_ret, state_gla, state_s5_re, state_s5_im, state_rwkv, state_shift,
           p_prompt, p_sample, w_in, ret_gn_g, ret_gn_b, gla_wg2, gla_bg, gla_gn,
           s5_log_dt, s5_a_re, s5_a_im, s5_b_re, s5_b_im, s5_c_re, s5_c_im, s5_d, s5_w_glu,
           rwkv_mu, rwkv_w0, rwkv_w1, rwkv_w2, rwkv_a0, rwkv_a1, rwkv_a2, rwkv_g1, rwkv_g2,
           rwkv_kk, rwkv_ka, rwkv_rk, rwkv_gn_g, rwkv_gn_b, w_branch, w_o,
           ln1_g, ln1_b, ln2_g, ln2_b, w_pe, w_pg, ffn_w1, ffn_w3, ffn_w2,
           moe_router, moe_w1, moe_w3, moe_w2):
    depth = w_in.shape[0]
    bp, tp, d_model = x_prompt.shape
    bs, ts, _ = x_sample.shape
    n_p, n_s = bp * tp, bs * ts
    n = n_p + n_s
    tm = ROW_TILE
    assert n_p % tm == 0 and n_s % tm == 0 and tm % bp == 0 and tm % bs == 0 and n_p % bs == 0
    alpha = (2 * depth) ** 0.25
    groups = [(0, bp, tp), (n_p, bs, ts)]
    w4 = 4 * BRANCH_W

    def time_major(a_p, a_s):
        return jnp.concatenate([a_p.transpose(1, 0, 2).reshape(n_p, -1),
                                a_s.transpose(1, 0, 2).reshape(n_s, -1)], axis=0)

    x = time_major(x_prompt, x_sample)
    ones_bd = _block_diag_const(HEAD_D, HEADS).astype(BF16)
    avg_bd = (_block_diag_const(HEAD_D, HEADS) / HEAD_D).astype(BF16)
    gamma = 1.0 - jnp.exp2(-5.0 - jnp.arange(HEADS, dtype=F32))
    tc_p, tc_s = min(tp, SCAN_CHUNK), ts
    rot_p = _rotary_tables(jnp.arange(tp, dtype=jnp.int32), 1 if tm // bp >= SUBLANES else bp, gamma, tc_p)
    rot_s = _rotary_tables(PAST_LEN + jnp.arange(ts, dtype=jnp.int32), 1 if tm // bs >= SUBLANES else bs,
                           gamma, tc_s)
    row = lambda v: v.reshape(1, -1)

    new = [[] for _ in range(6)]
    for i in range(depth):
        cols = _matmul(x, _reorder_w_in(w_in[i]), tm=1024, tn=D_IN_PAD // 3, name="in_proj")

        prep_consts = [
            _pad_rows(_pad_cols(gla_wg2[i], LANES), LANES).astype(BF16), row(gla_bg[i]),
            rwkv_mu[i], row(rwkv_w0[i]),
            _pad_cols(rwkv_w1[i], LANES).astype(BF16), _pad_rows(rwkv_w2[i], LANES).astype(BF16),
            row(rwkv_a0[i]),
            _pad_cols(rwkv_a1[i], LANES).astype(BF16), _pad_rows(rwkv_a2[i], LANES).astype(BF16),
            _pad_cols(rwkv_g1[i], LANES).astype(BF16), _pad_rows(rwkv_g2[i], LANES).astype(BF16),
            row(rwkv_kk[i]), row(rwkv_ka[i]), row(rwkv_rk[i]), ones_bd,
        ]
        post_consts = [avg_bd, row(ret_gn_g[i]), row(ret_gn_b[i]), row(gla_gn[i]), row(s5_d[i]),
                       s5_w_glu[i].astype(BF16), row(rwkv_gn_g[i]), row(rwkv_gn_b[i]),
                       w_branch[i].astype(BF16), w_o[i].astype(BF16), row(ln1_g[i]), row(ln1_b[i])]
        s5p = _s5_params(s5_log_dt[i], s5_a_re[i], s5_a_im[i], s5_b_re[i], s5_b_im[i], s5_c_re[i], s5_c_im[i])
        rwkv_blk = COL_RWKV // w4
        x1, layer_new = x, []
        routed = i % 2 == 1
        if routed:
            router_hi, router_lo = _split_bf16(_pad_cols(moe_router[i // 2], LANES))
            idx, wgt = jnp.zeros((n, LANES), jnp.int32), jnp.zeros((n, LANES), F32)
        for gi, (off, b_n, t_n) in enumerate(groups):
            n_g = b_n * t_n
            t0 = off // tm
            if gi == 0:
                st = [jnp.zeros((b_n,) + s.shape[2:], s.dtype)
                      for s in (state_ret, state_gla, state_s5_re, state_s5_im, state_rwkv)]
                shift0, rot_g, tc_g = jnp.zeros((b_n, w4), F32), rot_p, tc_p
            else:
                st = [state_ret[i], state_gla[i], state_s5_re[i], state_s5_im[i], state_rwkv[i]]
                shift0, rot_g, tc_g = state_shift[i], rot_s, tc_s
            vl_n = _lane_groups(b_n)
            grouped = vl_n > 1

            def cspec(width, cb, t0=t0):
                return pl.BlockSpec((tm, width), lambda r: (r + t0, cb))

            if grouped:
                t_shapes = [(2 * HEAD_D, n_g * vl_n * HEADS), (3 * GLA_DK, n_g * vl_n * HEADS),
                            (5 * HEAD_D, n_g * vl_n * HEADS), (3 * HEAD_D, n_g * HEADS)]
            else:
                t_shapes = [(2 * BRANCH_W, n_g), (3 * GLA_QK, n_g), (5 * BRANCH_W, n_g), (3 * BRANCH_W, n_g)]
            consts = [shift0] + prep_consts
            ka_ret, ka_gla, ka_rwkv, va, obonus, og = pl.pallas_call(
                functools.partial(_prep_kernel, grouped=grouped),
                grid=(n_g // tm,),
                in_specs=[cspec(BRANCH_W, 0), cspec(BRANCH_W, 1), cspec(BRANCH_W, 2),
                          ] + [_row_spec(rot_g[0].shape[0] // (n_g // tm), BRANCH_W, 0)] * 4 + [
                          cspec(GLA_QK, COL_GLA_Q // GLA_QK), cspec(GLA_QK, COL_GLA_K // GLA_QK),
                          cspec(BRANCH_W, COL_GLA_V // BRANCH_W), cspec(LANES, COL_GLA_R // LANES),
                          cspec(w4, rwkv_blk),
                          pl.BlockSpec((b_n, w4), lambda r, t0=t0, b_n=b_n:
                                       (jnp.maximum((r + t0) * (tm // b_n) - 1, 0), rwkv_blk))]
                         + [_const_spec(a.shape) for a in consts],
                out_specs=[pl.BlockSpec((sh[0], sh[1] // (n_g // tm)), lambda r: (0, r)) for sh in t_shapes]
                          + [_row_spec(tm, BRANCH_W, 0), _row_spec(tm, BRANCH_W, 0)],
                out_shape=[jax.ShapeDtypeStruct(sh, F32) for sh in t_shapes]
                          + [jax.ShapeDtypeStruct((n_g, BRANCH_W), F32)] * 2,
                compiler_params=_cparams("parallel"),
                name="mixer_prep",
            )(cols, cols, cols, *rot_g, cols, cols, cols, cols, cols, cols, *consts)

            dec = jnp.tile(jnp.repeat(gamma ** tc_g, b_n), vl_n).reshape(1, vl_n * HEADS * b_n)
            if not grouped:
                ka_ret = ka_ret.reshape(2, HEADS, HEAD_D, n_g)
                ka_gla = ka_gla.reshape(3, HEADS, GLA_DK, n_g)
                ka_rwkv = ka_rwkv.reshape(5, HEADS, HEAD_D, n_g)
                va = va.reshape(3, HEADS, HEAD_D, n_g)
            o, s_ret, s_gla, s_rwkv = _scan(
                ka_ret, ka_gla, ka_rwkv, va, b_n, t_n, dec,
                _state_to_lanes(st[0], vl_n, True), _state_to_lanes(st[1], vl_n, True),
                _state_to_lanes(st[4], vl_n, False))
            mixt = o if grouped else o.reshape(3 * BRANCH_W, n_g)
            y, hr, hi = _s5_scan(cols, off, *s5p, st[2].reshape(b_n, S5_CH), st[3].reshape(b_n, S5_CH), b_n, t_n)
            shift_new = cols[off + (t_n - 1) * b_n:off + t_n * b_n, COL_RWKV:COL_RWKV + w4]
            layer_new.append((_state_from_lanes(s_ret, b_n, vl_n, True), _state_from_lanes(s_gla, b_n, vl_n, True),
                              hr.reshape(b_n, S5_GROUPS, S5_STATE), hi.reshape(b_n, S5_GROUPS, S5_STATE),
                              _state_from_lanes(s_rwkv, b_n, vl_n, False), shift_new))

            post_in = [x1, mixt, cols, cols, y, cols, obonus, og, cols, cols, cols, cols, *post_consts]
            post_specs = ([cspec(d_model, 0),
                           pl.BlockSpec((mixt.shape[0], mixt.shape[1] // (n_g // tm)), lambda r: (0, r)),
                           cspec(BRANCH_W, 3), cspec(BRANCH_W, COL_GLA_G // BRANCH_W),
                           _row_spec(tm, BRANCH_W, 0), cspec(BRANCH_W, COL_S5 // BRANCH_W),
                           _row_spec(tm, BRANCH_W, 0), _row_spec(tm, BRANCH_W, 0)]
                          + [cspec(d_model, COL_GATE // d_model + q) for q in range(N_BRANCH)]
                          + [_const_spec(a.shape) for a in post_consts])
            out_specs, out_shape, aliases = [cspec(d_model, 0)], [jax.ShapeDtypeStruct((n, d_model), F32)], {0: 0}
            if routed:
                aliases.update({len(post_in) + 2: 1, len(post_in) + 3: 2})
                post_in += [router_hi, router_lo, idx, wgt]
                post_specs += [_const_spec(router_hi.shape), _const_spec(router_lo.shape),
                               cspec(LANES, 0), cspec(LANES, 0)]
                out_specs += [cspec(LANES, 0), cspec(LANES, 0)]
                out_shape += [jax.ShapeDtypeStruct((n, LANES), jnp.int32), jax.ShapeDtypeStruct((n, LANES), F32)]
            res = pl.pallas_call(
                functools.partial(_post_kernel, alpha=alpha, b_n=b_n),
                grid=(n_g // tm,),
                in_specs=post_specs,
                out_specs=out_specs,
                out_shape=out_shape,
                input_output_aliases=aliases,
                compiler_params=_cparams("parallel"),
                name="mixer_post",
            )(*post_in)
            x1 = res[0]
            if routed:
                idx, wgt = res[1], res[2]
        for lst, pair in zip(new, zip(*layer_new)):
            lst.append(pair)

        if i + 1 < depth:
            p_in = [time_major(p_prompt[i], p_sample[i])]
        else:
            p_in = [p_prompt[i].transpose(1, 0, 2).reshape(n_p, -1), p_sample[i].transpose(1, 0, 2).reshape(n_s, -1)]
        tail_consts = [w_pe[i].astype(BF16), w_pg[i].astype(BF16), row(ln2_g[i]), row(ln2_b[i])]
        j = i // 2
        spans = [(0, n, 0)] if i + 1 < depth else [(0, n_p, 0), (n_p // tm, n_s, 0)]
        if i % 2 == 0:
            ffn_consts = [ffn_w1[j].astype(BF16), ffn_w3[j].astype(BF16), ffn_w2[j].astype(BF16)] + tail_consts
            xs = [_rowwise(functools.partial(_ffn_kernel, alpha=alpha), rows,
                           [(x1, d_model, 0, r0), (p, p.shape[1], 0, p0)],
                           ffn_consts, [d_model], name="ffn")[0] for (r0, rows, p0), p in zip(spans, p_in)]
        else:
            slot_tok, slot_dst, block_e, nb_used, block_valid, n_out_rows = _moe_route(idx[:, :TOP_K], n)
            yk = _moe_blocks(x1, block_e, nb_used, block_valid, slot_tok, slot_dst, n_out_rows,
                             moe_w1[j].astype(BF16), moe_w3[j].astype(BF16), moe_w2[j].astype(BF16))
            xs = [_rowwise(functools.partial(_moe_combine_kernel, alpha=alpha), rows,
                           [(yk, d_model, 0, r0), (yk, d_model, 0, n // tm + r0), (x1, d_model, 0, r0),
                            (p, p.shape[1], 0, p0), (wgt, LANES, 0, r0)],
                           tail_consts, [d_model], name="moe_combine")[0] for (r0, rows, p0), p in zip(spans, p_in)]
        x = xs[0]

    y_prompt = xs[0].reshape(tp, bp, d_model).transpose(1, 0, 2)
    y_sample = xs[1].reshape(ts, bs, d_model).transpose(1, 0, 2)
    outs = [y_prompt, y_sample]
    for lst in new:
        outs.append(jnp.stack([pair[0] for pair in lst], 0))
        outs.append(jnp.stack([pair[1] for pair in lst], 0))
    return tuple(outs)
```
